```python
import jax
import jax.numpy as jnp
from jax import lax
import numpy as np

D_MODEL = 1024
BATCH = 16
SEQ = 2048
DEPTH = 1
DEC_BATCH = 128
DEC_SEQ = 4
PAST_LEN = 16384
PAGE_SIZE = 128

HEAD_DIM = 64
ROPE_DIM = HEAD_DIM // 4
ROPE_THETA = 500000.0
SWA_Q_HEADS = 8
SWA_KV_HEADS = 2
SWA_WINDOW = 128
DIL_PAIRS = ((128, 1), (512, 4), (2048, 16))
DIL_KV_HEADS = 4
DIL_Q_HEADS = DIL_KV_HEADS * len(DIL_PAIRS)
DIL_MAX_WINDOW = 2048
MEM_TOKENS = 256
MEM_HEADS = 4
N_BRANCHES = 3
N_EXPERTS = 32
TOP_K = 4
D_FF = D_MODEL
SWIGLU_LIMIT = 7.0
SWIGLU_ALPHA = 1.702
BLOCK = 128
MOE_BLOCK = 128
RMS_EPS = 1e-5
IN_SPLITS = (SWA_Q_HEADS * HEAD_DIM, SWA_KV_HEADS * HEAD_DIM, SWA_KV_HEADS * HEAD_DIM,
             DIL_Q_HEADS * HEAD_DIM, DIL_KV_HEADS * HEAD_DIM, DIL_KV_HEADS * HEAD_DIM,
             MEM_HEADS * HEAD_DIM, N_BRANCHES * D_MODEL)
IN_WIDTH = sum(IN_SPLITS)

kernel_name = 'hybrid_swa_dilated_mem_moe_step'


def rmsnorm(x, g):
    xf = x.astype(jnp.float32)
    y = xf * lax.rsqrt(jnp.mean(xf * xf, axis=-1, keepdims=True) + RMS_EPS)
    return (y * g.astype(jnp.float32)).astype(x.dtype)


def partial_rope(x, pos):
    half = ROPE_DIM // 2
    inv_freq = ROPE_THETA ** (-jnp.arange(half, dtype=jnp.float32) / half)
    ang = pos.astype(jnp.float32)[:, None] * inv_freq[None, :]
    cos = jnp.cos(ang)[:, None, :]
    sin = jnp.sin(ang)[:, None, :]
    xr = x[..., :ROPE_DIM].astype(jnp.float32)
    x1, x2 = xr[..., :half], xr[..., half:]
    rot = jnp.concatenate([x1 * cos - x2 * sin, x2 * cos + x1 * sin], axis=-1).astype(x.dtype)
    return jnp.concatenate([rot, x[..., ROPE_DIM:]], axis=-1)


def softmax_lse(s, sink=None):
    m = jnp.max(s, axis=-1, keepdims=True)
    if sink is not None:
        m = jnp.maximum(m, sink)
    e = jnp.exp(s - m)
    den = jnp.sum(e, axis=-1, keepdims=True)
    if sink is not None:
        den = den + jnp.exp(sink - m)
    return e / den, (m + jnp.log(den))[..., 0]


def banded_attention(q, k, v, max_dist, sink=None):
    B, L, Hq, D = q.shape
    Hkv = k.shape[2]
    G = Hq // Hkv
    T = BLOCK
    nb = -(-L // T)
    pad = nb * T - L
    qp = jnp.pad(q, ((0, 0), (0, pad), (0, 0), (0, 0)))
    kp = jnp.pad(k, ((0, 0), (T, pad), (0, 0), (0, 0)))
    vp = jnp.pad(v, ((0, 0), (T, pad), (0, 0), (0, 0)))
    qb = qp.reshape(B, nb, T, Hkv, G, D)
    kb = kp.reshape(B, nb + 1, T, Hkv, D)
    vb = vp.reshape(B, nb + 1, T, Hkv, D)
    kk = jnp.concatenate([kb[:, :-1], kb[:, 1:]], axis=2)
    vv = jnp.concatenate([vb[:, :-1], vb[:, 1:]], axis=2)
    qi = jnp.arange(T)[:, None] + T
    kj = jnp.arange(2 * T)[None, :]
    dist = qi - kj
    key_abs = jnp.arange(nb)[:, None, None] * T - T + kj[None]
    valid = (dist >= 0) & (dist <= max_dist) & (key_abs >= 0)
    s = jnp.einsum('bnqhgd,bnkhd->bnhgqk', qb, kk).astype(jnp.float32) * (HEAD_DIM ** -0.5)
    s = jnp.where(valid[None, :, None, None], s, -jnp.inf)
    p, lse = softmax_lse(s, sink)
    o = jnp.einsum('bnhgqk,bnkhd->bnqhgd', p.astype(v.dtype), vv).reshape(B, nb * T, Hq, D)[:, :L]
    lse = lse.transpose(0, 1, 4, 2, 3).reshape(B, nb * T, Hq)[:, :L]
    return o, lse


def gathered_attention(q, k_ext, v_ext, idx, sink=None):
    B, nq, Hq, D = q.shape
    Hkv = k_ext.shape[2]
    G = Hq // Hkv
    valid = idx >= 0
    safe = jnp.maximum(idx, 0)
    kg = k_ext[:, safe]
    vg = v_ext[:, safe]
    s = jnp.einsum('bqhgd,bqkhd->bhgqk', q.reshape(B, nq, Hkv, G, D), kg).astype(jnp.float32) * (HEAD_DIM ** -0.5)
    s = jnp.where(valid, s, -jnp.inf)
    p, lse = softmax_lse(s, sink)
    o = jnp.einsum('bhgqk,bqkhd->bqhgd', p.astype(v_ext.dtype), vg).reshape(B, nq, Hq, D)
    return o, lse.transpose(0, 3, 1, 2).reshape(B, nq, Hq)


def memory_attention(q, mk, mv):
    s = jnp.einsum('bqhd,bkhd->bhqk', q, mk).astype(jnp.float32) * (HEAD_DIM ** -0.5)
    p = jax.nn.softmax(s, axis=-1).astype(mv.dtype)
    return jnp.einsum('bhqk,bkhd->bqhd', p, mv)


def memory_kv(mem, g, w):
    B, M, _ = mem.shape
    mk, mv = jnp.split(rmsnorm(mem, g) @ w, 2, axis=-1)
    return mk.reshape(B, M, MEM_HEADS, HEAD_DIM), mv.reshape(B, M, MEM_HEADS, HEAD_DIM)


def _to_classes(t, dil):
    B, L = t.shape[:2]
    t = t.reshape((B, L // dil, dil) + t.shape[2:])
    return jnp.swapaxes(t, 1, 2).reshape((B * dil, L // dil) + t.shape[3:])


def _from_classes(t, dil, B):
    t = jnp.swapaxes(t.reshape((B, dil) + t.shape[1:]), 1, 2)
    return t.reshape((B, t.shape[1] * dil) + t.shape[3:])


def combine_dilations(outs, lses):
    alpha = jax.nn.softmax(jnp.stack(lses, axis=0), axis=0)
    o = jnp.stack(outs, axis=0).astype(jnp.float32)
    return jnp.sum(alpha[..., None] * o, axis=0).astype(outs[0].dtype)


def dilated_prompt(q, k, v):
    B = q.shape[0]
    outs, lses = [], []
    for g, (window, dil) in enumerate(DIL_PAIRS):
        qg = q[:, :, g * DIL_KV_HEADS:(g + 1) * DIL_KV_HEADS]
        o, lse = banded_attention(_to_classes(qg, dil), _to_classes(k, dil), _to_classes(v, dil), window // dil)
        outs.append(_from_classes(o, dil, B))
        lses.append(_from_classes(lse, dil, B))
    return combine_dilations(outs, lses)


def dilated_sample(q, k_ext, v_ext, n_past):
    qi = n_past + jnp.arange(q.shape[1])[:, None]
    outs, lses = [], []
    for g, (window, dil) in enumerate(DIL_PAIRS):
        idx = qi - dil * jnp.arange(window // dil + 1)[None, :]
        o, lse = gathered_attention(q[:, :, g * DIL_KV_HEADS:(g + 1) * DIL_KV_HEADS], k_ext, v_ext, idx)
        outs.append(o)
        lses.append(lse)
    return combine_dilations(outs, lses)


def project_inputs(x, norm_g, w_in, pos):
    B, L, _ = x.shape
    h = rmsnorm(x, norm_g)
    qa, ka, va, qb, kb, vb, qc, gate_logits = jnp.split(h @ w_in, np.cumsum(IN_SPLITS)[:-1].tolist(), axis=-1)
    heads = lambda t: t.reshape(B, L, -1, HEAD_DIM)
    return (partial_rope(heads(qa), pos), partial_rope(heads(ka), pos), heads(va),
            partial_rope(heads(qb), pos), partial_rope(heads(kb), pos), heads(vb),
            heads(qc), gate_logits)


def moe_ffn(h, w_router, b_router, w_gate_up, b_gate_up, w_down, b_down):
    shp = h.shape
    x2 = h.reshape(-1, D_MODEL)
    T = x2.shape[0]
    logits = (x2 @ w_router).astype(jnp.float32) + b_router.astype(jnp.float32)
    top_val, top_idx = lax.top_k(logits, TOP_K)
    gate = jax.nn.softmax(top_val, axis=-1)
    n_assign = T * TOP_K
    flat_e = top_idx.reshape(-1)
    order = jnp.argsort(flat_e)
    sorted_e = flat_e[order]
    sorted_tok = order // TOP_K
    sorted_gate = gate.reshape(-1)[order]
    counts = jnp.bincount(flat_e, length=N_EXPERTS)
    padded = (counts + MOE_BLOCK - 1) // MOE_BLOCK * MOE_BLOCK
    start = jnp.cumsum(counts) - counts
    pend = jnp.cumsum(padded)
    pstart = pend - padded
    dest = pstart[sorted_e] + jnp.arange(n_assign) - start[sorted_e]
    n_blocks = (n_assign + N_EXPERTS * (MOE_BLOCK - 1) + MOE_BLOCK - 1) // MOE_BLOCK
    row_tok = jnp.zeros((n_blocks * MOE_BLOCK,), jnp.int32).at[dest].set(sorted_tok.astype(jnp.int32))
    block_e = jnp.minimum(jnp.searchsorted(pend, jnp.arange(n_blocks) * MOE_BLOCK, side='right'), N_EXPERTS - 1)

    def expert_block(args):
        tok, e = args
        xb = x2[tok]
        gu = xb @ w_gate_up[e] + b_gate_up[e]
        gt = jnp.minimum(gu[:, :D_FF], SWIGLU_LIMIT)
        up = jnp.clip(gu[:, D_FF:], -SWIGLU_LIMIT, SWIGLU_LIMIT)
        glu = gt * jax.nn.sigmoid(gt * SWIGLU_ALPHA)
        return ((up + 1.0) * glu) @ w_down[e] + b_down[e]

    y_rows = lax.map(expert_block, (row_tok.reshape(n_blocks, MOE_BLOCK), block_e)).reshape(-1, D_MODEL)
    contrib = y_rows[dest] * sorted_gate[:, None].astype(y_rows.dtype)
    return jnp.zeros_like(x2).at[sorted_tok].add(contrib).reshape(shp)


def merge_and_ffn(x, oa, ob, oc, gate_logits, w_br_a, w_br_b, w_br_c, w_out, norm_ffn,
                  w_router, b_router, w_gate_up, b_gate_up, w_down, b_down):
    B, L, _ = x.shape
    g = jax.nn.sigmoid(gate_logits.astype(jnp.float32)).astype(x.dtype).reshape(B, L, N_BRANCHES, D_MODEL)
    merged = (g[:, :, 0] * (oa.reshape(B, L, -1) @ w_br_a)
              + g[:, :, 1] * (ob.reshape(B, L, -1) @ w_br_b)
              + g[:, :, 2] * (oc.reshape(B, L, -1) @ w_br_c))
    x = x + merged @ w_out
    return x + moe_ffn(rmsnorm(x, norm_ffn), w_router, b_router, w_gate_up, b_gate_up, w_down, b_down)


def setup_inputs(seed: int = 0) -> dict:
    key = jax.random.key(seed)
    ks = jax.random.split(key, 26)
    f32 = jnp.float32

    def nrm(k, shape, scale):
        return jax.random.normal(k, shape, f32) * scale

    def gain(k, shape):
        return 1.0 + 0.05 * jax.random.normal(k, shape, f32)

    d = D_MODEL
    la = min(SWA_WINDOW, PAST_LEN)
    lb = min(DIL_MAX_WINDOW, PAST_LEN)
    kva = (DEPTH, DEC_BATCH, la, SWA_KV_HEADS, HEAD_DIM)
    kvb = (DEPTH, DEC_BATCH, lb, DIL_KV_HEADS, HEAD_DIM)
    kvm = (DEPTH, DEC_BATCH, MEM_TOKENS, MEM_HEADS, HEAD_DIM)
    wa = SWA_Q_HEADS * HEAD_DIM
    wb = DIL_KV_HEADS * HEAD_DIM
    wc = MEM_HEADS * HEAD_DIM
    return {
        'x_prompt': nrm(ks[0], (BATCH, SEQ, d), 1.0),
        'x_sample': nrm(ks[1], (DEC_BATCH, DEC_SEQ, d), 1.0),
        'cache_swa_k': nrm(ks[2], kva, 1.0),
        'cache_swa_v': nrm(ks[3], kva, 1.0),
        'cache_dil_k': nrm(ks[4], kvb, 1.0),
        'cache_dil_v': nrm(ks[5], kvb, 1.0),
        'cache_mem_k': nrm(ks[6], kvm, 1.0),
        'cache_mem_v': nrm(ks[7], kvm, 1.0),
        'mem_prompt': nrm(ks[8], (BATCH, MEM_TOKENS, d), 1.0),
        'norm_attn': gain(ks[9], (DEPTH, d)),
        'norm_mem': gain(ks[10], (DEPTH, d)),
        'w_in': nrm(ks[11], (DEPTH, d, IN_WIDTH), d ** -0.5),
        'w_mem_kv': nrm(ks[12], (DEPTH, d, 2 * wc), d ** -0.5),
        'sinks': nrm(ks[13], (DEPTH, SWA_Q_HEADS), 0.5),
        'w_br_a': nrm(ks[14], (DEPTH, wa, d), wa ** -0.5),
        'w_br_b': nrm(ks[15], (DEPTH, wb, d), wb ** -0.5),
        'w_br_c': nrm(ks[16], (DEPTH, wc, d), wc ** -0.5),
        'w_out': nrm(ks[17], (DEPTH, d, d), d ** -0.5),
        'norm_ffn': gain(ks[18], (DEPTH, d)),
        'w_router': nrm(ks[19], (DEPTH, d, N_EXPERTS), d ** -0.5),
        'b_router': nrm(ks[20], (DEPTH, N_EXPERTS), 0.01),
        'w_gate_up': nrm(ks[21], (DEPTH, N_EXPERTS, d, 2 * D_FF), d ** -0.5),
        'b_gate_up': nrm(ks[22], (DEPTH, N_EXPERTS, 2 * D_FF), 0.01),
        'w_down': nrm(ks[23], (DEPTH, N_EXPERTS, D_FF, d), D_FF ** -0.5),
        'b_down': nrm(ks[24], (DEPTH, N_EXPERTS, d), 0.01),
        'norm_final': gain(ks[25], (d,)),
    }


def reference(x_prompt, x_sample, cache_swa_k, cache_swa_v, cache_dil_k, cache_dil_v, cache_mem_k, cache_mem_v,
              mem_prompt, norm_attn, norm_mem, w_in, w_mem_kv, sinks, w_br_a, w_br_b, w_br_c, w_out, norm_ffn,
              w_router, b_router, w_gate_up, b_gate_up, w_down, b_down, norm_final):
    n_prompt = x_prompt.shape[1]
    n_new = x_sample.shape[1]
    pos_p = jnp.arange(n_prompt, dtype=jnp.int32)
    pos_s = PAST_LEN + jnp.arange(n_new, dtype=jnp.int32)
    la = cache_swa_k.shape[2]
    lb = cache_dil_k.shape[2]
    la_p = min(SWA_WINDOW, n_prompt)
    lb_p = min(DIL_MAX_WINDOW, n_prompt)
    xp, xs = x_prompt, x_sample
    swa_k_p, swa_v_p, dil_k_p, dil_v_p, mem_k_p, mem_v_p = [], [], [], [], [], []
    swa_k_s, swa_v_s, dil_k_s, dil_v_s = [], [], [], []
    for l in range(DEPTH):
        sink = sinks[l].reshape(SWA_KV_HEADS, -1, 1, 1).astype(jnp.float32)
        ffn_w = (w_br_a[l], w_br_b[l], w_br_c[l], w_out[l], norm_ffn[l],
                 w_router[l], b_router[l], w_gate_up[l], b_gate_up[l], w_down[l], b_down[l])
        qa, ka, va, qb, kb, vb, qc, gl = project_inputs(xp, norm_attn[l], w_in[l], pos_p)
        mk, mv = memory_kv(mem_prompt, norm_mem[l], w_mem_kv[l])
        oa, _ = banded_attention(qa, ka, va, SWA_WINDOW - 1, sink)
        ob = dilated_prompt(qb, kb, vb)
        oc = memory_attention(qc, mk, mv)
        xp = merge_and_ffn(xp, oa, ob, oc, gl, *ffn_w)
        swa_k_p.append(ka[:, n_prompt - la_p:])
        swa_v_p.append(va[:, n_prompt - la_p:])
        dil_k_p.append(kb[:, n_prompt - lb_p:])
        dil_v_p.append(vb[:, n_prompt - lb_p:])
        mem_k_p.append(mk)
        mem_v_p.append(mv)
        qa, ka, va, qb, kb, vb, qc, gl = project_inputs(xs, norm_attn[l], w_in[l], pos_s)
        ka_ext = jnp.concatenate([cache_swa_k[l], ka], axis=1)
        va_ext = jnp.concatenate([cache_swa_v[l], va], axis=1)
        kb_ext = jnp.concatenate([cache_dil_k[l], kb], axis=1)
        vb_ext = jnp.concatenate([cache_dil_v[l], vb], axis=1)
        idx_a = la + jnp.arange(n_new)[:, None] - jnp.arange(SWA_WINDOW)[None, :]
        oa, _ = gathered_attention(qa, ka_ext, va_ext, idx_a, sink)
        ob = dilated_sample(qb, kb_ext, vb_ext, lb)
        oc = memory_attention(qc, cache_mem_k[l], cache_mem_v[l])
        xs = merge_and_ffn(xs, oa, ob, oc, gl, *ffn_w)
        swa_k_s.append(ka_ext[:, n_new:])
        swa_v_s.append(va_ext[:, n_new:])
        dil_k_s.append(kb_ext[:, n_new:])
        dil_v_s.append(vb_ext[:, n_new:])
    y_prompt = rmsnorm(xp, norm_final)
    y_sample = rmsnorm(xs, norm_final)
    return (y_prompt, y_sample,
            jnp.stack(swa_k_p), jnp.stack(swa_v_p), jnp.stack(dil_k_p), jnp.stack(dil_v_p),
            jnp.stack(mem_k_p), jnp.stack(mem_v_p),
            jnp.stack(swa_k_s), jnp.stack(swa_v_s), jnp.stack(dil_k_s), jnp.stack(dil_v_s))
```

```python
import functools

import jax
import jax.numpy as jnp
import numpy as np
from jax import lax
from jax.experimental import pallas as pl
from jax.experimental.pallas import tpu as pltpu

D_MODEL = 1024
HEAD_DIM = 64
ROPE_DIM = 16
ROPE_HALF = 8
ROPE_THETA = 500000.0
PAST_LEN = 16384
SWA_Q_HEADS = 8
SWA_KV_HEADS = 2
SWA_WINDOW = 128
DIL_PAIRS = ((128, 1), (512, 4), (2048, 16))
DIL_KV_HEADS = 4
MEM_HEADS = 4
N_EXPERTS = 32
TOP_K = 4
D_FF = 1024
SWIGLU_LIMIT = 7.0
SWIGLU_ALPHA = 1.702
RMS_EPS = 1e-5
ATT_BLOCK = 128
SCALE = HEAD_DIM ** -0.5

LANES = 128
NEG = -1e30
VMEM_LIMIT = 56 * 1024 * 1024


def _cparams(sem):
    return pltpu.CompilerParams(dimension_semantics=sem, vmem_limit_bytes=VMEM_LIMIT)


def _rms(x, g):
    return x * lax.rsqrt(jnp.mean(x * x, axis=-1, keepdims=True) + RMS_EPS) * g


def _norm_proj_body(sections, x_ref, g_ref, w_ref, cs_ref, *out_refs):
    h = _rms(x_ref[...], g_ref[...]).astype(jnp.bfloat16)
    if cs_ref is not None:
        cos = cs_ref[0]
        sin_lo = cs_ref[1]
        sin_hi = cs_ref[2]
    for (cols, kind, slabs), o_ref in zip(sections, out_refs):
        for c, lo in enumerate(cols):
            y = jnp.dot(h, w_ref[:, lo:lo + LANES], preferred_element_type=jnp.float32)
            if kind in ("rope", "rope_q"):
                y = (y * cos + pltpu.roll(y, LANES - ROPE_HALF, axis=1) * sin_lo
                     + pltpu.roll(y, ROPE_HALF, axis=1) * sin_hi)
            if kind in ("rope_q", "q"):
                y = y * SCALE
            if kind == "sigmoid":
                y = jax.nn.sigmoid(y)
            if slabs:
                o_ref[c] = y.astype(o_ref.dtype)
            else:
                o_ref[:, c * LANES:(c + 1) * LANES] = y.astype(o_ref.dtype)


def rope_tables(pos):
    inv_freq = ROPE_THETA ** (-jnp.arange(ROPE_HALF, dtype=jnp.float32) / ROPE_HALF)
    ang = pos.astype(jnp.float32)[:, None] * inv_freq[None, :]
    cos, sin = jnp.cos(ang), jnp.sin(ang)
    n = pos.shape[0]
    one = jnp.ones((n, HEAD_DIM - ROPE_DIM), jnp.float32)
    zero = jnp.zeros((n, HEAD_DIM - ROPE_HALF), jnp.float32)
    c = jnp.concatenate([cos, cos, one], axis=1)
    s_lo = jnp.concatenate([-sin, zero], axis=1)
    s_hi = jnp.concatenate([jnp.zeros((n, ROPE_HALF), jnp.float32), sin,
                            jnp.zeros((n, HEAD_DIM - ROPE_DIM), jnp.float32)], axis=1)
    tab = jnp.stack([c, s_lo, s_hi])
    return jnp.concatenate([tab, tab], axis=2)


def norm_proj(x, g, w, tables, sections, tm):
    R, D = x.shape
    in_specs = [pl.BlockSpec((tm, D), lambda i: (i, 0)),
                pl.BlockSpec((1, D), lambda i: (0, 0)),
                pl.BlockSpec(w.shape, lambda i: (0, 0))]
    args = [x, g, w]
    if tables is not None:
        nt = tables.shape[1] // tm
        in_specs.append(pl.BlockSpec((3, tm, LANES), lambda i: (0, i % nt, 0)))
        args.append(tables)
    out_shape, out_specs, secs = [], [], []
    for (cols, kind, slabs, dtype) in sections:
        secs.append((cols, kind, slabs))
        width = LANES * len(cols)
        if slabs:
            out_shape.append(jax.ShapeDtypeStruct((width // LANES, R, LANES), dtype))
            out_specs.append(pl.BlockSpec((width // LANES, tm, LANES), lambda i: (0, i, 0)))
        else:
            out_shape.append(jax.ShapeDtypeStruct((R, width), dtype))
            out_specs.append(pl.BlockSpec((tm, width), lambda i: (i, 0)))
    if tables is None:
        body = lambda x_ref, g_ref, w_ref, *o: _norm_proj_body(secs, x_ref, g_ref, w_ref, None, *o)
    else:
        body = functools.partial(_norm_proj_body, secs)
    return pl.pallas_call(
        body, grid=(R // tm,), in_specs=in_specs, out_specs=out_specs, out_shape=out_shape,
        compiler_params=_cparams(("parallel",)), name="norm_proj")(*args)


def _chunks(start, width):
    return tuple(range(start, start + width, LANES))


def in_sections():
    f32, bf16 = jnp.float32, jnp.bfloat16
    qb0 = 1024 - 256
    qb_cols = tuple(qb0 + HEAD_DIM * (4 * g + 2 * hp) for hp in range(2) for g in range(3))
    return [
        (_chunks(0, 512), "rope_q", False, bf16),
        (_chunks(512, 128), "rope", False, f32),
        (_chunks(640, 128), "plain", False, f32),
        (qb_cols, "rope_q", True, f32),
        (_chunks(1536, 256), "rope", False, f32),
        (_chunks(1792, 256), "plain", False, f32),
        (_chunks(2048, 256), "q", False, bf16),
        (_chunks(2304, 3072), "sigmoid", False, bf16),
    ]


_NT = (((1,), (1,)), ((), ()))


def _half_masks():
    lane = lax.broadcasted_iota(jnp.int32, (1, LANES), 1)
    return lane < HEAD_DIM, lane >= HEAD_DIM


def _softmax_pv(s, v_half, sink=None):
    m = jnp.max(s, axis=-1, keepdims=True)
    if sink is not None:
        m = jnp.maximum(m, sink)
    e = jnp.exp(s - m)
    den = jnp.sum(e, axis=-1, keepdims=True)
    if sink is not None:
        den = den + jnp.exp(sink - m)
    r = jnp.dot(e.astype(jnp.bfloat16), v_half, preferred_element_type=jnp.float32)
    return r / den, m, den


def _attn_swa_mem_body(L, sink_ref, qa_ref, ka_ref, va_ref, qc_ref, mk_ref, mv_ref, o_ref):
    bf16 = jnp.bfloat16
    lo, hi = _half_masks()
    halves = (lo, hi)
    T = ATT_BLOCK
    mem_k = [mk_ref[:, j * LANES:(j + 1) * LANES].astype(bf16) for j in range(2)]
    mem_v = [[jnp.where(halves[p], mv_ref[:, j * LANES:(j + 1) * LANES], 0.0).astype(bf16)
              for p in range(2)] for j in range(2)]
    qi = lax.broadcasted_iota(jnp.int32, (T, 2 * T), 0)
    kj = lax.broadcasted_iota(jnp.int32, (T, 2 * T), 1)

    def block(blk, carry):
        r0 = pl.multiple_of(blk * T, T)
        ws = pl.multiple_of(jnp.maximum(r0 - T, 0), T)
        dist = qi - kj + (r0 - ws)
        valid = (dist >= 0) & (dist <= SWA_WINDOW - 1)
        k = ka_ref[pl.ds(ws, 2 * T), :]
        v = va_ref[pl.ds(ws, 2 * T), :]
        k_lo = jnp.where(lo, k, 0.0)
        k_hi = jnp.where(hi, k, 0.0)
        v_lo = jnp.where(lo, v, 0.0)
        v_hi = jnp.where(hi, v, 0.0)
        k_dup = [(k_lo + pltpu.roll(k_lo, HEAD_DIM, axis=1)).astype(bf16),
                 (k_hi + pltpu.roll(k_hi, HEAD_DIM, axis=1)).astype(bf16)]
        v_half = [[v_lo.astype(bf16), pltpu.roll(v_lo, HEAD_DIM, axis=1).astype(bf16)],
                  [pltpu.roll(v_hi, HEAD_DIM, axis=1).astype(bf16), v_hi.astype(bf16)]]
        for j in range(SWA_Q_HEADS // 2):
            qp = qa_ref[pl.ds(r0, T), j * LANES:(j + 1) * LANES]
            kv = j // 2
            acc = jnp.zeros((T, LANES), jnp.float32)
            for p in range(2):
                qm = jnp.where(halves[p], qp, jnp.zeros_like(qp))
                s = lax.dot_general(qm, k_dup[kv], _NT, preferred_element_type=jnp.float32)
                s = jnp.where(valid, s, NEG)
                out, _, _ = _softmax_pv(s, v_half[kv][p], sink_ref[2 * j + p])
                acc = acc + out
            o_ref[pl.ds(r0, T), j * LANES:(j + 1) * LANES] = acc.astype(o_ref.dtype)
        for j in range(MEM_HEADS // 2):
            qp = qc_ref[pl.ds(r0, T), j * LANES:(j + 1) * LANES]
            acc = jnp.zeros((T, LANES), jnp.float32)
            for p in range(2):
                qm = jnp.where(halves[p], qp, jnp.zeros_like(qp))
                s = lax.dot_general(qm, mem_k[j], _NT, preferred_element_type=jnp.float32)
                out, _, _ = _softmax_pv(s, mem_v[j][p])
                acc = acc + out
            c0 = SWA_Q_HEADS * HEAD_DIM + j * LANES
            o_ref[pl.ds(r0, T), c0:c0 + LANES] = acc.astype(o_ref.dtype)
        return carry

    lax.fori_loop(0, L // T, block, 0)


def attn_swa_mem(sinks, qa, ka, va, qc, mk, mv, B, L):
    M = mk.shape[0] // B
    wa, wc = SWA_Q_HEADS * HEAD_DIM, MEM_HEADS * HEAD_DIM
    row = lambda w: pl.BlockSpec((L, w), lambda b: (b, 0))
    return pl.pallas_call(
        functools.partial(_attn_swa_mem_body, L),
        grid=(B,),
        in_specs=[pl.BlockSpec(memory_space=pltpu.SMEM), row(wa), row(LANES), row(LANES), row(wc),
                  pl.BlockSpec((M, wc), lambda b: (b, 0)), pl.BlockSpec((M, wc), lambda b: (b, 0))],
        out_specs=row(wa + wc),
        out_shape=jax.ShapeDtypeStruct((B * L, wa + wc), jnp.bfloat16),
        compiler_params=_cparams(("parallel",)), name="attn_swa_mem")(sinks, qa, ka, va, qc, mk, mv)


def _attn_dil_body(L, qb_ref, kb_ref, vb_ref, o_ref, og_ref, lse_ref):
    bf16 = jnp.bfloat16
    lo, hi = _half_masks()
    halves = (lo, hi)
    T = ATT_BLOCK
    for g, (window, dil) in enumerate(DIL_PAIRS):
        lc = L // dil
        nbc = lc // T
        W = min(2 * T, lc)
        max_dist = window // dil
        qi = lax.broadcasted_iota(jnp.int32, (T, W), 0)
        kj = lax.broadcasted_iota(jnp.int32, (T, W), 1)

        def unit(u, carry, g=g, dil=dil, nbc=nbc, W=W, max_dist=max_dist, qi=qi, kj=kj):
            c = u >> (nbc.bit_length() - 1)
            n = u & (nbc - 1)
            wsc = jnp.maximum(n * T - T, 0) if W == 2 * T else 0
            q0 = c + dil * T * n
            k0 = c + dil * wsc
            dist = qi - kj + (n * T - wsc)
            valid = (dist >= 0) & (dist <= max_dist)
            q = qb_ref[g, pl.ds(q0, T, stride=dil), :]
            k = kb_ref[pl.ds(k0, W, stride=dil), :]
            v = vb_ref[pl.ds(k0, W, stride=dil), :]
            kb = k.astype(bf16)
            acc = jnp.zeros((T, LANES), jnp.float32)
            lse = jnp.zeros((T, LANES), jnp.float32)
            for p in range(2):
                qm = jnp.where(halves[p], q, 0.0).astype(bf16)
                s = lax.dot_general(qm, kb, _NT, preferred_element_type=jnp.float32)
                s = jnp.where(valid, s, NEG)
                out, m, den = _softmax_pv(s, jnp.where(halves[p], v, 0.0).astype(bf16))
                acc = acc + out
                lse = jnp.where(halves[p], m + jnp.log(den), lse)
            og_ref[g, pl.ds(q0, T, stride=dil), :] = acc
            lse_ref[g, pl.ds(q0, T, stride=dil), :] = lse
            return carry

        lax.fori_loop(0, dil * nbc, unit, 0)

    def merge(i, carry):
        r0 = pl.multiple_of(i * T, T)
        ls = [lse_ref[g, pl.ds(r0, T), :] for g in range(len(DIL_PAIRS))]
        m = jnp.maximum(jnp.maximum(ls[0], ls[1]), ls[2])
        ws = [jnp.exp(l - m) for l in ls]
        tot = ws[0] + ws[1] + ws[2]
        out = sum((w / tot) * og_ref[g, pl.ds(r0, T), :] for g, w in enumerate(ws))
        o_ref[pl.ds(r0, T), :] = out.astype(o_ref.dtype)
        return carry

    lax.fori_loop(0, L // T, merge, 0)


def attn_dilated(qb, kb, vb, B, L):
    ng = len(DIL_PAIRS)
    return pl.pallas_call(
        functools.partial(_attn_dil_body, L),
        grid=(B, 2),
        in_specs=[pl.BlockSpec((ng, L, LANES), lambda b, hp: (hp, b, 0)),
                  pl.BlockSpec((L, LANES), lambda b, hp: (b, hp)),
                  pl.BlockSpec((L, LANES), lambda b, hp: (b, hp))],
        out_specs=pl.BlockSpec((L, LANES), lambda b, hp: (b, hp)),
        out_shape=jax.ShapeDtypeStruct((B * L, 2 * LANES), jnp.bfloat16),
        scratch_shapes=[pltpu.VMEM((ng, L, LANES), jnp.float32), pltpu.VMEM((ng, L, LANES), jnp.float32)],
        compiler_params=_cparams(("parallel", "parallel")), name="attn_dilated")(qb, kb, vb)


N_NEW = 4
SROWS = 8


def _attn_sample_body(bt, sink_ref, qa_ref, ka_ref, va_ref, qb_ref, kb_ref, vb_ref, qc_ref,
                      cak_ref, cav_ref, cbk_ref, cbv_ref, cmk_ref, cmv_ref,
                      oac_ref, ob_ref, oak_ref, oav_ref, obk_ref, obv_ref,
                      ka_ext, va_ext, kb_ext, vb_ext):
    f32, bf16 = jnp.float32, jnp.bfloat16
    lo, hi = _half_masks()
    halves = (lo, hi)
    S = SROWS
    la = cak_ref.shape[1]
    lb = cbk_ref.shape[1]
    wa = SWA_Q_HEADS * HEAD_DIM

    ra = lax.broadcasted_iota(jnp.int32, (SWA_Q_HEADS * S, la + S), 0)
    ca = lax.broadcasted_iota(jnp.int32, (SWA_Q_HEADS * S, la + S), 1)
    ia = ra & (S - 1)
    valid_a = (ca >= ia + 1) & (ca <= la + ia)
    rcol = lax.broadcasted_iota(jnp.int32, (SWA_Q_HEADS * S, 1), 0)
    sink_col = jnp.zeros((SWA_Q_HEADS * S, 1), f32)
    for h in range(SWA_Q_HEADS):
        sink_col = jnp.where((rcol >> 3) == h, sink_ref[h], sink_col)

    nrow_b = len(DIL_PAIRS) * 2 * S
    rb = lax.broadcasted_iota(jnp.int32, (nrow_b, lb + S), 0)
    cb = lax.broadcasted_iota(jnp.int32, (nrow_b, lb + S), 1)
    t = lb + (rb & (S - 1)) - cb
    gb = rb >> 4
    valid_b = jnp.zeros((nrow_b, lb + S), jnp.bool_)
    for g, (window, dil) in enumerate(DIL_PAIRS):
        valid_b = valid_b | ((gb == g) & (t >= 0) & (t <= window) & ((t & (dil - 1)) == 0))

    for b in range(bt):
        rows = slice(b * S, (b + 1) * S)
        ka_ext[0:la] = cak_ref[b]
        va_ext[0:la] = cav_ref[b]
        ka_ext[la:la + S] = ka_ref[rows, :]
        va_ext[la:la + S] = va_ref[rows, :]
        kb_ext[0:lb] = cbk_ref[b]
        vb_ext[0:lb] = cbv_ref[b]
        kb_ext[lb:lb + S] = kb_ref[rows, :]
        vb_ext[lb:lb + S] = vb_ref[rows, :]
        oak_ref[b] = ka_ext[N_NEW:N_NEW + la]
        oav_ref[b] = va_ext[N_NEW:N_NEW + la]
        obk_ref[b] = kb_ext[N_NEW:N_NEW + lb]
        obv_ref[b] = vb_ext[N_NEW:N_NEW + lb]

        pieces = []
        for h in range(SWA_Q_HEADS):
            q = jnp.where(halves[h % 2], qa_ref[rows, (h // 2) * LANES:(h // 2 + 1) * LANES].astype(f32), 0.0)
            if h % 2 != h // 4:
                q = pltpu.roll(q, HEAD_DIM, axis=1)
            pieces.append(q)
        qm = jnp.concatenate(pieces, axis=0).astype(bf16)
        s = lax.dot_general(qm, ka_ext[...].astype(bf16), _NT, preferred_element_type=f32)
        s = jnp.where(valid_a, s, NEG)
        out, _, _ = _softmax_pv(s, va_ext[...].astype(bf16), sink_col)
        for j in range(SWA_Q_HEADS // 2):
            parts = []
            for p in range(2):
                h = 2 * j + p
                o = out[h * S:(h + 1) * S]
                if h % 2 != h // 4:
                    o = pltpu.roll(o, HEAD_DIM, axis=1)
                parts.append(o)
            oac_ref[rows, j * LANES:(j + 1) * LANES] = jnp.where(lo, parts[0], parts[1])

        for j in range(MEM_HEADS // 2):
            cols = slice(j * LANES, (j + 1) * LANES)
            q = qc_ref[rows, cols]
            qm = jnp.concatenate([jnp.where(halves[p], q, jnp.zeros_like(q)) for p in range(2)], axis=0)
            s = lax.dot_general(qm, cmk_ref[b, :, cols].astype(bf16), _NT, preferred_element_type=f32)
            out, _, _ = _softmax_pv(s, cmv_ref[b, :, cols].astype(bf16))
            oac_ref[rows, wa + j * LANES:wa + (j + 1) * LANES] = jnp.where(lo, out[:S], out[S:])

        for hp in range(DIL_KV_HEADS // 2):
            cols = slice(hp * LANES, (hp + 1) * LANES)
            pieces = [jnp.where(halves[p], qb_ref[hp * len(DIL_PAIRS) + g, rows, :], 0.0)
                      for g in range(len(DIL_PAIRS)) for p in range(2)]
            qm = jnp.concatenate(pieces, axis=0).astype(bf16)
            s = lax.dot_general(qm, kb_ext[:, cols].astype(bf16), _NT, preferred_element_type=f32)
            s = jnp.where(valid_b, s, NEG)
            out, m, den = _softmax_pv(s, vb_ext[:, cols].astype(bf16))
            lse = m + jnp.log(den)
            res = []
            for p in range(2):
                r = [slice((g * 2 + p) * S, (g * 2 + p + 1) * S) for g in range(len(DIL_PAIRS))]
                mx = jnp.maximum(jnp.maximum(lse[r[0]], lse[r[1]]), lse[r[2]])
                w = [jnp.exp(lse[x] - mx) for x in r]
                tot = w[0] + w[1] + w[2]
                res.append(sum((w[g] / tot) * out[r[g]] for g in range(len(DIL_PAIRS))))
            ob_ref[rows, cols] = jnp.where(lo, res[0], res[1])


def attn_sample(sinks, qa, ka, va, qb, kb, vb, qc, cak, cav, cbk, cbv, cmk, cmv, bt):
    NB, la, wka = cak.shape
    lb, wkb = cbk.shape[1:]
    M, wm = cmk.shape[1:]
    wa, wc = SWA_Q_HEADS * HEAD_DIM, MEM_HEADS * HEAD_DIM
    ng = len(DIL_PAIRS)
    tok = lambda w: pl.BlockSpec((bt * SROWS, w), lambda i: (i, 0))
    cache = lambda n, w: pl.BlockSpec((bt, n, w), lambda i: (i, 0, 0))
    f32 = jnp.float32
    return pl.pallas_call(
        functools.partial(_attn_sample_body, bt),
        grid=(NB // bt,),
        in_specs=[pl.BlockSpec(memory_space=pltpu.SMEM), tok(wa), tok(wka), tok(wka),
                  pl.BlockSpec((2 * ng, bt * SROWS, LANES), lambda i: (0, i, 0)), tok(wkb), tok(wkb), tok(wc),
                  cache(la, wka), cache(la, wka), cache(lb, wkb), cache(lb, wkb), cache(M, wm), cache(M, wm)],
        out_specs=[tok(wa + wc), tok(wkb), cache(la, wka), cache(la, wka), cache(lb, wkb), cache(lb, wkb)],
        out_shape=[jax.ShapeDtypeStruct((NB * SROWS, wa + wc), f32), jax.ShapeDtypeStruct((NB * SROWS, wkb), f32),
                   jax.ShapeDtypeStruct(cak.shape, f32), jax.ShapeDtypeStruct(cak.shape, f32),
                   jax.ShapeDtypeStruct(cbk.shape, f32), jax.ShapeDtypeStruct(cbk.shape, f32)],
        scratch_shapes=[pltpu.VMEM((la + SROWS, wka), f32), pltpu.VMEM((la + SROWS, wka), f32),
                        pltpu.VMEM((lb + SROWS, wkb), f32), pltpu.VMEM((lb + SROWS, wkb), f32)],
        compiler_params=_cparams(("parallel",)), name="attn_sample")(
            sinks, qa, ka, va, qb, kb, vb, qc, cak, cav, cbk, cbv, cmk, cmv)


MOE_BLOCK = 256
_HI16 = 0xFFFF0000


def _pack_bf16_pairs(y):
    w = y.shape[1] // 2
    bits = lax.bitcast_convert_type(y.astype(jnp.bfloat16).astype(jnp.float32), jnp.uint32)
    return (bits[:, :w] >> 16) | (bits[:, w:] & jnp.uint32(_HI16))


def _unpack_bf16_pairs(u):
    lo = lax.bitcast_convert_type(u << 16, jnp.float32)
    hi = lax.bitcast_convert_type(u & jnp.uint32(_HI16), jnp.float32)
    return lo, hi


def _merge_route_body(oac_ref, ob_ref, gate_ref, x_ref, wa_ref, wb_ref, wc_ref, wo_ref, gffn_ref, wr_ref, br_ref,
                      cnt0_ref, x1_ref, h2_ref, mi_ref, mf_ref, cnt_ref, base_ref):
    f32, bf16 = jnp.float32, jnp.bfloat16
    tm = x_ref.shape[0]
    D = x_ref.shape[1]
    wa = SWA_Q_HEADS * HEAD_DIM

    @pl.when(pl.program_id(0) == 0)
    def _():
        base_ref[...] = cnt0_ref[...]

    ma = jnp.dot(oac_ref[:, :wa].astype(bf16), wa_ref[...], preferred_element_type=f32)
    mb = jnp.dot(ob_ref[...].astype(bf16), wb_ref[...], preferred_element_type=f32)
    mc = jnp.dot(oac_ref[:, wa:].astype(bf16), wc_ref[...], preferred_element_type=f32)
    merged = (gate_ref[:, :D].astype(f32) * ma + gate_ref[:, D:2 * D].astype(f32) * mb
              + gate_ref[:, 2 * D:].astype(f32) * mc)
    x1 = x_ref[...] + jnp.dot(merged.astype(bf16), wo_ref[...], preferred_element_type=f32)
    x1_ref[...] = x1
    h2 = _rms(x1, gffn_ref[...])
    h2_ref[...] = _pack_bf16_pairs(h2)

    logits = jnp.dot(h2.astype(bf16), wr_ref[...], preferred_element_type=f32) + br_ref[...]
    lane = lax.broadcasted_iota(jnp.int32, (tm, LANES), 1)
    work = logits
    vals, idxs = [], []
    for _ in range(TOP_K):
        m = jnp.max(work, axis=-1, keepdims=True)
        idx = jnp.min(jnp.where(work == m, lane, LANES), axis=-1, keepdims=True)
        vals.append(m)
        idxs.append(idx)
        work = jnp.where(lane == idx, -jnp.inf, work)
    es = [jnp.exp(v - vals[0]) for v in vals]
    tot = es[0] + es[1] + es[2] + es[3]

    onehot = [(lane == idx).astype(f32) for idx in idxs]
    assign = onehot[0] + onehot[1] + onehot[2] + onehot[3]
    ti = lax.broadcasted_iota(jnp.int32, (tm, tm), 0)
    tj = lax.broadcasted_iota(jnp.int32, (tm, tm), 1)
    before = jnp.dot((tj < ti).astype(bf16), assign.astype(bf16), preferred_element_type=f32) + base_ref[...]
    base_ref[...] = base_ref[...] + jnp.sum(assign, axis=0, keepdims=True)
    cnt_ref[...] = base_ref[...]

    mi = jnp.zeros((tm, LANES), jnp.int32)
    mf = jnp.zeros((tm, LANES), f32)
    for k in range(TOP_K):
        rank = jnp.sum(onehot[k] * before, axis=-1, keepdims=True).astype(jnp.int32)
        mi = jnp.where(lane == k, idxs[k], mi)
        mi = jnp.where(lane == TOP_K + k, rank, mi)
        mf = jnp.where(lane == k, es[k] / tot, mf)
    mi_ref[...] = mi
    mf_ref[...] = mf


def merge_route(o_ac, o_b, gates, x, wa, wb, wc, wo, g_ffn, wr, br, cnt0, tm):
    R, D = x.shape
    row = lambda w: pl.BlockSpec((tm, w), lambda i: (i, 0))
    full = lambda a: pl.BlockSpec(a.shape, lambda i: (0, 0))
    return pl.pallas_call(
        _merge_route_body,
        grid=(R // tm,),
        in_specs=[row(o_ac.shape[1]), row(o_b.shape[1]), row(gates.shape[1]), row(D),
                  full(wa), full(wb), full(wc), full(wo), full(g_ffn), full(wr), full(br), full(cnt0)],
        out_specs=[row(D), row(D // 2), row(LANES), row(LANES), pl.BlockSpec((1, LANES), lambda i: (0, 0))],
        out_shape=[jax.ShapeDtypeStruct((R, D), jnp.float32), jax.ShapeDtypeStruct((R, D // 2), jnp.uint32),
                   jax.ShapeDtypeStruct((R, LANES), jnp.int32), jax.ShapeDtypeStruct((R, LANES), jnp.float32),
                   jax.ShapeDtypeStruct((1, LANES), jnp.float32)],
        scratch_shapes=[pltpu.VMEM((1, LANES), jnp.float32)],
        compiler_params=_cparams(("arbitrary",)), name="merge_route")(
            o_ac, o_b, gates, x, wa, wb, wc, wo, g_ffn, wr, br, cnt0)


def _route_tables_body(nbp, cnt_ref, mi_ref, dest_ref, blk_ref):
    tm = mi_ref.shape[0]
    lane1 = lax.broadcasted_iota(jnp.int32, (1, LANES), 1)
    shift = MOE_BLOCK.bit_length() - 1
    cnt = cnt_ref[...].astype(jnp.int32)
    padded = ((cnt + (MOE_BLOCK - 1)) >> shift) << shift
    pend = padded
    s = 1
    while s < N_EXPERTS:
        pend = pend + jnp.where(lane1 >= s, pltpu.roll(pend, s, axis=1), 0)
        s *= 2
    pstart = pend - padded
    mi = mi_ref[...]
    lane = lax.broadcasted_iota(jnp.int32, (tm, LANES), 1)
    dest = jnp.zeros((tm, LANES), jnp.int32)
    for k in range(TOP_K):
        idx = jnp.sum(jnp.where(lane == k, mi, 0), axis=-1, keepdims=True)
        rank = jnp.sum(jnp.where(lane == TOP_K + k, mi, 0), axis=-1, keepdims=True)
        start = jnp.sum(jnp.where(lane == idx, pstart, 0), axis=-1, keepdims=True)
        dest = jnp.where(lane == k, start + rank, dest)
    dest_ref[...] = dest
    row0 = lax.broadcasted_iota(jnp.int32, (nbp, LANES), 0) * MOE_BLOCK
    ended = jnp.where((pend <= row0) & (lax.broadcasted_iota(jnp.int32, (nbp, LANES), 1) < N_EXPERTS), 1, 0)
    blk = jnp.minimum(jnp.sum(ended, axis=-1, keepdims=True), N_EXPERTS - 1)
    blk_ref[...] = jnp.broadcast_to(blk, (nbp, LANES))


def route_tables(cnt, mi, nbp, tm):
    R = mi.shape[0]
    return pl.pallas_call(
        functools.partial(_route_tables_body, nbp),
        grid=(R // tm,),
        in_specs=[pl.BlockSpec((1, LANES), lambda i: (0, 0)), pl.BlockSpec((tm, LANES), lambda i: (i, 0))],
        out_specs=[pl.BlockSpec((tm, LANES), lambda i: (i, 0)), pl.BlockSpec((nbp, LANES), lambda i: (0, 0))],
        out_shape=[jax.ShapeDtypeStruct((R, LANES), jnp.int32), jax.ShapeDtypeStruct((nbp, LANES), jnp.int32)],
        compiler_params=_cparams(("arbitrary",)), name="route_tables")(cnt, mi)


def _dispatch_body(dest_ref, h_ref, xs_in_ref, xs_ref, sem):
    del xs_in_ref
    tm = h_ref.shape[0]

    def row_copy(i, k):
        return pltpu.make_async_copy(h_ref.at[pl.ds(i, 1)], xs_ref.at[pl.ds(dest_ref[0, 0, TOP_K * i + k], 1)], sem)

    def start(i, c):
        for k in range(TOP_K):
            row_copy(i, k).start()
        return c

    def wait(i, c):
        for k in range(TOP_K):
            row_copy(i, k).wait()
        return c

    lax.fori_loop(0, tm, start, 0, unroll=8)
    lax.fori_loop(0, tm, wait, 0, unroll=8)


def dispatch(dest3, h2, xs_zero, tm):
    R, W = h2.shape
    return pl.pallas_call(
        _dispatch_body,
        grid=(R // tm,),
        in_specs=[pl.BlockSpec((1, 1, TOP_K * tm), lambda i: (i, 0, 0), memory_space=pltpu.SMEM),
                  pl.BlockSpec((tm, W), lambda i: (i, 0)),
                  pl.BlockSpec(memory_space=pl.ANY)],
        out_specs=pl.BlockSpec(memory_space=pl.ANY),
        out_shape=jax.ShapeDtypeStruct(xs_zero.shape, xs_zero.dtype),
        scratch_shapes=[pltpu.SemaphoreType.DMA],
        input_output_aliases={2: 0},
        compiler_params=_cparams(("arbitrary",)), name="dispatch")(dest3, h2, xs_zero)


def _moe_body(be_ref, x_ref, wgu_ref, bgu_ref, wd_ref, bd_ref, y_ref, wgu_s, wd_s):
    f32, bf16 = jnp.float32, jnp.bfloat16
    b = pl.program_id(0)
    half = wgu_s.shape[0] // 2

    @pl.when((b == 0) | (be_ref[b] != be_ref[jnp.maximum(b - 1, 0)]))
    def _():
        wgu_s[...] = wgu_ref[0].astype(bf16)
        wd_s[...] = wd_ref[0].astype(bf16)

    lo, hi = _unpack_bf16_pairs(x_ref[...])
    gu = (jnp.dot(lo.astype(bf16), wgu_s[:half], preferred_element_type=f32)
          + jnp.dot(hi.astype(bf16), wgu_s[half:], preferred_element_type=f32) + bgu_ref[0])
    gt = jnp.minimum(gu[:, :D_FF], SWIGLU_LIMIT)
    up = jnp.clip(gu[:, D_FF:], -SWIGLU_LIMIT, SWIGLU_LIMIT)
    act = (up + 1.0) * (gt * jax.nn.sigmoid(gt * SWIGLU_ALPHA))
    y = jnp.dot(act.astype(bf16), wd_s[...], preferred_element_type=f32) + bd_ref[0]
    y_ref[...] = _pack_bf16_pairs(y)


def moe_ffn(blk_e, xs, w_gate_up, b_gate_up, w_down, b_down):
    RS, W = xs.shape
    E, D, F2 = w_gate_up.shape
    grid_spec = pltpu.PrefetchScalarGridSpec(
        num_scalar_prefetch=1,
        grid=(RS // MOE_BLOCK,),
        in_specs=[pl.BlockSpec((MOE_BLOCK, W), lambda b, be: (b, 0)),
                  pl.BlockSpec((1, D, F2), lambda b, be: (be[b], 0, 0)),
                  pl.BlockSpec((1, 1, F2), lambda b, be: (be[b], 0, 0)),
                  pl.BlockSpec((1, F2 // 2, D), lambda b, be: (be[b], 0, 0)),
                  pl.BlockSpec((1, 1, D), lambda b, be: (be[b], 0, 0))],
        out_specs=pl.BlockSpec((MOE_BLOCK, W), lambda b, be: (b, 0)),
        scratch_shapes=[pltpu.VMEM((D, F2), jnp.bfloat16), pltpu.VMEM((F2 // 2, D), jnp.bfloat16)])
    return pl.pallas_call(
        _moe_body, grid_spec=grid_spec,
        out_shape=jax.ShapeDtypeStruct((RS, W), jnp.uint32),
        compiler_params=_cparams(("arbitrary",)), name="moe_ffn")(
            blk_e, xs, w_gate_up, b_gate_up.reshape(E, 1, F2), w_down, b_down.reshape(E, 1, D))


def _combine_body(dest_ref, x1_ref, mf_ref, g_ref, ys_ref, o_ref, buf, sem):
    tm = x1_ref.shape[0]
    half = x1_ref.shape[1] // 2

    def row_copy(i, k):
        return pltpu.make_async_copy(ys_ref.at[pl.ds(dest_ref[0, 0, TOP_K * i + k], 1)], buf.at[k, pl.ds(i, 1)], sem)

    def start(i, c):
        for k in range(TOP_K):
            row_copy(i, k).start()
        return c

    def wait(i, c):
        for k in range(TOP_K):
            row_copy(i, k).wait()
        return c

    lax.fori_loop(0, tm, start, 0, unroll=8)
    lax.fori_loop(0, tm, wait, 0, unroll=8)
    acc_lo = x1_ref[:, :half]
    acc_hi = x1_ref[:, half:]
    for k in range(TOP_K):
        lo, hi = _unpack_bf16_pairs(buf[k])
        gate = mf_ref[:, k:k + 1]
        acc_lo = acc_lo + gate * lo
        acc_hi = acc_hi + gate * hi
    y = jnp.concatenate([acc_lo, acc_hi], axis=1)
    o_ref[...] = _rms(y, g_ref[...])


def combine(dest3, x1, mf, g_final, ys, tm):
    R, D = x1.shape
    return pl.pallas_call(
        _combine_body,
        grid=(R // tm,),
        in_specs=[pl.BlockSpec((1, 1, TOP_K * tm), lambda i: (i, 0, 0), memory_space=pltpu.SMEM),
                  pl.BlockSpec((tm, D), lambda i: (i, 0)),
                  pl.BlockSpec((tm, LANES), lambda i: (i, 0)),
                  pl.BlockSpec((1, D), lambda i: (0, 0)),
                  pl.BlockSpec(memory_space=pl.ANY)],
        out_specs=pl.BlockSpec((tm, D), lambda i: (i, 0)),
        out_shape=jax.ShapeDtypeStruct((R, D), jnp.float32),
        scratch_shapes=[pltpu.VMEM((TOP_K, tm, D // 2), jnp.uint32), pltpu.SemaphoreType.DMA],
        compiler_params=_cparams(("arbitrary",)), name="combine")(dest3, x1, mf, g_final, ys)


def moe_layer(groups, cnt, g_final, w_gate_up, b_gate_up, w_down, b_down, tm):
    D = groups[0][0].shape[1]
    n_assign = sum(g[0].shape[0] for g in groups) * TOP_K
    nb = (n_assign + N_EXPERTS * (MOE_BLOCK - 1)) // MOE_BLOCK + 1
    nbp = -(-nb // 8) * 8
    xs = jnp.zeros((nb * MOE_BLOCK, D // 2), jnp.uint32)
    dests = []
    for x1, h2, mi, mf in groups:
        R = x1.shape[0]
        dest, blk = route_tables(cnt, mi, nbp, tm)
        dest3 = dest[:, :TOP_K].reshape(R // tm, 1, TOP_K * tm)
        xs = dispatch(dest3, h2, xs, tm)
        dests.append(dest3)
    ys = moe_ffn(blk[:nb, 0], xs, w_gate_up, b_gate_up, w_down, b_down)
    return [combine(dest3, x1, mf, g_final, ys, tm) for dest3, (x1, h2, mi, mf) in zip(dests, groups)]


def kernel(x_prompt, x_sample, cache_swa_k, cache_swa_v, cache_dil_k, cache_dil_v, cache_mem_k, cache_mem_v, mem_prompt, norm_attn, norm_mem, w_in, w_mem_kv, sinks, w_br_a, w_br_b, w_br_c, w_out, norm_ffn, w_router, b_router, w_gate_up, b_gate_up, w_down, b_down, norm_final):
    f32, bf16 = jnp.float32, jnp.bfloat16
    TM = 256
    B, L, D = x_prompt.shape
    NB, n_new, _ = x_sample.shape
    M = mem_prompt.shape[1]
    la, lb = cache_swa_k.shape[2], cache_dil_k.shape[2]
    wka, wkb, wc = SWA_KV_HEADS * HEAD_DIM, DIL_KV_HEADS * HEAD_DIM, MEM_HEADS * HEAD_DIM
    assert n_new == N_NEW and cache_swa_k.shape[0] == 1

    w_in_b = w_in[0].astype(bf16)
    g_attn = norm_attn[0].reshape(1, D)
    secs = in_sections()

    tabs_p = rope_tables(jnp.arange(L, dtype=jnp.int32))
    qa, ka, va, qb, kb, vb, qc, gates = norm_proj(x_prompt.reshape(B * L, D), g_attn, w_in_b, tabs_p, secs, TM)
    mk, mv = norm_proj(mem_prompt.reshape(B * M, D), norm_mem[0].reshape(1, D), w_mem_kv[0].astype(bf16), None,
                       [(_chunks(0, wc), "plain", False, f32), (_chunks(wc, wc), "plain", False, f32)], TM)
    o_ac = attn_swa_mem(sinks[0], qa, ka, va, qc, mk, mv, B, L)
    o_b = attn_dilated(qb, kb, vb, B, L)

    xs_pad = jnp.pad(x_sample, ((0, 0), (0, SROWS - N_NEW), (0, 0))).reshape(NB * SROWS, D)
    tabs_s = rope_tables(PAST_LEN + (jnp.arange(TM, dtype=jnp.int32) % SROWS))
    qa_s, ka_s, va_s, qb_s, kb_s, vb_s, qc_s, gates_s = norm_proj(xs_pad, g_attn, w_in_b, tabs_s, secs, TM)
    o_ac_s, o_b_s, swa_k_s, swa_v_s, dil_k_s, dil_v_s = attn_sample(
        sinks[0], qa_s, ka_s, va_s, qb_s, kb_s, vb_s, qc_s,
        cache_swa_k[0].reshape(NB, la, wka), cache_swa_v[0].reshape(NB, la, wka),
        cache_dil_k[0].reshape(NB, lb, wkb), cache_dil_v[0].reshape(NB, lb, wkb),
        cache_mem_k[0].reshape(NB, M, wc), cache_mem_v[0].reshape(NB, M, wc), 2)
    real = lambda t: t.reshape(NB, SROWS, -1)[:, :N_NEW].reshape(NB * N_NEW, -1)

    wr = jnp.zeros((D, LANES), f32).at[:, :N_EXPERTS].set(w_router[0]).astype(bf16)
    br = jnp.full((1, LANES), NEG, f32).at[0, :N_EXPERTS].set(b_router[0].astype(f32))
    wts = (w_br_a[0].astype(bf16), w_br_b[0].astype(bf16), w_br_c[0].astype(bf16), w_out[0].astype(bf16),
           norm_ffn[0].reshape(1, D), wr, br)
    x1_p, h2_p, mi_p, mf_p, cnt_p = merge_route(o_ac, o_b, gates, x_prompt.reshape(B * L, D), *wts,
                                                jnp.zeros((1, LANES), f32), TM)
    x1_s, h2_s, mi_s, mf_s, cnt = merge_route(real(o_ac_s), real(o_b_s), real(gates_s),
                                              x_sample.reshape(NB * N_NEW, D), *wts, cnt_p, TM)
    y_p, y_s = moe_layer([(x1_p, h2_p, mi_p, mf_p), (x1_s, h2_s, mi_s, mf_s)], cnt, norm_final.reshape(1, D),
                         w_gate_up[0], b_gate_up[0], w_down[0], b_down[0], TM)

    heads = lambda t, n, h: t.reshape(1, t.shape[0] // n, n, h, HEAD_DIM)
    ka3, va3 = ka.reshape(B, L, wka), va.reshape(B, L, wka)
    la_p = min(SWA_WINDOW, L)
    return (y_p.reshape(B, L, D), y_s.reshape(NB, N_NEW, D),
            heads(ka3[:, L - la_p:].reshape(B * la_p, wka), la_p, SWA_KV_HEADS),
            heads(va3[:, L - la_p:].reshape(B * la_p, wka), la_p, SWA_KV_HEADS),
            heads(kb, L, DIL_KV_HEADS), heads(vb, L, DIL_KV_HEADS),
            heads(mk, M, MEM_HEADS), heads(mv, M, MEM_HEADS),
            heads(swa_k_s.reshape(NB * la, wka), la, SWA_KV_HEADS), heads(swa_v_s.reshape(NB * la, wka), la, SWA_KV_HEADS),
            heads(dil_k_s.reshape(NB * lb, wkb), lb, DIL_KV_HEADS), heads(dil_v_s.reshape(NB * lb, wkb), lb, DIL_KV_HEADS))
```

```python
import functools

import jax
import jax.numpy as jnp
import numpy as np
from jax import lax
from jax.experimental import pallas as pl
from jax.experimental.pallas import tpu as pltpu

D_MODEL = 1024
HEAD_DIM = 64
ROPE_DIM = 16
ROPE_HALF = 8
ROPE_THETA = 500000.0
PAST_LEN = 16384
SWA_Q_HEADS = 8
SWA_KV_HEADS = 2
SWA_WINDOW = 128
DIL_PAIRS = ((128, 1), (512, 4), (2048, 16))
DIL_KV_HEADS = 4
MEM_HEADS = 4
N_EXPERTS = 32
TOP_K = 4
D_FF = 1024
SWIGLU_LIMIT = 7.0
SWIGLU_ALPHA = 1.702
RMS_EPS = 1e-5
ATT_BLOCK = 128
SCALE = HEAD_DIM ** -0.5

LANES = 128
NEG = -1e30
VMEM_LIMIT = 56 * 1024 * 1024


def _cparams(sem):
    return pltpu.CompilerParams(dimension_semantics=sem, vmem_limit_bytes=VMEM_LIMIT)


def _rms(x, g):
    return x * lax.rsqrt(jnp.mean(x * x, axis=-1, keepdims=True) + RMS_EPS) * g


def _norm_proj_body(sections, x_ref, g_ref, w_ref, cs_ref, *out_refs):
    h = _rms(x_ref[...], g_ref[...]).astype(jnp.bfloat16)
    if cs_ref is not None:
        cos = cs_ref[0]
        sin_lo = cs_ref[1]
        sin_hi = cs_ref[2]
    dest = {lo: (o_ref, c, kind, slabs)
            for (cols, kind, slabs), o_ref in zip(sections, out_refs) for c, lo in enumerate(cols)}
    todo = sorted(dest)
    while todo:
        lo = todo.pop(0)
        n = 2 if todo and todo[0] == lo + LANES else 1
        if n == 2:
            todo.pop(0)
        yy = jnp.dot(h, w_ref[:, lo:lo + n * LANES], preferred_element_type=jnp.float32)
        for part in range(n):
            o_ref, c, kind, slabs = dest[lo + part * LANES]
            y = yy[:, part * LANES:(part + 1) * LANES]
            if kind in ("rope", "rope_q"):
                y = (y * cos + pltpu.roll(y, LANES - ROPE_HALF, axis=1) * sin_lo
                     + pltpu.roll(y, ROPE_HALF, axis=1) * sin_hi)
            if kind in ("rope_q", "q"):
                y = y * SCALE
            if kind == "sigmoid":
                y = jax.nn.sigmoid(y)
            if slabs:
                o_ref[c] = y.astype(o_ref.dtype)
            else:
                o_ref[:, c * LANES:(c + 1) * LANES] = y.astype(o_ref.dtype)


def rope_tables(pos):
    inv_freq = ROPE_THETA ** (-jnp.arange(ROPE_HALF, dtype=jnp.float32) / ROPE_HALF)
    ang = pos.astype(jnp.float32)[:, None] * inv_freq[None, :]
    cos, sin = jnp.cos(ang), jnp.sin(ang)
    n = pos.shape[0]
    one = jnp.ones((n, HEAD_DIM - ROPE_DIM), jnp.float32)
    zero = jnp.zeros((n, HEAD_DIM - ROPE_HALF), jnp.float32)
    c = jnp.concatenate([cos, cos, one], axis=1)
    s_lo = jnp.concatenate([-sin, zero], axis=1)
    s_hi = jnp.concatenate([jnp.zeros((n, ROPE_HALF), jnp.float32), sin,
                            jnp.zeros((n, HEAD_DIM - ROPE_DIM), jnp.float32)], axis=1)
    tab = jnp.stack([c, s_lo, s_hi])
    return jnp.concatenate([tab, tab], axis=2)


def norm_proj(x, g, w, tables, sections, tm):
    R, D = x.shape
    in_specs = [pl.BlockSpec((tm, D), lambda i: (i, 0)),
                pl.BlockSpec((1, D), lambda i: (0, 0)),
                pl.BlockSpec(w.shape, lambda i: (0, 0))]
    args = [x, g, w]
    if tables is not None:
        nt = tables.shape[1] // tm
        in_specs.append(pl.BlockSpec((3, tm, LANES), lambda i: (0, i % nt, 0)))
        args.append(tables)
    out_shape, out_specs, secs = [], [], []
    for (cols, kind, slabs, dtype) in sections:
        secs.append((cols, kind, slabs))
        width = LANES * len(cols)
        if slabs:
            out_shape.append(jax.ShapeDtypeStruct((width // LANES, R, LANES), dtype))
            out_specs.append(pl.BlockSpec((width // LANES, tm, LANES), lambda i: (0, i, 0)))
        else:
            out_shape.append(jax.ShapeDtypeStruct((R, width), dtype))
            out_specs.append(pl.BlockSpec((tm, width), lambda i: (i, 0)))
    if tables is None:
        body = lambda x_ref, g_ref, w_ref, *o: _norm_proj_body(secs, x_ref, g_ref, w_ref, None, *o)
    else:
        body = functools.partial(_norm_proj_body, secs)
    return pl.pallas_call(
        body, grid=(R // tm,), in_specs=in_specs, out_specs=out_specs, out_shape=out_shape,
        compiler_params=_cparams(("parallel",)), name="norm_proj")(*args)


def _chunks(start, width):
    return tuple(range(start, start + width, LANES))


def in_sections():
    f32, bf16 = jnp.float32, jnp.bfloat16
    qb0 = 1024 - 256
    qb_cols = tuple(qb0 + HEAD_DIM * (4 * g + 2 * hp) for hp in range(2) for g in range(3))
    return [
        (_chunks(0, 512), "rope_q", False, bf16),
        (_chunks(512, 128), "rope", False, f32),
        (_chunks(640, 128), "plain", False, f32),
        (qb_cols, "rope_q", True, f32),
        (_chunks(1536, 256), "rope", False, f32),
        (_chunks(1792, 256), "plain", False, f32),
        (_chunks(2048, 256), "q", False, bf16),
        (_chunks(2304, 3072), "sigmoid", False, bf16),
    ]


_NT = (((1,), (1,)), ((), ()))


def _half_masks():
    lane = lax.broadcasted_iota(jnp.int32, (1, LANES), 1)
    return lane < HEAD_DIM, lane >= HEAD_DIM


def _softmax_pv(s, v_half, sink=None):
    m = jnp.max(s, axis=-1, keepdims=True)
    if sink is not None:
        m = jnp.maximum(m, sink)
    e = jnp.exp(s - m)
    den = jnp.sum(e, axis=-1, keepdims=True)
    if sink is not None:
        den = den + jnp.exp(sink - m)
    r = jnp.dot(e.astype(jnp.bfloat16), v_half, preferred_element_type=jnp.float32)
    return r / den, m, den


def _attn_swa_mem_body(L, sink_ref, qa_ref, ka_ref, va_ref, qc_ref, mk_ref, mv_ref, o_ref):
    bf16 = jnp.bfloat16
    lo, hi = _half_masks()
    halves = (lo, hi)
    T = ATT_BLOCK
    mem_k = [mk_ref[:, j * LANES:(j + 1) * LANES].astype(bf16) for j in range(2)]
    mem_v = [[jnp.where(halves[p], mv_ref[:, j * LANES:(j + 1) * LANES], 0.0).astype(bf16)
              for p in range(2)] for j in range(2)]
    qi = lax.broadcasted_iota(jnp.int32, (T, 2 * T), 0)
    kj = lax.broadcasted_iota(jnp.int32, (T, 2 * T), 1)

    def block(blk, carry):
        r0 = pl.multiple_of(blk * T, T)
        ws = pl.multiple_of(jnp.maximum(r0 - T, 0), T)
        dist = qi - kj + (r0 - ws)
        valid = (dist >= 0) & (dist <= SWA_WINDOW - 1)
        k = ka_ref[pl.ds(ws, 2 * T), :]
        v = va_ref[pl.ds(ws, 2 * T), :]
        k_lo = jnp.where(lo, k, 0.0)
        k_hi = jnp.where(hi, k, 0.0)
        v_lo = jnp.where(lo, v, 0.0)
        v_hi = jnp.where(hi, v, 0.0)
        k_dup = [(k_lo + pltpu.roll(k_lo, HEAD_DIM, axis=1)).astype(bf16),
                 (k_hi + pltpu.roll(k_hi, HEAD_DIM, axis=1)).astype(bf16)]
        v_half = [[v_lo.astype(bf16), pltpu.roll(v_lo, HEAD_DIM, axis=1).astype(bf16)],
                  [pltpu.roll(v_hi, HEAD_DIM, axis=1).astype(bf16), v_hi.astype(bf16)]]
        for j in range(SWA_Q_HEADS // 2):
            qp = qa_ref[pl.ds(r0, T), j * LANES:(j + 1) * LANES]
            kv = j // 2
            acc = jnp.zeros((T, LANES), jnp.float32)
            for p in range(2):
                qm = jnp.where(halves[p], qp, jnp.zeros_like(qp))
                s = lax.dot_general(qm, k_dup[kv], _NT, preferred_element_type=jnp.float32)
                s = jnp.where(valid, s, NEG)
                out, _, _ = _softmax_pv(s, v_half[kv][p], sink_ref[2 * j + p])
                acc = acc + out
            o_ref[pl.ds(r0, T), j * LANES:(j + 1) * LANES] = acc.astype(o_ref.dtype)
        for j in range(MEM_HEADS // 2):
            qp = qc_ref[pl.ds(r0, T), j * LANES:(j + 1) * LANES]
            acc = jnp.zeros((T, LANES), jnp.float32)
            for p in range(2):
                qm = jnp.where(halves[p], qp, jnp.zeros_like(qp))
                s = lax.dot_general(qm, mem_k[j], _NT, preferred_element_type=jnp.float32)
                out, _, _ = _softmax_pv(s, mem_v[j][p])
                acc = acc + out
            c0 = SWA_Q_HEADS * HEAD_DIM + j * LANES
            o_ref[pl.ds(r0, T), c0:c0 + LANES] = acc.astype(o_ref.dtype)
        return carry

    lax.fori_loop(0, L // T, block, 0)


def attn_swa_mem(sinks, qa, ka, va, qc, mk, mv, B, L):
    M = mk.shape[0] // B
    wa, wc = SWA_Q_HEADS * HEAD_DIM, MEM_HEADS * HEAD_DIM
    row = lambda w: pl.BlockSpec((L, w), lambda b: (b, 0))
    return pl.pallas_call(
        functools.partial(_attn_swa_mem_body, L),
        grid=(B,),
        in_specs=[pl.BlockSpec(memory_space=pltpu.SMEM), row(wa), row(LANES), row(LANES), row(wc),
                  pl.BlockSpec((M, wc), lambda b: (b, 0)), pl.BlockSpec((M, wc), lambda b: (b, 0))],
        out_specs=row(wa + wc),
        out_shape=jax.ShapeDtypeStruct((B * L, wa + wc), jnp.bfloat16),
        compiler_params=_cparams(("parallel",)), name="attn_swa_mem")(sinks, qa, ka, va, qc, mk, mv)


def _attn_dil_body(L, qb_ref, kb_ref, vb_ref, o_ref, og_ref, lse_ref):
    bf16 = jnp.bfloat16
    lo, hi = _half_masks()
    halves = (lo, hi)
    T = ATT_BLOCK
    for g, (window, dil) in enumerate(DIL_PAIRS):
        lc = L // dil
        nbc = lc // T
        W = min(2 * T, lc)
        max_dist = window // dil
        qi = lax.broadcasted_iota(jnp.int32, (T, W), 0)
        kj = lax.broadcasted_iota(jnp.int32, (T, W), 1)

        def unit(u, carry, g=g, dil=dil, nbc=nbc, W=W, max_dist=max_dist, qi=qi, kj=kj):
            c = u >> (nbc.bit_length() - 1)
            n = u & (nbc - 1)
            wsc = jnp.maximum(n * T - T, 0) if W == 2 * T else 0
            q0 = c + dil * T * n
            k0 = c + dil * wsc
            dist = qi - kj + (n * T - wsc)
            valid = (dist >= 0) & (dist <= max_dist)
            q = qb_ref[g, pl.ds(q0, T, stride=dil), :]
            k = kb_ref[pl.ds(k0, W, stride=dil), :]
            v = vb_ref[pl.ds(k0, W, stride=dil), :]
            kb = k.astype(bf16)
            acc = jnp.zeros((T, LANES), jnp.float32)
            lse = jnp.zeros((T, LANES), jnp.float32)
            for p in range(2):
                qm = jnp.where(halves[p], q, 0.0).astype(bf16)
                s = lax.dot_general(qm, kb, _NT, preferred_element_type=jnp.float32)
                s = jnp.where(valid, s, NEG)
                out, m, den = _softmax_pv(s, jnp.where(halves[p], v, 0.0).astype(bf16))
                acc = acc + out
                lse = jnp.where(halves[p], m + jnp.log(den), lse)
            og_ref[g, pl.ds(q0, T, stride=dil), :] = acc
            lse_ref[g, pl.ds(q0, T, stride=dil), :] = lse
            return carry

        lax.fori_loop(0, dil * nbc, unit, 0, unroll=4)

    def merge(i, carry):
        r0 = pl.multiple_of(i * T, T)
        ls = [lse_ref[g, pl.ds(r0, T), :] for g in range(len(DIL_PAIRS))]
        m = jnp.maximum(jnp.maximum(ls[0], ls[1]), ls[2])
        ws = [jnp.exp(l - m) for l in ls]
        tot = ws[0] + ws[1] + ws[2]
        out = sum((w / tot) * og_ref[g, pl.ds(r0, T), :] for g, w in enumerate(ws))
        o_ref[pl.ds(r0, T), :] = out.astype(o_ref.dtype)
        return carry

    lax.fori_loop(0, L // T, merge, 0)


def attn_dilated(qb, kb, vb, B, L):
    ng = len(DIL_PAIRS)
    return pl.pallas_call(
        functools.partial(_attn_dil_body, L),
        grid=(B, 2),
        in_specs=[pl.BlockSpec((ng, L, LANES), lambda b, hp: (hp, b, 0)),
                  pl.BlockSpec((L, LANES), lambda b, hp: (b, hp)),
                  pl.BlockSpec((L, LANES), lambda b, hp: (b, hp))],
        out_specs=pl.BlockSpec((L, LANES), lambda b, hp: (b, hp)),
        out_shape=jax.ShapeDtypeStruct((B * L, 2 * LANES), jnp.bfloat16),
        scratch_shapes=[pltpu.VMEM((ng, L, LANES), jnp.float32), pltpu.VMEM((ng, L, LANES), jnp.float32)],
        compiler_params=_cparams(("parallel", "parallel")), name="attn_dilated")(qb, kb, vb)


N_NEW = 4
SROWS = 8


def _softmax2_pv(s_c, s_n, vt_c, vt_n, sink=None):
    m = jnp.maximum(jnp.max(s_c, axis=-1, keepdims=True), jnp.max(s_n, axis=-1, keepdims=True))
    if sink is not None:
        m = jnp.maximum(m, sink)
    e_c = jnp.exp(s_c - m)
    e_n = jnp.exp(s_n - m)
    den = jnp.sum(e_c, axis=-1, keepdims=True) + jnp.sum(e_n, axis=-1, keepdims=True)
    if sink is not None:
        den = den + jnp.exp(sink - m)
    r = (lax.dot_general(e_c.astype(jnp.bfloat16), vt_c, _NT, preferred_element_type=jnp.float32)
         + lax.dot_general(e_n.astype(jnp.bfloat16), vt_n, _NT, preferred_element_type=jnp.float32))
    return r / den, m, den


def _advance(old_t, new_t):
    n = old_t.shape[1]
    lane = lax.broadcasted_iota(jnp.int32, (1, LANES), 1)
    shifted = pltpu.roll(old_t, n - N_NEW, axis=1)
    last = jnp.where(lane < LANES - N_NEW, shifted[:, n - LANES:], new_t)
    if n == LANES:
        return last
    return jnp.concatenate([shifted[:, :n - LANES], last], axis=1)


def _attn_sample_body(bt, sink_ref, qa_ref, qb_ref, qc_ref, nka_ref, nva_ref, nkb_ref, nvb_ref,
                      cak_ref, cav_ref, cbk_ref, cbv_ref, cmk_ref, cmv_ref,
                      oac_ref, ob_ref, oak_ref, oav_ref, obk_ref, obv_ref):
    f32, bf16 = jnp.float32, jnp.bfloat16
    lo, hi = _half_masks()
    halves = (lo, hi)
    S = SROWS
    la = cak_ref.shape[2]
    lb = cbk_ref.shape[2]
    wa = SWA_Q_HEADS * HEAD_DIM
    new0 = LANES - N_NEW

    na = SWA_Q_HEADS * S
    ia = lax.broadcasted_iota(jnp.int32, (na, la), 0) & (S - 1)
    valid_ac = lax.broadcasted_iota(jnp.int32, (na, la), 1) >= ia + 1
    ja = lax.broadcasted_iota(jnp.int32, (na, LANES), 1) - new0
    valid_an = (ja >= 0) & (ja <= (lax.broadcasted_iota(jnp.int32, (na, LANES), 0) & (S - 1)))
    rcol = lax.broadcasted_iota(jnp.int32, (na, 1), 0)
    sink_col = jnp.zeros((na, 1), f32)
    for h in range(SWA_Q_HEADS):
        sink_col = jnp.where((rcol >> 3) == h, sink_ref[h], sink_col)

    nb_rows = len(DIL_PAIRS) * 2 * S
    rb = lax.broadcasted_iota(jnp.int32, (nb_rows, lb), 0)
    t_c = lb + (rb & (S - 1)) - lax.broadcasted_iota(jnp.int32, (nb_rows, lb), 1)
    rn = lax.broadcasted_iota(jnp.int32, (nb_rows, LANES), 0)
    jn = lax.broadcasted_iota(jnp.int32, (nb_rows, LANES), 1) - new0
    t_n = (rn & (S - 1)) - jn
    valid_bc = jnp.zeros((nb_rows, lb), jnp.bool_)
    valid_bn = jnp.zeros((nb_rows, LANES), jnp.bool_)
    for g, (window, dil) in enumerate(DIL_PAIRS):
        valid_bc = valid_bc | (((rb >> 4) == g) & (t_c <= window) & ((t_c & (dil - 1)) == 0))
        valid_bn = valid_bn | (((rn >> 4) == g) & (jn >= 0) & (t_n >= 0) & ((t_n & (dil - 1)) == 0))

    for b in range(bt):
        rows = slice(b * S, (b + 1) * S)
        oak_ref[b] = _advance(cak_ref[b], nka_ref[b])
        oav_ref[b] = _advance(cav_ref[b], nva_ref[b])
        obk_ref[b] = _advance(cbk_ref[b], nkb_ref[b])
        obv_ref[b] = _advance(cbv_ref[b], nvb_ref[b])

        pieces = []
        for h in range(SWA_Q_HEADS):
            q = jnp.where(halves[h % 2], qa_ref[rows, (h // 2) * LANES:(h // 2 + 1) * LANES].astype(f32), 0.0)
            if h % 2 != h // 4:
                q = pltpu.roll(q, HEAD_DIM, axis=1)
            pieces.append(q)
        qm = jnp.concatenate(pieces, axis=0).astype(bf16)
        s_c = jnp.dot(qm, cak_ref[b].astype(bf16), preferred_element_type=f32)
        s_n = jnp.dot(qm, nka_ref[b].astype(bf16), preferred_element_type=f32)
        out, _, _ = _softmax2_pv(jnp.where(valid_ac, s_c, NEG), jnp.where(valid_an, s_n, NEG),
                                 cav_ref[b].astype(bf16), nva_ref[b].astype(bf16), sink_col)
        for j in range(SWA_Q_HEADS // 2):
            parts = []
            for p in range(2):
                h = 2 * j + p
                o = out[h * S:(h + 1) * S]
                if h % 2 != h // 4:
                    o = pltpu.roll(o, HEAD_DIM, axis=1)
                parts.append(o)
            oac_ref[rows, j * LANES:(j + 1) * LANES] = jnp.where(lo, parts[0], parts[1])

        for j in range(MEM_HEADS // 2):
            cols = slice(j * LANES, (j + 1) * LANES)
            q = qc_ref[rows, cols]
            qm = jnp.concatenate([jnp.where(halves[p], q, jnp.zeros_like(q)) for p in range(2)], axis=0)
            s = jnp.dot(qm, cmk_ref[b, cols, :].astype(bf16), preferred_element_type=f32)
            m = jnp.max(s, axis=-1, keepdims=True)
            e = jnp.exp(s - m)
            r = lax.dot_general(e.astype(bf16), cmv_ref[b, cols, :].astype(bf16), _NT, preferred_element_type=f32)
            out = r / jnp.sum(e, axis=-1, keepdims=True)
            oac_ref[rows, wa + j * LANES:wa + (j + 1) * LANES] = jnp.where(lo, out[:S], out[S:])

        for hp in range(DIL_KV_HEADS // 2):
            cols = slice(hp * LANES, (hp + 1) * LANES)
            pieces = [jnp.where(halves[p], qb_ref[hp * len(DIL_PAIRS) + g, rows, :], 0.0)
                      for g in range(len(DIL_PAIRS)) for p in range(2)]
            qm = jnp.concatenate(pieces, axis=0).astype(bf16)
            s_c = jnp.dot(qm, cbk_ref[b, cols, :].astype(bf16), preferred_element_type=f32)
            s_n = jnp.dot(qm, nkb_ref[b, cols, :].astype(bf16), preferred_element_type=f32)
            out, m, den = _softmax2_pv(jnp.where(valid_bc, s_c, NEG), jnp.where(valid_bn, s_n, NEG),
                                       cbv_ref[b, cols, :].astype(bf16), nvb_ref[b, cols, :].astype(bf16))
            lse = m + jnp.log(den)
            res = []
            for p in range(2):
                r = [slice((g * 2 + p) * S, (g * 2 + p + 1) * S) for g in range(len(DIL_PAIRS))]
                mx = jnp.maximum(jnp.maximum(lse[r[0]], lse[r[1]]), lse[r[2]])
                w = [jnp.exp(lse[x] - mx) for x in r]
                tot = w[0] + w[1] + w[2]
                res.append(sum((w[g] / tot) * out[r[g]] for g in range(len(DIL_PAIRS))))
            ob_ref[rows, cols] = jnp.where(lo, res[0], res[1])


def attn_sample(sinks, qa, qb, qc, nka, nva, nkb, nvb, cak, cav, cbk, cbv, cmk, cmv, bt):
    NB, wka, la = cak.shape
    wkb, lb = cbk.shape[1:]
    wm, M = cmk.shape[1:]
    wa, wc = SWA_Q_HEADS * HEAD_DIM, MEM_HEADS * HEAD_DIM
    ng = len(DIL_PAIRS)
    tok = lambda w: pl.BlockSpec((bt * SROWS, w), lambda i: (i, 0))
    buf = lambda f, n: pl.BlockSpec((bt, f, n), lambda i: (i, 0, 0))
    f32 = jnp.float32
    return pl.pallas_call(
        functools.partial(_attn_sample_body, bt),
        grid=(NB // bt,),
        in_specs=[pl.BlockSpec(memory_space=pltpu.SMEM), tok(wa),
                  pl.BlockSpec((2 * ng, bt * SROWS, LANES), lambda i: (0, i, 0)), tok(wc),
                  buf(wka, LANES), buf(wka, LANES), buf(wkb, LANES), buf(wkb, LANES),
                  buf(wka, la), buf(wka, la), buf(wkb, lb), buf(wkb, lb), buf(wm, M), buf(wm, M)],
        out_specs=[tok(wa + wc), tok(wkb), buf(wka, la), buf(wka, la), buf(wkb, lb), buf(wkb, lb)],
        out_shape=[jax.ShapeDtypeStruct((NB * SROWS, wa + wc), f32), jax.ShapeDtypeStruct((NB * SROWS, wkb), f32),
                   jax.ShapeDtypeStruct(cak.shape, f32), jax.ShapeDtypeStruct(cak.shape, f32),
                   jax.ShapeDtypeStruct(cbk.shape, f32), jax.ShapeDtypeStruct(cbk.shape, f32)],
        compiler_params=_cparams(("parallel",)), name="attn_sample")(
            sinks, qa, qb, qc, nka, nva, nkb, nvb, cak, cav, cbk, cbv, cmk, cmv)


MOE_BLOCK = 256
ROUTE_SUB = 256
_HI16 = 0xFFFF0000


def _pack_bf16_pairs(y):
    w = y.shape[1] // 2
    bits = lax.bitcast_convert_type(y.astype(jnp.bfloat16).astype(jnp.float32), jnp.uint32)
    return (bits[:, :w] >> 16) | (bits[:, w:] & jnp.uint32(_HI16))


def _unpack_bf16_pairs(u):
    lo = lax.bitcast_convert_type(u << 16, jnp.float32)
    hi = lax.bitcast_convert_type(u & jnp.uint32(_HI16), jnp.float32)
    return lo, hi


def _merge_route_body(oac_ref, ob_ref, gate_ref, x_ref, wa_ref, wb_ref, wc_ref, wo_ref, gffn_ref, wr_ref, br_ref,
                      cnt0_ref, x1_ref, h2_ref, mi_ref, mf_ref, cnt_ref, base_ref):
    f32, bf16 = jnp.float32, jnp.bfloat16
    D = x_ref.shape[1]
    wa = SWA_Q_HEADS * HEAD_DIM
    ts = ROUTE_SUB

    @pl.when(pl.program_id(0) == 0)
    def _():
        base_ref[...] = cnt0_ref[...]

    lane = lax.broadcasted_iota(jnp.int32, (ts, LANES), 1)
    ti = lax.broadcasted_iota(jnp.int32, (ts, ts), 0)
    tj = lax.broadcasted_iota(jnp.int32, (ts, ts), 1)
    earlier = (tj < ti).astype(bf16)
    base = base_ref[...]
    for sub in range(x_ref.shape[0] // ts):
        rows = slice(sub * ts, (sub + 1) * ts)
        ma = jnp.dot(oac_ref[rows, :wa].astype(bf16), wa_ref[...], preferred_element_type=f32)
        mb = jnp.dot(ob_ref[rows, :].astype(bf16), wb_ref[...], preferred_element_type=f32)
        mc = jnp.dot(oac_ref[rows, wa:].astype(bf16), wc_ref[...], preferred_element_type=f32)
        merged = (gate_ref[rows, :D].astype(f32) * ma + gate_ref[rows, D:2 * D].astype(f32) * mb
                  + gate_ref[rows, 2 * D:].astype(f32) * mc)
        x1 = x_ref[rows, :] + jnp.dot(merged.astype(bf16), wo_ref[...], preferred_element_type=f32)
        x1_ref[rows, :] = x1
        h2 = _rms(x1, gffn_ref[...])
        h2_ref[rows, :] = _pack_bf16_pairs(h2)

        work = jnp.dot(h2.astype(bf16), wr_ref[...], preferred_element_type=f32) + br_ref[...]
        vals, idxs = [], []
        for _ in range(TOP_K):
            m = jnp.max(work, axis=-1, keepdims=True)
            idx = jnp.min(jnp.where(work == m, lane, LANES), axis=-1, keepdims=True)
            vals.append(m)
            idxs.append(idx)
            work = jnp.where(lane == idx, -jnp.inf, work)
        es = [jnp.exp(v - vals[0]) for v in vals]
        tot = es[0] + es[1] + es[2] + es[3]

        onehot = [(lane == idx).astype(f32) for idx in idxs]
        assign = onehot[0] + onehot[1] + onehot[2] + onehot[3]
        before = jnp.dot(earlier, assign.astype(bf16), preferred_element_type=f32) + base
        base = base + jnp.sum(assign, axis=0, keepdims=True)

        mi = jnp.zeros((ts, LANES), jnp.int32)
        mf = jnp.zeros((ts, LANES), f32)
        for k in range(TOP_K):
            rank = jnp.sum(onehot[k] * before, axis=-1, keepdims=True).astype(jnp.int32)
            mi = jnp.where(lane == k, idxs[k], mi)
            mi = jnp.where(lane == TOP_K + k, rank, mi)
            mf = jnp.where(lane == k, es[k] / tot, mf)
        mi_ref[rows, :] = mi
        mf_ref[rows, :] = mf
    base_ref[...] = base
    cnt_ref[...] = base


def merge_route(o_ac, o_b, gates, x, wa, wb, wc, wo, g_ffn, wr, br, cnt0, tm):
    R, D = x.shape
    row = lambda w: pl.BlockSpec((tm, w), lambda i: (i, 0))
    full = lambda a: pl.BlockSpec(a.shape, lambda i: (0, 0))
    return pl.pallas_call(
        _merge_route_body,
        grid=(R // tm,),
        in_specs=[row(o_ac.shape[1]), row(o_b.shape[1]), row(gates.shape[1]), row(D),
                  full(wa), full(wb), full(wc), full(wo), full(g_ffn), full(wr), full(br), full(cnt0)],
        out_specs=[row(D), row(D // 2), row(LANES), row(LANES), pl.BlockSpec((1, LANES), lambda i: (0, 0))],
        out_shape=[jax.ShapeDtypeStruct((R, D), jnp.float32), jax.ShapeDtypeStruct((R, D // 2), jnp.uint32),
                   jax.ShapeDtypeStruct((R, LANES), jnp.int32), jax.ShapeDtypeStruct((R, LANES), jnp.float32),
                   jax.ShapeDtypeStruct((1, LANES), jnp.float32)],
        scratch_shapes=[pltpu.VMEM((1, LANES), jnp.float32)],
        compiler_params=_cparams(("arbitrary",)), name="merge_route")(
            o_ac, o_b, gates, x, wa, wb, wc, wo, g_ffn, wr, br, cnt0)


def _route_tables_body(nbp, cnt_ref, mi_ref, dest_ref, blk_ref):
    tm = mi_ref.shape[0]
    lane1 = lax.broadcasted_iota(jnp.int32, (1, LANES), 1)
    shift = MOE_BLOCK.bit_length() - 1
    cnt = cnt_ref[...].astype(jnp.int32)
    padded = ((cnt + (MOE_BLOCK - 1)) >> shift) << shift
    pend = padded
    s = 1
    while s < N_EXPERTS:
        pend = pend + jnp.where(lane1 >= s, pltpu.roll(pend, s, axis=1), 0)
        s *= 2
    pstart = pend - padded
    mi = mi_ref[...]
    lane = lax.broadcasted_iota(jnp.int32, (tm, LANES), 1)
    dest = jnp.zeros((tm, LANES), jnp.int32)
    for k in range(TOP_K):
        idx = jnp.sum(jnp.where(lane == k, mi, 0), axis=-1, keepdims=True)
        rank = jnp.sum(jnp.where(lane == TOP_K + k, mi, 0), axis=-1, keepdims=True)
        start = jnp.sum(jnp.where(lane == idx, pstart, 0), axis=-1, keepdims=True)
        dest = jnp.where(lane == k, start + rank, dest)
    dest_ref[...] = dest
    @pl.when(pl.program_id(0) == 0)
    def _():
        row0 = lax.broadcasted_iota(jnp.int32, (nbp, LANES), 0) * MOE_BLOCK
        ended = jnp.where((pend <= row0) & (lax.broadcasted_iota(jnp.int32, (nbp, LANES), 1) < N_EXPERTS), 1, 0)
        blk = jnp.minimum(jnp.sum(ended, axis=-1, keepdims=True), N_EXPERTS - 1)
        blk_ref[...] = jnp.broadcast_to(blk, (nbp, LANES))


def route_tables(cnt, mi, nbp, tm):
    R = mi.shape[0]
    return pl.pallas_call(
        functools.partial(_route_tables_body, nbp),
        grid=(R // tm,),
        in_specs=[pl.BlockSpec((1, LANES), lambda i: (0, 0)), pl.BlockSpec((tm, LANES), lambda i: (i, 0))],
        out_specs=[pl.BlockSpec((tm, LANES), lambda i: (i, 0)), pl.BlockSpec((nbp, LANES), lambda i: (0, 0))],
        out_shape=[jax.ShapeDtypeStruct((R, LANES), jnp.int32), jax.ShapeDtypeStruct((nbp, LANES), jnp.int32)],
        compiler_params=_cparams(("arbitrary",)), name="route_tables")(cnt, mi)


def _dispatch_body(dest_ref, h_ref, xs_in_ref, xs_ref, sem):
    del xs_in_ref
    tm = h_ref.shape[0]

    def row_copy(i, k):
        return pltpu.make_async_copy(h_ref.at[pl.ds(i, 1)], xs_ref.at[pl.ds(dest_ref[0, 0, TOP_K * i + k], 1)], sem)

    def start(i, c):
        for k in range(TOP_K):
            row_copy(i, k).start(priority=k % 2)
        return c

    def wait(i, c):
        for k in range(TOP_K):
            row_copy(i, k).wait()
        return c

    lax.fori_loop(0, tm, start, 0, unroll=8)
    lax.fori_loop(0, tm, wait, 0, unroll=8)


def dispatch(dest3, h2, xs_zero, tm):
    R, W = h2.shape
    return pl.pallas_call(
        _dispatch_body,
        grid=(R // tm,),
        in_specs=[pl.BlockSpec((1, 1, TOP_K * tm), lambda i: (i, 0, 0), memory_space=pltpu.SMEM),
                  pl.BlockSpec((tm, W), lambda i: (i, 0)),
                  pl.BlockSpec(memory_space=pl.ANY)],
        out_specs=pl.BlockSpec(memory_space=pl.ANY),
        out_shape=jax.ShapeDtypeStruct(xs_zero.shape, xs_zero.dtype),
        scratch_shapes=[pltpu.SemaphoreType.DMA],
        input_output_aliases={2: 0},
        compiler_params=_cparams(("arbitrary",)), name="dispatch")(dest3, h2, xs_zero)


def _moe_body(be_ref, x_ref, wgu_ref, bgu_ref, wd_ref, bd_ref, y_ref, wgu_s, wd_s):
    f32, bf16 = jnp.float32, jnp.bfloat16
    b = pl.program_id(0)
    half = wgu_s.shape[0] // 2

    @pl.when((b == 0) | (be_ref[b] != be_ref[jnp.maximum(b - 1, 0)]))
    def _():
        wgu_s[...] = wgu_ref[0].astype(bf16)
        wd_s[...] = wd_ref[0].astype(bf16)

    lo, hi = _unpack_bf16_pairs(x_ref[...])
    gu = (jnp.dot(lo.astype(bf16), wgu_s[:half], preferred_element_type=f32)
          + jnp.dot(hi.astype(bf16), wgu_s[half:], preferred_element_type=f32) + bgu_ref[0])
    gt = jnp.minimum(gu[:, :D_FF], SWIGLU_LIMIT)
    up = jnp.clip(gu[:, D_FF:], -SWIGLU_LIMIT, SWIGLU_LIMIT)
    act = (up + 1.0) * (gt * jax.nn.sigmoid(gt * SWIGLU_ALPHA))
    y = jnp.dot(act.astype(bf16), wd_s[...], preferred_element_type=f32) + bd_ref[0]
    y_ref[...] = _pack_bf16_pairs(y)


def moe_ffn(blk_e, xs, w_gate_up, b_gate_up, w_down, b_down):
    RS, W = xs.shape
    E, D, F2 = w_gate_up.shape
    grid_spec = pltpu.PrefetchScalarGridSpec(
        num_scalar_prefetch=1,
        grid=(RS // MOE_BLOCK,),
        in_specs=[pl.BlockSpec((MOE_BLOCK, W), lambda b, be: (b, 0)),
                  pl.BlockSpec((1, D, F2), lambda b, be: (be[b], 0, 0)),
                  pl.BlockSpec((1, 1, F2), lambda b, be: (be[b], 0, 0)),
                  pl.BlockSpec((1, F2 // 2, D), lambda b, be: (be[b], 0, 0)),
                  pl.BlockSpec((1, 1, D), lambda b, be: (be[b], 0, 0))],
        out_specs=pl.BlockSpec((MOE_BLOCK, W), lambda b, be: (b, 0)),
        scratch_shapes=[pltpu.VMEM((D, F2), jnp.bfloat16), pltpu.VMEM((F2 // 2, D), jnp.bfloat16)])
    return pl.pallas_call(
        _moe_body, grid_spec=grid_spec,
        out_shape=jax.ShapeDtypeStruct((RS, W), jnp.uint32),
        compiler_params=_cparams(("arbitrary",)), name="moe_ffn")(
            blk_e, xs, w_gate_up, b_gate_up.reshape(E, 1, F2), w_down, b_down.reshape(E, 1, D))


def _combine_body(dest_ref, x1_ref, mf_ref, g_ref, ys_ref, o_ref, buf, sem):
    tm = x1_ref.shape[0]
    half = x1_ref.shape[1] // 2

    def row_copy(i, k):
        return pltpu.make_async_copy(ys_ref.at[pl.ds(dest_ref[0, 0, TOP_K * i + k], 1)], buf.at[k, pl.ds(i, 1)], sem)

    def start(i, c):
        for k in range(TOP_K):
            row_copy(i, k).start(priority=k % 2)
        return c

    def wait(i, c):
        for k in range(TOP_K):
            row_copy(i, k).wait()
        return c

    lax.fori_loop(0, tm, start, 0, unroll=8)
    lax.fori_loop(0, tm, wait, 0, unroll=8)
    acc_lo = x1_ref[:, :half]
    acc_hi = x1_ref[:, half:]
    for k in range(TOP_K):
        lo, hi = _unpack_bf16_pairs(buf[k])
        gate = mf_ref[:, k:k + 1]
        acc_lo = acc_lo + gate * lo
        acc_hi = acc_hi + gate * hi
    y = jnp.concatenate([acc_lo, acc_hi], axis=1)
    o_ref[...] = _rms(y, g_ref[...])


def combine(dest3, x1, mf, g_final, ys, tm):
    R, D = x1.shape
    return pl.pallas_call(
        _combine_body,
        grid=(R // tm,),
        in_specs=[pl.BlockSpec((1, 1, TOP_K * tm), lambda i: (i, 0, 0), memory_space=pltpu.SMEM),
                  pl.BlockSpec((tm, D), lambda i: (i, 0)),
                  pl.BlockSpec((tm, LANES), lambda i: (i, 0)),
                  pl.BlockSpec((1, D), lambda i: (0, 0)),
                  pl.BlockSpec(memory_space=pl.ANY)],
        out_specs=pl.BlockSpec((tm, D), lambda i: (i, 0)),
        out_shape=jax.ShapeDtypeStruct((R, D), jnp.float32),
        scratch_shapes=[pltpu.VMEM((TOP_K, tm, D // 2), jnp.uint32), pltpu.SemaphoreType.DMA],
        compiler_params=_cparams(("arbitrary",)), name="combine")(dest3, x1, mf, g_final, ys)


def moe_layer(groups, cnt, g_final, w_gate_up, b_gate_up, w_down, b_down, tm):
    D = groups[0][0].shape[1]
    n_assign = sum(g[0].shape[0] for g in groups) * TOP_K
    nb = (n_assign + N_EXPERTS * (MOE_BLOCK - 1)) // MOE_BLOCK + 1
    nbp = -(-nb // 8) * 8
    xs = jnp.zeros((nb * MOE_BLOCK, D // 2), jnp.uint32)
    dests = []
    for x1, h2, mi, mf in groups:
        R = x1.shape[0]
        dest, blk = route_tables(cnt, mi, nbp, tm)
        dest3 = dest[:, :TOP_K].reshape(R // tm, 1, TOP_K * tm)
        xs = dispatch(dest3, h2, xs, tm)
        dests.append(dest3)
    ys = moe_ffn(blk[:nb, 0], xs, w_gate_up, b_gate_up, w_down, b_down)
    return [combine(dest3, x1, mf, g_final, ys, tm) for dest3, (x1, h2, mi, mf) in zip(dests, groups)]


def kernel(x_prompt, x_sample, cache_swa_k, cache_swa_v, cache_dil_k, cache_dil_v, cache_mem_k, cache_mem_v, mem_prompt, norm_attn, norm_mem, w_in, w_mem_kv, sinks, w_br_a, w_br_b, w_br_c, w_out, norm_ffn, w_router, b_router, w_gate_up, b_gate_up, w_down, b_down, norm_final):
    f32, bf16 = jnp.float32, jnp.bfloat16
    TM = 256
    B, L, D = x_prompt.shape
    NB, n_new, _ = x_sample.shape
    M = mem_prompt.shape[1]
    la, lb = cache_swa_k.shape[2], cache_dil_k.shape[2]
    wka, wkb, wc = SWA_KV_HEADS * HEAD_DIM, DIL_KV_HEADS * HEAD_DIM, MEM_HEADS * HEAD_DIM
    assert n_new == N_NEW and cache_swa_k.shape[0] == 1

    w_in_b = w_in[0].astype(bf16)
    g_attn = norm_attn[0].reshape(1, D)
    secs = in_sections()

    tabs_p = rope_tables(jnp.arange(L, dtype=jnp.int32))
    qa, ka, va, qb, kb, vb, qc, gates = norm_proj(x_prompt.reshape(B * L, D), g_attn, w_in_b, tabs_p, secs, TM)
    mk, mv = norm_proj(mem_prompt.reshape(B * M, D), norm_mem[0].reshape(1, D), w_mem_kv[0].astype(bf16), None,
                       [(_chunks(0, wc), "plain", False, f32), (_chunks(wc, wc), "plain", False, f32)], TM)
    o_ac = attn_swa_mem(sinks[0], qa, ka, va, qc, mk, mv, B, L)
    o_b = attn_dilated(qb, kb, vb, B, L)

    xs_pad = jnp.pad(x_sample, ((0, 0), (0, SROWS - N_NEW), (0, 0))).reshape(NB * SROWS, D)
    tabs_s = rope_tables(PAST_LEN + (jnp.arange(TM, dtype=jnp.int32) % SROWS))
    qa_s, ka_s, va_s, qb_s, kb_s, vb_s, qc_s, gates_s = norm_proj(xs_pad, g_attn, w_in_b, tabs_s, secs, TM)
    real = lambda t: t.reshape(NB, SROWS, -1)[:, :N_NEW].reshape(NB * N_NEW, -1)
    fmaj = lambda c: jnp.transpose(c[0], (0, 2, 3, 1)).reshape(NB, -1, c.shape[2])
    new_t = lambda t: jnp.pad(jnp.transpose(real(t).reshape(NB, N_NEW, -1), (0, 2, 1)),
                              ((0, 0), (0, 0), (LANES - N_NEW, 0)))
    o_ac_s, o_b_s, swa_k_s, swa_v_s, dil_k_s, dil_v_s = attn_sample(
        sinks[0], qa_s, qb_s, qc_s, new_t(ka_s), new_t(va_s), new_t(kb_s), new_t(vb_s),
        fmaj(cache_swa_k), fmaj(cache_swa_v), fmaj(cache_dil_k), fmaj(cache_dil_v),
        fmaj(cache_mem_k), fmaj(cache_mem_v), 2)

    wr = jnp.zeros((D, LANES), f32).at[:, :N_EXPERTS].set(w_router[0]).astype(bf16)
    br = jnp.full((1, LANES), NEG, f32).at[0, :N_EXPERTS].set(b_router[0].astype(f32))
    wts = (w_br_a[0].astype(bf16), w_br_b[0].astype(bf16), w_br_c[0].astype(bf16), w_out[0].astype(bf16),
           norm_ffn[0].reshape(1, D), wr, br)
    x1_p, h2_p, mi_p, mf_p, cnt_p = merge_route(o_ac, o_b, gates, x_prompt.reshape(B * L, D), *wts,
                                                jnp.zeros((1, LANES), f32), 2 * ROUTE_SUB)
    x1_s, h2_s, mi_s, mf_s, cnt = merge_route(real(o_ac_s), real(o_b_s), real(gates_s),
                                              x_sample.reshape(NB * N_NEW, D), *wts, cnt_p, 2 * ROUTE_SUB)
    y_p, y_s = moe_layer([(x1_p, h2_p, mi_p, mf_p), (x1_s, h2_s, mi_s, mf_s)], cnt, norm_final.reshape(1, D),
                         w_gate_up[0], b_gate_up[0], w_down[0], b_down[0], TM)

    heads = lambda t, n, h: t.reshape(1, t.shape[0] // n, n, h, HEAD_DIM)
    tmaj = lambda t, h: jnp.transpose(t.reshape(NB, h, HEAD_DIM, t.shape[2]), (0, 3, 1, 2))[None]
    ka3, va3 = ka.reshape(B, L, wka), va.reshape(B, L, wka)
    la_p = min(SWA_WINDOW, L)
    return (y_p.reshape(B, L, D), y_s.reshape(NB, N_NEW, D),
            heads(ka3[:, L - la_p:].reshape(B * la_p, wka), la_p, SWA_KV_HEADS),
            heads(va3[:, L - la_p:].reshape(B * la_p, wka), la_p, SWA_KV_HEADS),
            heads(kb, L, DIL_KV_HEADS), heads(vb, L, DIL_KV_HEADS),
            heads(mk, M, MEM_HEADS), heads(mv, M, MEM_HEADS),
            tmaj(swa_k_s, SWA_KV_HEADS), tmaj(swa_v_s, SWA_KV_HEADS),
            tmaj(dil_k_s, DIL_KV_HEADS), tmaj(dil_v_s, DIL_KV_HEADS))
```

```python
import functools

import jax
import jax.numpy as jnp
import numpy as np
from jax import lax
from jax.experimental import pallas as pl
from jax.experimental.pallas import tpu as pltpu

D_MODEL = 1024
HEAD_DIM = 64
ROPE_DIM = 16
ROPE_HALF = 8
ROPE_THETA = 500000.0
PAST_LEN = 16384
SWA_Q_HEADS = 8
SWA_KV_HEADS = 2
SWA_WINDOW = 128
DIL_PAIRS = ((128, 1), (512, 4), (2048, 16))
DIL_KV_HEADS = 4
MEM_HEADS = 4
N_EXPERTS = 32
TOP_K = 4
D_FF = 1024
SWIGLU_LIMIT = 7.0
SWIGLU_ALPHA = 1.702
RMS_EPS = 1e-5
ATT_BLOCK = 128
SCALE = HEAD_DIM ** -0.5

LANES = 128
NEG = -1e30
VMEM_LIMIT = 56 * 1024 * 1024


def _cparams(sem):
    return pltpu.CompilerParams(dimension_semantics=sem, vmem_limit_bytes=VMEM_LIMIT)


def _rms(x, g):
    return x * lax.rsqrt(jnp.mean(x * x, axis=-1, keepdims=True) + RMS_EPS) * g


def _norm_proj_body(sections, x_ref, g_ref, w_ref, cs_ref, *out_refs):
    h = _rms(x_ref[...], g_ref[...]).astype(jnp.bfloat16)
    if cs_ref is not None:
        cos = cs_ref[0]
        sin_lo = cs_ref[1]
        sin_hi = cs_ref[2]
    dest = {lo: (o_ref, c, kind, slabs)
            for (cols, kind, slabs), o_ref in zip(sections, out_refs) for c, lo in enumerate(cols)}
    todo = sorted(dest)
    while todo:
        lo = todo.pop(0)
        n = 2 if todo and todo[0] == lo + LANES else 1
        if n == 2:
            todo.pop(0)
        yy = jnp.dot(h, w_ref[:, lo:lo + n * LANES], preferred_element_type=jnp.float32)
        for part in range(n):
            o_ref, c, kind, slabs = dest[lo + part * LANES]
            y = yy[:, part * LANES:(part + 1) * LANES]
            if kind in ("rope", "rope_q"):
                y = (y * cos + pltpu.roll(y, LANES - ROPE_HALF, axis=1) * sin_lo
                     + pltpu.roll(y, ROPE_HALF, axis=1) * sin_hi)
            if kind in ("rope_q", "q"):
                y = y * SCALE
            if kind == "sigmoid":
                y = jax.nn.sigmoid(y)
            if slabs:
                o_ref[c] = y.astype(o_ref.dtype)
            else:
                o_ref[:, c * LANES:(c + 1) * LANES] = y.astype(o_ref.dtype)


def rope_tables(pos):
    inv_freq = ROPE_THETA ** (-jnp.arange(ROPE_HALF, dtype=jnp.float32) / ROPE_HALF)
    ang = pos.astype(jnp.float32)[:, None] * inv_freq[None, :]
    cos, sin = jnp.cos(ang), jnp.sin(ang)
    n = pos.shape[0]
    one = jnp.ones((n, HEAD_DIM - ROPE_DIM), jnp.float32)
    zero = jnp.zeros((n, HEAD_DIM - ROPE_HALF), jnp.float32)
    c = jnp.concatenate([cos, cos, one], axis=1)
    s_lo = jnp.concatenate([-sin, zero], axis=1)
    s_hi = jnp.concatenate([jnp.zeros((n, ROPE_HALF), jnp.float32), sin,
                            jnp.zeros((n, HEAD_DIM - ROPE_DIM), jnp.float32)], axis=1)
    tab = jnp.stack([c, s_lo, s_hi])
    return jnp.concatenate([tab, tab], axis=2)


def norm_proj(x, g, w, tables, sections, tm):
    R, D = x.shape
    in_specs = [pl.BlockSpec((tm, D), lambda i: (i, 0)),
                pl.BlockSpec((1, D), lambda i: (0, 0)),
                pl.BlockSpec(w.shape, lambda i: (0, 0))]
    args = [x, g, w]
    if tables is not None:
        nt = tables.shape[1] // tm
        in_specs.append(pl.BlockSpec((3, tm, LANES), lambda i: (0, i % nt, 0)))
        args.append(tables)
    out_shape, out_specs, secs = [], [], []
    for (cols, kind, slabs, dtype) in sections:
        secs.append((cols, kind, slabs))
        width = LANES * len(cols)
        if slabs:
            out_shape.append(jax.ShapeDtypeStruct((width // LANES, R, LANES), dtype))
            out_specs.append(pl.BlockSpec((width // LANES, tm, LANES), lambda i: (0, i, 0)))
        else:
            out_shape.append(jax.ShapeDtypeStruct((R, width), dtype))
            out_specs.append(pl.BlockSpec((tm, width), lambda i: (i, 0)))
    if tables is None:
        body = lambda x_ref, g_ref, w_ref, *o: _norm_proj_body(secs, x_ref, g_ref, w_ref, None, *o)
    else:
        body = functools.partial(_norm_proj_body, secs)
    return pl.pallas_call(
        body, grid=(R // tm,), in_specs=in_specs, out_specs=out_specs, out_shape=out_shape,
        compiler_params=_cparams(("parallel",)), name="norm_proj")(*args)


def _chunks(start, width):
    return tuple(range(start, start + width, LANES))


def in_sections():
    f32, bf16 = jnp.float32, jnp.bfloat16
    qb0 = 1024 - 256
    qb_cols = tuple(qb0 + HEAD_DIM * (4 * g + 2 * hp) for hp in range(2) for g in range(3))
    return [
        (_chunks(0, 512), "rope_q", False, bf16),
        (_chunks(512, 128), "rope", False, f32),
        (_chunks(640, 128), "plain", False, f32),
        (qb_cols, "rope_q", True, f32),
        (_chunks(1536, 256), "rope", False, f32),
        (_chunks(1792, 256), "plain", False, f32),
        (_chunks(2048, 256), "q", False, bf16),
        (_chunks(2304, 3072), "sigmoid", False, bf16),
    ]


_NT = (((1,), (1,)), ((), ()))


def _half_masks():
    lane = lax.broadcasted_iota(jnp.int32, (1, LANES), 1)
    return lane < HEAD_DIM, lane >= HEAD_DIM


def _softmax_pv(s, v_half, sink=None):
    m = jnp.max(s, axis=-1, keepdims=True)
    if sink is not None:
        m = jnp.maximum(m, sink)
    e = jnp.exp(s - m)
    den = jnp.sum(e, axis=-1, keepdims=True)
    if sink is not None:
        den = den + jnp.exp(sink - m)
    r = jnp.dot(e.astype(jnp.bfloat16), v_half, preferred_element_type=jnp.float32)
    return r / den, m, den


def _attn_swa_mem_body(L, sink_ref, qa_ref, ka_ref, va_ref, qc_ref, mk_ref, mv_ref, o_ref):
    bf16 = jnp.bfloat16
    lo, hi = _half_masks()
    halves = (lo, hi)
    T = ATT_BLOCK
    mem_k = [mk_ref[:, j * LANES:(j + 1) * LANES].astype(bf16) for j in range(2)]
    mem_v = [[jnp.where(halves[p], mv_ref[:, j * LANES:(j + 1) * LANES], 0.0).astype(bf16)
              for p in range(2)] for j in range(2)]
    qi = lax.broadcasted_iota(jnp.int32, (T, 2 * T), 0)
    kj = lax.broadcasted_iota(jnp.int32, (T, 2 * T), 1)

    def block(blk, carry):
        r0 = pl.multiple_of(blk * T, T)
        ws = pl.multiple_of(jnp.maximum(r0 - T, 0), T)
        dist = qi - kj + (r0 - ws)
        valid = (dist >= 0) & (dist <= SWA_WINDOW - 1)
        k = ka_ref[pl.ds(ws, 2 * T), :]
        v = va_ref[pl.ds(ws, 2 * T), :]
        k_lo = jnp.where(lo, k, 0.0)
        k_hi = jnp.where(hi, k, 0.0)
        v_lo = jnp.where(lo, v, 0.0)
        v_hi = jnp.where(hi, v, 0.0)
        k_dup = [(k_lo + pltpu.roll(k_lo, HEAD_DIM, axis=1)).astype(bf16),
                 (k_hi + pltpu.roll(k_hi, HEAD_DIM, axis=1)).astype(bf16)]
        v_half = [[v_lo.astype(bf16), pltpu.roll(v_lo, HEAD_DIM, axis=1).astype(bf16)],
                  [pltpu.roll(v_hi, HEAD_DIM, axis=1).astype(bf16), v_hi.astype(bf16)]]
        for j in range(SWA_Q_HEADS // 2):
            qp = qa_ref[pl.ds(r0, T), j * LANES:(j + 1) * LANES]
            kv = j // 2
            acc = jnp.zeros((T, LANES), jnp.float32)
            for p in range(2):
                qm = jnp.where(halves[p], qp, jnp.zeros_like(qp))
                s = lax.dot_general(qm, k_dup[kv], _NT, preferred_element_type=jnp.float32)
                s = jnp.where(valid, s, NEG)
                out, _, _ = _softmax_pv(s, v_half[kv][p], sink_ref[2 * j + p])
                acc = acc + out
            o_ref[pl.ds(r0, T), j * LANES:(j + 1) * LANES] = acc.astype(o_ref.dtype)
        for j in range(MEM_HEADS // 2):
            qp = qc_ref[pl.ds(r0, T), j * LANES:(j + 1) * LANES]
            acc = jnp.zeros((T, LANES), jnp.float32)
            for p in range(2):
                qm = jnp.where(halves[p], qp, jnp.zeros_like(qp))
                s = lax.dot_general(qm, mem_k[j], _NT, preferred_element_type=jnp.float32)
                out, _, _ = _softmax_pv(s, mem_v[j][p])
                acc = acc + out
            c0 = SWA_Q_HEADS * HEAD_DIM + j * LANES
            o_ref[pl.ds(r0, T), c0:c0 + LANES] = acc.astype(o_ref.dtype)
        return carry

    lax.fori_loop(0, L // T, block, 0)


def attn_swa_mem(sinks, qa, ka, va, qc, mk, mv, B, L):
    M = mk.shape[0] // B
    wa, wc = SWA_Q_HEADS * HEAD_DIM, MEM_HEADS * HEAD_DIM
    row = lambda w: pl.BlockSpec((L, w), lambda b: (b, 0))
    return pl.pallas_call(
        functools.partial(_attn_swa_mem_body, L),
        grid=(B,),
        in_specs=[pl.BlockSpec(memory_space=pltpu.SMEM), row(wa), row(LANES), row(LANES), row(wc),
                  pl.BlockSpec((M, wc), lambda b: (b, 0)), pl.BlockSpec((M, wc), lambda b: (b, 0))],
        out_specs=row(wa + wc),
        out_shape=jax.ShapeDtypeStruct((B * L, wa + wc), jnp.bfloat16),
        compiler_params=_cparams(("parallel",)), name="attn_swa_mem")(sinks, qa, ka, va, qc, mk, mv)


def _attn_dil_body(L, qb_ref, kb_ref, vb_ref, o_ref, kt_ref, vt_ref, og_ref, lse_ref):
    bf16 = jnp.bfloat16
    lo, hi = _half_masks()
    halves = (lo, hi)
    T = ATT_BLOCK
    kt_ref[...] = kb_ref[...].T
    vt_ref[...] = vb_ref[...].T
    for g, (window, dil) in enumerate(DIL_PAIRS):
        lc = L // dil
        nbc = lc // T
        W = min(2 * T, lc)
        max_dist = window // dil
        qi = lax.broadcasted_iota(jnp.int32, (T, W), 0)
        kj = lax.broadcasted_iota(jnp.int32, (T, W), 1)

        def unit(u, carry, g=g, dil=dil, nbc=nbc, W=W, max_dist=max_dist, qi=qi, kj=kj):
            c = u >> (nbc.bit_length() - 1)
            n = u & (nbc - 1)
            wsc = jnp.maximum(n * T - T, 0) if W == 2 * T else 0
            q0 = c + dil * T * n
            k0 = c + dil * wsc
            dist = qi - kj + (n * T - wsc)
            valid = (dist >= 0) & (dist <= max_dist)
            q = qb_ref[g, pl.ds(q0, T, stride=dil), :]
            k = kb_ref[pl.ds(k0, W, stride=dil), :]
            v = vb_ref[pl.ds(k0, W, stride=dil), :]
            kb = k.astype(bf16)
            acc = jnp.zeros((T, LANES), jnp.float32)
            lse = jnp.zeros((T, LANES), jnp.float32)
            for p in range(2):
                qm = jnp.where(halves[p], q, 0.0).astype(bf16)
                s = lax.dot_general(qm, kb, _NT, preferred_element_type=jnp.float32)
                s = jnp.where(valid, s, NEG)
                out, m, den = _softmax_pv(s, jnp.where(halves[p], v, 0.0).astype(bf16))
                acc = acc + out
                lse = jnp.where(halves[p], m + jnp.log(den), lse)
            og_ref[g, pl.ds(q0, T, stride=dil), :] = acc
            lse_ref[g, pl.ds(q0, T, stride=dil), :] = lse
            return carry

        lax.fori_loop(0, dil * nbc, unit, 0, unroll=4)

    def merge(i, carry):
        r0 = pl.multiple_of(i * T, T)
        ls = [lse_ref[g, pl.ds(r0, T), :] for g in range(len(DIL_PAIRS))]
        m = jnp.maximum(jnp.maximum(ls[0], ls[1]), ls[2])
        ws = [jnp.exp(l - m) for l in ls]
        tot = ws[0] + ws[1] + ws[2]
        out = sum((w / tot) * og_ref[g, pl.ds(r0, T), :] for g, w in enumerate(ws))
        o_ref[pl.ds(r0, T), :] = out.astype(o_ref.dtype)
        return carry

    lax.fori_loop(0, L // T, merge, 0)


def attn_dilated(qb, kb, vb, B, L):
    ng = len(DIL_PAIRS)
    fmaj = pl.BlockSpec((None, LANES, L), lambda b, hp: (b, hp, 0))
    return pl.pallas_call(
        functools.partial(_attn_dil_body, L),
        grid=(B, 2),
        in_specs=[pl.BlockSpec((ng, L, LANES), lambda b, hp: (hp, b, 0)),
                  pl.BlockSpec((L, LANES), lambda b, hp: (b, hp)),
                  pl.BlockSpec((L, LANES), lambda b, hp: (b, hp))],
        out_specs=[pl.BlockSpec((L, LANES), lambda b, hp: (b, hp)), fmaj, fmaj],
        out_shape=[jax.ShapeDtypeStruct((B * L, 2 * LANES), jnp.bfloat16),
                   jax.ShapeDtypeStruct((B, 2 * LANES, L), jnp.float32),
                   jax.ShapeDtypeStruct((B, 2 * LANES, L), jnp.float32)],
        scratch_shapes=[pltpu.VMEM((ng, L, LANES), jnp.float32), pltpu.VMEM((ng, L, LANES), jnp.float32)],
        compiler_params=_cparams(("parallel", "parallel")), name="attn_dilated")(qb, kb, vb)


N_NEW = 4
SROWS = 8


def _softmax2_pv(s_c, s_n, vt_c, vt_n, sink=None):
    m = jnp.maximum(jnp.max(s_c, axis=-1, keepdims=True), jnp.max(s_n, axis=-1, keepdims=True))
    if sink is not None:
        m = jnp.maximum(m, sink)
    e_c = jnp.exp(s_c - m)
    e_n = jnp.exp(s_n - m)
    den = jnp.sum(e_c, axis=-1, keepdims=True) + jnp.sum(e_n, axis=-1, keepdims=True)
    if sink is not None:
        den = den + jnp.exp(sink - m)
    r = (lax.dot_general(e_c.astype(jnp.bfloat16), vt_c, _NT, preferred_element_type=jnp.float32)
         + lax.dot_general(e_n.astype(jnp.bfloat16), vt_n, _NT, preferred_element_type=jnp.float32))
    return r / den, m, den


def _advance(old_t, new_t):
    n = old_t.shape[1]
    lane = lax.broadcasted_iota(jnp.int32, (1, LANES), 1)
    shifted = pltpu.roll(old_t, n - N_NEW, axis=1)
    last = jnp.where(lane < LANES - N_NEW, shifted[:, n - LANES:], new_t)
    if n == LANES:
        return last
    return jnp.concatenate([shifted[:, :n - LANES], last], axis=1)


def _attn_sample_body(bt, sink_ref, qa_ref, qb_ref, qc_ref, nka_ref, nva_ref, nkb_ref, nvb_ref,
                      cak_ref, cav_ref, cbk_ref, cbv_ref, cmk_ref, cmv_ref,
                      oac_ref, ob_ref, oak_ref, oav_ref, obk_ref, obv_ref):
    f32, bf16 = jnp.float32, jnp.bfloat16
    lo, hi = _half_masks()
    halves = (lo, hi)
    S = SROWS
    la = cak_ref.shape[2]
    lb = cbk_ref.shape[2]
    wa = SWA_Q_HEADS * HEAD_DIM
    new0 = LANES - N_NEW

    na = SWA_Q_HEADS * S
    ia = lax.broadcasted_iota(jnp.int32, (na, la), 0) & (S - 1)
    valid_ac = lax.broadcasted_iota(jnp.int32, (na, la), 1) >= ia + 1
    ja = lax.broadcasted_iota(jnp.int32, (na, LANES), 1) - new0
    valid_an = (ja >= 0) & (ja <= (lax.broadcasted_iota(jnp.int32, (na, LANES), 0) & (S - 1)))
    rcol = lax.broadcasted_iota(jnp.int32, (na, 1), 0)
    sink_col = jnp.zeros((na, 1), f32)
    for h in range(SWA_Q_HEADS):
        sink_col = jnp.where((rcol >> 3) == h, sink_ref[h], sink_col)

    nb_rows = len(DIL_PAIRS) * 2 * S
    rb = lax.broadcasted_iota(jnp.int32, (nb_rows, lb), 0)
    t_c = lb + (rb & (S - 1)) - lax.broadcasted_iota(jnp.int32, (nb_rows, lb), 1)
    rn = lax.broadcasted_iota(jnp.int32, (nb_rows, LANES), 0)
    jn = lax.broadcasted_iota(jnp.int32, (nb_rows, LANES), 1) - new0
    t_n = (rn & (S - 1)) - jn
    valid_bc = jnp.zeros((nb_rows, lb), jnp.bool_)
    valid_bn = jnp.zeros((nb_rows, LANES), jnp.bool_)
    for g, (window, dil) in enumerate(DIL_PAIRS):
        valid_bc = valid_bc | (((rb >> 4) == g) & (t_c <= window) & ((t_c & (dil - 1)) == 0))
        valid_bn = valid_bn | (((rn >> 4) == g) & (jn >= 0) & (t_n >= 0) & ((t_n & (dil - 1)) == 0))

    for b in range(bt):
        rows = slice(b * S, (b + 1) * S)
        oak_ref[b] = _advance(cak_ref[b], nka_ref[b])
        oav_ref[b] = _advance(cav_ref[b], nva_ref[b])
        obk_ref[b] = _advance(cbk_ref[b], nkb_ref[b])
        obv_ref[b] = _advance(cbv_ref[b], nvb_ref[b])

        pieces = []
        for h in range(SWA_Q_HEADS):
            q = jnp.where(halves[h % 2], qa_ref[rows, (h // 2) * LANES:(h // 2 + 1) * LANES].astype(f32), 0.0)
            if h % 2 != h // 4:
                q = pltpu.roll(q, HEAD_DIM, axis=1)
            pieces.append(q)
        qm = jnp.concatenate(pieces, axis=0).astype(bf16)
        s_c = jnp.dot(qm, cak_ref[b].astype(bf16), preferred_element_type=f32)
        s_n = jnp.dot(qm, nka_ref[b].astype(bf16), preferred_element_type=f32)
        out, _, _ = _softmax2_pv(jnp.where(valid_ac, s_c, NEG), jnp.where(valid_an, s_n, NEG),
                                 cav_ref[b].astype(bf16), nva_ref[b].astype(bf16), sink_col)
        for j in range(SWA_Q_HEADS // 2):
            parts = []
            for p in range(2):
                h = 2 * j + p
                o = out[h * S:(h + 1) * S]
                if h % 2 != h // 4:
                    o = pltpu.roll(o, HEAD_DIM, axis=1)
                parts.append(o)
            oac_ref[rows, j * LANES:(j + 1) * LANES] = jnp.where(lo, parts[0], parts[1])

        for j in range(MEM_HEADS // 2):
            cols = slice(j * LANES, (j + 1) * LANES)
            q = qc_ref[rows, cols]
            qm = jnp.concatenate([jnp.where(halves[p], q, jnp.zeros_like(q)) for p in range(2)], axis=0)
            s = jnp.dot(qm, cmk_ref[b, cols, :].astype(bf16), preferred_element_type=f32)
            m = jnp.max(s, axis=-1, keepdims=True)
            e = jnp.exp(s - m)
            r = lax.dot_general(e.astype(bf16), cmv_ref[b, cols, :].astype(bf16), _NT, preferred_element_type=f32)
            out = r / jnp.sum(e, axis=-1, keepdims=True)
            oac_ref[rows, wa + j * LANES:wa + (j + 1) * LANES] = jnp.where(lo, out[:S], out[S:])

        for hp in range(DIL_KV_HEADS // 2):
            cols = slice(hp * LANES, (hp + 1) * LANES)
            pieces = [jnp.where(halves[p], qb_ref[hp * len(DIL_PAIRS) + g, rows, :], 0.0)
                      for g in range(len(DIL_PAIRS)) for p in range(2)]
            qm = jnp.concatenate(pieces, axis=0).astype(bf16)
            s_c = jnp.dot(qm, cbk_ref[b, cols, :].astype(bf16), preferred_element_type=f32)
            s_n = jnp.dot(qm, nkb_ref[b, cols, :].astype(bf16), preferred_element_type=f32)
            out, m, den = _softmax2_pv(jnp.where(valid_bc, s_c, NEG), jnp.where(valid_bn, s_n, NEG),
                                       cbv_ref[b, cols, :].astype(bf16), nvb_ref[b, cols, :].astype(bf16))
            lse = m + jnp.log(den)
            res = []
            for p in range(2):
                r = [slice((g * 2 + p) * S, (g * 2 + p + 1) * S) for g in range(len(DIL_PAIRS))]
                mx = jnp.maximum(jnp.maximum(lse[r[0]], lse[r[1]]), lse[r[2]])
                w = [jnp.exp(lse[x] - mx) for x in r]
                tot = w[0] + w[1] + w[2]
                res.append(sum((w[g] / tot) * out[r[g]] for g in range(len(DIL_PAIRS))))
            ob_ref[rows, cols] = jnp.where(lo, res[0], res[1])


def attn_sample(sinks, qa, qb, qc, nka, nva, nkb, nvb, cak, cav, cbk, cbv, cmk, cmv, bt):
    NB, wka, la = cak.shape
    wkb, lb = cbk.shape[1:]
    wm, M = cmk.shape[1:]
    wa, wc = SWA_Q_HEADS * HEAD_DIM, MEM_HEADS * HEAD_DIM
    ng = len(DIL_PAIRS)
    tok = lambda w: pl.BlockSpec((bt * SROWS, w), lambda i: (i, 0))
    buf = lambda f, n: pl.BlockSpec((bt, f, n), lambda i: (i, 0, 0))
    f32 = jnp.float32
    return pl.pallas_call(
        functools.partial(_attn_sample_body, bt),
        grid=(NB // bt,),
        in_specs=[pl.BlockSpec(memory_space=pltpu.SMEM), tok(wa),
                  pl.BlockSpec((2 * ng, bt * SROWS, LANES), lambda i: (0, i, 0)), tok(wc),
                  buf(wka, LANES), buf(wka, LANES), buf(wkb, LANES), buf(wkb, LANES),
                  buf(wka, la), buf(wka, la), buf(wkb, lb), buf(wkb, lb), buf(wm, M), buf(wm, M)],
        out_specs=[tok(wa + wc), tok(wkb), buf(wka, la), buf(wka, la), buf(wkb, lb), buf(wkb, lb)],
        out_shape=[jax.ShapeDtypeStruct((NB * SROWS, wa + wc), f32), jax.ShapeDtypeStruct((NB * SROWS, wkb), f32),
                   jax.ShapeDtypeStruct(cak.shape, f32), jax.ShapeDtypeStruct(cak.shape, f32),
                   jax.ShapeDtypeStruct(cbk.shape, f32), jax.ShapeDtypeStruct(cbk.shape, f32)],
        compiler_params=_cparams(("parallel",)), name="attn_sample")(
            sinks, qa, qb, qc, nka, nva, nkb, nvb, cak, cav, cbk, cbv, cmk, cmv)


MOE_BLOCK = 256
ROUTE_SUB = 256
SUBLANES = 8


def _store_row_tiles(ref, row0, y):
    n = y.shape[0]
    for c in range(SUBLANES):
        ref[pl.ds(row0 * SUBLANES + c, n, stride=SUBLANES), :] = y[:, c * LANES:(c + 1) * LANES]


def _load_row_tiles(ref, row0, n):
    return jnp.concatenate([ref[pl.ds(row0 * SUBLANES + c, n, stride=SUBLANES), :] for c in range(SUBLANES)],
                           axis=1)


def _row_tile(ref, r):
    return ref.at[pl.ds(pl.multiple_of(r * SUBLANES, SUBLANES), SUBLANES)]


def _merge_route_body(oac_ref, ob_ref, gate_ref, x_ref, wa_ref, wb_ref, wc_ref, wo_ref, gffn_ref, wr_ref, br_ref,
                      cnt0_ref, x1_ref, h2_ref, mi_ref, mf_ref, cnt_ref, base_ref):
    f32, bf16 = jnp.float32, jnp.bfloat16
    D = x_ref.shape[1]
    wa = SWA_Q_HEADS * HEAD_DIM
    ts = ROUTE_SUB

    @pl.when(pl.program_id(0) == 0)
    def _():
        base_ref[...] = cnt0_ref[...]

    erow = lax.broadcasted_iota(jnp.int32, (N_EXPERTS, ts), 0)
    r8 = lax.broadcasted_iota(jnp.int32, (SUBLANES, ts), 0)
    ti = lax.broadcasted_iota(jnp.int32, (ts, ts), 0)
    tj = lax.broadcasted_iota(jnp.int32, (ts, ts), 1)
    later = (ti < tj).astype(bf16)
    base = base_ref[:, 0:1]
    for sub in range(x_ref.shape[0] // ts):
        rows = slice(sub * ts, (sub + 1) * ts)
        ma = jnp.dot(oac_ref[rows, :wa].astype(bf16), wa_ref[...], preferred_element_type=f32)
        mb = jnp.dot(ob_ref[rows, :].astype(bf16), wb_ref[...], preferred_element_type=f32)
        mc = jnp.dot(oac_ref[rows, wa:].astype(bf16), wc_ref[...], preferred_element_type=f32)
        merged = (gate_ref[rows, :D].astype(f32) * ma + gate_ref[rows, D:2 * D].astype(f32) * mb
                  + gate_ref[rows, 2 * D:].astype(f32) * mc)
        x1 = x_ref[rows, :] + jnp.dot(merged.astype(bf16), wo_ref[...], preferred_element_type=f32)
        x1_ref[rows, :] = x1
        h2 = _rms(x1, gffn_ref[...])
        _store_row_tiles(h2_ref, sub * ts, h2)

        work = lax.dot_general(wr_ref[...], h2.astype(bf16), _NT, preferred_element_type=f32) + br_ref[:, 0:1]
        vals, idxs = [], []
        for _ in range(TOP_K):
            m = jnp.max(work, axis=0, keepdims=True)
            idx = jnp.min(jnp.where(work == m, erow, N_EXPERTS), axis=0, keepdims=True)
            vals.append(m)
            idxs.append(idx)
            work = jnp.where(erow == idx, -jnp.inf, work)
        es = [jnp.exp(v - vals[0]) for v in vals]
        tot = es[0] + es[1] + es[2] + es[3]

        onehot = [(erow == idx).astype(f32) for idx in idxs]
        assign = onehot[0] + onehot[1] + onehot[2] + onehot[3]
        before = jnp.dot(assign.astype(bf16), later, preferred_element_type=f32) + base
        base = base + jnp.sum(assign, axis=1, keepdims=True)

        mi = jnp.zeros((SUBLANES, ts), jnp.int32)
        gates = jnp.zeros((SUBLANES, ts), f32)
        for k in range(TOP_K):
            rank = jnp.sum(onehot[k] * before, axis=0, keepdims=True).astype(jnp.int32)
            mi = jnp.where(r8 == k, idxs[k], mi)
            mi = jnp.where(r8 == TOP_K + k, rank, mi)
            gates = jnp.where(r8 == k, es[k] / tot, gates)
        mi_ref[:, rows] = mi
        mf_ref[rows, :] = jnp.concatenate([gates, jnp.zeros((LANES - SUBLANES, ts), f32)], axis=0).T
    base_ref[...] = jnp.broadcast_to(base, base_ref.shape)
    cnt_ref[...] = jnp.broadcast_to(base, cnt_ref.shape)


def merge_route(o_ac, o_b, gates, x, wa, wb, wc, wo, g_ffn, wr, br, cnt0, tm):
    R, D = x.shape
    row = lambda w: pl.BlockSpec((tm, w), lambda i: (i, 0))
    full = lambda a: pl.BlockSpec(a.shape, lambda i: (0, 0))
    return pl.pallas_call(
        _merge_route_body,
        grid=(R // tm,),
        in_specs=[row(o_ac.shape[1]), row(o_b.shape[1]), row(gates.shape[1]), row(D),
                  full(wa), full(wb), full(wc), full(wo), full(g_ffn), full(wr), full(br), full(cnt0)],
        out_specs=[row(D), pl.BlockSpec((tm * SUBLANES, LANES), lambda i: (i, 0)),
                   pl.BlockSpec((SUBLANES, tm), lambda i: (0, i)), row(LANES),
                   pl.BlockSpec((N_EXPERTS, LANES), lambda i: (0, 0))],
        out_shape=[jax.ShapeDtypeStruct((R, D), jnp.float32), jax.ShapeDtypeStruct((R * SUBLANES, LANES), jnp.float32),
                   jax.ShapeDtypeStruct((SUBLANES, R), jnp.int32), jax.ShapeDtypeStruct((R, LANES), jnp.float32),
                   jax.ShapeDtypeStruct((N_EXPERTS, LANES), jnp.float32)],
        scratch_shapes=[pltpu.VMEM((N_EXPERTS, LANES), jnp.float32)],
        compiler_params=_cparams(("arbitrary",)), name="merge_route")(
            o_ac, o_b, gates, x, wa, wb, wc, wo, g_ffn, wr, br, cnt0)


def _route_tables_body(cnt_ref, mi_ref, dest_ref, blk_ref, pad_ref):
    tm = mi_ref.shape[1]
    nbl = blk_ref.shape[1]
    erow1 = lax.broadcasted_iota(jnp.int32, (N_EXPERTS, LANES), 0)
    shift = MOE_BLOCK.bit_length() - 1
    cnt = cnt_ref[...].astype(jnp.int32)
    padded = ((cnt + (MOE_BLOCK - 1)) >> shift) << shift
    pend = padded
    s = 1
    while s < N_EXPERTS:
        pend = pend + jnp.where(erow1 >= s, pltpu.roll(pend, s, axis=0), 0)
        s *= 2
    pstart = pend - padded
    mi = mi_ref[...]
    erow = lax.broadcasted_iota(jnp.int32, (N_EXPERTS, tm), 0)
    r8 = lax.broadcasted_iota(jnp.int32, (SUBLANES, tm), 0)
    dest = jnp.zeros((SUBLANES, tm), jnp.int32)
    for k in range(TOP_K):
        start = jnp.sum(jnp.where(erow == mi[k:k + 1, :], pstart[:, 0:1], 0), axis=0, keepdims=True)
        dest = jnp.where(r8 == k, start + mi[TOP_K + k:TOP_K + k + 1, :], dest)
    dest_ref[...] = dest

    @pl.when(pl.program_id(0) == 0)
    def _():
        row0 = lax.broadcasted_iota(jnp.int32, (N_EXPERTS, nbl), 1) * MOE_BLOCK
        ended = jnp.sum(jnp.where(pend[:, 0:1] <= row0, 1, 0), axis=0, keepdims=True)
        blk_ref[...] = jnp.broadcast_to(jnp.minimum(ended, N_EXPERTS - 1), blk_ref.shape)
        pad_ref[0] = pstart + cnt
        pad_ref[1] = pend


def route_tables(cnt, mi, nbl, tm):
    R = mi.shape[1]
    return pl.pallas_call(
        _route_tables_body,
        grid=(R // tm,),
        in_specs=[pl.BlockSpec((N_EXPERTS, LANES), lambda i: (0, 0)), pl.BlockSpec((SUBLANES, tm), lambda i: (0, i))],
        out_specs=[pl.BlockSpec((SUBLANES, tm), lambda i: (0, i)), pl.BlockSpec((SUBLANES, nbl), lambda i: (0, 0)),
                   pl.BlockSpec((2, N_EXPERTS, LANES), lambda i: (0, 0, 0))],
        out_shape=[jax.ShapeDtypeStruct((SUBLANES, R), jnp.int32), jax.ShapeDtypeStruct((SUBLANES, nbl), jnp.int32),
                   jax.ShapeDtypeStruct((2, N_EXPERTS, LANES), jnp.int32)],
        compiler_params=_cparams(("arbitrary",)), name="route_tables")(cnt, mi)


def _dispatch_body(n_first, n_zero, pad0_ref, pad1_ref, dest_ref, ha_ref, hb_ref, xs_ref, zero_ref, sem):
    i = pl.program_id(0)
    tm = ha_ref.shape[0] // SUBLANES
    rs = xs_ref.shape[0] // SUBLANES

    def scatter_tile(h_ref):
        def row_copy(r, k):
            return pltpu.make_async_copy(_row_tile(h_ref, r), _row_tile(xs_ref, dest_ref[0, 0, k * tm + r]), sem)

        def start(r, c):
            for k in range(TOP_K):
                row_copy(r, k).start(priority=k % 2)
            return c

        def wait(r, c):
            for k in range(TOP_K):
                row_copy(r, k).wait()
            return c

        lax.fori_loop(0, tm, start, 0, unroll=8)
        lax.fori_loop(0, tm, wait, 0, unroll=8)

    @pl.when(i < n_first)
    def _():
        scatter_tile(ha_ref)

    @pl.when(i >= n_first)
    def _():
        scatter_tile(hb_ref)

    @pl.when(i == pl.num_programs(0) - 1)
    def _():
        zero_ref[...] = jnp.zeros_like(zero_ref)

        def zero_copy(r):
            return pltpu.make_async_copy(zero_ref, _row_tile(xs_ref, r), sem)

        def start(r, c):
            zero_copy(r).start()
            return c

        def wait(r, c):
            zero_copy(0).wait()
            return c

        for e in range(N_EXPERTS):
            lax.fori_loop(pad0_ref[e], pad1_ref[e], start, 0)
        lax.fori_loop(pad1_ref[N_EXPERTS - 1], rs, start, 0)
        lax.fori_loop(0, n_zero, wait, 0)


def dispatch(pad0, pad1, dest3, h_a, h_b, rs, tm):
    S = SUBLANES
    n_a, n_b = h_a.shape[0] // (tm * S), h_b.shape[0] // (tm * S)
    n_zero = rs - (h_a.shape[0] + h_b.shape[0]) // S * TOP_K
    grid_spec = pltpu.PrefetchScalarGridSpec(
        num_scalar_prefetch=2,
        grid=(n_a + n_b,),
        in_specs=[pl.BlockSpec((1, 1, TOP_K * tm), lambda i, p0, p1: (i, 0, 0), memory_space=pltpu.SMEM),
                  pl.BlockSpec((tm * S, LANES), lambda i, p0, p1: (jnp.minimum(i, n_a - 1), 0)),
                  pl.BlockSpec((tm * S, LANES), lambda i, p0, p1: (jnp.maximum(i - n_a, 0), 0))],
        out_specs=pl.BlockSpec(memory_space=pl.ANY),
        scratch_shapes=[pltpu.VMEM((S, LANES), h_a.dtype), pltpu.SemaphoreType.DMA])
    return pl.pallas_call(
        functools.partial(_dispatch_body, n_a, n_zero), grid_spec=grid_spec,
        out_shape=jax.ShapeDtypeStruct((rs * S, LANES), h_a.dtype),
        compiler_params=_cparams(("arbitrary",)), name="dispatch")(pad0, pad1, dest3, h_a, h_b)


def _moe_body(be_ref, x_ref, wgu_ref, bgu_ref, wd_ref, bd_ref, y_ref, wgu_s, wd_s):
    f32, bf16 = jnp.float32, jnp.bfloat16
    b = pl.program_id(0)

    @pl.when((b == 0) | (be_ref[b] != be_ref[jnp.maximum(b - 1, 0)]))
    def _():
        wgu_s[...] = wgu_ref[0].astype(bf16)
        wd_s[...] = wd_ref[0].astype(bf16)

    x = _load_row_tiles(x_ref, 0, MOE_BLOCK).astype(bf16)
    gu = jnp.dot(x, wgu_s[...], preferred_element_type=f32) + bgu_ref[0]
    gt = jnp.minimum(gu[:, :D_FF], SWIGLU_LIMIT)
    up = jnp.clip(gu[:, D_FF:], -SWIGLU_LIMIT, SWIGLU_LIMIT)
    act = (up + 1.0) * (gt * jax.nn.sigmoid(gt * SWIGLU_ALPHA))
    y = jnp.dot(act.astype(bf16), wd_s[...], preferred_element_type=f32) + bd_ref[0]
    _store_row_tiles(y_ref, 0, y)


def moe_ffn(blk_e, xs, w_gate_up, b_gate_up, w_down, b_down):
    RS = xs.shape[0] // SUBLANES
    E, D, F2 = w_gate_up.shape
    blk_rows = MOE_BLOCK * SUBLANES
    grid_spec = pltpu.PrefetchScalarGridSpec(
        num_scalar_prefetch=1,
        grid=(RS // MOE_BLOCK,),
        in_specs=[pl.BlockSpec((blk_rows, LANES), lambda b, be: (b, 0)),
                  pl.BlockSpec((1, D, F2), lambda b, be: (be[b], 0, 0)),
                  pl.BlockSpec((1, 1, F2), lambda b, be: (be[b], 0, 0)),
                  pl.BlockSpec((1, F2 // 2, D), lambda b, be: (be[b], 0, 0)),
                  pl.BlockSpec((1, 1, D), lambda b, be: (be[b], 0, 0))],
        out_specs=pl.BlockSpec((blk_rows, LANES), lambda b, be: (b, 0)),
        scratch_shapes=[pltpu.VMEM((D, F2), jnp.bfloat16), pltpu.VMEM((F2 // 2, D), jnp.bfloat16)])
    return pl.pallas_call(
        _moe_body, grid_spec=grid_spec,
        out_shape=jax.ShapeDtypeStruct(xs.shape, jnp.float32),
        compiler_params=_cparams(("arbitrary",)), name="moe_ffn")(
            blk_e, xs, w_gate_up, b_gate_up.reshape(E, 1, F2), w_down, b_down.reshape(E, 1, D))


def _combine_body(dest_ref, x1_ref, mf_ref, g_ref, ys_ref, o_ref, buf, sem):
    tm = x1_ref.shape[0]

    def row_copy(r, k):
        return pltpu.make_async_copy(_row_tile(ys_ref, dest_ref[0, 0, k * tm + r]), _row_tile(buf, k * tm + r), sem)

    def start(r, c):
        for k in range(TOP_K):
            row_copy(r, k).start(priority=k % 2)
        return c

    def wait(r, c):
        for k in range(TOP_K):
            row_copy(r, k).wait()
        return c

    lax.fori_loop(0, tm, start, 0, unroll=8)
    lax.fori_loop(0, tm, wait, 0, unroll=8)
    y = x1_ref[...]
    for k in range(TOP_K):
        y = y + mf_ref[:, k:k + 1] * _load_row_tiles(buf, k * tm, tm)
    o_ref[...] = _rms(y, g_ref[...])


def combine(dest3, x1, mf, g_final, ys, tm):
    R, D = x1.shape
    return pl.pallas_call(
        _combine_body,
        grid=(R // tm,),
        in_specs=[pl.BlockSpec((1, 1, TOP_K * tm), lambda i: (i, 0, 0), memory_space=pltpu.SMEM),
                  pl.BlockSpec((tm, D), lambda i: (i, 0)),
                  pl.BlockSpec((tm, LANES), lambda i: (i, 0)),
                  pl.BlockSpec((1, D), lambda i: (0, 0)),
                  pl.BlockSpec(memory_space=pl.ANY)],
        out_specs=pl.BlockSpec((tm, D), lambda i: (i, 0)),
        out_shape=jax.ShapeDtypeStruct((R, D), jnp.float32),
        scratch_shapes=[pltpu.VMEM((TOP_K * tm * SUBLANES, LANES), ys.dtype), pltpu.SemaphoreType.DMA],
        compiler_params=_cparams(("arbitrary",)), name="combine")(dest3, x1, mf, g_final, ys)


def moe_layer(group_a, group_b, cnt, g_final, w_gate_up, b_gate_up, w_down, b_down, tm):
    n_assign = (group_a[0].shape[0] + group_b[0].shape[0]) * TOP_K
    nb = (n_assign + N_EXPERTS * (MOE_BLOCK - 1)) // MOE_BLOCK + 1
    nbl = -(-nb // LANES) * LANES
    dests = []
    for x1, h2, mi, mf in (group_a, group_b):
        dest, blk, pad = route_tables(cnt, mi, nbl, tm)
        n_tiles = x1.shape[0] // tm
        dests.append(jnp.transpose(dest[:TOP_K].reshape(TOP_K, n_tiles, tm), (1, 0, 2)).reshape(n_tiles, 1, TOP_K * tm))
    xs = dispatch(pad[0, :, 0], pad[1, :, 0], jnp.concatenate(dests), group_a[1], group_b[1], nb * MOE_BLOCK, tm)
    ys = moe_ffn(blk[0, :nb], xs, w_gate_up, b_gate_up, w_down, b_down)
    return [combine(dest3, x1, mf, g_final, ys, tm) for dest3, (x1, h2, mi, mf) in zip(dests, (group_a, group_b))]


def kernel(x_prompt, x_sample, cache_swa_k, cache_swa_v, cache_dil_k, cache_dil_v, cache_mem_k, cache_mem_v, mem_prompt, norm_attn, norm_mem, w_in, w_mem_kv, sinks, w_br_a, w_br_b, w_br_c, w_out, norm_ffn, w_router, b_router, w_gate_up, b_gate_up, w_down, b_down, norm_final):
    f32, bf16 = jnp.float32, jnp.bfloat16
    TM = 256
    B, L, D = x_prompt.shape
    NB, n_new, _ = x_sample.shape
    M = mem_prompt.shape[1]
    la, lb = cache_swa_k.shape[2], cache_dil_k.shape[2]
    wka, wkb, wc = SWA_KV_HEADS * HEAD_DIM, DIL_KV_HEADS * HEAD_DIM, MEM_HEADS * HEAD_DIM
    assert n_new == N_NEW and cache_swa_k.shape[0] == 1

    w_in_b = w_in[0].astype(bf16)
    g_attn = norm_attn[0].reshape(1, D)
    secs = in_sections()

    tabs_p = rope_tables(jnp.arange(L, dtype=jnp.int32))
    qa, ka, va, qb, kb, vb, qc, gates = norm_proj(x_prompt.reshape(B * L, D), g_attn, w_in_b, tabs_p, secs, TM)
    mk, mv = norm_proj(mem_prompt.reshape(B * M, D), norm_mem[0].reshape(1, D), w_mem_kv[0].astype(bf16), None,
                       [(_chunks(0, wc), "plain", False, f32), (_chunks(wc, wc), "plain", False, f32)], TM)
    o_ac = attn_swa_mem(sinks[0], qa, ka, va, qc, mk, mv, B, L)
    o_b, kb_t, vb_t = attn_dilated(qb, kb, vb, B, L)

    xs_pad = jnp.pad(x_sample, ((0, 0), (0, SROWS - N_NEW), (0, 0))).reshape(NB * SROWS, D)
    tabs_s = rope_tables(PAST_LEN + (jnp.arange(TM, dtype=jnp.int32) % SROWS))
    qa_s, ka_s, va_s, qb_s, kb_s, vb_s, qc_s, gates_s = norm_proj(xs_pad, g_attn, w_in_b, tabs_s, secs, TM)
    real = lambda t: t.reshape(NB, SROWS, -1)[:, :N_NEW].reshape(NB * N_NEW, -1)
    fmaj = lambda c: jnp.transpose(c[0], (0, 2, 3, 1)).reshape(NB, -1, c.shape[2])
    new_t = lambda t: jnp.pad(jnp.transpose(real(t).reshape(NB, N_NEW, -1), (0, 2, 1)),
                              ((0, 0), (0, 0), (LANES - N_NEW, 0)))
    o_ac_s, o_b_s, swa_k_s, swa_v_s, dil_k_s, dil_v_s = attn_sample(
        sinks[0], qa_s, qb_s, qc_s, new_t(ka_s), new_t(va_s), new_t(kb_s), new_t(vb_s),
        fmaj(cache_swa_k), fmaj(cache_swa_v), fmaj(cache_dil_k), fmaj(cache_dil_v),
        fmaj(cache_mem_k), fmaj(cache_mem_v), 2)

    wr = w_router[0].T.astype(bf16)
    br = jnp.broadcast_to(b_router[0].astype(f32)[:, None], (N_EXPERTS, LANES))
    wts = (w_br_a[0].astype(bf16), w_br_b[0].astype(bf16), w_br_c[0].astype(bf16), w_out[0].astype(bf16),
           norm_ffn[0].reshape(1, D), wr, br)
    x1_p, h2_p, mi_p, mf_p, cnt_p = merge_route(o_ac, o_b, gates, x_prompt.reshape(B * L, D), *wts,
                                                jnp.zeros((N_EXPERTS, LANES), f32), 2 * ROUTE_SUB)
    x1_s, h2_s, mi_s, mf_s, cnt = merge_route(real(o_ac_s), real(o_b_s), real(gates_s),
                                              x_sample.reshape(NB * N_NEW, D), *wts, cnt_p, 2 * ROUTE_SUB)
    y_p, y_s = moe_layer((x1_p, h2_p, mi_p, mf_p), (x1_s, h2_s, mi_s, mf_s), cnt, norm_final.reshape(1, D),
                         w_gate_up[0], b_gate_up[0], w_down[0], b_down[0], TM)

    heads = lambda t, n, h: t.reshape(1, t.shape[0] // n, n, h, HEAD_DIM)
    tmaj = lambda t, h: jnp.transpose(t.reshape(t.shape[0], h, HEAD_DIM, t.shape[2]), (0, 3, 1, 2))[None]
    ka3, va3 = ka.reshape(B, L, wka), va.reshape(B, L, wka)
    la_p = min(SWA_WINDOW, L)
    return (y_p.reshape(B, L, D), y_s.reshape(NB, N_NEW, D),
            heads(ka3[:, L - la_p:].reshape(B * la_p, wka), la_p, SWA_KV_HEADS),
            heads(va3[:, L - la_p:].reshape(B * la_p, wka), la_p, SWA_KV_HEADS),
            tmaj(kb_t, DIL_KV_HEADS), tmaj(vb_t, DIL_KV_HEADS),
            heads(mk, M, MEM_HEADS), heads(mv, M, MEM_HEADS),
            tmaj(swa_k_s, SWA_KV_HEADS), tmaj(swa_v_s, SWA_KV_HEADS),
            tmaj(dil_k_s, DIL_KV_HEADS), tmaj(dil_v_s, DIL_KV_HEADS))
```

```python
import functools

import jax
import jax.numpy as jnp
import numpy as np
from jax import lax
from jax.experimental import pallas as pl
from jax.experimental.pallas import tpu as pltpu

D_MODEL = 1024
HEAD_DIM = 64
ROPE_DIM = 16
ROPE_HALF = 8
ROPE_THETA = 500000.0
PAST_LEN = 16384
SWA_Q_HEADS = 8
SWA_KV_HEADS = 2
SWA_WINDOW = 128
DIL_PAIRS = ((128, 1), (512, 4), (2048, 16))
DIL_KV_HEADS = 4
MEM_HEADS = 4
N_EXPERTS = 32
TOP_K = 4
D_FF = 1024
SWIGLU_LIMIT = 7.0
SWIGLU_ALPHA = 1.702
RMS_EPS = 1e-5
ATT_BLOCK = 128
SCALE = HEAD_DIM ** -0.5

LANES = 128
NEG = -1e30
VMEM_LIMIT = 56 * 1024 * 1024


def _cparams(sem):
    return pltpu.CompilerParams(dimension_semantics=sem, vmem_limit_bytes=VMEM_LIMIT)


def _rms(x, g):
    return x * lax.rsqrt(jnp.mean(x * x, axis=-1, keepdims=True) + RMS_EPS) * g


def _norm_proj_body(sections, x_ref, g_ref, w_ref, cs_ref, *out_refs):
    h = _rms(x_ref[...], g_ref[...]).astype(jnp.bfloat16)
    if cs_ref is not None:
        cos = cs_ref[0]
        sin_lo = cs_ref[1]
        sin_hi = cs_ref[2]
    dest = {lo: (o_ref, c, kind, slabs)
            for (cols, kind, slabs), o_ref in zip(sections, out_refs) for c, lo in enumerate(cols)}
    todo = sorted(dest)
    while todo:
        lo = todo.pop(0)
        n = 2 if todo and todo[0] == lo + LANES else 1
        if n == 2:
            todo.pop(0)
        yy = jnp.dot(h, w_ref[:, lo:lo + n * LANES], preferred_element_type=jnp.float32)
        for part in range(n):
            o_ref, c, kind, slabs = dest[lo + part * LANES]
            y = yy[:, part * LANES:(part + 1) * LANES]
            if kind in ("rope", "rope_q"):
                y = (y * cos + pltpu.roll(y, LANES - ROPE_HALF, axis=1) * sin_lo
                     + pltpu.roll(y, ROPE_HALF, axis=1) * sin_hi)
            if kind in ("rope_q", "q"):
                y = y * SCALE
            if kind == "sigmoid":
                y = jax.nn.sigmoid(y)
            if slabs:
                o_ref[c] = y.astype(o_ref.dtype)
            else:
                o_ref[:, c * LANES:(c + 1) * LANES] = y.astype(o_ref.dtype)


def rope_tables(pos):
    inv_freq = ROPE_THETA ** (-jnp.arange(ROPE_HALF, dtype=jnp.float32) / ROPE_HALF)
    ang = pos.astype(jnp.float32)[:, None] * inv_freq[None, :]
    cos, sin = jnp.cos(ang), jnp.sin(ang)
    n = pos.shape[0]
    one = jnp.ones((n, HEAD_DIM - ROPE_DIM), jnp.float32)
    zero = jnp.zeros((n, HEAD_DIM - ROPE_HALF), jnp.float32)
    c = jnp.concatenate([cos, cos, one], axis=1)
    s_lo = jnp.concatenate([-sin, zero], axis=1)
    s_hi = jnp.concatenate([jnp.zeros((n, ROPE_HALF), jnp.float32), sin,
                            jnp.zeros((n, HEAD_DIM - ROPE_DIM), jnp.float32)], axis=1)
    tab = jnp.stack([c, s_lo, s_hi])
    return jnp.concatenate([tab, tab], axis=2)


def norm_proj(x, g, w, tables, sections, tm):
    R, D = x.shape
    in_specs = [pl.BlockSpec((tm, D), lambda i: (i, 0)),
                pl.BlockSpec((1, D), lambda i: (0, 0)),
                pl.BlockSpec(w.shape, lambda i: (0, 0))]
    args = [x, g, w]
    if tables is not None:
        nt = tables.shape[1] // tm
        in_specs.append(pl.BlockSpec((3, tm, LANES), lambda i: (0, i % nt, 0)))
        args.append(tables)
    out_shape, out_specs, secs = [], [], []
    for (cols, kind, slabs, dtype) in sections:
        secs.append((cols, kind, slabs))
        width = LANES * len(cols)
        if slabs:
            out_shape.append(jax.ShapeDtypeStruct((width // LANES, R, LANES), dtype))
            out_specs.append(pl.BlockSpec((width // LANES, tm, LANES), lambda i: (0, i, 0)))
        else:
            out_shape.append(jax.ShapeDtypeStruct((R, width), dtype))
            out_specs.append(pl.BlockSpec((tm, width), lambda i: (i, 0)))
    if tables is None:
        body = lambda x_ref, g_ref, w_ref, *o: _norm_proj_body(secs, x_ref, g_ref, w_ref, None, *o)
    else:
        body = functools.partial(_norm_proj_body, secs)
    return pl.pallas_call(
        body, grid=(R // tm,), in_specs=in_specs, out_specs=out_specs, out_shape=out_shape,
        compiler_params=_cparams(("parallel",)), name="norm_proj")(*args)


def _chunks(start, width):
    return tuple(range(start, start + width, LANES))


def in_sections():
    f32, bf16 = jnp.float32, jnp.bfloat16
    qb0 = 1024 - 256
    qb_cols = tuple(qb0 + HEAD_DIM * (4 * g + 2 * hp) for hp in range(2) for g in range(3))
    return [
        (_chunks(0, 512), "rope_q", False, bf16),
        (_chunks(512, 128), "rope", False, f32),
        (_chunks(640, 128), "plain", False, f32),
        (qb_cols, "rope_q", True, f32),
        (_chunks(1536, 256), "rope", False, f32),
        (_chunks(1792, 256), "plain", False, f32),
        (_chunks(2048, 256), "q", False, bf16),
        (_chunks(2304, 3072), "sigmoid", False, bf16),
    ]


_NT = (((1,), (1,)), ((), ()))


def _half_masks():
    lane = lax.broadcasted_iota(jnp.int32, (1, LANES), 1)
    return lane < HEAD_DIM, lane >= HEAD_DIM


def _softmax_pv(s, v_half, sink=None):
    m = jnp.max(s, axis=-1, keepdims=True)
    if sink is not None:
        m = jnp.maximum(m, sink)
    e = jnp.exp(s - m)
    den = jnp.sum(e, axis=-1, keepdims=True)
    if sink is not None:
        den = den + jnp.exp(sink - m)
    r = jnp.dot(e.astype(jnp.bfloat16), v_half, preferred_element_type=jnp.float32)
    return r / den, m, den


def _attn_swa_mem_body(L, sink_ref, qa_ref, ka_ref, va_ref, qc_ref, mk_ref, mv_ref, o_ref):
    bf16 = jnp.bfloat16
    lo, hi = _half_masks()
    halves = (lo, hi)
    T = ATT_BLOCK
    G = SWA_Q_HEADS // SWA_KV_HEADS
    mem_k = [mk_ref[:, j * LANES:(j + 1) * LANES].astype(bf16) for j in range(2)]
    mem_v = [mv_ref[:, j * LANES:(j + 1) * LANES].astype(bf16) for j in range(2)]
    qi = lax.broadcasted_iota(jnp.int32, (G * T, 2 * T), 0) & (T - 1)
    kj = lax.broadcasted_iota(jnp.int32, (G * T, 2 * T), 1)
    hrow = lax.broadcasted_iota(jnp.int32, (G * T, 1), 0) >> (T.bit_length() - 1)

    def heads_of(pair):
        return jnp.concatenate([jnp.where(halves[p], pair, jnp.zeros_like(pair)) for p in range(2)], axis=0)

    def block(blk, carry):
        r0 = pl.multiple_of(blk * T, T)
        ws = pl.multiple_of(jnp.maximum(r0 - T, 0), T)
        dist = qi - kj + (r0 - ws)
        valid = (dist >= 0) & (dist <= SWA_WINDOW - 1)
        k = ka_ref[pl.ds(ws, 2 * T), :]
        v = va_ref[pl.ds(ws, 2 * T), :]
        for kv in range(SWA_KV_HEADS):
            k1 = jnp.where(halves[kv], k, 0.0)
            v1 = jnp.where(halves[kv], v, 0.0)
            k_dup = (k1 + pltpu.roll(k1, HEAD_DIM, axis=1)).astype(bf16)
            v_dup = (v1 + pltpu.roll(v1, HEAD_DIM, axis=1)).astype(bf16)
            pairs = range(kv * G // 2, (kv + 1) * G // 2)
            qm = jnp.concatenate([heads_of(qa_ref[pl.ds(r0, T), j * LANES:(j + 1) * LANES]) for j in pairs], axis=0)
            sink = jnp.zeros((G * T, 1), jnp.float32)
            for h in range(G):
                sink = jnp.where(hrow == h, sink_ref[kv * G + h], sink)
            s = lax.dot_general(qm, k_dup, _NT, preferred_element_type=jnp.float32)
            out, _, _ = _softmax_pv(jnp.where(valid, s, NEG), v_dup, sink)
            for n, j in enumerate(pairs):
                o2 = out[2 * n * T:2 * (n + 1) * T]
                o_ref[pl.ds(r0, T), j * LANES:(j + 1) * LANES] = jnp.where(lo, o2[:T], o2[T:]).astype(o_ref.dtype)
        for j in range(MEM_HEADS // 2):
            qm = heads_of(qc_ref[pl.ds(r0, T), j * LANES:(j + 1) * LANES])
            s = lax.dot_general(qm, mem_k[j], _NT, preferred_element_type=jnp.float32)
            out, _, _ = _softmax_pv(s, mem_v[j])
            c0 = SWA_Q_HEADS * HEAD_DIM + j * LANES
            o_ref[pl.ds(r0, T), c0:c0 + LANES] = jnp.where(lo, out[:T], out[T:]).astype(o_ref.dtype)
        return carry

    lax.fori_loop(0, L // T, block, 0)


def attn_swa_mem(sinks, qa, ka, va, qc, mk, mv, B, L):
    M = mk.shape[0] // B
    wa, wc = SWA_Q_HEADS * HEAD_DIM, MEM_HEADS * HEAD_DIM
    row = lambda w: pl.BlockSpec((L, w), lambda b: (b, 0))
    return pl.pallas_call(
        functools.partial(_attn_swa_mem_body, L),
        grid=(B,),
        in_specs=[pl.BlockSpec(memory_space=pltpu.SMEM), row(wa), row(LANES), row(LANES), row(wc),
                  pl.BlockSpec((M, wc), lambda b: (b, 0)), pl.BlockSpec((M, wc), lambda b: (b, 0))],
        out_specs=row(wa + wc),
        out_shape=jax.ShapeDtypeStruct((B * L, wa + wc), jnp.bfloat16),
        compiler_params=_cparams(("parallel",)), name="attn_swa_mem")(sinks, qa, ka, va, qc, mk, mv)


def _attn_dil_body(L, qb_ref, kb_ref, vb_ref, o_ref, kt_ref, vt_ref, og_ref, lse_ref):
    bf16 = jnp.bfloat16
    lo, hi = _half_masks()
    halves = (lo, hi)
    T = ATT_BLOCK
    kt_ref[...] = kb_ref[...].T
    vt_ref[...] = vb_ref[...].T
    for g, (window, dil) in enumerate(DIL_PAIRS):
        lc = L // dil
        nbc = lc // T
        W = min(2 * T, lc)
        max_dist = window // dil
        qi = lax.broadcasted_iota(jnp.int32, (2 * T, W), 0) & (T - 1)
        kj = lax.broadcasted_iota(jnp.int32, (2 * T, W), 1)

        def unit(u, carry, g=g, dil=dil, nbc=nbc, W=W, max_dist=max_dist, qi=qi, kj=kj):
            c = u >> (nbc.bit_length() - 1)
            n = u & (nbc - 1)
            wsc = jnp.maximum(n * T - T, 0) if W == 2 * T else 0
            q0 = c + dil * T * n
            k0 = c + dil * wsc
            dist = qi - kj + (n * T - wsc)
            valid = (dist >= 0) & (dist <= max_dist)
            q = qb_ref[g, pl.ds(q0, T, stride=dil), :]
            k = kb_ref[pl.ds(k0, W, stride=dil), :]
            v = vb_ref[pl.ds(k0, W, stride=dil), :]
            qm = jnp.concatenate([jnp.where(halves[p], q, 0.0) for p in range(2)], axis=0).astype(bf16)
            s = lax.dot_general(qm, k.astype(bf16), _NT, preferred_element_type=jnp.float32)
            out, m, den = _softmax_pv(jnp.where(valid, s, NEG), v.astype(bf16))
            lse = m + jnp.log(den)
            og_ref[g, pl.ds(q0, T, stride=dil), :] = jnp.where(lo, out[:T], out[T:])
            lse_ref[g, pl.ds(q0, T, stride=dil), :] = jnp.where(lo, lse[:T], lse[T:])
            return carry

        lax.fori_loop(0, dil * nbc, unit, 0, unroll=4)

    def merge(i, carry):
        r0 = pl.multiple_of(i * T, T)
        ls = [lse_ref[g, pl.ds(r0, T), :] for g in range(len(DIL_PAIRS))]
        m = jnp.maximum(jnp.maximum(ls[0], ls[1]), ls[2])
        ws = [jnp.exp(l - m) for l in ls]
        tot = ws[0] + ws[1] + ws[2]
        out = sum((w / tot) * og_ref[g, pl.ds(r0, T), :] for g, w in enumerate(ws))
        o_ref[pl.ds(r0, T), :] = out.astype(o_ref.dtype)
        return carry

    lax.fori_loop(0, L // T, merge, 0)


def attn_dilated(qb, kb, vb, B, L):
    ng = len(DIL_PAIRS)
    fmaj = pl.BlockSpec((None, LANES, L), lambda b, hp: (b, hp, 0))
    return pl.pallas_call(
        functools.partial(_attn_dil_body, L),
        grid=(B, 2),
        in_specs=[pl.BlockSpec((ng, L, LANES), lambda b, hp: (hp, b, 0)),
                  pl.BlockSpec((L, LANES), lambda b, hp: (b, hp)),
                  pl.BlockSpec((L, LANES), lambda b, hp: (b, hp))],
        out_specs=[pl.BlockSpec((L, LANES), lambda b, hp: (b, hp)), fmaj, fmaj],
        out_shape=[jax.ShapeDtypeStruct((B * L, 2 * LANES), jnp.bfloat16),
                   jax.ShapeDtypeStruct((B, 2 * LANES, L), jnp.float32),
                   jax.ShapeDtypeStruct((B, 2 * LANES, L), jnp.float32)],
        scratch_shapes=[pltpu.VMEM((ng, L, LANES), jnp.float32), pltpu.VMEM((ng, L, LANES), jnp.float32)],
        compiler_params=_cparams(("parallel", "parallel")), name="attn_dilated")(qb, kb, vb)


N_NEW = 4
SROWS = 8


def _softmax2_pv(s_c, s_n, vt_c, vt_n, sink=None):
    m = jnp.maximum(jnp.max(s_c, axis=-1, keepdims=True), jnp.max(s_n, axis=-1, keepdims=True))
    if sink is not None:
        m = jnp.maximum(m, sink)
    e_c = jnp.exp(s_c - m)
    e_n = jnp.exp(s_n - m)
    den = jnp.sum(e_c, axis=-1, keepdims=True) + jnp.sum(e_n, axis=-1, keepdims=True)
    if sink is not None:
        den = den + jnp.exp(sink - m)
    r = (lax.dot_general(e_c.astype(jnp.bfloat16), vt_c, _NT, preferred_element_type=jnp.float32)
         + lax.dot_general(e_n.astype(jnp.bfloat16), vt_n, _NT, preferred_element_type=jnp.float32))
    return r / den, m, den


def _advance(old_t, new_t):
    n = old_t.shape[1]
    lane = lax.broadcasted_iota(jnp.int32, (1, LANES), 1)
    shifted = pltpu.roll(old_t, n - N_NEW, axis=1)
    last = jnp.where(lane < LANES - N_NEW, shifted[:, n - LANES:], new_t)
    if n == LANES:
        return last
    return jnp.concatenate([shifted[:, :n - LANES], last], axis=1)


def _attn_sample_body(bt, sink_ref, qa_ref, qb_ref, qc_ref, nka_ref, nva_ref, nkb_ref, nvb_ref,
                      cak_ref, cav_ref, cbk_ref, cbv_ref, cmk_ref, cmv_ref,
                      oac_ref, ob_ref, oak_ref, oav_ref, obk_ref, obv_ref):
    f32, bf16 = jnp.float32, jnp.bfloat16
    lo, hi = _half_masks()
    halves = (lo, hi)
    S = SROWS
    la = cak_ref.shape[2]
    lb = cbk_ref.shape[2]
    wa = SWA_Q_HEADS * HEAD_DIM
    new0 = LANES - N_NEW

    na = SWA_Q_HEADS * S
    ia = lax.broadcasted_iota(jnp.int32, (na, la), 0) & (S - 1)
    valid_ac = lax.broadcasted_iota(jnp.int32, (na, la), 1) >= ia + 1
    ja = lax.broadcasted_iota(jnp.int32, (na, LANES), 1) - new0
    valid_an = (ja >= 0) & (ja <= (lax.broadcasted_iota(jnp.int32, (na, LANES), 0) & (S - 1)))
    rcol = lax.broadcasted_iota(jnp.int32, (na, 1), 0)
    sink_col = jnp.zeros((na, 1), f32)
    for h in range(SWA_Q_HEADS):
        sink_col = jnp.where((rcol >> 3) == h, sink_ref[h], sink_col)

    nb_rows = len(DIL_PAIRS) * 2 * S
    rb = lax.broadcasted_iota(jnp.int32, (nb_rows, lb), 0)
    t_c = lb + (rb & (S - 1)) - lax.broadcasted_iota(jnp.int32, (nb_rows, lb), 1)
    rn = lax.broadcasted_iota(jnp.int32, (nb_rows, LANES), 0)
    jn = lax.broadcasted_iota(jnp.int32, (nb_rows, LANES), 1) - new0
    t_n = (rn & (S - 1)) - jn
    valid_bc = jnp.zeros((nb_rows, lb), jnp.bool_)
    valid_bn = jnp.zeros((nb_rows, LANES), jnp.bool_)
    for g, (window, dil) in enumerate(DIL_PAIRS):
        valid_bc = valid_bc | (((rb >> 4) == g) & (t_c <= window) & ((t_c & (dil - 1)) == 0))
        valid_bn = valid_bn | (((rn >> 4) == g) & (jn >= 0) & (t_n >= 0) & ((t_n & (dil - 1)) == 0))

    for b in range(bt):
        rows = slice(b * S, (b + 1) * S)
        oak_ref[b] = _advance(cak_ref[b], nka_ref[b])
        oav_ref[b] = _advance(cav_ref[b], nva_ref[b])
        obk_ref[b] = _advance(cbk_ref[b], nkb_ref[b])
        obv_ref[b] = _advance(cbv_ref[b], nvb_ref[b])

        pieces = []
        for h in range(SWA_Q_HEADS):
            q = jnp.where(halves[h % 2], qa_ref[rows, (h // 2) * LANES:(h // 2 + 1) * LANES].astype(f32), 0.0)
            if h % 2 != h // 4:
                q = pltpu.roll(q, HEAD_DIM, axis=1)
            pieces.append(q)
        qm = jnp.concatenate(pieces, axis=0).astype(bf16)
        s_c = jnp.dot(qm, cak_ref[b].astype(bf16), preferred_element_type=f32)
        s_n = jnp.dot(qm, nka_ref[b].astype(bf16), preferred_element_type=f32)
        out, _, _ = _softmax2_pv(jnp.where(valid_ac, s_c, NEG), jnp.where(valid_an, s_n, NEG),
                                 cav_ref[b].astype(bf16), nva_ref[b].astype(bf16), sink_col)
        for j in range(SWA_Q_HEADS // 2):
            parts = []
            for p in range(2):
                h = 2 * j + p
                o = out[h * S:(h + 1) * S]
                if h % 2 != h // 4:
                    o = pltpu.roll(o, HEAD_DIM, axis=1)
                parts.append(o)
            oac_ref[rows, j * LANES:(j + 1) * LANES] = jnp.where(lo, parts[0], parts[1])

        for j in range(MEM_HEADS // 2):
            cols = slice(j * LANES, (j + 1) * LANES)
            q = qc_ref[rows, cols]
            qm = jnp.concatenate([jnp.where(halves[p], q, jnp.zeros_like(q)) for p in range(2)], axis=0)
            s = jnp.dot(qm, cmk_ref[b, cols, :].astype(bf16), preferred_element_type=f32)
            m = jnp.max(s, axis=-1, keepdims=True)
            e = jnp.exp(s - m)
            r = lax.dot_general(e.astype(bf16), cmv_ref[b, cols, :].astype(bf16), _NT, preferred_element_type=f32)
            out = r / jnp.sum(e, axis=-1, keepdims=True)
            oac_ref[rows, wa + j * LANES:wa + (j + 1) * LANES] = jnp.where(lo, out[:S], out[S:])

        for hp in range(DIL_KV_HEADS // 2):
            cols = slice(hp * LANES, (hp + 1) * LANES)
            pieces = [jnp.where(halves[p], qb_ref[hp * len(DIL_PAIRS) + g, rows, :], 0.0)
                      for g in range(len(DIL_PAIRS)) for p in range(2)]
            qm = jnp.concatenate(pieces, axis=0).astype(bf16)
            s_c = jnp.dot(qm, cbk_ref[b, cols, :].astype(bf16), preferred_element_type=f32)
            s_n = jnp.dot(qm, nkb_ref[b, cols, :].astype(bf16), preferred_element_type=f32)
            out, m, den = _softmax2_pv(jnp.where(valid_bc, s_c, NEG), jnp.where(valid_bn, s_n, NEG),
                                       cbv_ref[b, cols, :].astype(bf16), nvb_ref[b, cols, :].astype(bf16))
            lse = m + jnp.log(den)
            res = []
            for p in range(2):
                r = [slice((g * 2 + p) * S, (g * 2 + p + 1) * S) for g in range(len(DIL_PAIRS))]
                mx = jnp.maximum(jnp.maximum(lse[r[0]], lse[r[1]]), lse[r[2]])
                w = [jnp.exp(lse[x] - mx) for x in r]
                tot = w[0] + w[1] + w[2]
                res.append(sum((w[g] / tot) * out[r[g]] for g in range(len(DIL_PAIRS))))
            ob_ref[rows, cols] = jnp.where(lo, res[0], res[1])


def attn_sample(sinks, qa, qb, qc, nka, nva, nkb, nvb, cak, cav, cbk, cbv, cmk, cmv, bt):
    NB, wka, la = cak.shape
    wkb, lb = cbk.shape[1:]
    wm, M = cmk.shape[1:]
    wa, wc = SWA_Q_HEADS * HEAD_DIM, MEM_HEADS * HEAD_DIM
    ng = len(DIL_PAIRS)
    tok = lambda w: pl.BlockSpec((bt * SROWS, w), lambda i: (i, 0))
    buf = lambda f, n: pl.BlockSpec((bt, f, n), lambda i: (i, 0, 0))
    f32 = jnp.float32
    return pl.pallas_call(
        functools.partial(_attn_sample_body, bt),
        grid=(NB // bt,),
        in_specs=[pl.BlockSpec(memory_space=pltpu.SMEM), tok(wa),
                  pl.BlockSpec((2 * ng, bt * SROWS, LANES), lambda i: (0, i, 0)), tok(wc),
                  buf(wka, LANES), buf(wka, LANES), buf(wkb, LANES), buf(wkb, LANES),
                  buf(wka, la), buf(wka, la), buf(wkb, lb), buf(wkb, lb), buf(wm, M), buf(wm, M)],
        out_specs=[tok(wa + wc), tok(wkb), buf(wka, la), buf(wka, la), buf(wkb, lb), buf(wkb, lb)],
        out_shape=[jax.ShapeDtypeStruct((NB * SROWS, wa + wc), f32), jax.ShapeDtypeStruct((NB * SROWS, wkb), f32),
                   jax.ShapeDtypeStruct(cak.shape, f32), jax.ShapeDtypeStruct(cak.shape, f32),
                   jax.ShapeDtypeStruct(cbk.shape, f32), jax.ShapeDtypeStruct(cbk.shape, f32)],
        compiler_params=_cparams(("parallel",)), name="attn_sample")(
            sinks, qa, qb, qc, nka, nva, nkb, nvb, cak, cav, cbk, cbv, cmk, cmv)


MOE_BLOCK = 256
ROUTE_SUB = 256
SUBLANES = 8


def _store_row_tiles(ref, row0, y):
    n = y.shape[0]
    for c in range(SUBLANES):
        ref[pl.ds(row0 * SUBLANES + c, n, stride=SUBLANES), :] = y[:, c * LANES:(c + 1) * LANES]


def _load_row_tiles(ref, row0, n):
    return jnp.concatenate([ref[pl.ds(row0 * SUBLANES + c, n, stride=SUBLANES), :] for c in range(SUBLANES)],
                           axis=1)


def _row_tile(ref, r):
    return ref.at[pl.ds(pl.multiple_of(r * SUBLANES, SUBLANES), SUBLANES)]


def _merge_route_body(oac_ref, ob_ref, gate_ref, x_ref, wa_ref, wb_ref, wc_ref, wo_ref, gffn_ref, wr_ref, br_ref,
                      cnt0_ref, x1_ref, h2_ref, mi_ref, mf_ref, cnt_ref, base_ref):
    f32, bf16 = jnp.float32, jnp.bfloat16
    D = x_ref.shape[1]
    wa = SWA_Q_HEADS * HEAD_DIM
    ts = ROUTE_SUB

    @pl.when(pl.program_id(0) == 0)
    def _():
        base_ref[...] = cnt0_ref[...]

    erow = lax.broadcasted_iota(jnp.int32, (N_EXPERTS, ts), 0)
    r8 = lax.broadcasted_iota(jnp.int32, (SUBLANES, ts), 0)
    ti = lax.broadcasted_iota(jnp.int32, (ts, ts), 0)
    tj = lax.broadcasted_iota(jnp.int32, (ts, ts), 1)
    later = (ti < tj).astype(bf16)
    base = base_ref[:, 0:1]
    for sub in range(x_ref.shape[0] // ts):
        rows = slice(sub * ts, (sub + 1) * ts)
        ma = jnp.dot(oac_ref[rows, :wa].astype(bf16), wa_ref[...], preferred_element_type=f32)
        mb = jnp.dot(ob_ref[rows, :].astype(bf16), wb_ref[...], preferred_element_type=f32)
        mc = jnp.dot(oac_ref[rows, wa:].astype(bf16), wc_ref[...], preferred_element_type=f32)
        merged = (gate_ref[rows, :D].astype(f32) * ma + gate_ref[rows, D:2 * D].astype(f32) * mb
                  + gate_ref[rows, 2 * D:].astype(f32) * mc)
        x1 = x_ref[rows, :] + jnp.dot(merged.astype(bf16), wo_ref[...], preferred_element_type=f32)
        x1_ref[rows, :] = x1
        h2 = _rms(x1, gffn_ref[...])
        _store_row_tiles(h2_ref, sub * ts, h2)

        work = lax.dot_general(wr_ref[...], h2.astype(bf16), _NT, preferred_element_type=f32) + br_ref[:, 0:1]
        vals, idxs = [], []
        for _ in range(TOP_K):
            m = jnp.max(work, axis=0, keepdims=True)
            idx = jnp.min(jnp.where(work == m, erow, N_EXPERTS), axis=0, keepdims=True)
            vals.append(m)
            idxs.append(idx)
            work = jnp.where(erow == idx, -jnp.inf, work)
        es = [jnp.exp(v - vals[0]) for v in vals]
        tot = es[0] + es[1] + es[2] + es[3]

        onehot = [(erow == idx).astype(f32) for idx in idxs]
        assign = onehot[0] + onehot[1] + onehot[2] + onehot[3]
        before = jnp.dot(assign.astype(bf16), later, preferred_element_type=f32) + base
        base = base + jnp.sum(assign, axis=1, keepdims=True)

        mi = jnp.zeros((SUBLANES, ts), jnp.int32)
        gates = jnp.zeros((SUBLANES, ts), f32)
        for k in range(TOP_K):
            rank = jnp.sum(onehot[k] * before, axis=0, keepdims=True).astype(jnp.int32)
            mi = jnp.where(r8 == k, idxs[k], mi)
            mi = jnp.where(r8 == TOP_K + k, rank, mi)
            gates = jnp.where(r8 == k, es[k] / tot, gates)
        mi_ref[:, rows] = mi
        mf_ref[rows, :] = jnp.concatenate([gates, jnp.zeros((LANES - SUBLANES, ts), f32)], axis=0).T
    base_ref[...] = jnp.broadcast_to(base, base_ref.shape)
    cnt_ref[...] = jnp.broadcast_to(base, cnt_ref.shape)


def merge_route(o_ac, o_b, gates, x, wa, wb, wc, wo, g_ffn, wr, br, cnt0, tm):
    R, D = x.shape
    row = lambda w: pl.BlockSpec((tm, w), lambda i: (i, 0))
    full = lambda a: pl.BlockSpec(a.shape, lambda i: (0, 0))
    return pl.pallas_call(
        _merge_route_body,
        grid=(R // tm,),
        in_specs=[row(o_ac.shape[1]), row(o_b.shape[1]), row(gates.shape[1]), row(D),
                  full(wa), full(wb), full(wc), full(wo), full(g_ffn), full(wr), full(br), full(cnt0)],
        out_specs=[row(D), pl.BlockSpec((tm * SUBLANES, LANES), lambda i: (i, 0)),
                   pl.BlockSpec((SUBLANES, tm), lambda i: (0, i)), row(LANES),
                   pl.BlockSpec((N_EXPERTS, LANES), lambda i: (0, 0))],
        out_shape=[jax.ShapeDtypeStruct((R, D), jnp.float32), jax.ShapeDtypeStruct((R * SUBLANES, LANES), jnp.float32),
                   jax.ShapeDtypeStruct((SUBLANES, R), jnp.int32), jax.ShapeDtypeStruct((R, LANES), jnp.float32),
                   jax.ShapeDtypeStruct((N_EXPERTS, LANES), jnp.float32)],
        scratch_shapes=[pltpu.VMEM((N_EXPERTS, LANES), jnp.float32)],
        compiler_params=_cparams(("arbitrary",)), name="merge_route")(
            o_ac, o_b, gates, x, wa, wb, wc, wo, g_ffn, wr, br, cnt0)


def _route_tables_body(cnt_ref, mi_ref, dest_ref, blk_ref, pad_ref):
    tm = mi_ref.shape[1]
    nbl = blk_ref.shape[1]
    erow1 = lax.broadcasted_iota(jnp.int32, (N_EXPERTS, LANES), 0)
    shift = MOE_BLOCK.bit_length() - 1
    cnt = cnt_ref[...].astype(jnp.int32)
    padded = ((cnt + (MOE_BLOCK - 1)) >> shift) << shift
    pend = padded
    s = 1
    while s < N_EXPERTS:
        pend = pend + jnp.where(erow1 >= s, pltpu.roll(pend, s, axis=0), 0)
        s *= 2
    pstart = pend - padded
    mi = mi_ref[...]
    erow = lax.broadcasted_iota(jnp.int32, (N_EXPERTS, tm), 0)
    r8 = lax.broadcasted_iota(jnp.int32, (SUBLANES, tm), 0)
    dest = jnp.zeros((SUBLANES, tm), jnp.int32)
    for k in range(TOP_K):
        start = jnp.sum(jnp.where(erow == mi[k:k + 1, :], pstart[:, 0:1], 0), axis=0, keepdims=True)
        dest = jnp.where(r8 == k, start + mi[TOP_K + k:TOP_K + k + 1, :], dest)
    dest_ref[...] = dest

    @pl.when(pl.program_id(0) == 0)
    def _():
        row0 = lax.broadcasted_iota(jnp.int32, (N_EXPERTS, nbl), 1) * MOE_BLOCK
        ended = jnp.sum(jnp.where(pend[:, 0:1] <= row0, 1, 0), axis=0, keepdims=True)
        blk_ref[...] = jnp.broadcast_to(jnp.minimum(ended, N_EXPERTS - 1), blk_ref.shape)
        pad_ref[0] = pstart + cnt
        pad_ref[1] = pend


def route_tables(cnt, mi, nbl, tm):
    R = mi.shape[1]
    return pl.pallas_call(
        _route_tables_body,
        grid=(R // tm,),
        in_specs=[pl.BlockSpec((N_EXPERTS, LANES), lambda i: (0, 0)), pl.BlockSpec((SUBLANES, tm), lambda i: (0, i))],
        out_specs=[pl.BlockSpec((SUBLANES, tm), lambda i: (0, i)), pl.BlockSpec((SUBLANES, nbl), lambda i: (0, 0)),
                   pl.BlockSpec((2, N_EXPERTS, LANES), lambda i: (0, 0, 0))],
        out_shape=[jax.ShapeDtypeStruct((SUBLANES, R), jnp.int32), jax.ShapeDtypeStruct((SUBLANES, nbl), jnp.int32),
                   jax.ShapeDtypeStruct((2, N_EXPERTS, LANES), jnp.int32)],
        compiler_params=_cparams(("arbitrary",)), name="route_tables")(cnt, mi)


def _dispatch_body(n_first, n_zero, pad0_ref, pad1_ref, dest_ref, ha_ref, hb_ref, xs_ref, zero_ref, sem):
    i = pl.program_id(0)
    tm = ha_ref.shape[0] // SUBLANES
    rs = xs_ref.shape[0] // SUBLANES

    def scatter_tile(h_ref):
        def row_copy(r, k):
            return pltpu.make_async_copy(_row_tile(h_ref, r), _row_tile(xs_ref, dest_ref[0, 0, k * tm + r]), sem)

        def start(r, c):
            for k in range(TOP_K):
                row_copy(r, k).start(priority=k % 2)
            return c

        def wait(r, c):
            for k in range(TOP_K):
                row_copy(r, k).wait()
            return c

        lax.fori_loop(0, tm, start, 0, unroll=8)
        lax.fori_loop(0, tm, wait, 0, unroll=8)

    @pl.when(i < n_first)
    def _():
        scatter_tile(ha_ref)

    @pl.when(i >= n_first)
    def _():
        scatter_tile(hb_ref)

    @pl.when(i == pl.num_programs(0) - 1)
    def _():
        zero_ref[...] = jnp.zeros_like(zero_ref)

        def zero_copy(r):
            return pltpu.make_async_copy(zero_ref, _row_tile(xs_ref, r), sem)

        def start(r, c):
            zero_copy(r).start()
            return c

        def wait(r, c):
            zero_copy(0).wait()
            return c

        for e in range(N_EXPERTS):
            lax.fori_loop(pad0_ref[e], pad1_ref[e], start, 0)
        lax.fori_loop(pad1_ref[N_EXPERTS - 1], rs, start, 0)
        lax.fori_loop(0, n_zero, wait, 0)


def dispatch(pad0, pad1, dest3, h_a, h_b, rs, tm):
    S = SUBLANES
    n_a, n_b = h_a.shape[0] // (tm * S), h_b.shape[0] // (tm * S)
    n_zero = rs - (h_a.shape[0] + h_b.shape[0]) // S * TOP_K
    grid_spec = pltpu.PrefetchScalarGridSpec(
        num_scalar_prefetch=2,
        grid=(n_a + n_b,),
        in_specs=[pl.BlockSpec((1, 1, TOP_K * tm), lambda i, p0, p1: (i, 0, 0), memory_space=pltpu.SMEM),
                  pl.BlockSpec((tm * S, LANES), lambda i, p0, p1: (jnp.minimum(i, n_a - 1), 0)),
                  pl.BlockSpec((tm * S, LANES), lambda i, p0, p1: (jnp.maximum(i - n_a, 0), 0))],
        out_specs=pl.BlockSpec(memory_space=pl.ANY),
        scratch_shapes=[pltpu.VMEM((S, LANES), h_a.dtype), pltpu.SemaphoreType.DMA])
    return pl.pallas_call(
        functools.partial(_dispatch_body, n_a, n_zero), grid_spec=grid_spec,
        out_shape=jax.ShapeDtypeStruct((rs * S, LANES), h_a.dtype),
        compiler_params=_cparams(("arbitrary",)), name="dispatch")(pad0, pad1, dest3, h_a, h_b)


def _moe_body(be_ref, x_ref, wgu_hbm, bgu_ref, wd_hbm, bd_ref, y_ref, wgu_f, wd_f, wgu_s, wd_s, sems):
    f32, bf16 = jnp.float32, jnp.bfloat16
    b = pl.program_id(0)
    nblk = pl.num_programs(0)
    e = be_ref[b]

    def fetch(ex, go):
        for src, dst, s in ((wgu_hbm, wgu_f, 0), (wd_hbm, wd_f, 1)):
            cp = pltpu.make_async_copy(src.at[ex], dst, sems.at[s])
            cp.start() if go else cp.wait()

    @pl.when(b == 0)
    def _():
        fetch(e, True)

    @pl.when((b == 0) | (e != be_ref[jnp.maximum(b - 1, 0)]))
    def _():
        fetch(e, False)
        wgu_s[...] = wgu_f[...].astype(bf16)
        wd_s[...] = wd_f[...].astype(bf16)
        nxt = lax.while_loop(lambda j: (j < nblk) & (be_ref[jnp.minimum(j, nblk - 1)] == e), lambda j: j + 1, b + 1)

        @pl.when(nxt < nblk)
        def _():
            fetch(be_ref[jnp.minimum(nxt, nblk - 1)], True)

    x = _load_row_tiles(x_ref, 0, MOE_BLOCK).astype(bf16)
    gu = jnp.dot(x, wgu_s[...], preferred_element_type=f32) + bgu_ref[0]
    gt = jnp.minimum(gu[:, :D_FF], SWIGLU_LIMIT)
    up = jnp.clip(gu[:, D_FF:], -SWIGLU_LIMIT, SWIGLU_LIMIT)
    act = (up + 1.0) * (gt * jax.nn.sigmoid(gt * SWIGLU_ALPHA))
    y = jnp.dot(act.astype(bf16), wd_s[...], preferred_element_type=f32) + bd_ref[0]
    _store_row_tiles(y_ref, 0, y)


def moe_ffn(blk_e, xs, w_gate_up, b_gate_up, w_down, b_down):
    RS = xs.shape[0] // SUBLANES
    E, D, F2 = w_gate_up.shape
    blk_rows = MOE_BLOCK * SUBLANES
    grid_spec = pltpu.PrefetchScalarGridSpec(
        num_scalar_prefetch=1,
        grid=(RS // MOE_BLOCK,),
        in_specs=[pl.BlockSpec((blk_rows, LANES), lambda b, be: (b, 0)),
                  pl.BlockSpec(memory_space=pl.ANY),
                  pl.BlockSpec((1, 1, F2), lambda b, be: (be[b], 0, 0)),
                  pl.BlockSpec(memory_space=pl.ANY),
                  pl.BlockSpec((1, 1, D), lambda b, be: (be[b], 0, 0))],
        out_specs=pl.BlockSpec((blk_rows, LANES), lambda b, be: (b, 0)),
        scratch_shapes=[pltpu.VMEM((D, F2), w_gate_up.dtype), pltpu.VMEM((F2 // 2, D), w_down.dtype),
                        pltpu.VMEM((D, F2), jnp.bfloat16), pltpu.VMEM((F2 // 2, D), jnp.bfloat16),
                        pltpu.SemaphoreType.DMA((2,))])
    return pl.pallas_call(
        _moe_body, grid_spec=grid_spec,
        out_shape=jax.ShapeDtypeStruct(xs.shape, jnp.float32),
        compiler_params=_cparams(("arbitrary",)), name="moe_ffn")(
            blk_e, xs, w_gate_up, b_gate_up.reshape(E, 1, F2), w_down, b_down.reshape(E, 1, D))


def _combine_body(dest_ref, dest_next_ref, x1_ref, mf_ref, g_ref, ys_ref, o_ref, buf, sems):
    tm = x1_ref.shape[0]
    i = pl.program_id(0)
    slot = i % 2
    per_slot = TOP_K * tm

    def gather(d_ref, s, go):
        def row_copy(r, k):
            return pltpu.make_async_copy(_row_tile(ys_ref, d_ref[0, 0, k * tm + r]),
                                         _row_tile(buf, s * per_slot + k * tm + r), sems.at[s])

        def body(r, c):
            for k in range(TOP_K):
                if go:
                    row_copy(r, k).start(priority=k % 2)
                else:
                    row_copy(r, k).wait()
            return c

        lax.fori_loop(0, tm, body, 0, unroll=8)

    @pl.when(i == 0)
    def _():
        gather(dest_ref, slot, True)

    @pl.when(i + 1 < pl.num_programs(0))
    def _():
        gather(dest_next_ref, 1 - slot, True)

    gather(dest_ref, slot, False)
    y = x1_ref[...]
    for k in range(TOP_K):
        y = y + mf_ref[:, k:k + 1] * _load_row_tiles(buf, slot * per_slot + k * tm, tm)
    o_ref[...] = _rms(y, g_ref[...])


def combine(dest3, x1, mf, g_final, ys, tm):
    R, D = x1.shape
    n = R // tm
    dest_spec = lambda f: pl.BlockSpec((1, 1, TOP_K * tm), f, memory_space=pltpu.SMEM)
    return pl.pallas_call(
        _combine_body,
        grid=(n,),
        in_specs=[dest_spec(lambda i: (i, 0, 0)), dest_spec(lambda i: (jnp.minimum(i + 1, n - 1), 0, 0)),
                  pl.BlockSpec((tm, D), lambda i: (i, 0)),
                  pl.BlockSpec((tm, LANES), lambda i: (i, 0)),
                  pl.BlockSpec((1, D), lambda i: (0, 0)),
                  pl.BlockSpec(memory_space=pl.ANY)],
        out_specs=pl.BlockSpec((tm, D), lambda i: (i, 0)),
        out_shape=jax.ShapeDtypeStruct((R, D), jnp.float32),
        scratch_shapes=[pltpu.VMEM((2 * TOP_K * tm * SUBLANES, LANES), ys.dtype), pltpu.SemaphoreType.DMA((2,))],
        compiler_params=_cparams(("arbitrary",)), name="combine")(dest3, dest3, x1, mf, g_final, ys)


def moe_layer(group_a, group_b, cnt, g_final, w_gate_up, b_gate_up, w_down, b_down, tm):
    n_assign = (group_a[0].shape[0] + group_b[0].shape[0]) * TOP_K
    nb = (n_assign + N_EXPERTS * (MOE_BLOCK - 1)) // MOE_BLOCK + 1
    nbl = -(-nb // LANES) * LANES
    dests = []
    for x1, h2, mi, mf in (group_a, group_b):
        dest, blk, pad = route_tables(cnt, mi, nbl, min(x1.shape[0], 4 * tm))
        n_tiles = x1.shape[0] // tm
        dests.append(jnp.transpose(dest[:TOP_K].reshape(TOP_K, n_tiles, tm), (1, 0, 2)).reshape(n_tiles, 1, TOP_K * tm))
    xs = dispatch(pad[0, :, 0], pad[1, :, 0], jnp.concatenate(dests), group_a[1], group_b[1], nb * MOE_BLOCK, tm)
    ys = moe_ffn(blk[0, :nb], xs, w_gate_up, b_gate_up, w_down, b_down)
    return [combine(dest3, x1, mf, g_final, ys, tm) for dest3, (x1, h2, mi, mf) in zip(dests, (group_a, group_b))]


def kernel(x_prompt, x_sample, cache_swa_k, cache_swa_v, cache_dil_k, cache_dil_v, cache_mem_k, cache_mem_v, mem_prompt, norm_attn, norm_mem, w_in, w_mem_kv, sinks, w_br_a, w_br_b, w_br_c, w_out, norm_ffn, w_router, b_router, w_gate_up, b_gate_up, w_down, b_down, norm_final):
    f32, bf16 = jnp.float32, jnp.bfloat16
    TM = 256
    B, L, D = x_prompt.shape
    NB, n_new, _ = x_sample.shape
    M = mem_prompt.shape[1]
    la, lb = cache_swa_k.shape[2], cache_dil_k.shape[2]
    wka, wkb, wc = SWA_KV_HEADS * HEAD_DIM, DIL_KV_HEADS * HEAD_DIM, MEM_HEADS * HEAD_DIM
    assert n_new == N_NEW and cache_swa_k.shape[0] == 1

    w_in_b = w_in[0].astype(bf16)
    g_attn = norm_attn[0].reshape(1, D)
    secs = in_sections()

    tabs_p = rope_tables(jnp.arange(L, dtype=jnp.int32))
    qa, ka, va, qb, kb, vb, qc, gates = norm_proj(x_prompt.reshape(B * L, D), g_attn, w_in_b, tabs_p, secs, TM)
    mk, mv = norm_proj(mem_prompt.reshape(B * M, D), norm_mem[0].reshape(1, D), w_mem_kv[0].astype(bf16), None,
                       [(_chunks(0, wc), "plain", False, f32), (_chunks(wc, wc), "plain", False, f32)], TM)
    o_ac = attn_swa_mem(sinks[0], qa, ka, va, qc, mk, mv, B, L)
    o_b, kb_t, vb_t = attn_dilated(qb, kb, vb, B, L)

    xs_pad = jnp.pad(x_sample, ((0, 0), (0, SROWS - N_NEW), (0, 0))).reshape(NB * SROWS, D)
    tabs_s = rope_tables(PAST_LEN + (jnp.arange(TM, dtype=jnp.int32) % SROWS))
    qa_s, ka_s, va_s, qb_s, kb_s, vb_s, qc_s, gates_s = norm_proj(xs_pad, g_attn, w_in_b, tabs_s, secs, TM)
    real = lambda t: t.reshape(NB, SROWS, -1)[:, :N_NEW].reshape(NB * N_NEW, -1)
    fmaj = lambda c: jnp.transpose(c[0], (0, 2, 3, 1)).reshape(NB, -1, c.shape[2])
    new_t = lambda t: jnp.pad(jnp.transpose(real(t).reshape(NB, N_NEW, -1), (0, 2, 1)),
                              ((0, 0), (0, 0), (LANES - N_NEW, 0)))
    o_ac_s, o_b_s, swa_k_s, swa_v_s, dil_k_s, dil_v_s = attn_sample(
        sinks[0], qa_s, qb_s, qc_s, new_t(ka_s), new_t(va_s), new_t(kb_s), new_t(vb_s),
        fmaj(cache_swa_k), fmaj(cache_swa_v), fmaj(cache_dil_k), fmaj(cache_dil_v),
        fmaj(cache_mem_k), fmaj(cache_mem_v), 2)

    wr = w_router[0].T.astype(bf16)
    br = jnp.broadcast_to(b_router[0].astype(f32)[:, None], (N_EXPERTS, LANES))
    wts = (w_br_a[0].astype(bf16), w_br_b[0].astype(bf16), w_br_c[0].astype(bf16), w_out[0].astype(bf16),
           norm_ffn[0].reshape(1, D), wr, br)
    x1_p, h2_p, mi_p, mf_p, cnt_p = merge_route(o_ac, o_b, gates, x_prompt.reshape(B * L, D), *wts,
                                                jnp.zeros((N_EXPERTS, LANES), f32), 2 * ROUTE_SUB)
    x1_s, h2_s, mi_s, mf_s, cnt = merge_route(real(o_ac_s), real(o_b_s), real(gates_s),
                                              x_sample.reshape(NB * N_NEW, D), *wts, cnt_p, 2 * ROUTE_SUB)
    y_p, y_s = moe_layer((x1_p, h2_p, mi_p, mf_p), (x1_s, h2_s, mi_s, mf_s), cnt, norm_final.reshape(1, D),
                         w_gate_up[0], b_gate_up[0], w_down[0], b_down[0], TM)

    heads = lambda t, n, h: t.reshape(1, t.shape[0] // n, n, h, HEAD_DIM)
    tmaj = lambda t, h: jnp.transpose(t.reshape(t.shape[0], h, HEAD_DIM, t.shape[2]), (0, 3, 1, 2))[None]
    ka3, va3 = ka.reshape(B, L, wka), va.reshape(B, L, wka)
    la_p = min(SWA_WINDOW, L)
    return (y_p.reshape(B, L, D), y_s.reshape(NB, N_NEW, D),
            heads(ka3[:, L - la_p:].reshape(B * la_p, wka), la_p, SWA_KV_HEADS),
            heads(va3[:, L - la_p:].reshape(B * la_p, wka), la_p, SWA_KV_HEADS),
            tmaj(kb_t, DIL_KV_HEADS), tmaj(vb_t, DIL_KV_HEADS),
            heads(mk, M, MEM_HEADS), heads(mv, M, MEM_HEADS),
            tmaj(swa_k_s, SWA_KV_HEADS), tmaj(swa_v_s, SWA_KV_HEADS),
            tmaj(dil_k_s, DIL_KV_HEADS), tmaj(dil_v_s, DIL_KV_HEADS))
```

```python
import functools

import jax
import jax.numpy as jnp
import numpy as np
from jax import lax
from jax.experimental import pallas as pl
from jax.experimental.pallas import tpu as pltpu

D_MODEL = 1024
HEAD_DIM = 64
ROPE_DIM = 16
ROPE_HALF = 8
ROPE_THETA = 500000.0
PAST_LEN = 16384
SWA_Q_HEADS = 8
SWA_KV_HEADS = 2
SWA_WINDOW = 128
DIL_PAIRS = ((128, 1), (512, 4), (2048, 16))
DIL_KV_HEADS = 4
MEM_HEADS = 4
N_EXPERTS = 32
TOP_K = 4
D_FF = 1024
SWIGLU_LIMIT = 7.0
SWIGLU_ALPHA = 1.702
RMS_EPS = 1e-5
ATT_BLOCK = 128
SCALE = HEAD_DIM ** -0.5

LANES = 128
NEG = -1e30
VMEM_LIMIT = 56 * 1024 * 1024


def _cparams(sem):
    return pltpu.CompilerParams(dimension_semantics=sem, vmem_limit_bytes=VMEM_LIMIT)


def _rms(x, g):
    return x * lax.rsqrt(jnp.mean(x * x, axis=-1, keepdims=True) + RMS_EPS) * g


def _norm_proj_body(sections, x_ref, g_ref, w_ref, cs_ref, *out_refs):
    h = _rms(x_ref[...], g_ref[...]).astype(jnp.bfloat16)
    if cs_ref is not None:
        cos = cs_ref[0]
        sin_lo = cs_ref[1]
        sin_hi = cs_ref[2]
    dest = {lo: (o_ref, c, kind, slabs)
            for (cols, kind, slabs), o_ref in zip(sections, out_refs) for c, lo in enumerate(cols)}
    todo = sorted(dest)
    while todo:
        lo = todo.pop(0)
        n = 2 if todo and todo[0] == lo + LANES else 1
        if n == 2:
            todo.pop(0)
        yy = jnp.dot(h, w_ref[:, lo:lo + n * LANES], preferred_element_type=jnp.float32)
        for part in range(n):
            o_ref, c, kind, slabs = dest[lo + part * LANES]
            y = yy[:, part * LANES:(part + 1) * LANES]
            if kind in ("rope", "rope_q"):
                y = (y * cos + pltpu.roll(y, LANES - ROPE_HALF, axis=1) * sin_lo
                     + pltpu.roll(y, ROPE_HALF, axis=1) * sin_hi)
            if kind in ("rope_q", "q"):
                y = y * SCALE
            if kind == "sigmoid":
                y = jax.nn.sigmoid(y)
            if slabs:
                o_ref[c] = y.astype(o_ref.dtype)
            else:
                o_ref[:, c * LANES:(c + 1) * LANES] = y.astype(o_ref.dtype)


def rope_tables(pos):
    inv_freq = ROPE_THETA ** (-jnp.arange(ROPE_HALF, dtype=jnp.float32) / ROPE_HALF)
    ang = pos.astype(jnp.float32)[:, None] * inv_freq[None, :]
    cos, sin = jnp.cos(ang), jnp.sin(ang)
    n = pos.shape[0]
    one = jnp.ones((n, HEAD_DIM - ROPE_DIM), jnp.float32)
    zero = jnp.zeros((n, HEAD_DIM - ROPE_HALF), jnp.float32)
    c = jnp.concatenate([cos, cos, one], axis=1)
    s_lo = jnp.concatenate([-sin, zero], axis=1)
    s_hi = jnp.concatenate([jnp.zeros((n, ROPE_HALF), jnp.float32), sin,
                            jnp.zeros((n, HEAD_DIM - ROPE_DIM), jnp.float32)], axis=1)
    tab = jnp.stack([c, s_lo, s_hi])
    return jnp.concatenate([tab, tab], axis=2)


def norm_proj(x, g, w, tables, sections, tm):
    R, D = x.shape
    in_specs = [pl.BlockSpec((tm, D), lambda i: (i, 0)),
                pl.BlockSpec((1, D), lambda i: (0, 0)),
                pl.BlockSpec(w.shape, lambda i: (0, 0))]
    args = [x, g, w]
    if tables is not None:
        nt = tables.shape[1] // tm
        in_specs.append(pl.BlockSpec((3, tm, LANES), lambda i: (0, i % nt, 0)))
        args.append(tables)
    out_shape, out_specs, secs = [], [], []
    for (cols, kind, slabs, dtype) in sections:
        secs.append((cols, kind, slabs))
        width = LANES * len(cols)
        if slabs:
            out_shape.append(jax.ShapeDtypeStruct((width // LANES, R, LANES), dtype))
            out_specs.append(pl.BlockSpec((width // LANES, tm, LANES), lambda i: (0, i, 0)))
        else:
            out_shape.append(jax.ShapeDtypeStruct((R, width), dtype))
            out_specs.append(pl.BlockSpec((tm, width), lambda i: (i, 0)))
    if tables is None:
        body = lambda x_ref, g_ref, w_ref, *o: _norm_proj_body(secs, x_ref, g_ref, w_ref, None, *o)
    else:
        body = functools.partial(_norm_proj_body, secs)
    return pl.pallas_call(
        body, grid=(R // tm,), in_specs=in_specs, out_specs=out_specs, out_shape=out_shape,
        compiler_params=_cparams(("parallel",)), name="norm_proj")(*args)


def _chunks(start, width):
    return tuple(range(start, start + width, LANES))


def in_sections():
    f32, bf16 = jnp.float32, jnp.bfloat16
    qb0 = 1024 - 256
    qb_cols = tuple(qb0 + HEAD_DIM * (4 * g + 2 * hp) for hp in range(2) for g in range(3))
    return [
        (_chunks(0, 512), "rope_q", False, bf16),
        (_chunks(512, 128), "rope", False, f32),
        (_chunks(640, 128), "plain", False, f32),
        (qb_cols, "rope_q", True, f32),
        (_chunks(1536, 256), "rope", False, f32),
        (_chunks(1792, 256), "plain", False, f32),
        (_chunks(2048, 256), "q", False, bf16),
        (_chunks(2304, 3072), "sigmoid", False, bf16),
    ]


_NT = (((1,), (1,)), ((), ()))


def _half_masks():
    lane = lax.broadcasted_iota(jnp.int32, (1, LANES), 1)
    return lane < HEAD_DIM, lane >= HEAD_DIM


def _softmax_pv(s, v_half, sink=None):
    m = jnp.max(s, axis=-1, keepdims=True)
    if sink is not None:
        m = jnp.maximum(m, sink)
    e = jnp.exp(s - m)
    den = jnp.sum(e, axis=-1, keepdims=True)
    if sink is not None:
        den = den + jnp.exp(sink - m)
    r = jnp.dot(e.astype(jnp.bfloat16), v_half, preferred_element_type=jnp.float32)
    return r / den, m, den


def _attn_swa_mem_body(L, sink_ref, qa_ref, ka_ref, va_ref, qc_ref, mk_ref, mv_ref, o_ref):
    bf16 = jnp.bfloat16
    lo, hi = _half_masks()
    halves = (lo, hi)
    T = ATT_BLOCK
    G = SWA_Q_HEADS // SWA_KV_HEADS
    mem_k = [mk_ref[:, j * LANES:(j + 1) * LANES].astype(bf16) for j in range(2)]
    mem_v = [mv_ref[:, j * LANES:(j + 1) * LANES].astype(bf16) for j in range(2)]
    qi = lax.broadcasted_iota(jnp.int32, (2 * T, 2 * T), 0) & (T - 1)
    kj = lax.broadcasted_iota(jnp.int32, (2 * T, 2 * T), 1)
    hrow = lax.broadcasted_iota(jnp.int32, (2 * T, 1), 0) >> (T.bit_length() - 1)

    def heads_of(pair):
        return jnp.concatenate([jnp.where(halves[p], pair, jnp.zeros_like(pair)) for p in range(2)], axis=0)

    def block(blk, carry):
        r0 = pl.multiple_of(blk * T, T)
        ws = pl.multiple_of(jnp.maximum(r0 - T, 0), T)
        dist = qi - kj + (r0 - ws)
        valid = (dist >= 0) & (dist <= SWA_WINDOW - 1)
        k = ka_ref[pl.ds(ws, 2 * T), :]
        v = va_ref[pl.ds(ws, 2 * T), :]
        for kv in range(SWA_KV_HEADS):
            k1 = jnp.where(halves[kv], k, 0.0)
            v1 = jnp.where(halves[kv], v, 0.0)
            k_dup = (k1 + pltpu.roll(k1, HEAD_DIM, axis=1)).astype(bf16)
            v_dup = (v1 + pltpu.roll(v1, HEAD_DIM, axis=1)).astype(bf16)
            for j in range(kv * G // 2, (kv + 1) * G // 2):
                qm = heads_of(qa_ref[pl.ds(r0, T), j * LANES:(j + 1) * LANES])
                sink = jnp.where(hrow == 0, sink_ref[2 * j], sink_ref[2 * j + 1])
                s = lax.dot_general(qm, k_dup, _NT, preferred_element_type=jnp.float32)
                out, _, _ = _softmax_pv(jnp.where(valid, s, NEG), v_dup, sink)
                o_ref[pl.ds(r0, T), j * LANES:(j + 1) * LANES] = jnp.where(lo, out[:T], out[T:]).astype(o_ref.dtype)
        for j in range(MEM_HEADS // 2):
            qm = heads_of(qc_ref[pl.ds(r0, T), j * LANES:(j + 1) * LANES])
            s = lax.dot_general(qm, mem_k[j], _NT, preferred_element_type=jnp.float32)
            out, _, _ = _softmax_pv(s, mem_v[j])
            c0 = SWA_Q_HEADS * HEAD_DIM + j * LANES
            o_ref[pl.ds(r0, T), c0:c0 + LANES] = jnp.where(lo, out[:T], out[T:]).astype(o_ref.dtype)
        return carry

    lax.fori_loop(0, L // T, block, 0)


def attn_swa_mem(sinks, qa, ka, va, qc, mk, mv, B, L):
    M = mk.shape[0] // B
    wa, wc = SWA_Q_HEADS * HEAD_DIM, MEM_HEADS * HEAD_DIM
    row = lambda w: pl.BlockSpec((L, w), lambda b: (b, 0))
    return pl.pallas_call(
        functools.partial(_attn_swa_mem_body, L),
        grid=(B,),
        in_specs=[pl.BlockSpec(memory_space=pltpu.SMEM), row(wa), row(LANES), row(LANES), row(wc),
                  pl.BlockSpec((M, wc), lambda b: (b, 0)), pl.BlockSpec((M, wc), lambda b: (b, 0))],
        out_specs=row(wa + wc),
        out_shape=jax.ShapeDtypeStruct((B * L, wa + wc), jnp.bfloat16),
        compiler_params=_cparams(("parallel",)), name="attn_swa_mem")(sinks, qa, ka, va, qc, mk, mv)


def _attn_dil_body(L, qb_ref, kb_ref, vb_ref, o_ref, kt_ref, vt_ref, og_ref, lse_ref):
    bf16 = jnp.bfloat16
    lo, hi = _half_masks()
    halves = (lo, hi)
    T = ATT_BLOCK
    kt_ref[...] = kb_ref[...].T
    vt_ref[...] = vb_ref[...].T
    for g, (window, dil) in enumerate(DIL_PAIRS):
        lc = L // dil
        nbc = lc // T
        W = min(2 * T, lc)
        max_dist = window // dil
        qi = lax.broadcasted_iota(jnp.int32, (2 * T, W), 0) & (T - 1)
        kj = lax.broadcasted_iota(jnp.int32, (2 * T, W), 1)

        def unit(u, carry, g=g, dil=dil, nbc=nbc, W=W, max_dist=max_dist, qi=qi, kj=kj):
            c = u >> (nbc.bit_length() - 1)
            n = u & (nbc - 1)
            wsc = jnp.maximum(n * T - T, 0) if W == 2 * T else 0
            q0 = c + dil * T * n
            k0 = c + dil * wsc
            dist = qi - kj + (n * T - wsc)
            valid = (dist >= 0) & (dist <= max_dist)
            q = qb_ref[g, pl.ds(q0, T, stride=dil), :]
            k = kb_ref[pl.ds(k0, W, stride=dil), :]
            v = vb_ref[pl.ds(k0, W, stride=dil), :]
            qm = jnp.concatenate([jnp.where(halves[p], q, 0.0) for p in range(2)], axis=0).astype(bf16)
            s = lax.dot_general(qm, k.astype(bf16), _NT, preferred_element_type=jnp.float32)
            out, m, den = _softmax_pv(jnp.where(valid, s, NEG), v.astype(bf16))
            lse = m + jnp.log(den)
            og_ref[g, pl.ds(q0, T, stride=dil), :] = jnp.where(lo, out[:T], out[T:])
            lse_ref[g, pl.ds(q0, T, stride=dil), :] = jnp.where(lo, lse[:T], lse[T:])
            return carry

        lax.fori_loop(0, dil * nbc, unit, 0, unroll=4)

    def merge(i, carry):
        r0 = pl.multiple_of(i * T, T)
        ls = [lse_ref[g, pl.ds(r0, T), :] for g in range(len(DIL_PAIRS))]
        m = jnp.maximum(jnp.maximum(ls[0], ls[1]), ls[2])
        ws = [jnp.exp(l - m) for l in ls]
        tot = ws[0] + ws[1] + ws[2]
        out = sum((w / tot) * og_ref[g, pl.ds(r0, T), :] for g, w in enumerate(ws))
        o_ref[pl.ds(r0, T), :] = out.astype(o_ref.dtype)
        return carry

    lax.fori_loop(0, L // T, merge, 0)


def attn_dilated(qb, kb, vb, B, L):
    ng = len(DIL_PAIRS)
    fmaj = pl.BlockSpec((None, LANES, L), lambda b, hp: (b, hp, 0))
    return pl.pallas_call(
        functools.partial(_attn_dil_body, L),
        grid=(B, 2),
        in_specs=[pl.BlockSpec((ng, L, LANES), lambda b, hp: (hp, b, 0)),
                  pl.BlockSpec((L, LANES), lambda b, hp: (b, hp)),
                  pl.BlockSpec((L, LANES), lambda b, hp: (b, hp))],
        out_specs=[pl.BlockSpec((L, LANES), lambda b, hp: (b, hp)), fmaj, fmaj],
        out_shape=[jax.ShapeDtypeStruct((B * L, 2 * LANES), jnp.bfloat16),
                   jax.ShapeDtypeStruct((B, 2 * LANES, L), jnp.float32),
                   jax.ShapeDtypeStruct((B, 2 * LANES, L), jnp.float32)],
        scratch_shapes=[pltpu.VMEM((ng, L, LANES), jnp.float32), pltpu.VMEM((ng, L, LANES), jnp.float32)],
        compiler_params=_cparams(("parallel", "parallel")), name="attn_dilated")(qb, kb, vb)


N_NEW = 4
SROWS = 8


def _softmax2_pv(s_c, s_n, vt_c, vt_n, sink=None):
    m = jnp.maximum(jnp.max(s_c, axis=-1, keepdims=True), jnp.max(s_n, axis=-1, keepdims=True))
    if sink is not None:
        m = jnp.maximum(m, sink)
    e_c = jnp.exp(s_c - m)
    e_n = jnp.exp(s_n - m)
    den = jnp.sum(e_c, axis=-1, keepdims=True) + jnp.sum(e_n, axis=-1, keepdims=True)
    if sink is not None:
        den = den + jnp.exp(sink - m)
    r = (lax.dot_general(e_c.astype(jnp.bfloat16), vt_c, _NT, preferred_element_type=jnp.float32)
         + lax.dot_general(e_n.astype(jnp.bfloat16), vt_n, _NT, preferred_element_type=jnp.float32))
    return r / den, m, den


def _advance(old_t, new_t):
    n = old_t.shape[1]
    lane = lax.broadcasted_iota(jnp.int32, (1, LANES), 1)
    shifted = pltpu.roll(old_t, n - N_NEW, axis=1)
    last = jnp.where(lane < LANES - N_NEW, shifted[:, n - LANES:], new_t)
    if n == LANES:
        return last
    return jnp.concatenate([shifted[:, :n - LANES], last], axis=1)


def _attn_sample_body(bt, sink_ref, qa_ref, qb_ref, qc_ref, nka_ref, nva_ref, nkb_ref, nvb_ref,
                      cak_ref, cav_ref, cbk_ref, cbv_ref, cmk_ref, cmv_ref,
                      oac_ref, ob_ref, oak_ref, oav_ref, obk_ref, obv_ref):
    f32, bf16 = jnp.float32, jnp.bfloat16
    lo, hi = _half_masks()
    halves = (lo, hi)
    S = SROWS
    la = cak_ref.shape[2]
    lb = cbk_ref.shape[2]
    wa = SWA_Q_HEADS * HEAD_DIM
    new0 = LANES - N_NEW

    na = SWA_Q_HEADS * S
    ia = lax.broadcasted_iota(jnp.int32, (na, la), 0) & (S - 1)
    valid_ac = lax.broadcasted_iota(jnp.int32, (na, la), 1) >= ia + 1
    ja = lax.broadcasted_iota(jnp.int32, (na, LANES), 1) - new0
    valid_an = (ja >= 0) & (ja <= (lax.broadcasted_iota(jnp.int32, (na, LANES), 0) & (S - 1)))
    rcol = lax.broadcasted_iota(jnp.int32, (na, 1), 0)
    sink_col = jnp.zeros((na, 1), f32)
    for h in range(SWA_Q_HEADS):
        sink_col = jnp.where((rcol >> 3) == h, sink_ref[h], sink_col)

    nb_rows = len(DIL_PAIRS) * 2 * S
    rb = lax.broadcasted_iota(jnp.int32, (nb_rows, lb), 0)
    t_c = lb + (rb & (S - 1)) - lax.broadcasted_iota(jnp.int32, (nb_rows, lb), 1)
    rn = lax.broadcasted_iota(jnp.int32, (nb_rows, LANES), 0)
    jn = lax.broadcasted_iota(jnp.int32, (nb_rows, LANES), 1) - new0
    t_n = (rn & (S - 1)) - jn
    valid_bc = jnp.zeros((nb_rows, lb), jnp.bool_)
    valid_bn = jnp.zeros((nb_rows, LANES), jnp.bool_)
    for g, (window, dil) in enumerate(DIL_PAIRS):
        valid_bc = valid_bc | (((rb >> 4) == g) & (t_c <= window) & ((t_c & (dil - 1)) == 0))
        valid_bn = valid_bn | (((rn >> 4) == g) & (jn >= 0) & (t_n >= 0) & ((t_n & (dil - 1)) == 0))

    def new_t(x):
        padded = jnp.concatenate([x, jnp.zeros((LANES - S, x.shape[1]), f32)], axis=0)
        return pltpu.roll(padded.T, new0, axis=1)

    for b in range(bt):
        rows = slice(b * S, (b + 1) * S)
        nka, nva = new_t(nka_ref[rows, :]), new_t(nva_ref[rows, :])
        nkb, nvb = new_t(nkb_ref[rows, :]), new_t(nvb_ref[rows, :])
        oak_ref[b] = _advance(cak_ref[b], nka)
        oav_ref[b] = _advance(cav_ref[b], nva)
        obk_ref[b] = _advance(cbk_ref[b], nkb)
        obv_ref[b] = _advance(cbv_ref[b], nvb)

        pieces = []
        for h in range(SWA_Q_HEADS):
            q = jnp.where(halves[h % 2], qa_ref[rows, (h // 2) * LANES:(h // 2 + 1) * LANES].astype(f32), 0.0)
            if h % 2 != h // 4:
                q = pltpu.roll(q, HEAD_DIM, axis=1)
            pieces.append(q)
        qm = jnp.concatenate(pieces, axis=0).astype(bf16)
        s_c = jnp.dot(qm, cak_ref[b].astype(bf16), preferred_element_type=f32)
        s_n = jnp.dot(qm, nka.astype(bf16), preferred_element_type=f32)
        out, _, _ = _softmax2_pv(jnp.where(valid_ac, s_c, NEG), jnp.where(valid_an, s_n, NEG),
                                 cav_ref[b].astype(bf16), nva.astype(bf16), sink_col)
        for j in range(SWA_Q_HEADS // 2):
            parts = []
            for p in range(2):
                h = 2 * j + p
                o = out[h * S:(h + 1) * S]
                if h % 2 != h // 4:
                    o = pltpu.roll(o, HEAD_DIM, axis=1)
                parts.append(o)
            oac_ref[rows, j * LANES:(j + 1) * LANES] = jnp.where(lo, parts[0], parts[1])

        for j in range(MEM_HEADS // 2):
            cols = slice(j * LANES, (j + 1) * LANES)
            q = qc_ref[rows, cols]
            qm = jnp.concatenate([jnp.where(halves[p], q, jnp.zeros_like(q)) for p in range(2)], axis=0)
            s = jnp.dot(qm, cmk_ref[b, cols, :].astype(bf16), preferred_element_type=f32)
            m = jnp.max(s, axis=-1, keepdims=True)
            e = jnp.exp(s - m)
            r = lax.dot_general(e.astype(bf16), cmv_ref[b, cols, :].astype(bf16), _NT, preferred_element_type=f32)
            out = r / jnp.sum(e, axis=-1, keepdims=True)
            oac_ref[rows, wa + j * LANES:wa + (j + 1) * LANES] = jnp.where(lo, out[:S], out[S:])

        for hp in range(DIL_KV_HEADS // 2):
            cols = slice(hp * LANES, (hp + 1) * LANES)
            pieces = [jnp.where(halves[p], qb_ref[hp * len(DIL_PAIRS) + g, rows, :], 0.0)
                      for g in range(len(DIL_PAIRS)) for p in range(2)]
            qm = jnp.concatenate(pieces, axis=0).astype(bf16)
            s_c = jnp.dot(qm, cbk_ref[b, cols, :].astype(bf16), preferred_element_type=f32)
            s_n = jnp.dot(qm, nkb[cols, :].astype(bf16), preferred_element_type=f32)
            out, m, den = _softmax2_pv(jnp.where(valid_bc, s_c, NEG), jnp.where(valid_bn, s_n, NEG),
                                       cbv_ref[b, cols, :].astype(bf16), nvb[cols, :].astype(bf16))
            lse = m + jnp.log(den)
            res = []
            for p in range(2):
                r = [slice((g * 2 + p) * S, (g * 2 + p + 1) * S) for g in range(len(DIL_PAIRS))]
                mx = jnp.maximum(jnp.maximum(lse[r[0]], lse[r[1]]), lse[r[2]])
                w = [jnp.exp(lse[x] - mx) for x in r]
                tot = w[0] + w[1] + w[2]
                res.append(sum((w[g] / tot) * out[r[g]] for g in range(len(DIL_PAIRS))))
            ob_ref[rows, cols] = jnp.where(lo, res[0], res[1])


def attn_sample(sinks, qa, qb, qc, nka, nva, nkb, nvb, cak, cav, cbk, cbv, cmk, cmv, bt):
    NB, wka, la = cak.shape
    wkb, lb = cbk.shape[1:]
    wm, M = cmk.shape[1:]
    wa, wc = SWA_Q_HEADS * HEAD_DIM, MEM_HEADS * HEAD_DIM
    ng = len(DIL_PAIRS)
    tok = lambda w: pl.BlockSpec((bt * SROWS, w), lambda i: (i, 0))
    buf = lambda f, n: pl.BlockSpec((bt, f, n), lambda i: (i, 0, 0))
    f32 = jnp.float32
    return pl.pallas_call(
        functools.partial(_attn_sample_body, bt),
        grid=(NB // bt,),
        in_specs=[pl.BlockSpec(memory_space=pltpu.SMEM), tok(wa),
                  pl.BlockSpec((2 * ng, bt * SROWS, LANES), lambda i: (0, i, 0)), tok(wc),
                  tok(wka), tok(wka), tok(wkb), tok(wkb),
                  buf(wka, la), buf(wka, la), buf(wkb, lb), buf(wkb, lb), buf(wm, M), buf(wm, M)],
        out_specs=[tok(wa + wc), tok(wkb), buf(wka, la), buf(wka, la), buf(wkb, lb), buf(wkb, lb)],
        out_shape=[jax.ShapeDtypeStruct((NB * SROWS, wa + wc), f32), jax.ShapeDtypeStruct((NB * SROWS, wkb), f32),
                   jax.ShapeDtypeStruct(cak.shape, f32), jax.ShapeDtypeStruct(cak.shape, f32),
                   jax.ShapeDtypeStruct(cbk.shape, f32), jax.ShapeDtypeStruct(cbk.shape, f32)],
        compiler_params=_cparams(("parallel",)), name="attn_sample")(
            sinks, qa, qb, qc, nka, nva, nkb, nvb, cak, cav, cbk, cbv, cmk, cmv)


MOE_BLOCK = 256
ROUTE_SUB = 256
SUBLANES = 8


def _store_row_tiles(ref, row0, y):
    n = y.shape[0]
    for c in range(SUBLANES):
        ref[pl.ds(row0 * SUBLANES + c, n, stride=SUBLANES), :] = y[:, c * LANES:(c + 1) * LANES]


def _load_row_tiles(ref, row0, n):
    return jnp.concatenate([ref[pl.ds(row0 * SUBLANES + c, n, stride=SUBLANES), :] for c in range(SUBLANES)],
                           axis=1)


def _row_tile(ref, r):
    return ref.at[pl.ds(pl.multiple_of(r * SUBLANES, SUBLANES), SUBLANES)]


def _merge_route_body(oac_ref, ob_ref, gate_ref, x_ref, wa_ref, wb_ref, wc_ref, wo_ref, gffn_ref, wr_ref, br_ref,
                      cnt0_ref, x1_ref, h2_ref, mi_ref, mf_ref, cnt_ref, base_ref):
    f32, bf16 = jnp.float32, jnp.bfloat16
    D = x_ref.shape[1]
    wa = SWA_Q_HEADS * HEAD_DIM
    ts = ROUTE_SUB

    @pl.when(pl.program_id(0) == 0)
    def _():
        base_ref[...] = cnt0_ref[...]

    erow = lax.broadcasted_iota(jnp.int32, (N_EXPERTS, ts), 0)
    r8 = lax.broadcasted_iota(jnp.int32, (SUBLANES, ts), 0)
    ti = lax.broadcasted_iota(jnp.int32, (ts, ts), 0)
    tj = lax.broadcasted_iota(jnp.int32, (ts, ts), 1)
    later = (ti < tj).astype(bf16)
    base = base_ref[:, 0:1]
    for sub in range(x_ref.shape[0] // ts):
        rows = slice(sub * ts, (sub + 1) * ts)
        ma = jnp.dot(oac_ref[rows, :wa].astype(bf16), wa_ref[...], preferred_element_type=f32)
        mb = jnp.dot(ob_ref[rows, :].astype(bf16), wb_ref[...], preferred_element_type=f32)
        mc = jnp.dot(oac_ref[rows, wa:].astype(bf16), wc_ref[...], preferred_element_type=f32)
        merged = (gate_ref[rows, :D].astype(f32) * ma + gate_ref[rows, D:2 * D].astype(f32) * mb
                  + gate_ref[rows, 2 * D:].astype(f32) * mc)
        x1 = x_ref[rows, :] + jnp.dot(merged.astype(bf16), wo_ref[...], preferred_element_type=f32)
        x1_ref[rows, :] = x1
        h2 = _rms(x1, gffn_ref[...])
        _store_row_tiles(h2_ref, sub * ts, h2)

        work = lax.dot_general(wr_ref[...], h2.astype(bf16), _NT, preferred_element_type=f32) + br_ref[:, 0:1]
        vals, idxs = [], []
        for _ in range(TOP_K):
            m = jnp.max(work, axis=0, keepdims=True)
            idx = jnp.min(jnp.where(work == m, erow, N_EXPERTS), axis=0, keepdims=True)
            vals.append(m)
            idxs.append(idx)
            work = jnp.where(erow == idx, -jnp.inf, work)
        es = [jnp.exp(v - vals[0]) for v in vals]
        tot = es[0] + es[1] + es[2] + es[3]

        onehot = [(erow == idx).astype(f32) for idx in idxs]
        assign = onehot[0] + onehot[1] + onehot[2] + onehot[3]
        before = jnp.dot(assign.astype(bf16), later, preferred_element_type=f32) + base
        base = base + jnp.sum(assign, axis=1, keepdims=True)

        mi = jnp.zeros((SUBLANES, ts), jnp.int32)
        gates = jnp.zeros((SUBLANES, ts), f32)
        for k in range(TOP_K):
            rank = jnp.sum(onehot[k] * before, axis=0, keepdims=True).astype(jnp.int32)
            mi = jnp.where(r8 == k, idxs[k], mi)
            mi = jnp.where(r8 == TOP_K + k, rank, mi)
            gates = jnp.where(r8 == k, es[k] / tot, gates)
        mi_ref[:, rows] = mi
        mf_ref[rows, :] = jnp.concatenate([gates, jnp.zeros((LANES - SUBLANES, ts), f32)], axis=0).T
    base_ref[...] = jnp.broadcast_to(base, base_ref.shape)
    cnt_ref[...] = jnp.broadcast_to(base, cnt_ref.shape)


def merge_route(o_ac, o_b, gates, x, wa, wb, wc, wo, g_ffn, wr, br, cnt0, tm):
    R, D = x.shape
    row = lambda w: pl.BlockSpec((tm, w), lambda i: (i, 0))
    full = lambda a: pl.BlockSpec(a.shape, lambda i: (0, 0))
    return pl.pallas_call(
        _merge_route_body,
        grid=(R // tm,),
        in_specs=[row(o_ac.shape[1]), row(o_b.shape[1]), row(gates.shape[1]), row(D),
                  full(wa), full(wb), full(wc), full(wo), full(g_ffn), full(wr), full(br), full(cnt0)],
        out_specs=[row(D), pl.BlockSpec((tm * SUBLANES, LANES), lambda i: (i, 0)),
                   pl.BlockSpec((SUBLANES, tm), lambda i: (0, i)), row(LANES),
                   pl.BlockSpec((N_EXPERTS, LANES), lambda i: (0, 0))],
        out_shape=[jax.ShapeDtypeStruct((R, D), jnp.float32), jax.ShapeDtypeStruct((R * SUBLANES, LANES), jnp.float32),
                   jax.ShapeDtypeStruct((SUBLANES, R), jnp.int32), jax.ShapeDtypeStruct((R, LANES), jnp.float32),
                   jax.ShapeDtypeStruct((N_EXPERTS, LANES), jnp.float32)],
        scratch_shapes=[pltpu.VMEM((N_EXPERTS, LANES), jnp.float32)],
        compiler_params=_cparams(("arbitrary",)), name="merge_route")(
            o_ac, o_b, gates, x, wa, wb, wc, wo, g_ffn, wr, br, cnt0)


def _route_tables_body(cnt_ref, mi_ref, dest_ref, blk_ref, pad_ref):
    tm = mi_ref.shape[1]
    nbl = blk_ref.shape[1]
    erow1 = lax.broadcasted_iota(jnp.int32, (N_EXPERTS, LANES), 0)
    shift = MOE_BLOCK.bit_length() - 1
    cnt = cnt_ref[...].astype(jnp.int32)
    padded = ((cnt + (MOE_BLOCK - 1)) >> shift) << shift
    pend = padded
    s = 1
    while s < N_EXPERTS:
        pend = pend + jnp.where(erow1 >= s, pltpu.roll(pend, s, axis=0), 0)
        s *= 2
    pstart = pend - padded
    mi = mi_ref[...]
    erow = lax.broadcasted_iota(jnp.int32, (N_EXPERTS, tm), 0)
    r8 = lax.broadcasted_iota(jnp.int32, (SUBLANES, tm), 0)
    dest = jnp.zeros((SUBLANES, tm), jnp.int32)
    for k in range(TOP_K):
        start = jnp.sum(jnp.where(erow == mi[k:k + 1, :], pstart[:, 0:1], 0), axis=0, keepdims=True)
        dest = jnp.where(r8 == k, start + mi[TOP_K + k:TOP_K + k + 1, :], dest)
    dest_ref[...] = dest

    @pl.when(pl.program_id(0) == 0)
    def _():
        row0 = lax.broadcasted_iota(jnp.int32, (N_EXPERTS, nbl), 1) * MOE_BLOCK
        ended = jnp.sum(jnp.where(pend[:, 0:1] <= row0, 1, 0), axis=0, keepdims=True)
        blk_ref[...] = jnp.broadcast_to(jnp.minimum(ended, N_EXPERTS - 1), blk_ref.shape)
        pad_ref[0] = pstart + cnt
        pad_ref[1] = pend


def route_tables(cnt, mi, nbl, tm):
    R = mi.shape[1]
    return pl.pallas_call(
        _route_tables_body,
        grid=(R // tm,),
        in_specs=[pl.BlockSpec((N_EXPERTS, LANES), lambda i: (0, 0)), pl.BlockSpec((SUBLANES, tm), lambda i: (0, i))],
        out_specs=[pl.BlockSpec((SUBLANES, tm), lambda i: (0, i)), pl.BlockSpec((SUBLANES, nbl), lambda i: (0, 0)),
                   pl.BlockSpec((2, N_EXPERTS, LANES), lambda i: (0, 0, 0))],
        out_shape=[jax.ShapeDtypeStruct((SUBLANES, R), jnp.int32), jax.ShapeDtypeStruct((SUBLANES, nbl), jnp.int32),
                   jax.ShapeDtypeStruct((2, N_EXPERTS, LANES), jnp.int32)],
        compiler_params=_cparams(("arbitrary",)), name="route_tables")(cnt, mi)


def _dispatch_body(n_first, n_zero, pad0_ref, pad1_ref, dest_ref, ha_ref, hb_ref, ha_hbm, xs_ref, zero_ref, sem):
    i = pl.program_id(0)
    tm = ha_ref.shape[0] // SUBLANES
    rs = xs_ref.shape[0] // SUBLANES

    def scatter_tile(h_ref, hbm_ref=None):
        def row_copy(r, k):
            src = _row_tile(h_ref, r) if hbm_ref is None or k < TOP_K // 2 else _row_tile(hbm_ref, i * tm + r)
            return pltpu.make_async_copy(src, _row_tile(xs_ref, dest_ref[0, 0, k * tm + r]), sem)

        def start(r, c):
            for k in range(TOP_K):
                row_copy(r, k).start(priority=k % 2)
            return c

        def wait(r, c):
            for k in range(TOP_K):
                row_copy(r, k).wait()
            return c

        lax.fori_loop(0, tm, start, 0, unroll=8)
        lax.fori_loop(0, tm, wait, 0, unroll=8)

    @pl.when(i < n_first)
    def _():
        scatter_tile(ha_ref, ha_hbm)

    @pl.when(i >= n_first)
    def _():
        scatter_tile(hb_ref)

    @pl.when(i == pl.num_programs(0) - 1)
    def _():
        zero_ref[...] = jnp.zeros_like(zero_ref)

        def zero_copy(r):
            return pltpu.make_async_copy(zero_ref, _row_tile(xs_ref, r), sem)

        def start(r, c):
            zero_copy(r).start()
            return c

        def wait(r, c):
            zero_copy(0).wait()
            return c

        for e in range(N_EXPERTS):
            lax.fori_loop(pad0_ref[e], pad1_ref[e], start, 0)
        lax.fori_loop(pad1_ref[N_EXPERTS - 1], rs, start, 0)
        lax.fori_loop(0, n_zero, wait, 0)


def dispatch(pad0, pad1, dest3, h_a, h_b, rs, tm):
    S = SUBLANES
    n_a, n_b = h_a.shape[0] // (tm * S), h_b.shape[0] // (tm * S)
    n_zero = rs - (h_a.shape[0] + h_b.shape[0]) // S * TOP_K
    grid_spec = pltpu.PrefetchScalarGridSpec(
        num_scalar_prefetch=2,
        grid=(n_a + n_b,),
        in_specs=[pl.BlockSpec((1, 1, TOP_K * tm), lambda i, p0, p1: (i, 0, 0), memory_space=pltpu.SMEM),
                  pl.BlockSpec((tm * S, LANES), lambda i, p0, p1: (jnp.minimum(i, n_a - 1), 0)),
                  pl.BlockSpec((tm * S, LANES), lambda i, p0, p1: (jnp.maximum(i - n_a, 0), 0)),
                  pl.BlockSpec(memory_space=pl.ANY)],
        out_specs=pl.BlockSpec(memory_space=pl.ANY),
        scratch_shapes=[pltpu.VMEM((S, LANES), h_a.dtype), pltpu.SemaphoreType.DMA])
    return pl.pallas_call(
        functools.partial(_dispatch_body, n_a, n_zero), grid_spec=grid_spec,
        out_shape=jax.ShapeDtypeStruct((rs * S, LANES), h_a.dtype),
        compiler_params=_cparams(("arbitrary",)), name="dispatch")(pad0, pad1, dest3, h_a, h_b, h_a)


def _moe_body(be_ref, nact_ref, x_ref, wgu_hbm, bgu_ref, wd_hbm, bd_ref, y_ref, wgu_f, wd_f, wgu_s, wd_s, sems):
    f32, bf16 = jnp.float32, jnp.bfloat16
    b = pl.program_id(0)
    nblk = pl.num_programs(0)
    e = be_ref[b]

    def fetch(ex, go):
        for src, dst, s in ((wgu_hbm, wgu_f, 0), (wd_hbm, wd_f, 1)):
            cp = pltpu.make_async_copy(src.at[ex], dst, sems.at[s])
            cp.start() if go else cp.wait()

    @pl.when(b == 0)
    def _():
        fetch(e, True)

    @pl.when((b == 0) | (e != be_ref[jnp.maximum(b - 1, 0)]))
    def _():
        fetch(e, False)
        wgu_s[...] = wgu_f[...].astype(bf16)
        wd_s[...] = wd_f[...].astype(bf16)
        nxt = lax.while_loop(lambda j: (j < nblk) & (be_ref[jnp.minimum(j, nblk - 1)] == e), lambda j: j + 1, b + 1)

        @pl.when(nxt < nblk)
        def _():
            fetch(be_ref[jnp.minimum(nxt, nblk - 1)], True)

    @pl.when(b < nact_ref[0])
    def _():
        x = _load_row_tiles(x_ref, 0, MOE_BLOCK).astype(bf16)
        gu = jnp.dot(x, wgu_s[...], preferred_element_type=f32) + bgu_ref[0]
        gt = jnp.minimum(gu[:, :D_FF], SWIGLU_LIMIT)
        up = jnp.clip(gu[:, D_FF:], -SWIGLU_LIMIT, SWIGLU_LIMIT)
        act = (up + 1.0) * (gt * jax.nn.sigmoid(gt * SWIGLU_ALPHA))
        y = jnp.dot(act.astype(bf16), wd_s[...], preferred_element_type=f32) + bd_ref[0]
        _store_row_tiles(y_ref, 0, y)

    @pl.when(b >= nact_ref[0])
    def _():
        y_ref[...] = jnp.zeros_like(y_ref)


def moe_ffn(blk_e, n_active, xs, w_gate_up, b_gate_up, w_down, b_down):
    RS = xs.shape[0] // SUBLANES
    E, D, F2 = w_gate_up.shape
    blk_rows = MOE_BLOCK * SUBLANES
    grid_spec = pltpu.PrefetchScalarGridSpec(
        num_scalar_prefetch=2,
        grid=(RS // MOE_BLOCK,),
        in_specs=[pl.BlockSpec((blk_rows, LANES), lambda b, be, na: (b, 0)),
                  pl.BlockSpec(memory_space=pl.ANY),
                  pl.BlockSpec((1, 1, F2), lambda b, be, na: (be[b], 0, 0)),
                  pl.BlockSpec(memory_space=pl.ANY),
                  pl.BlockSpec((1, 1, D), lambda b, be, na: (be[b], 0, 0))],
        out_specs=pl.BlockSpec((blk_rows, LANES), lambda b, be, na: (b, 0)),
        scratch_shapes=[pltpu.VMEM((D, F2), w_gate_up.dtype), pltpu.VMEM((F2 // 2, D), w_down.dtype),
                        pltpu.VMEM((D, F2), jnp.bfloat16), pltpu.VMEM((F2 // 2, D), jnp.bfloat16),
                        pltpu.SemaphoreType.DMA((2,))])
    return pl.pallas_call(
        _moe_body, grid_spec=grid_spec,
        out_shape=jax.ShapeDtypeStruct(xs.shape, jnp.float32),
        compiler_params=_cparams(("arbitrary",)), name="moe_ffn")(
            blk_e, n_active, xs, w_gate_up, b_gate_up.reshape(E, 1, F2), w_down, b_down.reshape(E, 1, D))


def _combine_body(dest_ref, dest_next_ref, x1_ref, mf_ref, g_ref, ys_ref, o_ref, buf, sems):
    tm = x1_ref.shape[0]
    i = pl.program_id(0)
    slot = i % 2
    per_slot = TOP_K * tm

    def gather(d_ref, s, go):
        def row_copy(r, k):
            return pltpu.make_async_copy(_row_tile(ys_ref, d_ref[0, 0, k * tm + r]),
                                         _row_tile(buf, s * per_slot + k * tm + r), sems.at[s])

        def body(r, c):
            for k in range(TOP_K):
                if go:
                    row_copy(r, k).start(priority=k % 2)
                else:
                    row_copy(r, k).wait()
            return c

        lax.fori_loop(0, tm, body, 0, unroll=8)

    @pl.when(i == 0)
    def _():
        gather(dest_ref, slot, True)

    @pl.when(i + 1 < pl.num_programs(0))
    def _():
        gather(dest_next_ref, 1 - slot, True)

    gather(dest_ref, slot, False)
    y = x1_ref[...]
    for k in range(TOP_K):
        y = y + mf_ref[:, k:k + 1] * _load_row_tiles(buf, slot * per_slot + k * tm, tm)
    o_ref[...] = _rms(y, g_ref[...])


def combine(dest3, x1, mf, g_final, ys, tm):
    R, D = x1.shape
    n = R // tm
    dest_spec = lambda f: pl.BlockSpec((1, 1, TOP_K * tm), f, memory_space=pltpu.SMEM)
    return pl.pallas_call(
        _combine_body,
        grid=(n,),
        in_specs=[dest_spec(lambda i: (i, 0, 0)), dest_spec(lambda i: (jnp.minimum(i + 1, n - 1), 0, 0)),
                  pl.BlockSpec((tm, D), lambda i: (i, 0)),
                  pl.BlockSpec((tm, LANES), lambda i: (i, 0)),
                  pl.BlockSpec((1, D), lambda i: (0, 0)),
                  pl.BlockSpec(memory_space=pl.ANY)],
        out_specs=pl.BlockSpec((tm, D), lambda i: (i, 0)),
        out_shape=jax.ShapeDtypeStruct((R, D), jnp.float32),
        scratch_shapes=[pltpu.VMEM((2 * TOP_K * tm * SUBLANES, LANES), ys.dtype), pltpu.SemaphoreType.DMA((2,))],
        compiler_params=_cparams(("arbitrary",)), name="combine")(dest3, dest3, x1, mf, g_final, ys)


def moe_layer(group_a, group_b, cnt, g_final, w_gate_up, b_gate_up, w_down, b_down, tm):
    n_assign = (group_a[0].shape[0] + group_b[0].shape[0]) * TOP_K
    nb = (n_assign + N_EXPERTS * (MOE_BLOCK - 1)) // MOE_BLOCK + 1
    nbl = -(-nb // LANES) * LANES
    dests = []
    for x1, h2, mi, mf in (group_a, group_b):
        dest, blk, pad = route_tables(cnt, mi, nbl, min(x1.shape[0], 4 * tm))
        n_tiles = x1.shape[0] // tm
        dests.append(jnp.transpose(dest[:TOP_K].reshape(TOP_K, n_tiles, tm), (1, 0, 2)).reshape(n_tiles, 1, TOP_K * tm))
    xs = dispatch(pad[0, :, 0], pad[1, :, 0], jnp.concatenate(dests), group_a[1], group_b[1], nb * MOE_BLOCK, tm)
    n_active = (pad[1, N_EXPERTS - 1, 0] // MOE_BLOCK).reshape(1)
    ys = moe_ffn(blk[0, :nb], n_active, xs, w_gate_up, b_gate_up, w_down, b_down)
    return [combine(dest3, x1, mf, g_final, ys, tm) for dest3, (x1, h2, mi, mf) in zip(dests, (group_a, group_b))]


def kernel(x_prompt, x_sample, cache_swa_k, cache_swa_v, cache_dil_k, cache_dil_v, cache_mem_k, cache_mem_v, mem_prompt, norm_attn, norm_mem, w_in, w_mem_kv, sinks, w_br_a, w_br_b, w_br_c, w_out, norm_ffn, w_router, b_router, w_gate_up, b_gate_up, w_down, b_down, norm_final):
    f32, bf16 = jnp.float32, jnp.bfloat16
    TM = 256
    B, L, D = x_prompt.shape
    NB, n_new, _ = x_sample.shape
    M = mem_prompt.shape[1]
    la, lb = cache_swa_k.shape[2], cache_dil_k.shape[2]
    wka, wkb, wc = SWA_KV_HEADS * HEAD_DIM, DIL_KV_HEADS * HEAD_DIM, MEM_HEADS * HEAD_DIM
    assert n_new == N_NEW and cache_swa_k.shape[0] == 1

    w_in_b = w_in[0].astype(bf16)
    g_attn = norm_attn[0].reshape(1, D)
    secs = in_sections()

    tabs_p = rope_tables(jnp.arange(L, dtype=jnp.int32))
    qa, ka, va, qb, kb, vb, qc, gates = norm_proj(x_prompt.reshape(B * L, D), g_attn, w_in_b, tabs_p, secs, 2 * TM)
    mk, mv = norm_proj(mem_prompt.reshape(B * M, D), norm_mem[0].reshape(1, D), w_mem_kv[0].astype(bf16), None,
                       [(_chunks(0, wc), "plain", False, f32), (_chunks(wc, wc), "plain", False, f32)], TM)
    o_ac = attn_swa_mem(sinks[0], qa, ka, va, qc, mk, mv, B, L)
    o_b, kb_t, vb_t = attn_dilated(qb, kb, vb, B, L)

    xs_pad = jnp.pad(x_sample, ((0, 0), (0, SROWS - N_NEW), (0, 0))).reshape(NB * SROWS, D)
    tabs_s = rope_tables(PAST_LEN + (jnp.arange(TM, dtype=jnp.int32) % SROWS))
    qa_s, ka_s, va_s, qb_s, kb_s, vb_s, qc_s, gates_s = norm_proj(xs_pad, g_attn, w_in_b, tabs_s, secs, TM)
    real = lambda t: t.reshape(NB, SROWS, -1)[:, :N_NEW].reshape(NB * N_NEW, -1)
    fmaj = lambda c: jnp.transpose(c[0], (0, 2, 3, 1)).reshape(NB, -1, c.shape[2])
    o_ac_s, o_b_s, swa_k_s, swa_v_s, dil_k_s, dil_v_s = attn_sample(
        sinks[0], qa_s, qb_s, qc_s, ka_s, va_s, kb_s, vb_s,
        fmaj(cache_swa_k), fmaj(cache_swa_v), fmaj(cache_dil_k), fmaj(cache_dil_v),
        fmaj(cache_mem_k), fmaj(cache_mem_v), 2)

    wr = w_router[0].T.astype(bf16)
    br = jnp.broadcast_to(b_router[0].astype(f32)[:, None], (N_EXPERTS, LANES))
    wts = (w_br_a[0].astype(bf16), w_br_b[0].astype(bf16), w_br_c[0].astype(bf16), w_out[0].astype(bf16),
           norm_ffn[0].reshape(1, D), wr, br)
    x1_p, h2_p, mi_p, mf_p, cnt_p = merge_route(o_ac, o_b, gates, x_prompt.reshape(B * L, D), *wts,
                                                jnp.zeros((N_EXPERTS, LANES), f32), 2 * ROUTE_SUB)
    x1_s, h2_s, mi_s, mf_s, cnt = merge_route(real(o_ac_s), real(o_b_s), real(gates_s),
                                              x_sample.reshape(NB * N_NEW, D), *wts, cnt_p, 2 * ROUTE_SUB)
    y_p, y_s = moe_layer((x1_p, h2_p, mi_p, mf_p), (x1_s, h2_s, mi_s, mf_s), cnt, norm_final.reshape(1, D),
                         w_gate_up[0], b_gate_up[0], w_down[0], b_down[0], TM)

    heads = lambda t, n, h: t.reshape(1, t.shape[0] // n, n, h, HEAD_DIM)
    tmaj = lambda t, h: jnp.transpose(t.reshape(t.shape[0], h, HEAD_DIM, t.shape[2]), (0, 3, 1, 2))[None]
    ka3, va3 = ka.reshape(B, L, wka), va.reshape(B, L, wka)
    la_p = min(SWA_WINDOW, L)
    return (y_p.reshape(B, L, D), y_s.reshape(NB, N_NEW, D),
            heads(ka3[:, L - la_p:].reshape(B * la_p, wka), la_p, SWA_KV_HEADS),
            heads(va3[:, L - la_p:].reshape(B * la_p, wka), la_p, SWA_KV_HEADS),
            tmaj(kb_t, DIL_KV_HEADS), tmaj(vb_t, DIL_KV_HEADS),
            heads(mk, M, MEM_HEADS), heads(mv, M, MEM_HEADS),
            tmaj(swa_k_s, SWA_KV_HEADS), tmaj(swa_v_s, SWA_KV_HEADS),
            tmaj(dil_k_s, DIL_KV_HEADS), tmaj(dil_v_s, DIL_KV_HEADS))
```

```python
import functools

import jax
import jax.numpy as jnp
import numpy as np
from jax import lax
from jax.experimental import pallas as pl
from jax.experimental.pallas import tpu as pltpu

D_MODEL = 1024
HEAD_DIM = 64
ROPE_DIM = 16
ROPE_HALF = 8
ROPE_THETA = 500000.0
PAST_LEN = 16384
SWA_Q_HEADS = 8
SWA_KV_HEADS = 2
SWA_WINDOW = 128
DIL_PAIRS = ((128, 1), (512, 4), (2048, 16))
DIL_KV_HEADS = 4
MEM_HEADS = 4
N_EXPERTS = 32
TOP_K = 4
D_FF = 1024
SWIGLU_LIMIT = 7.0
SWIGLU_ALPHA = 1.702
RMS_EPS = 1e-5
ATT_BLOCK = 128
SCALE = HEAD_DIM ** -0.5

LANES = 128
NEG = -1e30
VMEM_LIMIT = 56 * 1024 * 1024


def _cparams(sem):
    return pltpu.CompilerParams(dimension_semantics=sem, vmem_limit_bytes=VMEM_LIMIT)


def _rms(x, g):
    return x * lax.rsqrt(jnp.mean(x * x, axis=-1, keepdims=True) + RMS_EPS) * g


def _norm_proj_body(sections, x_ref, g_ref, w_ref, cs_ref, *out_refs):
    h = _rms(x_ref[...], g_ref[...]).astype(jnp.bfloat16)
    if cs_ref is not None:
        cos = cs_ref[0]
        sin_lo = cs_ref[1]
        sin_hi = cs_ref[2]
    dest = {lo: (o_ref, c, kind, slabs)
            for (cols, kind, slabs), o_ref in zip(sections, out_refs) for c, lo in enumerate(cols)}
    todo = sorted(dest)
    while todo:
        lo = todo.pop(0)
        n = 2 if todo and todo[0] == lo + LANES else 1
        if n == 2:
            todo.pop(0)
        yy = jnp.dot(h, w_ref[:, lo:lo + n * LANES], preferred_element_type=jnp.float32)
        for part in range(n):
            o_ref, c, kind, slabs = dest[lo + part * LANES]
            y = yy[:, part * LANES:(part + 1) * LANES]
            if kind in ("rope", "rope_q"):
                y = (y * cos + pltpu.roll(y, LANES - ROPE_HALF, axis=1) * sin_lo
                     + pltpu.roll(y, ROPE_HALF, axis=1) * sin_hi)
            if kind in ("rope_q", "q"):
                y = y * SCALE
            if kind == "sigmoid":
                y = jax.nn.sigmoid(y)
            if slabs:
                o_ref[c] = y.astype(o_ref.dtype)
            else:
                o_ref[:, c * LANES:(c + 1) * LANES] = y.astype(o_ref.dtype)


def rope_tables(pos):
    inv_freq = ROPE_THETA ** (-jnp.arange(ROPE_HALF, dtype=jnp.float32) / ROPE_HALF)
    ang = pos.astype(jnp.float32)[:, None] * inv_freq[None, :]
    cos, sin = jnp.cos(ang), jnp.sin(ang)
    n = pos.shape[0]
    one = jnp.ones((n, HEAD_DIM - ROPE_DIM), jnp.float32)
    zero = jnp.zeros((n, HEAD_DIM - ROPE_HALF), jnp.float32)
    c = jnp.concatenate([cos, cos, one], axis=1)
    s_lo = jnp.concatenate([-sin, zero], axis=1)
    s_hi = jnp.concatenate([jnp.zeros((n, ROPE_HALF), jnp.float32), sin,
                            jnp.zeros((n, HEAD_DIM - ROPE_DIM), jnp.float32)], axis=1)
    tab = jnp.stack([c, s_lo, s_hi])
    return jnp.concatenate([tab, tab], axis=2)


def norm_proj(x, g, w, tables, sections, tm):
    R, D = x.shape
    in_specs = [pl.BlockSpec((tm, D), lambda i: (i, 0)),
                pl.BlockSpec((1, D), lambda i: (0, 0)),
                pl.BlockSpec(w.shape, lambda i: (0, 0))]
    args = [x, g, w]
    if tables is not None:
        nt = tables.shape[1] // tm
        in_specs.append(pl.BlockSpec((3, tm, LANES), lambda i: (0, i % nt, 0)))
        args.append(tables)
    out_shape, out_specs, secs = [], [], []
    for (cols, kind, slabs, dtype) in sections:
        secs.append((cols, kind, slabs))
        width = LANES * len(cols)
        if slabs:
            out_shape.append(jax.ShapeDtypeStruct((width // LANES, R, LANES), dtype))
            out_specs.append(pl.BlockSpec((width // LANES, tm, LANES), lambda i: (0, i, 0)))
        else:
            out_shape.append(jax.ShapeDtypeStruct((R, width), dtype))
            out_specs.append(pl.BlockSpec((tm, width), lambda i: (i, 0)))
    if tables is None:
        body = lambda x_ref, g_ref, w_ref, *o: _norm_proj_body(secs, x_ref, g_ref, w_ref, None, *o)
    else:
        body = functools.partial(_norm_proj_body, secs)
    return pl.pallas_call(
        body, grid=(R // tm,), in_specs=in_specs, out_specs=out_specs, out_shape=out_shape,
        compiler_params=_cparams(("parallel",)), name="norm_proj")(*args)


def _chunks(start, width):
    return tuple(range(start, start + width, LANES))


def in_sections():
    f32, bf16 = jnp.float32, jnp.bfloat16
    qb0 = 1024 - 256
    qb_cols = tuple(qb0 + HEAD_DIM * (4 * g + 2 * hp) for hp in range(2) for g in range(3))
    return [
        (_chunks(0, 512), "rope_q", False, bf16),
        (_chunks(512, 128), "rope", False, f32),
        (_chunks(640, 128), "plain", False, f32),
        (qb_cols, "rope_q", True, f32),
        (_chunks(1536, 256), "rope", False, f32),
        (_chunks(1792, 256), "plain", False, f32),
        (_chunks(2048, 256), "q", False, bf16),
        (_chunks(2304, 3072), "sigmoid", False, bf16),
    ]


_NT = (((1,), (1,)), ((), ()))


def _half_masks():
    lane = lax.broadcasted_iota(jnp.int32, (1, LANES), 1)
    return lane < HEAD_DIM, lane >= HEAD_DIM


def _softmax_pv(s, v_half, sink=None):
    m = jnp.max(s, axis=-1, keepdims=True)
    if sink is not None:
        m = jnp.maximum(m, sink)
    e = jnp.exp(s - m)
    den = jnp.sum(e, axis=-1, keepdims=True)
    if sink is not None:
        den = den + jnp.exp(sink - m)
    r = jnp.dot(e.astype(jnp.bfloat16), v_half, preferred_element_type=jnp.float32)
    return r / den, m, den


def _attn_swa_mem_body(L, sink_ref, qa_ref, ka_ref, va_ref, qc_ref, mk_ref, mv_ref, o_ref):
    bf16 = jnp.bfloat16
    lo, hi = _half_masks()
    halves = (lo, hi)
    T = ATT_BLOCK
    G = SWA_Q_HEADS // SWA_KV_HEADS
    mem_k = [mk_ref[:, j * LANES:(j + 1) * LANES].astype(bf16) for j in range(2)]
    mem_v = [mv_ref[:, j * LANES:(j + 1) * LANES].astype(bf16) for j in range(2)]
    qi = lax.broadcasted_iota(jnp.int32, (2 * T, 2 * T), 0) & (T - 1)
    kj = lax.broadcasted_iota(jnp.int32, (2 * T, 2 * T), 1)
    hrow = lax.broadcasted_iota(jnp.int32, (2 * T, 1), 0) >> (T.bit_length() - 1)

    def heads_of(pair):
        return jnp.concatenate([jnp.where(halves[p], pair, jnp.zeros_like(pair)) for p in range(2)], axis=0)

    def block(blk, carry):
        r0 = pl.multiple_of(blk * T, T)
        ws = pl.multiple_of(jnp.maximum(r0 - T, 0), T)
        dist = qi - kj + (r0 - ws)
        valid = (dist >= 0) & (dist <= SWA_WINDOW - 1)
        k = ka_ref[pl.ds(ws, 2 * T), :]
        v = va_ref[pl.ds(ws, 2 * T), :]
        for kv in range(SWA_KV_HEADS):
            k1 = jnp.where(halves[kv], k, 0.0)
            v1 = jnp.where(halves[kv], v, 0.0)
            k_dup = (k1 + pltpu.roll(k1, HEAD_DIM, axis=1)).astype(bf16)
            v_dup = (v1 + pltpu.roll(v1, HEAD_DIM, axis=1)).astype(bf16)
            for j in range(kv * G // 2, (kv + 1) * G // 2):
                qm = heads_of(qa_ref[pl.ds(r0, T), j * LANES:(j + 1) * LANES])
                sink = jnp.where(hrow == 0, sink_ref[2 * j], sink_ref[2 * j + 1])
                s = lax.dot_general(qm, k_dup, _NT, preferred_element_type=jnp.float32)
                out, _, _ = _softmax_pv(jnp.where(valid, s, NEG), v_dup, sink)
                o_ref[pl.ds(r0, T), j * LANES:(j + 1) * LANES] = jnp.where(lo, out[:T], out[T:]).astype(o_ref.dtype)
        for j in range(MEM_HEADS // 2):
            qm = heads_of(qc_ref[pl.ds(r0, T), j * LANES:(j + 1) * LANES])
            s = lax.dot_general(qm, mem_k[j], _NT, preferred_element_type=jnp.float32)
            out, _, _ = _softmax_pv(s, mem_v[j])
            c0 = SWA_Q_HEADS * HEAD_DIM + j * LANES
            o_ref[pl.ds(r0, T), c0:c0 + LANES] = jnp.where(lo, out[:T], out[T:]).astype(o_ref.dtype)
        return carry

    lax.fori_loop(0, L // T, block, 0)


def attn_swa_mem(sinks, qa, ka, va, qc, mk, mv, B, L):
    M = mk.shape[0] // B
    wa, wc = SWA_Q_HEADS * HEAD_DIM, MEM_HEADS * HEAD_DIM
    row = lambda w: pl.BlockSpec((L, w), lambda b: (b, 0))
    return pl.pallas_call(
        functools.partial(_attn_swa_mem_body, L),
        grid=(B,),
        in_specs=[pl.BlockSpec(memory_space=pltpu.SMEM), row(wa), row(LANES), row(LANES), row(wc),
                  pl.BlockSpec((M, wc), lambda b: (b, 0)), pl.BlockSpec((M, wc), lambda b: (b, 0))],
        out_specs=row(wa + wc),
        out_shape=jax.ShapeDtypeStruct((B * L, wa + wc), jnp.bfloat16),
        compiler_params=_cparams(("parallel",)), name="attn_swa_mem")(sinks, qa, ka, va, qc, mk, mv)


def _attn_dil_body(L, qb_ref, kb_ref, vb_ref, o_ref, kt_ref, vt_ref, og_ref, lse_ref):
    bf16 = jnp.bfloat16
    lo, hi = _half_masks()
    halves = (lo, hi)
    T = ATT_BLOCK
    kt_ref[...] = kb_ref[...].T
    vt_ref[...] = vb_ref[...].T
    for g, (window, dil) in enumerate(DIL_PAIRS):
        lc = L // dil
        nbc = lc // T
        W = min(2 * T, lc)
        max_dist = window // dil
        qi = lax.broadcasted_iota(jnp.int32, (2 * T, W), 0) & (T - 1)
        kj = lax.broadcasted_iota(jnp.int32, (2 * T, W), 1)

        def unit(u, carry, g=g, dil=dil, nbc=nbc, W=W, max_dist=max_dist, qi=qi, kj=kj):
            c = u >> (nbc.bit_length() - 1)
            n = u & (nbc - 1)
            wsc = jnp.maximum(n * T - T, 0) if W == 2 * T else 0
            q0 = c + dil * T * n
            k0 = c + dil * wsc
            dist = qi - kj + (n * T - wsc)
            valid = (dist >= 0) & (dist <= max_dist)
            q = qb_ref[g, pl.ds(q0, T, stride=dil), :]
            k = kb_ref[pl.ds(k0, W, stride=dil), :]
            v = vb_ref[pl.ds(k0, W, stride=dil), :]
            qm = jnp.concatenate([jnp.where(halves[p], q, 0.0) for p in range(2)], axis=0).astype(bf16)
            s = lax.dot_general(qm, k.astype(bf16), _NT, preferred_element_type=jnp.float32)
            out, m, den = _softmax_pv(jnp.where(valid, s, NEG), v.astype(bf16))
            lse = m + jnp.log(den)
            og_ref[g, pl.ds(q0, T, stride=dil), :] = jnp.where(lo, out[:T], out[T:])
            lse_ref[g, pl.ds(q0, T, stride=dil), :] = jnp.where(lo, lse[:T], lse[T:])
            return carry

        lax.fori_loop(0, dil * nbc, unit, 0, unroll=4)

    def merge(i, carry):
        r0 = pl.multiple_of(i * T, T)
        ls = [lse_ref[g, pl.ds(r0, T), :] for g in range(len(DIL_PAIRS))]
        m = jnp.maximum(jnp.maximum(ls[0], ls[1]), ls[2])
        ws = [jnp.exp(l - m) for l in ls]
        tot = ws[0] + ws[1] + ws[2]
        out = sum((w / tot) * og_ref[g, pl.ds(r0, T), :] for g, w in enumerate(ws))
        o_ref[pl.ds(r0, T), :] = out.astype(o_ref.dtype)
        return carry

    lax.fori_loop(0, L // T, merge, 0)


def attn_dilated(qb, kb, vb, B, L):
    ng = len(DIL_PAIRS)
    fmaj = pl.BlockSpec((None, LANES, L), lambda b, hp: (b, hp, 0))
    return pl.pallas_call(
        functools.partial(_attn_dil_body, L),
        grid=(B, 2),
        in_specs=[pl.BlockSpec((ng, L, LANES), lambda b, hp: (hp, b, 0)),
                  pl.BlockSpec((L, LANES), lambda b, hp: (b, hp)),
                  pl.BlockSpec((L, LANES), lambda b, hp: (b, hp))],
        out_specs=[pl.BlockSpec((L, LANES), lambda b, hp: (b, hp)), fmaj, fmaj],
        out_shape=[jax.ShapeDtypeStruct((B * L, 2 * LANES), jnp.bfloat16),
                   jax.ShapeDtypeStruct((B, 2 * LANES, L), jnp.float32),
                   jax.ShapeDtypeStruct((B, 2 * LANES, L), jnp.float32)],
        scratch_shapes=[pltpu.VMEM((ng, L, LANES), jnp.float32), pltpu.VMEM((ng, L, LANES), jnp.float32)],
        compiler_params=_cparams(("parallel", "parallel")), name="attn_dilated")(qb, kb, vb)


N_NEW = 4
SROWS = 8


def _softmax2_pv(s_c, s_n, vt_c, vt_n, sink=None):
    m = jnp.maximum(jnp.max(s_c, axis=-1, keepdims=True), jnp.max(s_n, axis=-1, keepdims=True))
    if sink is not None:
        m = jnp.maximum(m, sink)
    e_c = jnp.exp(s_c - m)
    e_n = jnp.exp(s_n - m)
    den = jnp.sum(e_c, axis=-1, keepdims=True) + jnp.sum(e_n, axis=-1, keepdims=True)
    if sink is not None:
        den = den + jnp.exp(sink - m)
    r = (lax.dot_general(e_c.astype(jnp.bfloat16), vt_c, _NT, preferred_element_type=jnp.float32)
         + lax.dot_general(e_n.astype(jnp.bfloat16), vt_n, _NT, preferred_element_type=jnp.float32))
    return r / den, m, den


def _advance(old_t, new_t):
    n = old_t.shape[1]
    lane = lax.broadcasted_iota(jnp.int32, (1, LANES), 1)
    shifted = pltpu.roll(old_t, n - N_NEW, axis=1)
    last = jnp.where(lane < LANES - N_NEW, shifted[:, n - LANES:], new_t)
    if n == LANES:
        return last
    return jnp.concatenate([shifted[:, :n - LANES], last], axis=1)


def _attn_sample_body(bt, sink_ref, qa_ref, qb_ref, qc_ref, nka_ref, nva_ref, nkb_ref, nvb_ref,
                      cak_ref, cav_ref, cbk_ref, cbv_ref, cmk_ref, cmv_ref,
                      oac_ref, ob_ref, oak_ref, oav_ref, obk_ref, obv_ref):
    f32, bf16 = jnp.float32, jnp.bfloat16
    lo, hi = _half_masks()
    halves = (lo, hi)
    S = SROWS
    la = cak_ref.shape[2]
    lb = cbk_ref.shape[2]
    wa = SWA_Q_HEADS * HEAD_DIM
    new0 = LANES - N_NEW

    na = SWA_Q_HEADS * S
    ia = lax.broadcasted_iota(jnp.int32, (na, la), 0) & (S - 1)
    valid_ac = lax.broadcasted_iota(jnp.int32, (na, la), 1) >= ia + 1
    ja = lax.broadcasted_iota(jnp.int32, (na, LANES), 1) - new0
    valid_an = (ja >= 0) & (ja <= (lax.broadcasted_iota(jnp.int32, (na, LANES), 0) & (S - 1)))
    rcol = lax.broadcasted_iota(jnp.int32, (na, 1), 0)
    sink_col = jnp.zeros((na, 1), f32)
    for h in range(SWA_Q_HEADS):
        sink_col = jnp.where((rcol >> 3) == h, sink_ref[h], sink_col)

    nb_rows = len(DIL_PAIRS) * 2 * S
    rb = lax.broadcasted_iota(jnp.int32, (nb_rows, lb), 0)
    t_c = lb + (rb & (S - 1)) - lax.broadcasted_iota(jnp.int32, (nb_rows, lb), 1)
    rn = lax.broadcasted_iota(jnp.int32, (nb_rows, LANES), 0)
    jn = lax.broadcasted_iota(jnp.int32, (nb_rows, LANES), 1) - new0
    t_n = (rn & (S - 1)) - jn
    valid_bc = jnp.zeros((nb_rows, lb), jnp.bool_)
    valid_bn = jnp.zeros((nb_rows, LANES), jnp.bool_)
    for g, (window, dil) in enumerate(DIL_PAIRS):
        valid_bc = valid_bc | (((rb >> 4) == g) & (t_c <= window) & ((t_c & (dil - 1)) == 0))
        valid_bn = valid_bn | (((rn >> 4) == g) & (jn >= 0) & (t_n >= 0) & ((t_n & (dil - 1)) == 0))

    def new_t(x):
        padded = jnp.concatenate([x, jnp.zeros((LANES - S, x.shape[1]), f32)], axis=0)
        return pltpu.roll(padded.T, new0, axis=1)

    for b in range(bt):
        rows = slice(b * S, (b + 1) * S)
        nka, nva = new_t(nka_ref[rows, :]), new_t(nva_ref[rows, :])
        nkb, nvb = new_t(nkb_ref[rows, :]), new_t(nvb_ref[rows, :])
        oak_ref[b] = _advance(cak_ref[b], nka)
        oav_ref[b] = _advance(cav_ref[b], nva)
        obk_ref[b] = _advance(cbk_ref[b], nkb)
        obv_ref[b] = _advance(cbv_ref[b], nvb)

        pieces = []
        for h in range(SWA_Q_HEADS):
            q = jnp.where(halves[h % 2], qa_ref[rows, (h // 2) * LANES:(h // 2 + 1) * LANES].astype(f32), 0.0)
            if h % 2 != h // 4:
                q = pltpu.roll(q, HEAD_DIM, axis=1)
            pieces.append(q)
        qm = jnp.concatenate(pieces, axis=0).astype(bf16)
        s_c = jnp.dot(qm, cak_ref[b].astype(bf16), preferred_element_type=f32)
        s_n = jnp.dot(qm, nka.astype(bf16), preferred_element_type=f32)
        out, _, _ = _softmax2_pv(jnp.where(valid_ac, s_c, NEG), jnp.where(valid_an, s_n, NEG),
                                 cav_ref[b].astype(bf16), nva.astype(bf16), sink_col)
        for j in range(SWA_Q_HEADS // 2):
            parts = []
            for p in range(2):
                h = 2 * j + p
                o = out[h * S:(h + 1) * S]
                if h % 2 != h // 4:
                    o = pltpu.roll(o, HEAD_DIM, axis=1)
                parts.append(o)
            oac_ref[rows, j * LANES:(j + 1) * LANES] = jnp.where(lo, parts[0], parts[1])

        for j in range(MEM_HEADS // 2):
            cols = slice(j * LANES, (j + 1) * LANES)
            q = qc_ref[rows, cols]
            qm = jnp.concatenate([jnp.where(halves[p], q, jnp.zeros_like(q)) for p in range(2)], axis=0)
            s = jnp.dot(qm, cmk_ref[b, cols, :].astype(bf16), preferred_element_type=f32)
            m = jnp.max(s, axis=-1, keepdims=True)
            e = jnp.exp(s - m)
            r = lax.dot_general(e.astype(bf16), cmv_ref[b, cols, :].astype(bf16), _NT, preferred_element_type=f32)
            out = r / jnp.sum(e, axis=-1, keepdims=True)
            oac_ref[rows, wa + j * LANES:wa + (j + 1) * LANES] = jnp.where(lo, out[:S], out[S:])

        for hp in range(DIL_KV_HEADS // 2):
            cols = slice(hp * LANES, (hp + 1) * LANES)
            pieces = [jnp.where(halves[p], qb_ref[hp * len(DIL_PAIRS) + g, rows, :], 0.0)
                      for g in range(len(DIL_PAIRS)) for p in range(2)]
            qm = jnp.concatenate(pieces, axis=0).astype(bf16)
            s_c = jnp.dot(qm, cbk_ref[b, cols, :].astype(bf16), preferred_element_type=f32)
            s_n = jnp.dot(qm, nkb[cols, :].astype(bf16), preferred_element_type=f32)
            out, m, den = _softmax2_pv(jnp.where(valid_bc, s_c, NEG), jnp.where(valid_bn, s_n, NEG),
                                       cbv_ref[b, cols, :].astype(bf16), nvb[cols, :].astype(bf16))
            lse = m + jnp.log(den)
            res = []
            for p in range(2):
                r = [slice((g * 2 + p) * S, (g * 2 + p + 1) * S) for g in range(len(DIL_PAIRS))]
                mx = jnp.maximum(jnp.maximum(lse[r[0]], lse[r[1]]), lse[r[2]])
                w = [jnp.exp(lse[x] - mx) for x in r]
                tot = w[0] + w[1] + w[2]
                res.append(sum((w[g] / tot) * out[r[g]] for g in range(len(DIL_PAIRS))))
            ob_ref[rows, cols] = jnp.where(lo, res[0], res[1])


def attn_sample(sinks, qa, qb, qc, nka, nva, nkb, nvb, cak, cav, cbk, cbv, cmk, cmv, bt):
    NB, wka, la = cak.shape
    wkb, lb = cbk.shape[1:]
    wm, M = cmk.shape[1:]
    wa, wc = SWA_Q_HEADS * HEAD_DIM, MEM_HEADS * HEAD_DIM
    ng = len(DIL_PAIRS)
    tok = lambda w: pl.BlockSpec((bt * SROWS, w), lambda i: (i, 0))
    buf = lambda f, n: pl.BlockSpec((bt, f, n), lambda i: (i, 0, 0))
    f32 = jnp.float32
    return pl.pallas_call(
        functools.partial(_attn_sample_body, bt),
        grid=(NB // bt,),
        in_specs=[pl.BlockSpec(memory_space=pltpu.SMEM), tok(wa),
                  pl.BlockSpec((2 * ng, bt * SROWS, LANES), lambda i: (0, i, 0)), tok(wc),
                  tok(wka), tok(wka), tok(wkb), tok(wkb),
                  buf(wka, la), buf(wka, la), buf(wkb, lb), buf(wkb, lb), buf(wm, M), buf(wm, M)],
        out_specs=[tok(wa + wc), tok(wkb), buf(wka, la), buf(wka, la), buf(wkb, lb), buf(wkb, lb)],
        out_shape=[jax.ShapeDtypeStruct((NB * SROWS, wa + wc), f32), jax.ShapeDtypeStruct((NB * SROWS, wkb), f32),
                   jax.ShapeDtypeStruct(cak.shape, f32), jax.ShapeDtypeStruct(cak.shape, f32),
                   jax.ShapeDtypeStruct(cbk.shape, f32), jax.ShapeDtypeStruct(cbk.shape, f32)],
        compiler_params=_cparams(("parallel",)), name="attn_sample")(
            sinks, qa, qb, qc, nka, nva, nkb, nvb, cak, cav, cbk, cbv, cmk, cmv)


MOE_BLOCK = 256
ROUTE_SUB = 256
SUBLANES = 8


def _store_row_tiles(ref, row0, y):
    n = y.shape[0]
    for c in range(SUBLANES):
        ref[pl.ds(row0 * SUBLANES + c, n, stride=SUBLANES), :] = y[:, c * LANES:(c + 1) * LANES]


def _load_row_tiles(ref, row0, n):
    return jnp.concatenate([ref[pl.ds(row0 * SUBLANES + c, n, stride=SUBLANES), :] for c in range(SUBLANES)],
                           axis=1)


def _row_tile(ref, r):
    return ref.at[pl.ds(pl.multiple_of(r * SUBLANES, SUBLANES), SUBLANES)]


def _merge_route_body(oac_ref, ob_ref, gate_ref, x_ref, wa_ref, wb_ref, wc_ref, wo_ref, gffn_ref, wr_ref, br_ref,
                      cnt0_ref, x1_ref, h2_ref, mi_ref, mf_ref, cnt_ref, base_ref):
    f32, bf16 = jnp.float32, jnp.bfloat16
    D = x_ref.shape[1]
    wa = SWA_Q_HEADS * HEAD_DIM
    ts = ROUTE_SUB

    @pl.when(pl.program_id(0) == 0)
    def _():
        base_ref[...] = cnt0_ref[...]

    erow = lax.broadcasted_iota(jnp.int32, (N_EXPERTS, ts), 0)
    r8 = lax.broadcasted_iota(jnp.int32, (SUBLANES, ts), 0)
    ti = lax.broadcasted_iota(jnp.int32, (ts, ts), 0)
    tj = lax.broadcasted_iota(jnp.int32, (ts, ts), 1)
    later = (ti < tj).astype(bf16)
    base = base_ref[:, 0:1]
    for sub in range(x_ref.shape[0] // ts):
        rows = slice(sub * ts, (sub + 1) * ts)
        ma = jnp.dot(oac_ref[rows, :wa].astype(bf16), wa_ref[...], preferred_element_type=f32)
        mb = jnp.dot(ob_ref[rows, :].astype(bf16), wb_ref[...], preferred_element_type=f32)
        mc = jnp.dot(oac_ref[rows, wa:].astype(bf16), wc_ref[...], preferred_element_type=f32)
        merged = (gate_ref[rows, :D].astype(f32) * ma + gate_ref[rows, D:2 * D].astype(f32) * mb
                  + gate_ref[rows, 2 * D:].astype(f32) * mc)
        x1 = x_ref[rows, :] + jnp.dot(merged.astype(bf16), wo_ref[...], preferred_element_type=f32)
        x1_ref[rows, :] = x1
        h2 = _rms(x1, gffn_ref[...])
        _store_row_tiles(h2_ref, sub * ts, h2)

        work = lax.dot_general(wr_ref[...], h2.astype(bf16), _NT, preferred_element_type=f32) + br_ref[:, 0:1]
        vals, idxs = [], []
        for _ in range(TOP_K):
            m = jnp.max(work, axis=0, keepdims=True)
            idx = jnp.min(jnp.where(work == m, erow, N_EXPERTS), axis=0, keepdims=True)
            vals.append(m)
            idxs.append(idx)
            work = jnp.where(erow == idx, -jnp.inf, work)
        es = [jnp.exp(v - vals[0]) for v in vals]
        tot = es[0] + es[1] + es[2] + es[3]

        onehot = [(erow == idx).astype(f32) for idx in idxs]
        assign = onehot[0] + onehot[1] + onehot[2] + onehot[3]
        before = jnp.dot(assign.astype(bf16), later, preferred_element_type=f32) + base
        base = base + jnp.sum(assign, axis=1, keepdims=True)

        mi = jnp.zeros((SUBLANES, ts), jnp.int32)
        gates = jnp.zeros((SUBLANES, ts), f32)
        for k in range(TOP_K):
            rank = jnp.sum(onehot[k] * before, axis=0, keepdims=True).astype(jnp.int32)
            mi = jnp.where(r8 == k, idxs[k], mi)
            mi = jnp.where(r8 == TOP_K + k, rank, mi)
            gates = jnp.where(r8 == k, es[k] / tot, gates)
        mi_ref[:, rows] = mi
        mf_ref[rows, :] = jnp.concatenate([gates, jnp.zeros((LANES - SUBLANES, ts), f32)], axis=0).T
    base_ref[...] = jnp.broadcast_to(base, base_ref.shape)
    cnt_ref[...] = jnp.broadcast_to(base, cnt_ref.shape)


def merge_route(o_ac, o_b, gates, x, wa, wb, wc, wo, g_ffn, wr, br, cnt0, tm):
    R, D = x.shape
    row = lambda w: pl.BlockSpec((tm, w), lambda i: (i, 0))
    full = lambda a: pl.BlockSpec(a.shape, lambda i: (0, 0))
    return pl.pallas_call(
        _merge_route_body,
        grid=(R // tm,),
        in_specs=[row(o_ac.shape[1]), row(o_b.shape[1]), row(gates.shape[1]), row(D),
                  full(wa), full(wb), full(wc), full(wo), full(g_ffn), full(wr), full(br), full(cnt0)],
        out_specs=[row(D), pl.BlockSpec((tm * SUBLANES, LANES), lambda i: (i, 0)),
                   pl.BlockSpec((SUBLANES, tm), lambda i: (0, i)), row(LANES),
                   pl.BlockSpec((N_EXPERTS, LANES), lambda i: (0, 0))],
        out_shape=[jax.ShapeDtypeStruct((R, D), jnp.float32), jax.ShapeDtypeStruct((R * SUBLANES, LANES), jnp.float32),
                   jax.ShapeDtypeStruct((SUBLANES, R), jnp.int32), jax.ShapeDtypeStruct((R, LANES), jnp.float32),
                   jax.ShapeDtypeStruct((N_EXPERTS, LANES), jnp.float32)],
        scratch_shapes=[pltpu.VMEM((N_EXPERTS, LANES), jnp.float32)],
        compiler_params=_cparams(("arbitrary",)), name="merge_route")(
            o_ac, o_b, gates, x, wa, wb, wc, wo, g_ffn, wr, br, cnt0)


def _route_tables_body(cnt_ref, mi_ref, dest_ref, blk_ref, pad_ref):
    tm = mi_ref.shape[1]
    nbl = blk_ref.shape[1]
    erow1 = lax.broadcasted_iota(jnp.int32, (N_EXPERTS, LANES), 0)
    shift = MOE_BLOCK.bit_length() - 1
    cnt = cnt_ref[...].astype(jnp.int32)
    padded = ((cnt + (MOE_BLOCK - 1)) >> shift) << shift
    pend = padded
    s = 1
    while s < N_EXPERTS:
        pend = pend + jnp.where(erow1 >= s, pltpu.roll(pend, s, axis=0), 0)
        s *= 2
    pstart = pend - padded
    mi = mi_ref[...]
    erow = lax.broadcasted_iota(jnp.int32, (N_EXPERTS, tm), 0)
    r8 = lax.broadcasted_iota(jnp.int32, (SUBLANES, tm), 0)
    dest = jnp.zeros((SUBLANES, tm), jnp.int32)
    for k in range(TOP_K):
        start = jnp.sum(jnp.where(erow == mi[k:k + 1, :], pstart[:, 0:1], 0), axis=0, keepdims=True)
        dest = jnp.where(r8 == k, start + mi[TOP_K + k:TOP_K + k + 1, :], dest)
    dest_ref[...] = dest

    @pl.when(pl.program_id(0) == 0)
    def _():
        row0 = lax.broadcasted_iota(jnp.int32, (N_EXPERTS, nbl), 1) * MOE_BLOCK
        ended = jnp.sum(jnp.where(pend[:, 0:1] <= row0, 1, 0), axis=0, keepdims=True)
        blk_ref[...] = jnp.broadcast_to(jnp.minimum(ended, N_EXPERTS - 1), blk_ref.shape)
        pad_ref[0] = pstart + cnt
        pad_ref[1] = pend


def route_tables(cnt, mi, nbl, tm):
    R = mi.shape[1]
    return pl.pallas_call(
        _route_tables_body,
        grid=(R // tm,),
        in_specs=[pl.BlockSpec((N_EXPERTS, LANES), lambda i: (0, 0)), pl.BlockSpec((SUBLANES, tm), lambda i: (0, i))],
        out_specs=[pl.BlockSpec((SUBLANES, tm), lambda i: (0, i)), pl.BlockSpec((SUBLANES, nbl), lambda i: (0, 0)),
                   pl.BlockSpec((2, N_EXPERTS, LANES), lambda i: (0, 0, 0))],
        out_shape=[jax.ShapeDtypeStruct((SUBLANES, R), jnp.int32), jax.ShapeDtypeStruct((SUBLANES, nbl), jnp.int32),
                   jax.ShapeDtypeStruct((2, N_EXPERTS, LANES), jnp.int32)],
        compiler_params=_cparams(("arbitrary",)), name="route_tables")(cnt, mi)


def _dispatch_body(n_first, n_steps, n_zero, pad0_ref, pad1_ref, dest_ref, ha_ref, hb_ref, xs_ref, zero_ref, sem):
    i = pl.program_id(0)
    tm = ha_ref.shape[0] // SUBLANES
    rs = xs_ref.shape[0] // SUBLANES

    def scatter_tile(h_ref):
        def row_copy(r, k):
            return pltpu.make_async_copy(_row_tile(h_ref, r), _row_tile(xs_ref, dest_ref[0, 0, k * tm + r]), sem)

        def start(r, c):
            for k in range(TOP_K):
                row_copy(r, k).start(priority=k % 2)
            return c

        def wait(r, c):
            for k in range(TOP_K):
                row_copy(r, k).wait()
            return c

        lax.fori_loop(0, tm, start, 0, unroll=8)
        lax.fori_loop(0, tm, wait, 0, unroll=8)

    @pl.when(i < n_first)
    def _():
        scatter_tile(ha_ref)

    @pl.when(i >= n_first)
    def _():
        scatter_tile(hb_ref)

    @pl.when(i == n_steps - 1)
    def _():
        zero_ref[...] = jnp.zeros_like(zero_ref)

        def zero_copy(r):
            return pltpu.make_async_copy(zero_ref, _row_tile(xs_ref, r), sem)

        def start(r, c):
            zero_copy(r).start()
            return c

        def wait(r, c):
            zero_copy(0).wait()
            return c

        for e in range(N_EXPERTS):
            lax.fori_loop(pad0_ref[e], pad1_ref[e], start, 0)
        lax.fori_loop(pad1_ref[N_EXPERTS - 1], rs, start, 0)
        lax.fori_loop(0, n_zero, wait, 0)


def dispatch(pad0, pad1, dest3, h_a, h_b, rs, tm):
    S = SUBLANES
    n_a, n_b = h_a.shape[0] // (tm * S), h_b.shape[0] // (tm * S)
    n_zero = rs - (h_a.shape[0] + h_b.shape[0]) // S * TOP_K
    grid_spec = pltpu.PrefetchScalarGridSpec(
        num_scalar_prefetch=2,
        grid=(n_a + n_b,),
        in_specs=[pl.BlockSpec((1, 1, TOP_K * tm), lambda i, p0, p1: (i, 0, 0), memory_space=pltpu.SMEM),
                  pl.BlockSpec((tm * S, LANES), lambda i, p0, p1: (jnp.minimum(i, n_a - 1), 0)),
                  pl.BlockSpec((tm * S, LANES), lambda i, p0, p1: (jnp.maximum(i - n_a, 0), 0))],
        out_specs=pl.BlockSpec(memory_space=pl.ANY),
        scratch_shapes=[pltpu.VMEM((S, LANES), h_a.dtype), pltpu.SemaphoreType.DMA])
    return pl.pallas_call(
        functools.partial(_dispatch_body, n_a, n_a + n_b, n_zero), grid_spec=grid_spec,
        out_shape=jax.ShapeDtypeStruct((rs * S, LANES), h_a.dtype),
        compiler_params=_cparams(("arbitrary",)), name="dispatch")(pad0, pad1, dest3, h_a, h_b)


def _moe_body(nblk, be_ref, nact_ref, x_ref, wgu_hbm, bgu_ref, wd_hbm, bd_ref, y_ref, wgu_f, wd_f, wgu_s, wd_s,
              sems):
    f32, bf16 = jnp.float32, jnp.bfloat16
    b = pl.program_id(0)
    e = be_ref[b]

    def fetch(ex, go):
        for src, dst, s in ((wgu_hbm, wgu_f, 0), (wd_hbm, wd_f, 1)):
            cp = pltpu.make_async_copy(src.at[ex], dst, sems.at[s])
            cp.start() if go else cp.wait()

    @pl.when(b == 0)
    def _():
        fetch(e, True)

    @pl.when((b == 0) | (e != be_ref[jnp.maximum(b - 1, 0)]))
    def _():
        fetch(e, False)
        wgu_s[...] = wgu_f[...].astype(bf16)
        wd_s[...] = wd_f[...].astype(bf16)
        nxt = lax.while_loop(lambda j: (j < nblk) & (be_ref[jnp.minimum(j, nblk - 1)] == e), lambda j: j + 1, b + 1)

        @pl.when(nxt < nblk)
        def _():
            fetch(be_ref[jnp.minimum(nxt, nblk - 1)], True)

    @pl.when(b < nact_ref[0])
    def _():
        x = _load_row_tiles(x_ref, 0, MOE_BLOCK).astype(bf16)
        gu = jnp.dot(x, wgu_s[...], preferred_element_type=f32) + bgu_ref[0]
        gt = jnp.minimum(gu[:, :D_FF], SWIGLU_LIMIT)
        up = jnp.clip(gu[:, D_FF:], -SWIGLU_LIMIT, SWIGLU_LIMIT)
        act = (up + 1.0) * (gt * jax.nn.sigmoid(gt * SWIGLU_ALPHA))
        y = jnp.dot(act.astype(bf16), wd_s[...], preferred_element_type=f32) + bd_ref[0]
        _store_row_tiles(y_ref, 0, y)

    @pl.when(b >= nact_ref[0])
    def _():
        y_ref[...] = jnp.zeros_like(y_ref)


def moe_ffn(blk_e, n_active, xs, w_gate_up, b_gate_up, w_down, b_down):
    RS = xs.shape[0] // SUBLANES
    E, D, F2 = w_gate_up.shape
    blk_rows = MOE_BLOCK * SUBLANES
    grid_spec = pltpu.PrefetchScalarGridSpec(
        num_scalar_prefetch=2,
        grid=(RS // MOE_BLOCK,),
        in_specs=[pl.BlockSpec((blk_rows, LANES), lambda b, be, na: (b, 0)),
                  pl.BlockSpec(memory_space=pl.ANY),
                  pl.BlockSpec((1, 1, F2), lambda b, be, na: (be[b], 0, 0)),
                  pl.BlockSpec(memory_space=pl.ANY),
                  pl.BlockSpec((1, 1, D), lambda b, be, na: (be[b], 0, 0))],
        out_specs=pl.BlockSpec((blk_rows, LANES), lambda b, be, na: (b, 0)),
        scratch_shapes=[pltpu.VMEM((D, F2), w_gate_up.dtype), pltpu.VMEM((F2 // 2, D), w_down.dtype),
                        pltpu.VMEM((D, F2), jnp.bfloat16), pltpu.VMEM((F2 // 2, D), jnp.bfloat16),
                        pltpu.SemaphoreType.DMA((2,))])
    return pl.pallas_call(
        functools.partial(_moe_body, RS // MOE_BLOCK), grid_spec=grid_spec,
        out_shape=jax.ShapeDtypeStruct(xs.shape, jnp.float32),
        compiler_params=_cparams(("arbitrary",)), name="moe_ffn")(
            blk_e, n_active, xs, w_gate_up, b_gate_up.reshape(E, 1, F2), w_down, b_down.reshape(E, 1, D))


def _combine_body(n_tiles, dest_ref, dest_next_ref, x1_ref, mf_ref, g_ref, ys_ref, o_ref, buf, sems):
    tm = x1_ref.shape[0]
    i = pl.program_id(0)
    slot = i % 2
    per_slot = TOP_K * tm

    def gather(d_ref, s, go):
        def row_copy(r, k):
            return pltpu.make_async_copy(_row_tile(ys_ref, d_ref[0, 0, k * tm + r]),
                                         _row_tile(buf, s * per_slot + k * tm + r), sems.at[s])

        def body(r, c):
            for k in range(TOP_K):
                if go:
                    row_copy(r, k).start(priority=k % 2)
                else:
                    row_copy(r, k).wait()
            return c

        lax.fori_loop(0, tm, body, 0, unroll=8)

    @pl.when(i == 0)
    def _():
        gather(dest_ref, slot, True)

    @pl.when(i + 1 < n_tiles)
    def _():
        gather(dest_next_ref, 1 - slot, True)

    gather(dest_ref, slot, False)
    y = x1_ref[...]
    for k in range(TOP_K):
        y = y + mf_ref[:, k:k + 1] * _load_row_tiles(buf, slot * per_slot + k * tm, tm)
    o_ref[...] = _rms(y, g_ref[...])


def combine(dest3, x1, mf, g_final, ys, tm):
    R, D = x1.shape
    n = R // tm
    dest_spec = lambda f: pl.BlockSpec((1, 1, TOP_K * tm), f, memory_space=pltpu.SMEM)
    return pl.pallas_call(
        functools.partial(_combine_body, n),
        grid=(n,),
        in_specs=[dest_spec(lambda i: (i, 0, 0)), dest_spec(lambda i: (jnp.minimum(i + 1, n - 1), 0, 0)),
                  pl.BlockSpec((tm, D), lambda i: (i, 0)),
                  pl.BlockSpec((tm, LANES), lambda i: (i, 0)),
                  pl.BlockSpec((1, D), lambda i: (0, 0)),
                  pl.BlockSpec(memory_space=pl.ANY)],
        out_specs=pl.BlockSpec((tm, D), lambda i: (i, 0)),
        out_shape=jax.ShapeDtypeStruct((R, D), jnp.float32),
        scratch_shapes=[pltpu.VMEM((2 * TOP_K * tm * SUBLANES, LANES), ys.dtype), pltpu.SemaphoreType.DMA((2,))],
        compiler_params=_cparams(("arbitrary",)), name="combine")(dest3, dest3, x1, mf, g_final, ys)


def moe_layer(group_a, group_b, cnt, g_final, w_gate_up, b_gate_up, w_down, b_down, tm):
    n_assign = (group_a[0].shape[0] + group_b[0].shape[0]) * TOP_K
    nb = (n_assign + N_EXPERTS * (MOE_BLOCK - 1)) // MOE_BLOCK + 1
    nbl = -(-nb // LANES) * LANES
    dests = []
    for x1, h2, mi, mf in (group_a, group_b):
        dest, blk, pad = route_tables(cnt, mi, nbl, min(x1.shape[0], 4 * tm))
        n_tiles = x1.shape[0] // tm
        dests.append(jnp.transpose(dest[:TOP_K].reshape(TOP_K, n_tiles, tm), (1, 0, 2)).reshape(n_tiles, 1, TOP_K * tm))
    xs = dispatch(pad[0, :, 0], pad[1, :, 0], jnp.concatenate(dests), group_a[1], group_b[1], nb * MOE_BLOCK, tm)
    n_active = (pad[1, N_EXPERTS - 1, 0] // MOE_BLOCK).reshape(1)
    ys = moe_ffn(blk[0, :nb], n_active, xs, w_gate_up, b_gate_up, w_down, b_down)
    return [combine(dest3, x1, mf, g_final, ys, tm) for dest3, (x1, h2, mi, mf) in zip(dests, (group_a, group_b))]


def kernel(x_prompt, x_sample, cache_swa_k, cache_swa_v, cache_dil_k, cache_dil_v, cache_mem_k, cache_mem_v, mem_prompt, norm_attn, norm_mem, w_in, w_mem_kv, sinks, w_br_a, w_br_b, w_br_c, w_out, norm_ffn, w_router, b_router, w_gate_up, b_gate_up, w_down, b_down, norm_final):
    f32, bf16 = jnp.float32, jnp.bfloat16
    TM = 256
    B, L, D = x_prompt.shape
    NB, n_new, _ = x_sample.shape
    M = mem_prompt.shape[1]
    la, lb = cache_swa_k.shape[2], cache_dil_k.shape[2]
    wka, wkb, wc = SWA_KV_HEADS * HEAD_DIM, DIL_KV_HEADS * HEAD_DIM, MEM_HEADS * HEAD_DIM
    assert n_new == N_NEW and cache_swa_k.shape[0] == 1

    w_in_b = w_in[0].astype(bf16)
    g_attn = norm_attn[0].reshape(1, D)
    secs = in_sections()

    tabs_p = rope_tables(jnp.arange(L, dtype=jnp.int32))
    qa, ka, va, qb, kb, vb, qc, gates = norm_proj(x_prompt.reshape(B * L, D), g_attn, w_in_b, tabs_p, secs, TM)
    mk, mv = norm_proj(mem_prompt.reshape(B * M, D), norm_mem[0].reshape(1, D), w_mem_kv[0].astype(bf16), None,
                       [(_chunks(0, wc), "plain", False, f32), (_chunks(wc, wc), "plain", False, f32)], TM)
    o_ac = attn_swa_mem(sinks[0], qa, ka, va, qc, mk, mv, B, L)
    o_b, kb_t, vb_t = attn_dilated(qb, kb, vb, B, L)

    xs_pad = jnp.pad(x_sample, ((0, 0), (0, SROWS - N_NEW), (0, 0))).reshape(NB * SROWS, D)
    tabs_s = rope_tables(PAST_LEN + (jnp.arange(TM, dtype=jnp.int32) % SROWS))
    qa_s, ka_s, va_s, qb_s, kb_s, vb_s, qc_s, gates_s = norm_proj(xs_pad, g_attn, w_in_b, tabs_s, secs, TM)
    real = lambda t: t.reshape(NB, SROWS, -1)[:, :N_NEW].reshape(NB * N_NEW, -1)
    fmaj = lambda c: jnp.transpose(c[0], (0, 2, 3, 1)).reshape(NB, -1, c.shape[2])
    o_ac_s, o_b_s, swa_k_s, swa_v_s, dil_k_s, dil_v_s = attn_sample(
        sinks[0], qa_s, qb_s, qc_s, ka_s, va_s, kb_s, vb_s,
        fmaj(cache_swa_k), fmaj(cache_swa_v), fmaj(cache_dil_k), fmaj(cache_dil_v),
        fmaj(cache_mem_k), fmaj(cache_mem_v), 2)

    wr = w_router[0].T.astype(bf16)
    br = jnp.broadcast_to(b_router[0].astype(f32)[:, None], (N_EXPERTS, LANES))
    wts = (w_br_a[0].astype(bf16), w_br_b[0].astype(bf16), w_br_c[0].astype(bf16), w_out[0].astype(bf16),
           norm_ffn[0].reshape(1, D), wr, br)
    x1_p, h2_p, mi_p, mf_p, cnt_p = merge_route(o_ac, o_b, gates, x_prompt.reshape(B * L, D), *wts,
                                                jnp.zeros((N_EXPERTS, LANES), f32), 2 * ROUTE_SUB)
    x1_s, h2_s, mi_s, mf_s, cnt = merge_route(real(o_ac_s), real(o_b_s), real(gates_s),
                                              x_sample.reshape(NB * N_NEW, D), *wts, cnt_p, 2 * ROUTE_SUB)
    y_p, y_s = moe_layer((x1_p, h2_p, mi_p, mf_p), (x1_s, h2_s, mi_s, mf_s), cnt, norm_final.reshape(1, D),
                         w_gate_up[0], b_gate_up[0], w_down[0], b_down[0], TM)

    heads = lambda t, n, h: t.reshape(1, t.shape[0] // n, n, h, HEAD_DIM)
    tmaj = lambda t, h: jnp.transpose(t.reshape(t.shape[0], h, HEAD_DIM, t.shape[2]), (0, 3, 1, 2))[None]
    ka3, va3 = ka.reshape(B, L, wka), va.reshape(B, L, wka)
    la_p = min(SWA_WINDOW, L)
    return (y_p.reshape(B, L, D), y_s.reshape(NB, N_NEW, D),
            heads(ka3[:, L - la_p:].reshape(B * la_p, wka), la_p, SWA_KV_HEADS),
            heads(va3[:, L - la_p:].reshape(B * la_p, wka), la_p, SWA_KV_HEADS),
            tmaj(kb_t, DIL_KV_HEADS), tmaj(vb_t, DIL_KV_HEADS),
            heads(mk, M, MEM_HEADS), heads(mv, M, MEM_HEADS),
            tmaj(swa_k_s, SWA_KV_HEADS), tmaj(swa_v_s, SWA_KV_HEADS),
            tmaj(dil_k_s, DIL_KV_HEADS), tmaj(dil_v_s, DIL_KV_HEADS))
```

```python
import functools

import jax
import jax.numpy as jnp
import numpy as np
from jax import lax
from jax.experimental import pallas as pl
from jax.experimental.pallas import tpu as pltpu

D_MODEL = 1024
HEAD_DIM = 64
ROPE_DIM = 16
ROPE_HALF = 8
ROPE_THETA = 500000.0
PAST_LEN = 16384
SWA_Q_HEADS = 8
SWA_KV_HEADS = 2
SWA_WINDOW = 128
DIL_PAIRS = ((128, 1), (512, 4), (2048, 16))
DIL_KV_HEADS = 4
MEM_HEADS = 4
N_EXPERTS = 32
TOP_K = 4
D_FF = 1024
SWIGLU_LIMIT = 7.0
SWIGLU_ALPHA = 1.702
RMS_EPS = 1e-5
ATT_BLOCK = 128
SCALE = HEAD_DIM ** -0.5

LANES = 128
NEG = -1e30
VMEM_LIMIT = 56 * 1024 * 1024


def _cparams(sem):
    return pltpu.CompilerParams(dimension_semantics=sem, vmem_limit_bytes=VMEM_LIMIT)


def _rms(x, g):
    return x * lax.rsqrt(jnp.mean(x * x, axis=-1, keepdims=True) + RMS_EPS) * g


def _norm_proj_body(sections, x_ref, g_ref, w_ref, cs_ref, *out_refs):
    h = _rms(x_ref[...], g_ref[...]).astype(jnp.bfloat16)
    if cs_ref is not None:
        cos = cs_ref[0]
        sin_lo = cs_ref[1]
        sin_hi = cs_ref[2]
    dest = {lo: (o_ref, c, kind, slabs)
            for (cols, kind, slabs), o_ref in zip(sections, out_refs) for c, lo in enumerate(cols)}
    todo = sorted(dest)
    while todo:
        lo = todo.pop(0)
        n = 2 if todo and todo[0] == lo + LANES else 1
        if n == 2:
            todo.pop(0)
        yy = jnp.dot(h, w_ref[:, lo:lo + n * LANES], preferred_element_type=jnp.float32)
        for part in range(n):
            o_ref, c, kind, slabs = dest[lo + part * LANES]
            y = yy[:, part * LANES:(part + 1) * LANES]
            if kind in ("rope", "rope_q"):
                y = (y * cos + pltpu.roll(y, LANES - ROPE_HALF, axis=1) * sin_lo
                     + pltpu.roll(y, ROPE_HALF, axis=1) * sin_hi)
            if kind in ("rope_q", "q"):
                y = y * SCALE
            if kind == "sigmoid":
                y = jax.nn.sigmoid(y)
            if slabs:
                o_ref[c] = y.astype(o_ref.dtype)
            else:
                o_ref[:, c * LANES:(c + 1) * LANES] = y.astype(o_ref.dtype)


def rope_tables(pos):
    inv_freq = ROPE_THETA ** (-jnp.arange(ROPE_HALF, dtype=jnp.float32) / ROPE_HALF)
    ang = pos.astype(jnp.float32)[:, None] * inv_freq[None, :]
    cos, sin = jnp.cos(ang), jnp.sin(ang)
    n = pos.shape[0]
    one = jnp.ones((n, HEAD_DIM - ROPE_DIM), jnp.float32)
    zero = jnp.zeros((n, HEAD_DIM - ROPE_HALF), jnp.float32)
    c = jnp.concatenate([cos, cos, one], axis=1)
    s_lo = jnp.concatenate([-sin, zero], axis=1)
    s_hi = jnp.concatenate([jnp.zeros((n, ROPE_HALF), jnp.float32), sin,
                            jnp.zeros((n, HEAD_DIM - ROPE_DIM), jnp.float32)], axis=1)
    tab = jnp.stack([c, s_lo, s_hi])
    return jnp.concatenate([tab, tab], axis=2)


def norm_proj(x, g, w, tables, sections, tm):
    R, D = x.shape
    in_specs = [pl.BlockSpec((tm, D), lambda i: (i, 0)),
                pl.BlockSpec((1, D), lambda i: (0, 0)),
                pl.BlockSpec(w.shape, lambda i: (0, 0))]
    args = [x, g, w]
    if tables is not None:
        nt = tables.shape[1] // tm
        in_specs.append(pl.BlockSpec((3, tm, LANES), lambda i: (0, i % nt, 0)))
        args.append(tables)
    out_shape, out_specs, secs = [], [], []
    for (cols, kind, slabs, dtype) in sections:
        secs.append((cols, kind, slabs))
        width = LANES * len(cols)
        if slabs:
            out_shape.append(jax.ShapeDtypeStruct((width // LANES, R, LANES), dtype))
            out_specs.append(pl.BlockSpec((width // LANES, tm, LANES), lambda i: (0, i, 0)))
        else:
            out_shape.append(jax.ShapeDtypeStruct((R, width), dtype))
            out_specs.append(pl.BlockSpec((tm, width), lambda i: (i, 0)))
    if tables is None:
        body = lambda x_ref, g_ref, w_ref, *o: _norm_proj_body(secs, x_ref, g_ref, w_ref, None, *o)
    else:
        body = functools.partial(_norm_proj_body, secs)
    return pl.pallas_call(
        body, grid=(R // tm,), in_specs=in_specs, out_specs=out_specs, out_shape=out_shape,
        compiler_params=_cparams(("parallel",)), name="norm_proj")(*args)


def _chunks(start, width):
    return tuple(range(start, start + width, LANES))


def in_sections():
    f32, bf16 = jnp.float32, jnp.bfloat16
    qb0 = 1024 - 256
    qb_cols = tuple(qb0 + HEAD_DIM * (4 * g + 2 * hp) for hp in range(2) for g in range(3))
    return [
        (_chunks(0, 512), "rope_q", False, bf16),
        (_chunks(512, 128), "rope", False, f32),
        (_chunks(640, 128), "plain", False, f32),
        (qb_cols, "rope_q", True, f32),
        (_chunks(1536, 256), "rope", False, f32),
        (_chunks(1792, 256), "plain", False, f32),
        (_chunks(2048, 256), "q", False, bf16),
        (_chunks(2304, 3072), "sigmoid", False, bf16),
    ]


_NT = (((1,), (1,)), ((), ()))


def _half_masks():
    lane = lax.broadcasted_iota(jnp.int32, (1, LANES), 1)
    return lane < HEAD_DIM, lane >= HEAD_DIM


def _softmax_pv(s, v_half, sink=None):
    m = jnp.max(s, axis=-1, keepdims=True)
    if sink is not None:
        m = jnp.maximum(m, sink)
    e = jnp.exp(s - m)
    den = jnp.sum(e, axis=-1, keepdims=True)
    if sink is not None:
        den = den + jnp.exp(sink - m)
    r = jnp.dot(e.astype(jnp.bfloat16), v_half, preferred_element_type=jnp.float32)
    return r / den, m, den


def _attn_swa_mem_body(L, sink_ref, qa_ref, ka_ref, va_ref, qc_ref, mk_ref, mv_ref, o_ref):
    bf16 = jnp.bfloat16
    lo, hi = _half_masks()
    halves = (lo, hi)
    T = ATT_BLOCK
    G = SWA_Q_HEADS // SWA_KV_HEADS
    mem_k = [mk_ref[:, j * LANES:(j + 1) * LANES].astype(bf16) for j in range(2)]
    mem_v = [mv_ref[:, j * LANES:(j + 1) * LANES].astype(bf16) for j in range(2)]
    qi = lax.broadcasted_iota(jnp.int32, (2 * T, 2 * T), 0) & (T - 1)
    kj = lax.broadcasted_iota(jnp.int32, (2 * T, 2 * T), 1)
    hrow = lax.broadcasted_iota(jnp.int32, (2 * T, 1), 0) >> (T.bit_length() - 1)

    def heads_of(pair):
        return jnp.concatenate([jnp.where(halves[p], pair, jnp.zeros_like(pair)) for p in range(2)], axis=0)

    def block(blk, carry):
        r0 = pl.multiple_of(blk * T, T)
        ws = pl.multiple_of(jnp.maximum(r0 - T, 0), T)
        dist = qi - kj + (r0 - ws)
        valid = (dist >= 0) & (dist <= SWA_WINDOW - 1)
        k = ka_ref[pl.ds(ws, 2 * T), :]
        v = va_ref[pl.ds(ws, 2 * T), :]
        for kv in range(SWA_KV_HEADS):
            k1 = jnp.where(halves[kv], k, 0.0)
            v1 = jnp.where(halves[kv], v, 0.0)
            k_dup = (k1 + pltpu.roll(k1, HEAD_DIM, axis=1)).astype(bf16)
            v_dup = (v1 + pltpu.roll(v1, HEAD_DIM, axis=1)).astype(bf16)
            for j in range(kv * G // 2, (kv + 1) * G // 2):
                qm = heads_of(qa_ref[pl.ds(r0, T), j * LANES:(j + 1) * LANES])
                sink = jnp.where(hrow == 0, sink_ref[2 * j], sink_ref[2 * j + 1])
                s = lax.dot_general(qm, k_dup, _NT, preferred_element_type=jnp.float32)
                out, _, _ = _softmax_pv(jnp.where(valid, s, NEG), v_dup, sink)
                o_ref[pl.ds(r0, T), j * LANES:(j + 1) * LANES] = jnp.where(lo, out[:T], out[T:]).astype(o_ref.dtype)
        for j in range(MEM_HEADS // 2):
            qm = heads_of(qc_ref[pl.ds(r0, T), j * LANES:(j + 1) * LANES])
            s = lax.dot_general(qm, mem_k[j], _NT, preferred_element_type=jnp.float32)
            out, _, _ = _softmax_pv(s, mem_v[j])
            c0 = SWA_Q_HEADS * HEAD_DIM + j * LANES
            o_ref[pl.ds(r0, T), c0:c0 + LANES] = jnp.where(lo, out[:T], out[T:]).astype(o_ref.dtype)
        return carry

    lax.fori_loop(0, L // T, block, 0)


def attn_swa_mem(sinks, qa, ka, va, qc, mk, mv, B, L):
    M = mk.shape[0] // B
    wa, wc = SWA_Q_HEADS * HEAD_DIM, MEM_HEADS * HEAD_DIM
    row = lambda w: pl.BlockSpec((L, w), lambda b: (b, 0))
    return pl.pallas_call(
        functools.partial(_attn_swa_mem_body, L),
        grid=(B,),
        in_specs=[pl.BlockSpec(memory_space=pltpu.SMEM), row(wa), row(LANES), row(LANES), row(wc),
                  pl.BlockSpec((M, wc), lambda b: (b, 0)), pl.BlockSpec((M, wc), lambda b: (b, 0))],
        out_specs=row(wa + wc),
        out_shape=jax.ShapeDtypeStruct((B * L, wa + wc), jnp.bfloat16),
        compiler_params=_cparams(("parallel",)), name="attn_swa_mem")(sinks, qa, ka, va, qc, mk, mv)


def _attn_dil_body(L, qb_ref, kb_ref, vb_ref, o_ref, kt_ref, vt_ref, og_ref, lse_ref):
    bf16 = jnp.bfloat16
    lo, hi = _half_masks()
    halves = (lo, hi)
    T = ATT_BLOCK
    kt_ref[...] = kb_ref[...].T
    vt_ref[...] = vb_ref[...].T
    for g, (window, dil) in enumerate(DIL_PAIRS):
        lc = L // dil
        nbc = lc // T
        W = min(2 * T, lc)
        max_dist = window // dil
        qi = lax.broadcasted_iota(jnp.int32, (2 * T, W), 0) & (T - 1)
        kj = lax.broadcasted_iota(jnp.int32, (2 * T, W), 1)

        def unit(u, carry, g=g, dil=dil, nbc=nbc, W=W, max_dist=max_dist, qi=qi, kj=kj):
            c = u >> (nbc.bit_length() - 1)
            n = u & (nbc - 1)
            wsc = jnp.maximum(n * T - T, 0) if W == 2 * T else 0
            q0 = c + dil * T * n
            k0 = c + dil * wsc
            dist = qi - kj + (n * T - wsc)
            valid = (dist >= 0) & (dist <= max_dist)
            q = qb_ref[g, pl.ds(q0, T, stride=dil), :]
            k = kb_ref[pl.ds(k0, W, stride=dil), :]
            v = vb_ref[pl.ds(k0, W, stride=dil), :]
            qm = jnp.concatenate([jnp.where(halves[p], q, 0.0) for p in range(2)], axis=0).astype(bf16)
            s = lax.dot_general(qm, k.astype(bf16), _NT, preferred_element_type=jnp.float32)
            out, m, den = _softmax_pv(jnp.where(valid, s, NEG), v.astype(bf16))
            lse = m + jnp.log(den)
            og_ref[g, pl.ds(q0, T, stride=dil), :] = jnp.where(lo, out[:T], out[T:])
            lse_ref[g, pl.ds(q0, T, stride=dil), :] = jnp.where(lo, lse[:T], lse[T:])
            return carry

        lax.fori_loop(0, dil * nbc, unit, 0, unroll=4)

    def merge(i, carry):
        r0 = pl.multiple_of(i * T, T)
        ls = [lse_ref[g, pl.ds(r0, T), :] for g in range(len(DIL_PAIRS))]
        m = jnp.maximum(jnp.maximum(ls[0], ls[1]), ls[2])
        ws = [jnp.exp(l - m) for l in ls]
        tot = ws[0] + ws[1] + ws[2]
        out = sum((w / tot) * og_ref[g, pl.ds(r0, T), :] for g, w in enumerate(ws))
        o_ref[pl.ds(r0, T), :] = out.astype(o_ref.dtype)
        return carry

    lax.fori_loop(0, L // T, merge, 0)


def attn_dilated(qb, kb, vb, B, L):
    ng = len(DIL_PAIRS)
    fmaj = pl.BlockSpec((None, LANES, L), lambda b, hp: (b, hp, 0))
    return pl.pallas_call(
        functools.partial(_attn_dil_body, L),
        grid=(B, 2),
        in_specs=[pl.BlockSpec((ng, L, LANES), lambda b, hp: (hp, b, 0)),
                  pl.BlockSpec((L, LANES), lambda b, hp: (b, hp)),
                  pl.BlockSpec((L, LANES), lambda b, hp: (b, hp))],
        out_specs=[pl.BlockSpec((L, LANES), lambda b, hp: (b, hp)), fmaj, fmaj],
        out_shape=[jax.ShapeDtypeStruct((B * L, 2 * LANES), jnp.bfloat16),
                   jax.ShapeDtypeStruct((B, 2 * LANES, L), jnp.float32),
                   jax.ShapeDtypeStruct((B, 2 * LANES, L), jnp.float32)],
        scratch_shapes=[pltpu.VMEM((ng, L, LANES), jnp.float32), pltpu.VMEM((ng, L, LANES), jnp.float32)],
        compiler_params=_cparams(("parallel", "parallel")), name="attn_dilated")(qb, kb, vb)


N_NEW = 4
SROWS = 8


def _softmax2_pv(s_c, s_n, vt_c, vt_n, sink=None):
    m = jnp.maximum(jnp.max(s_c, axis=-1, keepdims=True), jnp.max(s_n, axis=-1, keepdims=True))
    if sink is not None:
        m = jnp.maximum(m, sink)
    e_c = jnp.exp(s_c - m)
    e_n = jnp.exp(s_n - m)
    den = jnp.sum(e_c, axis=-1, keepdims=True) + jnp.sum(e_n, axis=-1, keepdims=True)
    if sink is not None:
        den = den + jnp.exp(sink - m)
    r = (lax.dot_general(e_c.astype(jnp.bfloat16), vt_c, _NT, preferred_element_type=jnp.float32)
         + lax.dot_general(e_n.astype(jnp.bfloat16), vt_n, _NT, preferred_element_type=jnp.float32))
    return r / den, m, den


def _advance(old_t, new_t):
    n = old_t.shape[1]
    lane = lax.broadcasted_iota(jnp.int32, (1, LANES), 1)
    shifted = pltpu.roll(old_t, n - N_NEW, axis=1)
    last = jnp.where(lane < LANES - N_NEW, shifted[:, n - LANES:], new_t)
    if n == LANES:
        return last
    return jnp.concatenate([shifted[:, :n - LANES], last], axis=1)


def _attn_sample_body(bt, sink_ref, qa_ref, qb_ref, qc_ref, nka_ref, nva_ref, nkb_ref, nvb_ref,
                      cak_ref, cav_ref, cbk_ref, cbv_ref, cmk_ref, cmv_ref,
                      oac_ref, ob_ref, oak_ref, oav_ref, obk_ref, obv_ref):
    f32, bf16 = jnp.float32, jnp.bfloat16
    lo, hi = _half_masks()
    halves = (lo, hi)
    S = SROWS
    la = cak_ref.shape[2]
    lb = cbk_ref.shape[2]
    wa = SWA_Q_HEADS * HEAD_DIM
    new0 = LANES - N_NEW

    na = SWA_Q_HEADS * S
    ia = lax.broadcasted_iota(jnp.int32, (na, la), 0) & (S - 1)
    valid_ac = lax.broadcasted_iota(jnp.int32, (na, la), 1) >= ia + 1
    ja = lax.broadcasted_iota(jnp.int32, (na, LANES), 1) - new0
    valid_an = (ja >= 0) & (ja <= (lax.broadcasted_iota(jnp.int32, (na, LANES), 0) & (S - 1)))
    rcol = lax.broadcasted_iota(jnp.int32, (na, 1), 0)
    sink_col = jnp.zeros((na, 1), f32)
    for h in range(SWA_Q_HEADS):
        sink_col = jnp.where((rcol >> 3) == h, sink_ref[h], sink_col)

    nb_rows = len(DIL_PAIRS) * 2 * S
    rb = lax.broadcasted_iota(jnp.int32, (nb_rows, lb), 0)
    t_c = lb + (rb & (S - 1)) - lax.broadcasted_iota(jnp.int32, (nb_rows, lb), 1)
    rn = lax.broadcasted_iota(jnp.int32, (nb_rows, LANES), 0)
    jn = lax.broadcasted_iota(jnp.int32, (nb_rows, LANES), 1) - new0
    t_n = (rn & (S - 1)) - jn
    valid_bc = jnp.zeros((nb_rows, lb), jnp.bool_)
    valid_bn = jnp.zeros((nb_rows, LANES), jnp.bool_)
    for g, (window, dil) in enumerate(DIL_PAIRS):
        valid_bc = valid_bc | (((rb >> 4) == g) & (t_c <= window) & ((t_c & (dil - 1)) == 0))
        valid_bn = valid_bn | (((rn >> 4) == g) & (jn >= 0) & (t_n >= 0) & ((t_n & (dil - 1)) == 0))

    def new_t(x):
        padded = jnp.concatenate([x, jnp.zeros((LANES - S, x.shape[1]), f32)], axis=0)
        return pltpu.roll(padded.T, new0, axis=1)

    for b in range(bt):
        rows = slice(b * S, (b + 1) * S)
        nka, nva = new_t(nka_ref[rows, :]), new_t(nva_ref[rows, :])
        nkb, nvb = new_t(nkb_ref[rows, :]), new_t(nvb_ref[rows, :])
        oak_ref[b] = _advance(cak_ref[b], nka)
        oav_ref[b] = _advance(cav_ref[b], nva)
        obk_ref[b] = _advance(cbk_ref[b], nkb)
        obv_ref[b] = _advance(cbv_ref[b], nvb)

        pieces = []
        for h in range(SWA_Q_HEADS):
            q = jnp.where(halves[h % 2], qa_ref[rows, (h // 2) * LANES:(h // 2 + 1) * LANES].astype(f32), 0.0)
            if h % 2 != h // 4:
                q = pltpu.roll(q, HEAD_DIM, axis=1)
            pieces.append(q)
        qm = jnp.concatenate(pieces, axis=0).astype(bf16)
        s_c = jnp.dot(qm, cak_ref[b].astype(bf16), preferred_element_type=f32)
        s_n = jnp.dot(qm, nka.astype(bf16), preferred_element_type=f32)
        out, _, _ = _softmax2_pv(jnp.where(valid_ac, s_c, NEG), jnp.where(valid_an, s_n, NEG),
                                 cav_ref[b].astype(bf16), nva.astype(bf16), sink_col)
        for j in range(SWA_Q_HEADS // 2):
            parts = []
            for p in range(2):
                h = 2 * j + p
                o = out[h * S:(h + 1) * S]
                if h % 2 != h // 4:
                    o = pltpu.roll(o, HEAD_DIM, axis=1)
                parts.append(o)
            oac_ref[rows, j * LANES:(j + 1) * LANES] = jnp.where(lo, parts[0], parts[1])

        for j in range(MEM_HEADS // 2):
            cols = slice(j * LANES, (j + 1) * LANES)
            q = qc_ref[rows, cols]
            qm = jnp.concatenate([jnp.where(halves[p], q, jnp.zeros_like(q)) for p in range(2)], axis=0)
            s = jnp.dot(qm, cmk_ref[b, cols, :].astype(bf16), preferred_element_type=f32)
            m = jnp.max(s, axis=-1, keepdims=True)
            e = jnp.exp(s - m)
            r = lax.dot_general(e.astype(bf16), cmv_ref[b, cols, :].astype(bf16), _NT, preferred_element_type=f32)
            out = r / jnp.sum(e, axis=-1, keepdims=True)
            oac_ref[rows, wa + j * LANES:wa + (j + 1) * LANES] = jnp.where(lo, out[:S], out[S:])

        for hp in range(DIL_KV_HEADS // 2):
            cols = slice(hp * LANES, (hp + 1) * LANES)
            pieces = [jnp.where(halves[p], qb_ref[hp * len(DIL_PAIRS) + g, rows, :], 0.0)
                      for g in range(len(DIL_PAIRS)) for p in range(2)]
            qm = jnp.concatenate(pieces, axis=0).astype(bf16)
            s_c = jnp.dot(qm, cbk_ref[b, cols, :].astype(bf16), preferred_element_type=f32)
            s_n = jnp.dot(qm, nkb[cols, :].astype(bf16), preferred_element_type=f32)
            out, m, den = _softmax2_pv(jnp.where(valid_bc, s_c, NEG), jnp.where(valid_bn, s_n, NEG),
                                       cbv_ref[b, cols, :].astype(bf16), nvb[cols, :].astype(bf16))
            lse = m + jnp.log(den)
            res = []
            for p in range(2):
                r = [slice((g * 2 + p) * S, (g * 2 + p + 1) * S) for g in range(len(DIL_PAIRS))]
                mx = jnp.maximum(jnp.maximum(lse[r[0]], lse[r[1]]), lse[r[2]])
                w = [jnp.exp(lse[x] - mx) for x in r]
                tot = w[0] + w[1] + w[2]
                res.append(sum((w[g] / tot) * out[r[g]] for g in range(len(DIL_PAIRS))))
            ob_ref[rows, cols] = jnp.where(lo, res[0], res[1])


def attn_sample(sinks, qa, qb, qc, nka, nva, nkb, nvb, cak, cav, cbk, cbv, cmk, cmv, bt):
    NB, wka, la = cak.shape
    wkb, lb = cbk.shape[1:]
    wm, M = cmk.shape[1:]
    wa, wc = SWA_Q_HEADS * HEAD_DIM, MEM_HEADS * HEAD_DIM
    ng = len(DIL_PAIRS)
    tok = lambda w: pl.BlockSpec((bt * SROWS, w), lambda i: (i, 0))
    buf = lambda f, n: pl.BlockSpec((bt, f, n), lambda i: (i, 0, 0))
    f32 = jnp.float32
    return pl.pallas_call(
        functools.partial(_attn_sample_body, bt),
        grid=(NB // bt,),
        in_specs=[pl.BlockSpec(memory_space=pltpu.SMEM), tok(wa),
                  pl.BlockSpec((2 * ng, bt * SROWS, LANES), lambda i: (0, i, 0)), tok(wc),
                  tok(wka), tok(wka), tok(wkb), tok(wkb),
                  buf(wka, la), buf(wka, la), buf(wkb, lb), buf(wkb, lb), buf(wm, M), buf(wm, M)],
        out_specs=[tok(wa + wc), tok(wkb), buf(wka, la), buf(wka, la), buf(wkb, lb), buf(wkb, lb)],
        out_shape=[jax.ShapeDtypeStruct((NB * SROWS, wa + wc), f32), jax.ShapeDtypeStruct((NB * SROWS, wkb), f32),
                   jax.ShapeDtypeStruct(cak.shape, f32), jax.ShapeDtypeStruct(cak.shape, f32),
                   jax.ShapeDtypeStruct(cbk.shape, f32), jax.ShapeDtypeStruct(cbk.shape, f32)],
        compiler_params=_cparams(("parallel",)), name="attn_sample")(
            sinks, qa, qb, qc, nka, nva, nkb, nvb, cak, cav, cbk, cbv, cmk, cmv)


MOE_BLOCK = 512
ROUTE_SUB = 256
SUBLANES = 8


def _store_row_tiles(ref, row0, y):
    n = y.shape[0]
    for c in range(SUBLANES):
        ref[pl.ds(row0 * SUBLANES + c, n, stride=SUBLANES), :] = y[:, c * LANES:(c + 1) * LANES]


def _load_row_tiles(ref, row0, n):
    return jnp.concatenate([ref[pl.ds(row0 * SUBLANES + c, n, stride=SUBLANES), :] for c in range(SUBLANES)],
                           axis=1)


def _row_tile(ref, r):
    return ref.at[pl.ds(pl.multiple_of(r * SUBLANES, SUBLANES), SUBLANES)]


def _merge_route_body(oac_ref, ob_ref, gate_ref, x_ref, wa_ref, wb_ref, wc_ref, wo_ref, gffn_ref, wr_ref, br_ref,
                      cnt0_ref, x1_ref, h2_ref, mi_ref, mf_ref, cnt_ref, base_ref):
    f32, bf16 = jnp.float32, jnp.bfloat16
    D = x_ref.shape[1]
    wa = SWA_Q_HEADS * HEAD_DIM
    ts = ROUTE_SUB

    @pl.when(pl.program_id(0) == 0)
    def _():
        base_ref[...] = cnt0_ref[...]

    erow = lax.broadcasted_iota(jnp.int32, (N_EXPERTS, ts), 0)
    r8 = lax.broadcasted_iota(jnp.int32, (SUBLANES, ts), 0)
    ti = lax.broadcasted_iota(jnp.int32, (ts, ts), 0)
    tj = lax.broadcasted_iota(jnp.int32, (ts, ts), 1)
    later = (ti < tj).astype(bf16)
    base = base_ref[:, 0:1]
    for sub in range(x_ref.shape[0] // ts):
        rows = slice(sub * ts, (sub + 1) * ts)
        ma = jnp.dot(oac_ref[rows, :wa].astype(bf16), wa_ref[...], preferred_element_type=f32)
        mb = jnp.dot(ob_ref[rows, :].astype(bf16), wb_ref[...], preferred_element_type=f32)
        mc = jnp.dot(oac_ref[rows, wa:].astype(bf16), wc_ref[...], preferred_element_type=f32)
        merged = (gate_ref[rows, :D].astype(f32) * ma + gate_ref[rows, D:2 * D].astype(f32) * mb
                  + gate_ref[rows, 2 * D:].astype(f32) * mc)
        x1 = x_ref[rows, :] + jnp.dot(merged.astype(bf16), wo_ref[...], preferred_element_type=f32)
        x1_ref[rows, :] = x1
        h2 = _rms(x1, gffn_ref[...])
        _store_row_tiles(h2_ref, sub * ts, h2)

        work = lax.dot_general(wr_ref[...], h2.astype(bf16), _NT, preferred_element_type=f32) + br_ref[:, 0:1]
        vals, idxs = [], []
        for _ in range(TOP_K):
            m = jnp.max(work, axis=0, keepdims=True)
            idx = jnp.min(jnp.where(work == m, erow, N_EXPERTS), axis=0, keepdims=True)
            vals.append(m)
            idxs.append(idx)
            work = jnp.where(erow == idx, -jnp.inf, work)
        es = [jnp.exp(v - vals[0]) for v in vals]
        tot = es[0] + es[1] + es[2] + es[3]

        onehot = [(erow == idx).astype(f32) for idx in idxs]
        assign = onehot[0] + onehot[1] + onehot[2] + onehot[3]
        before = jnp.dot(assign.astype(bf16), later, preferred_element_type=f32) + base
        base = base + jnp.sum(assign, axis=1, keepdims=True)

        mi = jnp.zeros((SUBLANES, ts), jnp.int32)
        gates = jnp.zeros((SUBLANES, ts), f32)
        for k in range(TOP_K):
            rank = jnp.sum(onehot[k] * before, axis=0, keepdims=True).astype(jnp.int32)
            mi = jnp.where(r8 == k, idxs[k], mi)
            mi = jnp.where(r8 == TOP_K + k, rank, mi)
            gates = jnp.where(r8 == k, es[k] / tot, gates)
        mi_ref[:, rows] = mi
        mf_ref[rows, :] = jnp.concatenate([gates, jnp.zeros((LANES - SUBLANES, ts), f32)], axis=0).T
    base_ref[...] = jnp.broadcast_to(base, base_ref.shape)
    cnt_ref[...] = jnp.broadcast_to(base, cnt_ref.shape)


def merge_route(o_ac, o_b, gates, x, wa, wb, wc, wo, g_ffn, wr, br, cnt0, tm):
    R, D = x.shape
    row = lambda w: pl.BlockSpec((tm, w), lambda i: (i, 0))
    full = lambda a: pl.BlockSpec(a.shape, lambda i: (0, 0))
    return pl.pallas_call(
        _merge_route_body,
        grid=(R // tm,),
        in_specs=[row(o_ac.shape[1]), row(o_b.shape[1]), row(gates.shape[1]), row(D),
                  full(wa), full(wb), full(wc), full(wo), full(g_ffn), full(wr), full(br), full(cnt0)],
        out_specs=[row(D), pl.BlockSpec((tm * SUBLANES, LANES), lambda i: (i, 0)),
                   pl.BlockSpec((SUBLANES, tm), lambda i: (0, i)), row(LANES),
                   pl.BlockSpec((N_EXPERTS, LANES), lambda i: (0, 0))],
        out_shape=[jax.ShapeDtypeStruct((R, D), jnp.float32), jax.ShapeDtypeStruct((R * SUBLANES, LANES), jnp.float32),
                   jax.ShapeDtypeStruct((SUBLANES, R), jnp.int32), jax.ShapeDtypeStruct((R, LANES), jnp.float32),
                   jax.ShapeDtypeStruct((N_EXPERTS, LANES), jnp.float32)],
        scratch_shapes=[pltpu.VMEM((N_EXPERTS, LANES), jnp.float32)],
        compiler_params=_cparams(("arbitrary",)), name="merge_route")(
            o_ac, o_b, gates, x, wa, wb, wc, wo, g_ffn, wr, br, cnt0)


def _route_tables_body(cnt_ref, mi_ref, dest_ref, blk_ref, pad_ref):
    tm = mi_ref.shape[1]
    nbl = blk_ref.shape[1]
    erow1 = lax.broadcasted_iota(jnp.int32, (N_EXPERTS, LANES), 0)
    shift = MOE_BLOCK.bit_length() - 1
    cnt = cnt_ref[...].astype(jnp.int32)
    padded = ((cnt + (MOE_BLOCK - 1)) >> shift) << shift
    pend = padded
    s = 1
    while s < N_EXPERTS:
        pend = pend + jnp.where(erow1 >= s, pltpu.roll(pend, s, axis=0), 0)
        s *= 2
    pstart = pend - padded
    mi = mi_ref[...]
    erow = lax.broadcasted_iota(jnp.int32, (N_EXPERTS, tm), 0)
    r8 = lax.broadcasted_iota(jnp.int32, (SUBLANES, tm), 0)
    dest = jnp.zeros((SUBLANES, tm), jnp.int32)
    for k in range(TOP_K):
        start = jnp.sum(jnp.where(erow == mi[k:k + 1, :], pstart[:, 0:1], 0), axis=0, keepdims=True)
        dest = jnp.where(r8 == k, start + mi[TOP_K + k:TOP_K + k + 1, :], dest)
    dest_ref[...] = dest

    @pl.when(pl.program_id(0) == 0)
    def _():
        row0 = lax.broadcasted_iota(jnp.int32, (N_EXPERTS, nbl), 1) * MOE_BLOCK
        ended = jnp.sum(jnp.where(pend[:, 0:1] <= row0, 1, 0), axis=0, keepdims=True)
        blk_ref[...] = jnp.broadcast_to(jnp.minimum(ended, N_EXPERTS - 1), blk_ref.shape)
        pad_ref[0] = pstart + cnt
        pad_ref[1] = pend


def route_tables(cnt, mi, nbl, tm):
    R = mi.shape[1]
    return pl.pallas_call(
        _route_tables_body,
        grid=(R // tm,),
        in_specs=[pl.BlockSpec((N_EXPERTS, LANES), lambda i: (0, 0)), pl.BlockSpec((SUBLANES, tm), lambda i: (0, i))],
        out_specs=[pl.BlockSpec((SUBLANES, tm), lambda i: (0, i)), pl.BlockSpec((SUBLANES, nbl), lambda i: (0, 0)),
                   pl.BlockSpec((2, N_EXPERTS, LANES), lambda i: (0, 0, 0))],
        out_shape=[jax.ShapeDtypeStruct((SUBLANES, R), jnp.int32), jax.ShapeDtypeStruct((SUBLANES, nbl), jnp.int32),
                   jax.ShapeDtypeStruct((2, N_EXPERTS, LANES), jnp.int32)],
        compiler_params=_cparams(("arbitrary",)), name="route_tables")(cnt, mi)


def _dispatch_body(n_first, n_steps, n_zero, pad0_ref, pad1_ref, dest_ref, ha_ref, hb_ref, xs_ref, zero_ref, sem):
    i = pl.program_id(0)
    tm = ha_ref.shape[0] // SUBLANES
    rs = xs_ref.shape[0] // SUBLANES

    def scatter_tile(h_ref):
        def row_copy(r, k):
            return pltpu.make_async_copy(_row_tile(h_ref, r), _row_tile(xs_ref, dest_ref[0, 0, k * tm + r]), sem)

        def start(r, c):
            for k in range(TOP_K):
                row_copy(r, k).start(priority=k % 2)
            return c

        def wait(r, c):
            for k in range(TOP_K):
                row_copy(r, k).wait()
            return c

        lax.fori_loop(0, tm, start, 0, unroll=8)
        lax.fori_loop(0, tm, wait, 0, unroll=8)

    @pl.when(i < n_first)
    def _():
        scatter_tile(ha_ref)

    @pl.when(i >= n_first)
    def _():
        scatter_tile(hb_ref)

    @pl.when(i == n_steps - 1)
    def _():
        zero_ref[...] = jnp.zeros_like(zero_ref)

        def zero_copy(r):
            return pltpu.make_async_copy(zero_ref, _row_tile(xs_ref, r), sem)

        def start(r, c):
            zero_copy(r).start()
            return c

        def wait(r, c):
            zero_copy(0).wait()
            return c

        for e in range(N_EXPERTS):
            lax.fori_loop(pad0_ref[e], pad1_ref[e], start, 0)
        lax.fori_loop(pad1_ref[N_EXPERTS - 1], rs, start, 0)
        lax.fori_loop(0, n_zero, wait, 0)


def dispatch(pad0, pad1, dest3, h_a, h_b, rs, tm):
    S = SUBLANES
    n_a, n_b = h_a.shape[0] // (tm * S), h_b.shape[0] // (tm * S)
    n_zero = rs - (h_a.shape[0] + h_b.shape[0]) // S * TOP_K
    grid_spec = pltpu.PrefetchScalarGridSpec(
        num_scalar_prefetch=2,
        grid=(n_a + n_b,),
        in_specs=[pl.BlockSpec((1, 1, TOP_K * tm), lambda i, p0, p1: (i, 0, 0), memory_space=pltpu.SMEM),
                  pl.BlockSpec((tm * S, LANES), lambda i, p0, p1: (jnp.minimum(i, n_a - 1), 0)),
                  pl.BlockSpec((tm * S, LANES), lambda i, p0, p1: (jnp.maximum(i - n_a, 0), 0))],
        out_specs=pl.BlockSpec(memory_space=pl.ANY),
        scratch_shapes=[pltpu.VMEM((S, LANES), h_a.dtype), pltpu.SemaphoreType.DMA])
    return pl.pallas_call(
        functools.partial(_dispatch_body, n_a, n_a + n_b, n_zero), grid_spec=grid_spec,
        out_shape=jax.ShapeDtypeStruct((rs * S, LANES), h_a.dtype),
        compiler_params=_cparams(("arbitrary",)), name="dispatch")(pad0, pad1, dest3, h_a, h_b)


def _moe_body(nblk, be_ref, nact_ref, x_ref, wgu_hbm, bgu_ref, wd_hbm, bd_ref, y_ref, wgu_f, wd_f, wgu_s, wd_s,
              sems):
    f32, bf16 = jnp.float32, jnp.bfloat16
    b = pl.program_id(0)
    e = be_ref[b]

    def fetch(ex, go):
        for src, dst, s in ((wgu_hbm, wgu_f, 0), (wd_hbm, wd_f, 1)):
            cp = pltpu.make_async_copy(src.at[ex], dst, sems.at[s])
            cp.start() if go else cp.wait()

    @pl.when(b == 0)
    def _():
        fetch(e, True)

    @pl.when((b == 0) | (e != be_ref[jnp.maximum(b - 1, 0)]))
    def _():
        fetch(e, False)
        wgu_s[...] = wgu_f[...].astype(bf16)
        wd_s[...] = wd_f[...].astype(bf16)
        nxt = lax.while_loop(lambda j: (j < nblk) & (be_ref[jnp.minimum(j, nblk - 1)] == e), lambda j: j + 1, b + 1)

        @pl.when(nxt < nblk)
        def _():
            fetch(be_ref[jnp.minimum(nxt, nblk - 1)], True)

    @pl.when(b < nact_ref[0])
    def _():
        x = _load_row_tiles(x_ref, 0, MOE_BLOCK).astype(bf16)
        gu = jnp.dot(x, wgu_s[...], preferred_element_type=f32) + bgu_ref[0]
        gt = jnp.minimum(gu[:, :D_FF], SWIGLU_LIMIT)
        up = jnp.clip(gu[:, D_FF:], -SWIGLU_LIMIT, SWIGLU_LIMIT)
        act = (up + 1.0) * (gt * jax.nn.sigmoid(gt * SWIGLU_ALPHA))
        y = jnp.dot(act.astype(bf16), wd_s[...], preferred_element_type=f32) + bd_ref[0]
        _store_row_tiles(y_ref, 0, y)

    @pl.when(b >= nact_ref[0])
    def _():
        y_ref[...] = jnp.zeros_like(y_ref)


def moe_ffn(blk_e, n_active, xs, w_gate_up, b_gate_up, w_down, b_down):
    RS = xs.shape[0] // SUBLANES
    E, D, F2 = w_gate_up.shape
    blk_rows = MOE_BLOCK * SUBLANES
    grid_spec = pltpu.PrefetchScalarGridSpec(
        num_scalar_prefetch=2,
        grid=(RS // MOE_BLOCK,),
        in_specs=[pl.BlockSpec((blk_rows, LANES), lambda b, be, na: (b, 0)),
                  pl.BlockSpec(memory_space=pl.ANY),
                  pl.BlockSpec((1, 1, F2), lambda b, be, na: (be[b], 0, 0)),
                  pl.BlockSpec(memory_space=pl.ANY),
                  pl.BlockSpec((1, 1, D), lambda b, be, na: (be[b], 0, 0))],
        out_specs=pl.BlockSpec((blk_rows, LANES), lambda b, be, na: (b, 0)),
        scratch_shapes=[pltpu.VMEM((D, F2), w_gate_up.dtype), pltpu.VMEM((F2 // 2, D), w_down.dtype),
                        pltpu.VMEM((D, F2), jnp.bfloat16), pltpu.VMEM((F2 // 2, D), jnp.bfloat16),
                        pltpu.SemaphoreType.DMA((2,))])
    return pl.pallas_call(
        functools.partial(_moe_body, RS // MOE_BLOCK), grid_spec=grid_spec,
        out_shape=jax.ShapeDtypeStruct(xs.shape, jnp.float32),
        compiler_params=_cparams(("arbitrary",)), name="moe_ffn")(
            blk_e, n_active, xs, w_gate_up, b_gate_up.reshape(E, 1, F2), w_down, b_down.reshape(E, 1, D))


def _combine_body(n_tiles, dest_ref, dest_next_ref, x1_ref, mf_ref, g_ref, ys_ref, o_ref, buf, sems):
    tm = x1_ref.shape[0]
    i = pl.program_id(0)
    slot = i % 2
    per_slot = TOP_K * tm

    def gather(d_ref, s, go):
        def row_copy(r, k):
            return pltpu.make_async_copy(_row_tile(ys_ref, d_ref[0, 0, k * tm + r]),
                                         _row_tile(buf, s * per_slot + k * tm + r), sems.at[s])

        def body(r, c):
            for k in range(TOP_K):
                if go:
                    row_copy(r, k).start(priority=k % 2)
                else:
                    row_copy(r, k).wait()
            return c

        lax.fori_loop(0, tm, body, 0, unroll=8)

    @pl.when(i == 0)
    def _():
        gather(dest_ref, slot, True)

    @pl.when(i + 1 < n_tiles)
    def _():
        gather(dest_next_ref, 1 - slot, True)

    gather(dest_ref, slot, False)
    y = x1_ref[...]
    for k in range(TOP_K):
        y = y + mf_ref[:, k:k + 1] * _load_row_tiles(buf, slot * per_slot + k * tm, tm)
    o_ref[...] = _rms(y, g_ref[...])


def combine(dest3, x1, mf, g_final, ys, tm):
    R, D = x1.shape
    n = R // tm
    dest_spec = lambda f: pl.BlockSpec((1, 1, TOP_K * tm), f, memory_space=pltpu.SMEM)
    return pl.pallas_call(
        functools.partial(_combine_body, n),
        grid=(n,),
        in_specs=[dest_spec(lambda i: (i, 0, 0)), dest_spec(lambda i: (jnp.minimum(i + 1, n - 1), 0, 0)),
                  pl.BlockSpec((tm, D), lambda i: (i, 0)),
                  pl.BlockSpec((tm, LANES), lambda i: (i, 0)),
                  pl.BlockSpec((1, D), lambda i: (0, 0)),
                  pl.BlockSpec(memory_space=pl.ANY)],
        out_specs=pl.BlockSpec((tm, D), lambda i: (i, 0)),
        out_shape=jax.ShapeDtypeStruct((R, D), jnp.float32),
        scratch_shapes=[pltpu.VMEM((2 * TOP_K * tm * SUBLANES, LANES), ys.dtype), pltpu.SemaphoreType.DMA((2,))],
        compiler_params=_cparams(("arbitrary",)), name="combine")(dest3, dest3, x1, mf, g_final, ys)


def moe_layer(group_a, group_b, cnt, g_final, w_gate_up, b_gate_up, w_down, b_down, tm):
    n_assign = (group_a[0].shape[0] + group_b[0].shape[0]) * TOP_K
    nb = (n_assign + N_EXPERTS * (MOE_BLOCK - 1)) // MOE_BLOCK + 1
    nbl = -(-nb // LANES) * LANES
    dests = []
    for x1, h2, mi, mf in (group_a, group_b):
        dest, blk, pad = route_tables(cnt, mi, nbl, min(x1.shape[0], 4 * tm))
        n_tiles = x1.shape[0] // tm
        dests.append(jnp.transpose(dest[:TOP_K].reshape(TOP_K, n_tiles, tm), (1, 0, 2)).reshape(n_tiles, 1, TOP_K * tm))
    xs = dispatch(pad[0, :, 0], pad[1, :, 0], jnp.concatenate(dests), group_a[1], group_b[1], nb * MOE_BLOCK, tm)
    n_active = (pad[1, N_EXPERTS - 1, 0] // MOE_BLOCK).reshape(1)
    ys = moe_ffn(blk[0, :nb], n_active, xs, w_gate_up, b_gate_up, w_down, b_down)
    return [combine(dest3, x1, mf, g_final, ys, tm) for dest3, (x1, h2, mi, mf) in zip(dests, (group_a, group_b))]


def kernel(x_prompt, x_sample, cache_swa_k, cache_swa_v, cache_dil_k, cache_dil_v, cache_mem_k, cache_mem_v, mem_prompt, norm_attn, norm_mem, w_in, w_mem_kv, sinks, w_br_a, w_br_b, w_br_c, w_out, norm_ffn, w_router, b_router, w_gate_up, b_gate_up, w_down, b_down, norm_final):
    f32, bf16 = jnp.float32, jnp.bfloat16
    TM = 256
    B, L, D = x_prompt.shape
    NB, n_new, _ = x_sample.shape
    M = mem_prompt.shape[1]
    la, lb = cache_swa_k.shape[2], cache_dil_k.shape[2]
    wka, wkb, wc = SWA_KV_HEADS * HEAD_DIM, DIL_KV_HEADS * HEAD_DIM, MEM_HEADS * HEAD_DIM
    assert n_new == N_NEW and cache_swa_k.shape[0] == 1

    w_in_b = w_in[0].astype(bf16)
    g_attn = norm_attn[0].reshape(1, D)
    secs = in_sections()

    tabs_p = rope_tables(jnp.arange(L, dtype=jnp.int32))
    qa, ka, va, qb, kb, vb, qc, gates = norm_proj(x_prompt.reshape(B * L, D), g_attn, w_in_b, tabs_p, secs, TM)
    mk, mv = norm_proj(mem_prompt.reshape(B * M, D), norm_mem[0].reshape(1, D), w_mem_kv[0].astype(bf16), None,
                       [(_chunks(0, wc), "plain", False, f32), (_chunks(wc, wc), "plain", False, f32)], TM)
    o_ac = attn_swa_mem(sinks[0], qa, ka, va, qc, mk, mv, B, L)
    o_b, kb_t, vb_t = attn_dilated(qb, kb, vb, B, L)

    xs_pad = jnp.pad(x_sample, ((0, 0), (0, SROWS - N_NEW), (0, 0))).reshape(NB * SROWS, D)
    tabs_s = rope_tables(PAST_LEN + (jnp.arange(TM, dtype=jnp.int32) % SROWS))
    qa_s, ka_s, va_s, qb_s, kb_s, vb_s, qc_s, gates_s = norm_proj(xs_pad, g_attn, w_in_b, tabs_s, secs, TM)
    real = lambda t: t.reshape(NB, SROWS, -1)[:, :N_NEW].reshape(NB * N_NEW, -1)
    fmaj = lambda c: jnp.transpose(c[0], (0, 2, 3, 1)).reshape(NB, -1, c.shape[2])
    o_ac_s, o_b_s, swa_k_s, swa_v_s, dil_k_s, dil_v_s = attn_sample(
        sinks[0], qa_s, qb_s, qc_s, ka_s, va_s, kb_s, vb_s,
        fmaj(cache_swa_k), fmaj(cache_swa_v), fmaj(cache_dil_k), fmaj(cache_dil_v),
        fmaj(cache_mem_k), fmaj(cache_mem_v), 2)

    wr = w_router[0].T.astype(bf16)
    br = jnp.broadcast_to(b_router[0].astype(f32)[:, None], (N_EXPERTS, LANES))
    wts = (w_br_a[0].astype(bf16), w_br_b[0].astype(bf16), w_br_c[0].astype(bf16), w_out[0].astype(bf16),
           norm_ffn[0].reshape(1, D), wr, br)
    x1_p, h2_p, mi_p, mf_p, cnt_p = merge_route(o_ac, o_b, gates, x_prompt.reshape(B * L, D), *wts,
                                                jnp.zeros((N_EXPERTS, LANES), f32), 2 * ROUTE_SUB)
    x1_s, h2_s, mi_s, mf_s, cnt = merge_route(real(o_ac_s), real(o_b_s), real(gates_s),
                                              x_sample.reshape(NB * N_NEW, D), *wts, cnt_p, 2 * ROUTE_SUB)
    y_p, y_s = moe_layer((x1_p, h2_p, mi_p, mf_p), (x1_s, h2_s, mi_s, mf_s), cnt, norm_final.reshape(1, D),
                         w_gate_up[0], b_gate_up[0], w_down[0], b_down[0], TM)

    heads = lambda t, n, h: t.reshape(1, t.shape[0] // n, n, h, HEAD_DIM)
    tmaj = lambda t, h: jnp.transpose(t.reshape(t.shape[0], h, HEAD_DIM, t.shape[2]), (0, 3, 1, 2))[None]
    ka3, va3 = ka.reshape(B, L, wka), va.reshape(B, L, wka)
    la_p = min(SWA_WINDOW, L)
    return (y_p.reshape(B, L, D), y_s.reshape(NB, N_NEW, D),
            heads(ka3[:, L - la_p:].reshape(B * la_p, wka), la_p, SWA_KV_HEADS),
            heads(va3[:, L - la_p:].reshape(B * la_p, wka), la_p, SWA_KV_HEADS),
            tmaj(kb_t, DIL_KV_HEADS), tmaj(vb_t, DIL_KV_HEADS),
            heads(mk, M, MEM_HEADS), heads(mv, M, MEM_HEADS),
            tmaj(swa_k_s, SWA_KV_HEADS), tmaj(swa_v_s, SWA_KV_HEADS),
            tmaj(dil_k_s, DIL_KV_HEADS), tmaj(dil_v_s, DIL_KV_HEADS))
```

```python
import functools

import jax
import jax.numpy as jnp
import numpy as np
from jax import lax
from jax.experimental import pallas as pl
from jax.experimental.pallas import tpu as pltpu

D_MODEL = 1024
HEAD_DIM = 64
ROPE_DIM = 16
ROPE_HALF = 8
ROPE_THETA = 500000.0
PAST_LEN = 16384
SWA_Q_HEADS = 8
SWA_KV_HEADS = 2
SWA_WINDOW = 128
DIL_PAIRS = ((128, 1), (512, 4), (2048, 16))
DIL_KV_HEADS = 4
MEM_HEADS = 4
N_EXPERTS = 32
TOP_K = 4
D_FF = 1024
SWIGLU_LIMIT = 7.0
SWIGLU_ALPHA = 1.702
RMS_EPS = 1e-5
ATT_BLOCK = 128
SCALE = HEAD_DIM ** -0.5

LANES = 128
NEG = -1e30
VMEM_LIMIT = 56 * 1024 * 1024


def _cparams(sem):
    return pltpu.CompilerParams(dimension_semantics=sem, vmem_limit_bytes=VMEM_LIMIT)


def _rms(x, g):
    return x * lax.rsqrt(jnp.mean(x * x, axis=-1, keepdims=True) + RMS_EPS) * g


def _norm_proj_body(sections, x_ref, g_ref, w_ref, cs_ref, *out_refs):
    h = _rms(x_ref[...], g_ref[...]).astype(jnp.bfloat16)
    if cs_ref is not None:
        cos = cs_ref[0]
        sin_lo = cs_ref[1]
        sin_hi = cs_ref[2]
    dest = {lo: (o_ref, c, kind, slabs)
            for (cols, kind, slabs), o_ref in zip(sections, out_refs) for c, lo in enumerate(cols)}
    todo = sorted(dest)
    while todo:
        lo = todo.pop(0)
        n = 2 if todo and todo[0] == lo + LANES else 1
        if n == 2:
            todo.pop(0)
        yy = jnp.dot(h, w_ref[:, lo:lo + n * LANES], preferred_element_type=jnp.float32)
        for part in range(n):
            o_ref, c, kind, slabs = dest[lo + part * LANES]
            y = yy[:, part * LANES:(part + 1) * LANES]
            if kind in ("rope", "rope_q"):
                y = (y * cos + pltpu.roll(y, LANES - ROPE_HALF, axis=1) * sin_lo
                     + pltpu.roll(y, ROPE_HALF, axis=1) * sin_hi)
            if kind in ("rope_q", "q"):
                y = y * SCALE
            if kind == "sigmoid":
                y = jax.nn.sigmoid(y)
            if slabs:
                o_ref[c] = y.astype(o_ref.dtype)
            else:
                o_ref[:, c * LANES:(c + 1) * LANES] = y.astype(o_ref.dtype)


def rope_tables(pos):
    inv_freq = ROPE_THETA ** (-jnp.arange(ROPE_HALF, dtype=jnp.float32) / ROPE_HALF)
    ang = pos.astype(jnp.float32)[:, None] * inv_freq[None, :]
    cos, sin = jnp.cos(ang), jnp.sin(ang)
    n = pos.shape[0]
    one = jnp.ones((n, HEAD_DIM - ROPE_DIM), jnp.float32)
    zero = jnp.zeros((n, HEAD_DIM - ROPE_HALF), jnp.float32)
    c = jnp.concatenate([cos, cos, one], axis=1)
    s_lo = jnp.concatenate([-sin, zero], axis=1)
    s_hi = jnp.concatenate([jnp.zeros((n, ROPE_HALF), jnp.float32), sin,
                            jnp.zeros((n, HEAD_DIM - ROPE_DIM), jnp.float32)], axis=1)
    tab = jnp.stack([c, s_lo, s_hi])
    return jnp.concatenate([tab, tab], axis=2)


def norm_proj(x, g, w, tables, sections, tm):
    R, D = x.shape
    in_specs = [pl.BlockSpec((tm, D), lambda i: (i, 0)),
                pl.BlockSpec((1, D), lambda i: (0, 0)),
                pl.BlockSpec(w.shape, lambda i: (0, 0))]
    args = [x, g, w]
    if tables is not None:
        nt = tables.shape[1] // tm
        in_specs.append(pl.BlockSpec((3, tm, LANES), lambda i: (0, i % nt, 0)))
        args.append(tables)
    out_shape, out_specs, secs = [], [], []
    for (cols, kind, slabs, dtype) in sections:
        secs.append((cols, kind, slabs))
        width = LANES * len(cols)
        if slabs:
            out_shape.append(jax.ShapeDtypeStruct((width // LANES, R, LANES), dtype))
            out_specs.append(pl.BlockSpec((width // LANES, tm, LANES), lambda i: (0, i, 0)))
        else:
            out_shape.append(jax.ShapeDtypeStruct((R, width), dtype))
            out_specs.append(pl.BlockSpec((tm, width), lambda i: (i, 0)))
    if tables is None:
        body = lambda x_ref, g_ref, w_ref, *o: _norm_proj_body(secs, x_ref, g_ref, w_ref, None, *o)
    else:
        body = functools.partial(_norm_proj_body, secs)
    return pl.pallas_call(
        body, grid=(R // tm,), in_specs=in_specs, out_specs=out_specs, out_shape=out_shape,
        compiler_params=_cparams(("parallel",)), name="norm_proj")(*args)


def _chunks(start, width):
    return tuple(range(start, start + width, LANES))


def in_sections():
    f32, bf16 = jnp.float32, jnp.bfloat16
    qb0 = 1024 - 256
    qb_cols = tuple(qb0 + HEAD_DIM * (4 * g + 2 * hp) for hp in range(2) for g in range(3))
    return [
        (_chunks(0, 512), "rope_q", False, bf16),
        (_chunks(512, 128), "rope", False, f32),
        (_chunks(640, 128), "plain", False, f32),
        (qb_cols, "rope_q", True, f32),
        (_chunks(1536, 256), "rope", False, f32),
        (_chunks(1792, 256), "plain", False, f32),
        (_chunks(2048, 256), "q", False, bf16),
        (_chunks(2304, 3072), "sigmoid", False, bf16),
    ]


_NT = (((1,), (1,)), ((), ()))


def _half_masks():
    lane = lax.broadcasted_iota(jnp.int32, (1, LANES), 1)
    return lane < HEAD_DIM, lane >= HEAD_DIM


def _softmax_pv(s, v_half, sink=None):
    m = jnp.max(s, axis=-1, keepdims=True)
    if sink is not None:
        m = jnp.maximum(m, sink)
    e = jnp.exp(s - m)
    den = jnp.sum(e, axis=-1, keepdims=True)
    if sink is not None:
        den = den + jnp.exp(sink - m)
    r = jnp.dot(e.astype(jnp.bfloat16), v_half, preferred_element_type=jnp.float32)
    return r / den, m, den


def _attn_swa_mem_body(L, sink_ref, qa_ref, ka_ref, va_ref, qc_ref, mk_ref, mv_ref, o_ref):
    bf16 = jnp.bfloat16
    lo, hi = _half_masks()
    halves = (lo, hi)
    T = ATT_BLOCK
    G = SWA_Q_HEADS // SWA_KV_HEADS
    mem_k = [mk_ref[:, j * LANES:(j + 1) * LANES].astype(bf16) for j in range(2)]
    mem_v = [mv_ref[:, j * LANES:(j + 1) * LANES].astype(bf16) for j in range(2)]
    qi = lax.broadcasted_iota(jnp.int32, (2 * T, 2 * T), 0) & (T - 1)
    kj = lax.broadcasted_iota(jnp.int32, (2 * T, 2 * T), 1)
    hrow = lax.broadcasted_iota(jnp.int32, (2 * T, 1), 0) >> (T.bit_length() - 1)

    def heads_of(pair):
        return jnp.concatenate([jnp.where(halves[p], pair, jnp.zeros_like(pair)) for p in range(2)], axis=0)

    def block(blk, carry):
        r0 = pl.multiple_of(blk * T, T)
        ws = pl.multiple_of(jnp.maximum(r0 - T, 0), T)
        dist = qi - kj + (r0 - ws)
        valid = (dist >= 0) & (dist <= SWA_WINDOW - 1)
        k = ka_ref[pl.ds(ws, 2 * T), :]
        v = va_ref[pl.ds(ws, 2 * T), :]
        for kv in range(SWA_KV_HEADS):
            k1 = jnp.where(halves[kv], k, 0.0)
            v1 = jnp.where(halves[kv], v, 0.0)
            k_dup = (k1 + pltpu.roll(k1, HEAD_DIM, axis=1)).astype(bf16)
            v_dup = (v1 + pltpu.roll(v1, HEAD_DIM, axis=1)).astype(bf16)
            for j in range(kv * G // 2, (kv + 1) * G // 2):
                qm = heads_of(qa_ref[pl.ds(r0, T), j * LANES:(j + 1) * LANES])
                sink = jnp.where(hrow == 0, sink_ref[2 * j], sink_ref[2 * j + 1])
                s = lax.dot_general(qm, k_dup, _NT, preferred_element_type=jnp.float32)
                out, _, _ = _softmax_pv(jnp.where(valid, s, NEG), v_dup, sink)
                o_ref[pl.ds(r0, T), j * LANES:(j + 1) * LANES] = jnp.where(lo, out[:T], out[T:]).astype(o_ref.dtype)
        for j in range(MEM_HEADS // 2):
            qm = heads_of(qc_ref[pl.ds(r0, T), j * LANES:(j + 1) * LANES])
            s = lax.dot_general(qm, mem_k[j], _NT, preferred_element_type=jnp.float32)
            out, _, _ = _softmax_pv(s, mem_v[j])
            c0 = SWA_Q_HEADS * HEAD_DIM + j * LANES
            o_ref[pl.ds(r0, T), c0:c0 + LANES] = jnp.where(lo, out[:T], out[T:]).astype(o_ref.dtype)
        return carry

    lax.fori_loop(0, L // T, block, 0)


def attn_swa_mem(sinks, qa, ka, va, qc, mk, mv, B, L):
    M = mk.shape[0] // B
    wa, wc = SWA_Q_HEADS * HEAD_DIM, MEM_HEADS * HEAD_DIM
    row = lambda w: pl.BlockSpec((L, w), lambda b: (b, 0))
    return pl.pallas_call(
        functools.partial(_attn_swa_mem_body, L),
        grid=(B,),
        in_specs=[pl.BlockSpec(memory_space=pltpu.SMEM), row(wa), row(LANES), row(LANES), row(wc),
                  pl.BlockSpec((M, wc), lambda b: (b, 0)), pl.BlockSpec((M, wc), lambda b: (b, 0))],
        out_specs=row(wa + wc),
        out_shape=jax.ShapeDtypeStruct((B * L, wa + wc), jnp.bfloat16),
        compiler_params=_cparams(("parallel",)), name="attn_swa_mem")(sinks, qa, ka, va, qc, mk, mv)


def _attn_dil_body(L, qb_ref, kb_ref, vb_ref, o_ref, kt_ref, vt_ref, og_ref, lse_ref):
    bf16 = jnp.bfloat16
    lo, hi = _half_masks()
    halves = (lo, hi)
    T = ATT_BLOCK
    kt_ref[...] = kb_ref[...].T
    vt_ref[...] = vb_ref[...].T
    for g, (window, dil) in enumerate(DIL_PAIRS):
        lc = L // dil
        nbc = lc // T
        W = min(2 * T, lc)
        max_dist = window // dil
        qi = lax.broadcasted_iota(jnp.int32, (2 * T, W), 0) & (T - 1)
        kj = lax.broadcasted_iota(jnp.int32, (2 * T, W), 1)

        def unit(u, carry, g=g, dil=dil, nbc=nbc, W=W, max_dist=max_dist, qi=qi, kj=kj):
            c = u >> (nbc.bit_length() - 1)
            n = u & (nbc - 1)
            wsc = jnp.maximum(n * T - T, 0) if W == 2 * T else 0
            q0 = c + dil * T * n
            k0 = c + dil * wsc
            dist = qi - kj + (n * T - wsc)
            valid = (dist >= 0) & (dist <= max_dist)
            q = qb_ref[g, pl.ds(q0, T, stride=dil), :]
            k = kb_ref[pl.ds(k0, W, stride=dil), :]
            v = vb_ref[pl.ds(k0, W, stride=dil), :]
            qm = jnp.concatenate([jnp.where(halves[p], q, 0.0) for p in range(2)], axis=0).astype(bf16)
            s = lax.dot_general(qm, k.astype(bf16), _NT, preferred_element_type=jnp.float32)
            out, m, den = _softmax_pv(jnp.where(valid, s, NEG), v.astype(bf16))
            lse = m + jnp.log(den)
            og_ref[g, pl.ds(q0, T, stride=dil), :] = jnp.where(lo, out[:T], out[T:])
            lse_ref[g, pl.ds(q0, T, stride=dil), :] = jnp.where(lo, lse[:T], lse[T:])
            return carry

        lax.fori_loop(0, dil * nbc, unit, 0, unroll=4)

    def merge(i, carry):
        r0 = pl.multiple_of(i * T, T)
        ls = [lse_ref[g, pl.ds(r0, T), :] for g in range(len(DIL_PAIRS))]
        m = jnp.maximum(jnp.maximum(ls[0], ls[1]), ls[2])
        ws = [jnp.exp(l - m) for l in ls]
        tot = ws[0] + ws[1] + ws[2]
        out = sum((w / tot) * og_ref[g, pl.ds(r0, T), :] for g, w in enumerate(ws))
        o_ref[pl.ds(r0, T), :] = out.astype(o_ref.dtype)
        return carry

    lax.fori_loop(0, L // T, merge, 0)


def attn_dilated(qb, kb, vb, B, L):
    ng = len(DIL_PAIRS)
    fmaj = pl.BlockSpec((None, LANES, L), lambda b, hp: (b, hp, 0))
    return pl.pallas_call(
        functools.partial(_attn_dil_body, L),
        grid=(B, 2),
        in_specs=[pl.BlockSpec((ng, L, LANES), lambda b, hp: (hp, b, 0)),
                  pl.BlockSpec((L, LANES), lambda b, hp: (b, hp)),
                  pl.BlockSpec((L, LANES), lambda b, hp: (b, hp))],
        out_specs=[pl.BlockSpec((L, LANES), lambda b, hp: (b, hp)), fmaj, fmaj],
        out_shape=[jax.ShapeDtypeStruct((B * L, 2 * LANES), jnp.bfloat16),
                   jax.ShapeDtypeStruct((B, 2 * LANES, L), jnp.float32),
                   jax.ShapeDtypeStruct((B, 2 * LANES, L), jnp.float32)],
        scratch_shapes=[pltpu.VMEM((ng, L, LANES), jnp.float32), pltpu.VMEM((ng, L, LANES), jnp.float32)],
        compiler_params=_cparams(("parallel", "parallel")), name="attn_dilated")(qb, kb, vb)


N_NEW = 4
SROWS = 8


def _softmax2_pv(s_c, s_n, vt_c, vt_n, sink=None):
    m = jnp.maximum(jnp.max(s_c, axis=-1, keepdims=True), jnp.max(s_n, axis=-1, keepdims=True))
    if sink is not None:
        m = jnp.maximum(m, sink)
    e_c = jnp.exp(s_c - m)
    e_n = jnp.exp(s_n - m)
    den = jnp.sum(e_c, axis=-1, keepdims=True) + jnp.sum(e_n, axis=-1, keepdims=True)
    if sink is not None:
        den = den + jnp.exp(sink - m)
    r = (lax.dot_general(e_c.astype(jnp.bfloat16), vt_c, _NT, preferred_element_type=jnp.float32)
         + lax.dot_general(e_n.astype(jnp.bfloat16), vt_n, _NT, preferred_element_type=jnp.float32))
    return r / den, m, den


def _advance(old_t, new_t):
    n = old_t.shape[1]
    lane = lax.broadcasted_iota(jnp.int32, (1, LANES), 1)
    shifted = pltpu.roll(old_t, n - N_NEW, axis=1)
    last = jnp.where(lane < LANES - N_NEW, shifted[:, n - LANES:], new_t)
    if n == LANES:
        return last
    return jnp.concatenate([shifted[:, :n - LANES], last], axis=1)


def _attn_sample_body(bt, sink_ref, qa_ref, qb_ref, qc_ref, nka_ref, nva_ref, nkb_ref, nvb_ref,
                      cak_ref, cav_ref, cbk_ref, cbv_ref, cmk_ref, cmv_ref,
                      oac_ref, ob_ref, oak_ref, oav_ref, obk_ref, obv_ref):
    f32, bf16 = jnp.float32, jnp.bfloat16
    lo, hi = _half_masks()
    halves = (lo, hi)
    S = SROWS
    la = cak_ref.shape[2]
    lb = cbk_ref.shape[2]
    wa = SWA_Q_HEADS * HEAD_DIM
    new0 = LANES - N_NEW

    na = SWA_Q_HEADS * S
    ia = lax.broadcasted_iota(jnp.int32, (na, la), 0) & (S - 1)
    valid_ac = lax.broadcasted_iota(jnp.int32, (na, la), 1) >= ia + 1
    ja = lax.broadcasted_iota(jnp.int32, (na, LANES), 1) - new0
    valid_an = (ja >= 0) & (ja <= (lax.broadcasted_iota(jnp.int32, (na, LANES), 0) & (S - 1)))
    rcol = lax.broadcasted_iota(jnp.int32, (na, 1), 0)
    sink_col = jnp.zeros((na, 1), f32)
    for h in range(SWA_Q_HEADS):
        sink_col = jnp.where((rcol >> 3) == h, sink_ref[h], sink_col)

    nb_rows = len(DIL_PAIRS) * 2 * S
    rb = lax.broadcasted_iota(jnp.int32, (nb_rows, lb), 0)
    t_c = lb + (rb & (S - 1)) - lax.broadcasted_iota(jnp.int32, (nb_rows, lb), 1)
    rn = lax.broadcasted_iota(jnp.int32, (nb_rows, LANES), 0)
    jn = lax.broadcasted_iota(jnp.int32, (nb_rows, LANES), 1) - new0
    t_n = (rn & (S - 1)) - jn
    valid_bc = jnp.zeros((nb_rows, lb), jnp.bool_)
    valid_bn = jnp.zeros((nb_rows, LANES), jnp.bool_)
    for g, (window, dil) in enumerate(DIL_PAIRS):
        valid_bc = valid_bc | (((rb >> 4) == g) & (t_c <= window) & ((t_c & (dil - 1)) == 0))
        valid_bn = valid_bn | (((rn >> 4) == g) & (jn >= 0) & (t_n >= 0) & ((t_n & (dil - 1)) == 0))

    def new_t(x):
        padded = jnp.concatenate([x, jnp.zeros((LANES - S, x.shape[1]), f32)], axis=0)
        return pltpu.roll(padded.T, new0, axis=1)

    for b in range(bt):
        rows = slice(b * S, (b + 1) * S)
        nka, nva = new_t(nka_ref[rows, :]), new_t(nva_ref[rows, :])
        nkb, nvb = new_t(nkb_ref[rows, :]), new_t(nvb_ref[rows, :])
        oak_ref[b] = _advance(cak_ref[b], nka)
        oav_ref[b] = _advance(cav_ref[b], nva)
        obk_ref[b] = _advance(cbk_ref[b], nkb)
        obv_ref[b] = _advance(cbv_ref[b], nvb)

        pieces = []
        for h in range(SWA_Q_HEADS):
            q = jnp.where(halves[h % 2], qa_ref[rows, (h // 2) * LANES:(h // 2 + 1) * LANES].astype(f32), 0.0)
            if h % 2 != h // 4:
                q = pltpu.roll(q, HEAD_DIM, axis=1)
            pieces.append(q)
        qm = jnp.concatenate(pieces, axis=0).astype(bf16)
        s_c = jnp.dot(qm, cak_ref[b].astype(bf16), preferred_element_type=f32)
        s_n = jnp.dot(qm, nka.astype(bf16), preferred_element_type=f32)
        out, _, _ = _softmax2_pv(jnp.where(valid_ac, s_c, NEG), jnp.where(valid_an, s_n, NEG),
                                 cav_ref[b].astype(bf16), nva.astype(bf16), sink_col)
        for j in range(SWA_Q_HEADS // 2):
            parts = []
            for p in range(2):
                h = 2 * j + p
                o = out[h * S:(h + 1) * S]
                if h % 2 != h // 4:
                    o = pltpu.roll(o, HEAD_DIM, axis=1)
                parts.append(o)
            oac_ref[rows, j * LANES:(j + 1) * LANES] = jnp.where(lo, parts[0], parts[1])

        for j in range(MEM_HEADS // 2):
            cols = slice(j * LANES, (j + 1) * LANES)
            q = qc_ref[rows, cols]
            qm = jnp.concatenate([jnp.where(halves[p], q, jnp.zeros_like(q)) for p in range(2)], axis=0)
            s = jnp.dot(qm, cmk_ref[b, cols, :].astype(bf16), preferred_element_type=f32)
            m = jnp.max(s, axis=-1, keepdims=True)
            e = jnp.exp(s - m)
            r = lax.dot_general(e.astype(bf16), cmv_ref[b, cols, :].astype(bf16), _NT, preferred_element_type=f32)
            out = r / jnp.sum(e, axis=-1, keepdims=True)
            oac_ref[rows, wa + j * LANES:wa + (j + 1) * LANES] = jnp.where(lo, out[:S], out[S:])

        for hp in range(DIL_KV_HEADS // 2):
            cols = slice(hp * LANES, (hp + 1) * LANES)
            pieces = [jnp.where(halves[p], qb_ref[hp * len(DIL_PAIRS) + g, rows, :], 0.0)
                      for g in range(len(DIL_PAIRS)) for p in range(2)]
            qm = jnp.concatenate(pieces, axis=0).astype(bf16)
            s_c = jnp.dot(qm, cbk_ref[b, cols, :].astype(bf16), preferred_element_type=f32)
            s_n = jnp.dot(qm, nkb[cols, :].astype(bf16), preferred_element_type=f32)
            out, m, den = _softmax2_pv(jnp.where(valid_bc, s_c, NEG), jnp.where(valid_bn, s_n, NEG),
                                       cbv_ref[b, cols, :].astype(bf16), nvb[cols, :].astype(bf16))
            lse = m + jnp.log(den)
            res = []
            for p in range(2):
                r = [slice((g * 2 + p) * S, (g * 2 + p + 1) * S) for g in range(len(DIL_PAIRS))]
                mx = jnp.maximum(jnp.maximum(lse[r[0]], lse[r[1]]), lse[r[2]])
                w = [jnp.exp(lse[x] - mx) for x in r]
                tot = w[0] + w[1] + w[2]
                res.append(sum((w[g] / tot) * out[r[g]] for g in range(len(DIL_PAIRS))))
            ob_ref[rows, cols] = jnp.where(lo, res[0], res[1])


def attn_sample(sinks, qa, qb, qc, nka, nva, nkb, nvb, cak, cav, cbk, cbv, cmk, cmv, bt):
    NB, wka, la = cak.shape
    wkb, lb = cbk.shape[1:]
    wm, M = cmk.shape[1:]
    wa, wc = SWA_Q_HEADS * HEAD_DIM, MEM_HEADS * HEAD_DIM
    ng = len(DIL_PAIRS)
    tok = lambda w: pl.BlockSpec((bt * SROWS, w), lambda i: (i, 0))
    buf = lambda f, n: pl.BlockSpec((bt, f, n), lambda i: (i, 0, 0))
    f32 = jnp.float32
    return pl.pallas_call(
        functools.partial(_attn_sample_body, bt),
        grid=(NB // bt,),
        in_specs=[pl.BlockSpec(memory_space=pltpu.SMEM), tok(wa),
                  pl.BlockSpec((2 * ng, bt * SROWS, LANES), lambda i: (0, i, 0)), tok(wc),
                  tok(wka), tok(wka), tok(wkb), tok(wkb),
                  buf(wka, la), buf(wka, la), buf(wkb, lb), buf(wkb, lb), buf(wm, M), buf(wm, M)],
        out_specs=[tok(wa + wc), tok(wkb), buf(wka, la), buf(wka, la), buf(wkb, lb), buf(wkb, lb)],
        out_shape=[jax.ShapeDtypeStruct((NB * SROWS, wa + wc), f32), jax.ShapeDtypeStruct((NB * SROWS, wkb), f32),
                   jax.ShapeDtypeStruct(cak.shape, f32), jax.ShapeDtypeStruct(cak.shape, f32),
                   jax.ShapeDtypeStruct(cbk.shape, f32), jax.ShapeDtypeStruct(cbk.shape, f32)],
        compiler_params=_cparams(("parallel",)), name="attn_sample")(
            sinks, qa, qb, qc, nka, nva, nkb, nvb, cak, cav, cbk, cbv, cmk, cmv)


MOE_BLOCK = 512
ROUTE_SUB = 256
SUBLANES = 8


def _store_row_tiles(ref, row0, y):
    n = y.shape[0]
    for c in range(SUBLANES):
        ref[pl.ds(row0 * SUBLANES + c, n, stride=SUBLANES), :] = y[:, c * LANES:(c + 1) * LANES]


def _load_row_tiles(ref, row0, n):
    return jnp.concatenate([ref[pl.ds(row0 * SUBLANES + c, n, stride=SUBLANES), :] for c in range(SUBLANES)],
                           axis=1)


def _row_tile(ref, r):
    return ref.at[pl.ds(pl.multiple_of(r * SUBLANES, SUBLANES), SUBLANES)]


def _merge_route_body(oac_ref, ob_ref, gate_ref, x_ref, wa_ref, wb_ref, wc_ref, wo_ref, gffn_ref, wr_ref, br_ref,
                      cnt0_ref, x1_ref, h2_ref, mi_ref, mf_ref, cnt_ref, base_ref):
    f32, bf16 = jnp.float32, jnp.bfloat16
    D = x_ref.shape[1]
    wa = SWA_Q_HEADS * HEAD_DIM
    ts = ROUTE_SUB

    @pl.when(pl.program_id(0) == 0)
    def _():
        base_ref[...] = cnt0_ref[...]

    erow = lax.broadcasted_iota(jnp.int32, (N_EXPERTS, ts), 0)
    r8 = lax.broadcasted_iota(jnp.int32, (SUBLANES, ts), 0)
    ti = lax.broadcasted_iota(jnp.int32, (ts, ts), 0)
    tj = lax.broadcasted_iota(jnp.int32, (ts, ts), 1)
    later = (ti < tj).astype(bf16)
    base = base_ref[:, 0:1]
    for sub in range(x_ref.shape[0] // ts):
        rows = slice(sub * ts, (sub + 1) * ts)
        ma = jnp.dot(oac_ref[rows, :wa].astype(bf16), wa_ref[...], preferred_element_type=f32)
        mb = jnp.dot(ob_ref[rows, :].astype(bf16), wb_ref[...], preferred_element_type=f32)
        mc = jnp.dot(oac_ref[rows, wa:].astype(bf16), wc_ref[...], preferred_element_type=f32)
        merged = (gate_ref[rows, :D].astype(f32) * ma + gate_ref[rows, D:2 * D].astype(f32) * mb
                  + gate_ref[rows, 2 * D:].astype(f32) * mc)
        x1 = x_ref[rows, :] + jnp.dot(merged.astype(bf16), wo_ref[...], preferred_element_type=f32)
        x1_ref[rows, :] = x1
        h2 = _rms(x1, gffn_ref[...])
        _store_row_tiles(h2_ref, sub * ts, h2)

        work = lax.dot_general(wr_ref[...], h2.astype(bf16), _NT, preferred_element_type=f32) + br_ref[:, 0:1]
        vals, idxs = [], []
        for _ in range(TOP_K):
            m = jnp.max(work, axis=0, keepdims=True)
            idx = jnp.min(jnp.where(work == m, erow, N_EXPERTS), axis=0, keepdims=True)
            vals.append(m)
            idxs.append(idx)
            work = jnp.where(erow == idx, -jnp.inf, work)
        es = [jnp.exp(v - vals[0]) for v in vals]
        tot = es[0] + es[1] + es[2] + es[3]

        onehot = [(erow == idx).astype(f32) for idx in idxs]
        assign = onehot[0] + onehot[1] + onehot[2] + onehot[3]
        before = jnp.dot(assign.astype(bf16), later, preferred_element_type=f32) + base
        base = base + jnp.sum(assign, axis=1, keepdims=True)

        mi = jnp.zeros((SUBLANES, ts), jnp.int32)
        gates = jnp.zeros((SUBLANES, ts), f32)
        for k in range(TOP_K):
            rank = jnp.sum(onehot[k] * before, axis=0, keepdims=True).astype(jnp.int32)
            mi = jnp.where(r8 == k, idxs[k], mi)
            mi = jnp.where(r8 == TOP_K + k, rank, mi)
            gates = jnp.where(r8 == k, es[k] / tot, gates)
        mi_ref[:, rows] = mi
        mf_ref[rows, :] = jnp.concatenate([gates, jnp.zeros((LANES - SUBLANES, ts), f32)], axis=0).T
    base_ref[...] = jnp.broadcast_to(base, base_ref.shape)
    cnt_ref[...] = jnp.broadcast_to(base, cnt_ref.shape)


def merge_route(o_ac, o_b, gates, x, wa, wb, wc, wo, g_ffn, wr, br, cnt0, tm):
    R, D = x.shape
    row = lambda w: pl.BlockSpec((tm, w), lambda i: (i, 0))
    full = lambda a: pl.BlockSpec(a.shape, lambda i: (0, 0))
    return pl.pallas_call(
        _merge_route_body,
        grid=(R // tm,),
        in_specs=[row(o_ac.shape[1]), row(o_b.shape[1]), row(gates.shape[1]), row(D),
                  full(wa), full(wb), full(wc), full(wo), full(g_ffn), full(wr), full(br), full(cnt0)],
        out_specs=[row(D), pl.BlockSpec((tm * SUBLANES, LANES), lambda i: (i, 0)),
                   pl.BlockSpec((SUBLANES, tm), lambda i: (0, i)), row(LANES),
                   pl.BlockSpec((N_EXPERTS, LANES), lambda i: (0, 0))],
        out_shape=[jax.ShapeDtypeStruct((R, D), jnp.float32), jax.ShapeDtypeStruct((R * SUBLANES, LANES), jnp.float32),
                   jax.ShapeDtypeStruct((SUBLANES, R), jnp.int32), jax.ShapeDtypeStruct((R, LANES), jnp.float32),
                   jax.ShapeDtypeStruct((N_EXPERTS, LANES), jnp.float32)],
        scratch_shapes=[pltpu.VMEM((N_EXPERTS, LANES), jnp.float32)],
        compiler_params=_cparams(("arbitrary",)), name="merge_route")(
            o_ac, o_b, gates, x, wa, wb, wc, wo, g_ffn, wr, br, cnt0)


def _route_tables_body(cnt_ref, mi_ref, dest_ref, blk_ref, pad_ref):
    tm = mi_ref.shape[1]
    nbl = blk_ref.shape[1]
    erow1 = lax.broadcasted_iota(jnp.int32, (N_EXPERTS, LANES), 0)
    shift = MOE_BLOCK.bit_length() - 1
    cnt = cnt_ref[...].astype(jnp.int32)
    padded = ((cnt + (MOE_BLOCK - 1)) >> shift) << shift
    pend = padded
    s = 1
    while s < N_EXPERTS:
        pend = pend + jnp.where(erow1 >= s, pltpu.roll(pend, s, axis=0), 0)
        s *= 2
    pstart = pend - padded
    mi = mi_ref[...]
    erow = lax.broadcasted_iota(jnp.int32, (N_EXPERTS, tm), 0)
    r8 = lax.broadcasted_iota(jnp.int32, (SUBLANES, tm), 0)
    dest = jnp.zeros((SUBLANES, tm), jnp.int32)
    for k in range(TOP_K):
        start = jnp.sum(jnp.where(erow == mi[k:k + 1, :], pstart[:, 0:1], 0), axis=0, keepdims=True)
        dest = jnp.where(r8 == k, start + mi[TOP_K + k:TOP_K + k + 1, :], dest)
    dest_ref[...] = dest

    @pl.when(pl.program_id(0) == 0)
    def _():
        row0 = lax.broadcasted_iota(jnp.int32, (N_EXPERTS, nbl), 1) * MOE_BLOCK
        ended = jnp.sum(jnp.where(pend[:, 0:1] <= row0, 1, 0), axis=0, keepdims=True)
        blk_ref[...] = jnp.broadcast_to(jnp.minimum(ended, N_EXPERTS - 1), blk_ref.shape)
        pad_ref[0] = pstart + cnt
        pad_ref[1] = pend


def route_tables(cnt, mi, nbl, tm):
    R = mi.shape[1]
    return pl.pallas_call(
        _route_tables_body,
        grid=(R // tm,),
        in_specs=[pl.BlockSpec((N_EXPERTS, LANES), lambda i: (0, 0)), pl.BlockSpec((SUBLANES, tm), lambda i: (0, i))],
        out_specs=[pl.BlockSpec((SUBLANES, tm), lambda i: (0, i)), pl.BlockSpec((SUBLANES, nbl), lambda i: (0, 0)),
                   pl.BlockSpec((2, N_EXPERTS, LANES), lambda i: (0, 0, 0))],
        out_shape=[jax.ShapeDtypeStruct((SUBLANES, R), jnp.int32), jax.ShapeDtypeStruct((SUBLANES, nbl), jnp.int32),
                   jax.ShapeDtypeStruct((2, N_EXPERTS, LANES), jnp.int32)],
        compiler_params=_cparams(("arbitrary",)), name="route_tables")(cnt, mi)


def _dispatch_body(n_first, n_steps, pad0_ref, pad1_ref, dest_ref, ha_ref, hb_ref, xs_ref, zero_ref, sem):
    i = pl.program_id(0)
    tm = ha_ref.shape[0] // SUBLANES
    rs = xs_ref.shape[0] // SUBLANES

    def scatter_tile(h_ref):
        def row_copy(r, k):
            return pltpu.make_async_copy(_row_tile(h_ref, r), _row_tile(xs_ref, dest_ref[0, 0, k * tm + r]), sem)

        def start(r, c):
            for k in range(TOP_K):
                row_copy(r, k).start(priority=k % 2)
            return c

        def wait(r, c):
            for k in range(TOP_K):
                row_copy(r, k).wait()
            return c

        lax.fori_loop(0, tm, start, 0, unroll=8)
        lax.fori_loop(0, tm, wait, 0, unroll=8)

    @pl.when(i < n_first)
    def _():
        scatter_tile(ha_ref)

    @pl.when(i >= n_first)
    def _():
        scatter_tile(hb_ref)

    @pl.when(i == n_steps - 1)
    def _():
        zero_ref[...] = jnp.zeros_like(zero_ref)
        zrows = zero_ref.shape[0] // SUBLANES

        def zero_range(lo, hi, go):
            n_big = (hi - lo) // zrows

            def big(j, c):
                start = pl.multiple_of((lo + j * zrows) * SUBLANES, SUBLANES)
                cp = pltpu.make_async_copy(zero_ref, xs_ref.at[pl.ds(start, zrows * SUBLANES)], sem)
                cp.start() if go else cp.wait()
                return c

            def one(r, c):
                cp = pltpu.make_async_copy(zero_ref.at[pl.ds(0, SUBLANES)], _row_tile(xs_ref, r), sem)
                cp.start() if go else cp.wait()
                return c

            lax.fori_loop(0, n_big, big, 0)
            lax.fori_loop(lo + n_big * zrows, hi, one, 0)

        for go in (True, False):
            for e in range(N_EXPERTS):
                zero_range(pad0_ref[e], pad1_ref[e], go)
            zero_range(pad1_ref[N_EXPERTS - 1], rs, go)


def dispatch(pad0, pad1, dest3, h_a, h_b, rs, tm):
    S = SUBLANES
    n_a, n_b = h_a.shape[0] // (tm * S), h_b.shape[0] // (tm * S)
    grid_spec = pltpu.PrefetchScalarGridSpec(
        num_scalar_prefetch=2,
        grid=(n_a + n_b,),
        in_specs=[pl.BlockSpec((1, 1, TOP_K * tm), lambda i, p0, p1: (i, 0, 0), memory_space=pltpu.SMEM),
                  pl.BlockSpec((tm * S, LANES), lambda i, p0, p1: (jnp.minimum(i, n_a - 1), 0)),
                  pl.BlockSpec((tm * S, LANES), lambda i, p0, p1: (jnp.maximum(i - n_a, 0), 0))],
        out_specs=pl.BlockSpec(memory_space=pl.ANY),
        scratch_shapes=[pltpu.VMEM((16 * S, LANES), h_a.dtype), pltpu.SemaphoreType.DMA])
    return pl.pallas_call(
        functools.partial(_dispatch_body, n_a, n_a + n_b), grid_spec=grid_spec,
        out_shape=jax.ShapeDtypeStruct((rs * S, LANES), h_a.dtype),
        compiler_params=_cparams(("arbitrary",)), name="dispatch")(pad0, pad1, dest3, h_a, h_b)


def _moe_body(nblk, be_ref, nact_ref, x_ref, wgu_hbm, bgu_ref, wd_hbm, bd_ref, y_ref, wgu_f, wd_f, wgu_s, wd_s,
              sems):
    f32, bf16 = jnp.float32, jnp.bfloat16
    b = pl.program_id(0)
    e = be_ref[b]

    def fetch(ex, go):
        for src, dst, s in ((wgu_hbm, wgu_f, 0), (wd_hbm, wd_f, 1)):
            cp = pltpu.make_async_copy(src.at[ex], dst, sems.at[s])
            cp.start() if go else cp.wait()

    @pl.when(b == 0)
    def _():
        fetch(e, True)

    @pl.when((b == 0) | (e != be_ref[jnp.maximum(b - 1, 0)]))
    def _():
        fetch(e, False)
        wgu_s[...] = wgu_f[...].astype(bf16)
        wd_s[...] = wd_f[...].astype(bf16)
        nxt = lax.while_loop(lambda j: (j < nblk) & (be_ref[jnp.minimum(j, nblk - 1)] == e), lambda j: j + 1, b + 1)

        @pl.when(nxt < nblk)
        def _():
            fetch(be_ref[jnp.minimum(nxt, nblk - 1)], True)

    @pl.when(b < nact_ref[0])
    def _():
        x = _load_row_tiles(x_ref, 0, MOE_BLOCK).astype(bf16)
        gu = jnp.dot(x, wgu_s[...], preferred_element_type=f32) + bgu_ref[0]
        gt = jnp.minimum(gu[:, :D_FF], SWIGLU_LIMIT)
        up = jnp.clip(gu[:, D_FF:], -SWIGLU_LIMIT, SWIGLU_LIMIT)
        act = (up + 1.0) * (gt * jax.nn.sigmoid(gt * SWIGLU_ALPHA))
        y = jnp.dot(act.astype(bf16), wd_s[...], preferred_element_type=f32) + bd_ref[0]
        _store_row_tiles(y_ref, 0, y)

    @pl.when(b >= nact_ref[0])
    def _():
        y_ref[...] = jnp.zeros_like(y_ref)


def moe_ffn(blk_e, n_active, xs, w_gate_up, b_gate_up, w_down, b_down):
    RS = xs.shape[0] // SUBLANES
    E, D, F2 = w_gate_up.shape
    blk_rows = MOE_BLOCK * SUBLANES
    grid_spec = pltpu.PrefetchScalarGridSpec(
        num_scalar_prefetch=2,
        grid=(RS // MOE_BLOCK,),
        in_specs=[pl.BlockSpec((blk_rows, LANES), lambda b, be, na: (b, 0)),
                  pl.BlockSpec(memory_space=pl.ANY),
                  pl.BlockSpec((1, 1, F2), lambda b, be, na: (be[b], 0, 0)),
                  pl.BlockSpec(memory_space=pl.ANY),
                  pl.BlockSpec((1, 1, D), lambda b, be, na: (be[b], 0, 0))],
        out_specs=pl.BlockSpec((blk_rows, LANES), lambda b, be, na: (b, 0)),
        scratch_shapes=[pltpu.VMEM((D, F2), w_gate_up.dtype), pltpu.VMEM((F2 // 2, D), w_down.dtype),
                        pltpu.VMEM((D, F2), jnp.bfloat16), pltpu.VMEM((F2 // 2, D), jnp.bfloat16),
                        pltpu.SemaphoreType.DMA((2,))])
    return pl.pallas_call(
        functools.partial(_moe_body, RS // MOE_BLOCK), grid_spec=grid_spec,
        out_shape=jax.ShapeDtypeStruct(xs.shape, jnp.float32),
        compiler_params=_cparams(("arbitrary",)), name="moe_ffn")(
            blk_e, n_active, xs, w_gate_up, b_gate_up.reshape(E, 1, F2), w_down, b_down.reshape(E, 1, D))


def _combine_body(n_tiles, dest_ref, dest_next_ref, x1_ref, mf_ref, g_ref, ys_ref, o_ref, buf, sems):
    tm = x1_ref.shape[0]
    i = pl.program_id(0)
    slot = i % 2
    per_slot = TOP_K * tm

    def gather(d_ref, s, go):
        def row_copy(r, k):
            return pltpu.make_async_copy(_row_tile(ys_ref, d_ref[0, 0, k * tm + r]),
                                         _row_tile(buf, s * per_slot + k * tm + r), sems.at[s])

        def body(r, c):
            for k in range(TOP_K):
                if go:
                    row_copy(r, k).start(priority=k % 2)
                else:
                    row_copy(r, k).wait()
            return c

        lax.fori_loop(0, tm, body, 0, unroll=8)

    @pl.when(i == 0)
    def _():
        gather(dest_ref, slot, True)

    @pl.when(i + 1 < n_tiles)
    def _():
        gather(dest_next_ref, 1 - slot, True)

    gather(dest_ref, slot, False)
    y = x1_ref[...]
    for k in range(TOP_K):
        y = y + mf_ref[:, k:k + 1] * _load_row_tiles(buf, slot * per_slot + k * tm, tm)
    o_ref[...] = _rms(y, g_ref[...])


def combine(dest3, x1, mf, g_final, ys, tm):
    R, D = x1.shape
    n = R // tm
    dest_spec = lambda f: pl.BlockSpec((1, 1, TOP_K * tm), f, memory_space=pltpu.SMEM)
    return pl.pallas_call(
        functools.partial(_combine_body, n),
        grid=(n,),
        in_specs=[dest_spec(lambda i: (i, 0, 0)), dest_spec(lambda i: (jnp.minimum(i + 1, n - 1), 0, 0)),
                  pl.BlockSpec((tm, D), lambda i: (i, 0)),
                  pl.BlockSpec((tm, LANES), lambda i: (i, 0)),
                  pl.BlockSpec((1, D), lambda i: (0, 0)),
                  pl.BlockSpec(memory_space=pl.ANY)],
        out_specs=pl.BlockSpec((tm, D), lambda i: (i, 0)),
        out_shape=jax.ShapeDtypeStruct((R, D), jnp.float32),
        scratch_shapes=[pltpu.VMEM((2 * TOP_K * tm * SUBLANES, LANES), ys.dtype), pltpu.SemaphoreType.DMA((2,))],
        compiler_params=_cparams(("arbitrary",)), name="combine")(dest3, dest3, x1, mf, g_final, ys)


def moe_layer(group_a, group_b, cnt, g_final, w_gate_up, b_gate_up, w_down, b_down, tm):
    n_assign = (group_a[0].shape[0] + group_b[0].shape[0]) * TOP_K
    nb = (n_assign + N_EXPERTS * (MOE_BLOCK - 1)) // MOE_BLOCK + 1
    nbl = -(-nb // LANES) * LANES
    dests = []
    for x1, h2, mi, mf in (group_a, group_b):
        dest, blk, pad = route_tables(cnt, mi, nbl, min(x1.shape[0], 4 * tm))
        n_tiles = x1.shape[0] // tm
        dests.append(jnp.transpose(dest[:TOP_K].reshape(TOP_K, n_tiles, tm), (1, 0, 2)).reshape(n_tiles, 1, TOP_K * tm))
    xs = dispatch(pad[0, :, 0], pad[1, :, 0], jnp.concatenate(dests), group_a[1], group_b[1], nb * MOE_BLOCK, tm)
    n_active = (pad[1, N_EXPERTS - 1, 0] // MOE_BLOCK).reshape(1)
    ys = moe_ffn(blk[0, :nb], n_active, xs, w_gate_up, b_gate_up, w_down, b_down)
    return [combine(dest3, x1, mf, g_final, ys, tm) for dest3, (x1, h2, mi, mf) in zip(dests, (group_a, group_b))]


def kernel(x_prompt, x_sample, cache_swa_k, cache_swa_v, cache_dil_k, cache_dil_v, cache_mem_k, cache_mem_v, mem_prompt, norm_attn, norm_mem, w_in, w_mem_kv, sinks, w_br_a, w_br_b, w_br_c, w_out, norm_ffn, w_router, b_router, w_gate_up, b_gate_up, w_down, b_down, norm_final):
    f32, bf16 = jnp.float32, jnp.bfloat16
    TM = 256
    B, L, D = x_prompt.shape
    NB, n_new, _ = x_sample.shape
    M = mem_prompt.shape[1]
    la, lb = cache_swa_k.shape[2], cache_dil_k.shape[2]
    wka, wkb, wc = SWA_KV_HEADS * HEAD_DIM, DIL_KV_HEADS * HEAD_DIM, MEM_HEADS * HEAD_DIM
    assert n_new == N_NEW and cache_swa_k.shape[0] == 1

    w_in_b = w_in[0].astype(bf16)
    g_attn = norm_attn[0].reshape(1, D)
    secs = in_sections()

    tabs_p = rope_tables(jnp.arange(L, dtype=jnp.int32))
    qa, ka, va, qb, kb, vb, qc, gates = norm_proj(x_prompt.reshape(B * L, D), g_attn, w_in_b, tabs_p, secs, TM)
    mk, mv = norm_proj(mem_prompt.reshape(B * M, D), norm_mem[0].reshape(1, D), w_mem_kv[0].astype(bf16), None,
                       [(_chunks(0, wc), "plain", False, f32), (_chunks(wc, wc), "plain", False, f32)], TM)
    o_ac = attn_swa_mem(sinks[0], qa, ka, va, qc, mk, mv, B, L)
    o_b, kb_t, vb_t = attn_dilated(qb, kb, vb, B, L)

    xs_pad = jnp.pad(x_sample, ((0, 0), (0, SROWS - N_NEW), (0, 0))).reshape(NB * SROWS, D)
    tabs_s = rope_tables(PAST_LEN + (jnp.arange(TM, dtype=jnp.int32) % SROWS))
    qa_s, ka_s, va_s, qb_s, kb_s, vb_s, qc_s, gates_s = norm_proj(xs_pad, g_attn, w_in_b, tabs_s, secs, TM)
    real = lambda t: t.reshape(NB, SROWS, -1)[:, :N_NEW].reshape(NB * N_NEW, -1)
    fmaj = lambda c: jnp.transpose(c[0], (0, 2, 3, 1)).reshape(NB, -1, c.shape[2])
    o_ac_s, o_b_s, swa_k_s, swa_v_s, dil_k_s, dil_v_s = attn_sample(
        sinks[0], qa_s, qb_s, qc_s, ka_s, va_s, kb_s, vb_s,
        fmaj(cache_swa_k), fmaj(cache_swa_v), fmaj(cache_dil_k), fmaj(cache_dil_v),
        fmaj(cache_mem_k), fmaj(cache_mem_v), 2)

    wr = w_router[0].T.astype(bf16)
    br = jnp.broadcast_to(b_router[0].astype(f32)[:, None], (N_EXPERTS, LANES))
    wts = (w_br_a[0].astype(bf16), w_br_b[0].astype(bf16), w_br_c[0].astype(bf16), w_out[0].astype(bf16),
           norm_ffn[0].reshape(1, D), wr, br)
    x1_p, h2_p, mi_p, mf_p, cnt_p = merge_route(o_ac, o_b, gates, x_prompt.reshape(B * L, D), *wts,
                                                jnp.zeros((N_EXPERTS, LANES), f32), 2 * ROUTE_SUB)
    x1_s, h2_s, mi_s, mf_s, cnt = merge_route(real(o_ac_s), real(o_b_s), real(gates_s),
                                              x_sample.reshape(NB * N_NEW, D), *wts, cnt_p, 2 * ROUTE_SUB)
    y_p, y_s = moe_layer((x1_p, h2_p, mi_p, mf_p), (x1_s, h2_s, mi_s, mf_s), cnt, norm_final.reshape(1, D),
                         w_gate_up[0], b_gate_up[0], w_down[0], b_down[0], TM)

    heads = lambda t, n, h: t.reshape(1, t.shape[0] // n, n, h, HEAD_DIM)
    tmaj = lambda t, h: jnp.transpose(t.reshape(t.shape[0], h, HEAD_DIM, t.shape[2]), (0, 3, 1, 2))[None]
    ka3, va3 = ka.reshape(B, L, wka), va.reshape(B, L, wka)
    la_p = min(SWA_WINDOW, L)
    return (y_p.reshape(B, L, D), y_s.reshape(NB, N_NEW, D),
            heads(ka3[:, L - la_p:].reshape(B * la_p, wka), la_p, SWA_KV_HEADS),
            heads(va3[:, L - la_p:].reshape(B * la_p, wka), la_p, SWA_KV_HEADS),
            tmaj(kb_t, DIL_KV_HEADS), tmaj(vb_t, DIL_KV_HEADS),
            heads(mk, M, MEM_HEADS), heads(mv, M, MEM_HEADS),
            tmaj(swa_k_s, SWA_KV_HEADS), tmaj(swa_v_s, SWA_KV_HEADS),
            tmaj(dil_k_s, DIL_KV_HEADS), tmaj(dil_v_s, DIL_KV_HEADS))
```

```python
import functools

import jax
import jax.numpy as jnp
import numpy as np
from jax import lax
from jax.experimental import pallas as pl
from jax.experimental.pallas import tpu as pltpu

D_MODEL = 1024
HEAD_DIM = 64
ROPE_DIM = 16
ROPE_HALF = 8
ROPE_THETA = 500000.0
PAST_LEN = 16384
SWA_Q_HEADS = 8
SWA_KV_HEADS = 2
SWA_WINDOW = 128
DIL_PAIRS = ((128, 1), (512, 4), (2048, 16))
DIL_KV_HEADS = 4
MEM_HEADS = 4
N_EXPERTS = 32
TOP_K = 4
D_FF = 1024
SWIGLU_LIMIT = 7.0
SWIGLU_ALPHA = 1.702
RMS_EPS = 1e-5
ATT_BLOCK = 128
SCALE = HEAD_DIM ** -0.5

LANES = 128
NEG = -1e30
VMEM_LIMIT = 56 * 1024 * 1024


def _cparams(sem):
    return pltpu.CompilerParams(dimension_semantics=sem, vmem_limit_bytes=VMEM_LIMIT)


def _rms(x, g):
    return x * lax.rsqrt(jnp.mean(x * x, axis=-1, keepdims=True) + RMS_EPS) * g


def _norm_proj_body(sections, x_ref, g_ref, w_ref, cs_ref, *out_refs):
    h = _rms(x_ref[...], g_ref[...]).astype(jnp.bfloat16)
    if cs_ref is not None:
        cos = cs_ref[0]
        sin_lo = cs_ref[1]
        sin_hi = cs_ref[2]
    dest = {lo: (o_ref, c, kind, slabs)
            for (cols, kind, slabs), o_ref in zip(sections, out_refs) for c, lo in enumerate(cols)}
    todo = sorted(dest)
    while todo:
        lo = todo.pop(0)
        n = 2 if todo and todo[0] == lo + LANES else 1
        if n == 2:
            todo.pop(0)
        yy = jnp.dot(h, w_ref[:, lo:lo + n * LANES], preferred_element_type=jnp.float32)
        for part in range(n):
            o_ref, c, kind, slabs = dest[lo + part * LANES]
            y = yy[:, part * LANES:(part + 1) * LANES]
            if kind in ("rope", "rope_q"):
                y = (y * cos + pltpu.roll(y, LANES - ROPE_HALF, axis=1) * sin_lo
                     + pltpu.roll(y, ROPE_HALF, axis=1) * sin_hi)
            if kind in ("rope_q", "q"):
                y = y * SCALE
            if kind == "sigmoid":
                y = jax.nn.sigmoid(y)
            if slabs:
                o_ref[c] = y.astype(o_ref.dtype)
            else:
                o_ref[:, c * LANES:(c + 1) * LANES] = y.astype(o_ref.dtype)


def rope_tables(pos):
    inv_freq = ROPE_THETA ** (-jnp.arange(ROPE_HALF, dtype=jnp.float32) / ROPE_HALF)
    ang = pos.astype(jnp.float32)[:, None] * inv_freq[None, :]
    cos, sin = jnp.cos(ang), jnp.sin(ang)
    n = pos.shape[0]
    one = jnp.ones((n, HEAD_DIM - ROPE_DIM), jnp.float32)
    zero = jnp.zeros((n, HEAD_DIM - ROPE_HALF), jnp.float32)
    c = jnp.concatenate([cos, cos, one], axis=1)
    s_lo = jnp.concatenate([-sin, zero], axis=1)
    s_hi = jnp.concatenate([jnp.zeros((n, ROPE_HALF), jnp.float32), sin,
                            jnp.zeros((n, HEAD_DIM - ROPE_DIM), jnp.float32)], axis=1)
    tab = jnp.stack([c, s_lo, s_hi])
    return jnp.concatenate([tab, tab], axis=2)


def norm_proj(x, g, w, tables, sections, tm):
    R, D = x.shape
    in_specs = [pl.BlockSpec((tm, D), lambda i: (i, 0)),
                pl.BlockSpec((1, D), lambda i: (0, 0)),
                pl.BlockSpec(w.shape, lambda i: (0, 0))]
    args = [x, g, w]
    if tables is not None:
        nt = tables.shape[1] // tm
        in_specs.append(pl.BlockSpec((3, tm, LANES), lambda i: (0, i % nt, 0)))
        args.append(tables)
    out_shape, out_specs, secs = [], [], []
    for (cols, kind, slabs, dtype) in sections:
        secs.append((cols, kind, slabs))
        width = LANES * len(cols)
        if slabs:
            out_shape.append(jax.ShapeDtypeStruct((width // LANES, R, LANES), dtype))
            out_specs.append(pl.BlockSpec((width // LANES, tm, LANES), lambda i: (0, i, 0)))
        else:
            out_shape.append(jax.ShapeDtypeStruct((R, width), dtype))
            out_specs.append(pl.BlockSpec((tm, width), lambda i: (i, 0)))
    if tables is None:
        body = lambda x_ref, g_ref, w_ref, *o: _norm_proj_body(secs, x_ref, g_ref, w_ref, None, *o)
    else:
        body = functools.partial(_norm_proj_body, secs)
    return pl.pallas_call(
        body, grid=(R // tm,), in_specs=in_specs, out_specs=out_specs, out_shape=out_shape,
        compiler_params=_cparams(("parallel",)), name="norm_proj")(*args)


def _chunks(start, width):
    return tuple(range(start, start + width, LANES))


def in_sections():
    f32, bf16 = jnp.float32, jnp.bfloat16
    qb0 = 1024 - 256
    qb_cols = tuple(qb0 + HEAD_DIM * (4 * g + 2 * hp) for hp in range(2) for g in range(3))
    return [
        (_chunks(0, 512), "rope_q", False, bf16),
        (_chunks(512, 128), "rope", False, f32),
        (_chunks(640, 128), "plain", False, f32),
        (qb_cols, "rope_q", True, f32),
        (_chunks(1536, 256), "rope", False, f32),
        (_chunks(1792, 256), "plain", False, f32),
        (_chunks(2048, 256), "q", False, bf16),
        (_chunks(2304, 3072), "sigmoid", False, bf16),
    ]


_NT = (((1,), (1,)), ((), ()))


def _half_masks():
    lane = lax.broadcasted_iota(jnp.int32, (1, LANES), 1)
    return lane < HEAD_DIM, lane >= HEAD_DIM


def _softmax_pv(s, v_half, sink=None):
    m = jnp.max(s, axis=-1, keepdims=True)
    if sink is not None:
        m = jnp.maximum(m, sink)
    e = jnp.exp(s - m)
    den = jnp.sum(e, axis=-1, keepdims=True)
    if sink is not None:
        den = den + jnp.exp(sink - m)
    r = jnp.dot(e.astype(jnp.bfloat16), v_half, preferred_element_type=jnp.float32)
    return r / den, m, den


def _attn_swa_mem_body(L, sink_ref, qa_ref, ka_ref, va_ref, qc_ref, mk_ref, mv_ref, o_ref,
                       kat_ref, vat_ref, mkt_ref, mvt_ref):
    bf16 = jnp.bfloat16
    lo, hi = _half_masks()
    halves = (lo, hi)
    T = ATT_BLOCK
    kat_ref[...] = ka_ref[L - SWA_WINDOW:, :].T
    vat_ref[...] = va_ref[L - SWA_WINDOW:, :].T
    mkt_ref[...] = mk_ref[...].T
    mvt_ref[...] = mv_ref[...].T
    G = SWA_Q_HEADS // SWA_KV_HEADS
    mem_k = [mk_ref[:, j * LANES:(j + 1) * LANES].astype(bf16) for j in range(2)]
    mem_v = [mv_ref[:, j * LANES:(j + 1) * LANES].astype(bf16) for j in range(2)]
    qi = lax.broadcasted_iota(jnp.int32, (2 * T, 2 * T), 0) & (T - 1)
    kj = lax.broadcasted_iota(jnp.int32, (2 * T, 2 * T), 1)
    hrow = lax.broadcasted_iota(jnp.int32, (2 * T, 1), 0) >> (T.bit_length() - 1)

    def heads_of(pair):
        return jnp.concatenate([jnp.where(halves[p], pair, jnp.zeros_like(pair)) for p in range(2)], axis=0)

    def block(blk, carry):
        r0 = pl.multiple_of(blk * T, T)
        ws = pl.multiple_of(jnp.maximum(r0 - T, 0), T)
        dist = qi - kj + (r0 - ws)
        valid = (dist >= 0) & (dist <= SWA_WINDOW - 1)
        k = ka_ref[pl.ds(ws, 2 * T), :]
        v = va_ref[pl.ds(ws, 2 * T), :]
        for kv in range(SWA_KV_HEADS):
            k1 = jnp.where(halves[kv], k, 0.0)
            v1 = jnp.where(halves[kv], v, 0.0)
            k_dup = (k1 + pltpu.roll(k1, HEAD_DIM, axis=1)).astype(bf16)
            v_dup = (v1 + pltpu.roll(v1, HEAD_DIM, axis=1)).astype(bf16)
            for j in range(kv * G // 2, (kv + 1) * G // 2):
                qm = heads_of(qa_ref[pl.ds(r0, T), j * LANES:(j + 1) * LANES])
                sink = jnp.where(hrow == 0, sink_ref[2 * j], sink_ref[2 * j + 1])
                s = lax.dot_general(qm, k_dup, _NT, preferred_element_type=jnp.float32)
                out, _, _ = _softmax_pv(jnp.where(valid, s, NEG), v_dup, sink)
                o_ref[pl.ds(r0, T), j * LANES:(j + 1) * LANES] = jnp.where(lo, out[:T], out[T:]).astype(o_ref.dtype)
        for j in range(MEM_HEADS // 2):
            qm = heads_of(qc_ref[pl.ds(r0, T), j * LANES:(j + 1) * LANES])
            s = lax.dot_general(qm, mem_k[j], _NT, preferred_element_type=jnp.float32)
            out, _, _ = _softmax_pv(s, mem_v[j])
            c0 = SWA_Q_HEADS * HEAD_DIM + j * LANES
            o_ref[pl.ds(r0, T), c0:c0 + LANES] = jnp.where(lo, out[:T], out[T:]).astype(o_ref.dtype)
        return carry

    lax.fori_loop(0, L // T, block, 0)


def attn_swa_mem(sinks, qa, ka, va, qc, mk, mv, B, L):
    M = mk.shape[0] // B
    wa, wc = SWA_Q_HEADS * HEAD_DIM, MEM_HEADS * HEAD_DIM
    wka = ka.shape[1]
    n_win = min(SWA_WINDOW, L)
    row = lambda w: pl.BlockSpec((L, w), lambda b: (b, 0))
    fmaj = lambda f, n: pl.BlockSpec((None, f, n), lambda b: (b, 0, 0))
    f32 = jnp.float32
    return pl.pallas_call(
        functools.partial(_attn_swa_mem_body, L),
        grid=(B,),
        in_specs=[pl.BlockSpec(memory_space=pltpu.SMEM), row(wa), row(LANES), row(LANES), row(wc),
                  pl.BlockSpec((M, wc), lambda b: (b, 0)), pl.BlockSpec((M, wc), lambda b: (b, 0))],
        out_specs=[row(wa + wc), fmaj(wka, n_win), fmaj(wka, n_win), fmaj(wc, M), fmaj(wc, M)],
        out_shape=[jax.ShapeDtypeStruct((B * L, wa + wc), jnp.bfloat16),
                   jax.ShapeDtypeStruct((B, wka, n_win), f32), jax.ShapeDtypeStruct((B, wka, n_win), f32),
                   jax.ShapeDtypeStruct((B, wc, M), f32), jax.ShapeDtypeStruct((B, wc, M), f32)],
        compiler_params=_cparams(("parallel",)), name="attn_swa_mem")(sinks, qa, ka, va, qc, mk, mv)


def _attn_dil_body(L, qb_ref, kb_ref, vb_ref, o_ref, kt_ref, vt_ref, og_ref, lse_ref):
    bf16 = jnp.bfloat16
    lo, hi = _half_masks()
    halves = (lo, hi)
    T = ATT_BLOCK
    kt_ref[...] = kb_ref[...].T
    vt_ref[...] = vb_ref[...].T
    for g, (window, dil) in enumerate(DIL_PAIRS):
        lc = L // dil
        nbc = lc // T
        W = min(2 * T, lc)
        max_dist = window // dil
        qi = lax.broadcasted_iota(jnp.int32, (2 * T, W), 0) & (T - 1)
        kj = lax.broadcasted_iota(jnp.int32, (2 * T, W), 1)

        def unit(u, carry, g=g, dil=dil, nbc=nbc, W=W, max_dist=max_dist, qi=qi, kj=kj):
            c = u >> (nbc.bit_length() - 1)
            n = u & (nbc - 1)
            wsc = jnp.maximum(n * T - T, 0) if W == 2 * T else 0
            q0 = c + dil * T * n
            k0 = c + dil * wsc
            dist = qi - kj + (n * T - wsc)
            valid = (dist >= 0) & (dist <= max_dist)
            q = qb_ref[g, pl.ds(q0, T, stride=dil), :]
            k = kb_ref[pl.ds(k0, W, stride=dil), :]
            v = vb_ref[pl.ds(k0, W, stride=dil), :]
            qm = jnp.concatenate([jnp.where(halves[p], q, 0.0) for p in range(2)], axis=0).astype(bf16)
            s = lax.dot_general(qm, k.astype(bf16), _NT, preferred_element_type=jnp.float32)
            out, m, den = _softmax_pv(jnp.where(valid, s, NEG), v.astype(bf16))
            lse = m + jnp.log(den)
            og_ref[g, pl.ds(q0, T, stride=dil), :] = jnp.where(lo, out[:T], out[T:])
            lse_ref[g, pl.ds(q0, T, stride=dil), :] = jnp.where(lo, lse[:T], lse[T:])
            return carry

        lax.fori_loop(0, dil * nbc, unit, 0, unroll=4)

    def merge(i, carry):
        r0 = pl.multiple_of(i * T, T)
        ls = [lse_ref[g, pl.ds(r0, T), :] for g in range(len(DIL_PAIRS))]
        m = jnp.maximum(jnp.maximum(ls[0], ls[1]), ls[2])
        ws = [jnp.exp(l - m) for l in ls]
        tot = ws[0] + ws[1] + ws[2]
        out = sum((w / tot) * og_ref[g, pl.ds(r0, T), :] for g, w in enumerate(ws))
        o_ref[pl.ds(r0, T), :] = out.astype(o_ref.dtype)
        return carry

    lax.fori_loop(0, L // T, merge, 0)


def attn_dilated(qb, kb, vb, B, L):
    ng = len(DIL_PAIRS)
    fmaj = pl.BlockSpec((None, LANES, L), lambda b, hp: (b, hp, 0))
    return pl.pallas_call(
        functools.partial(_attn_dil_body, L),
        grid=(B, 2),
        in_specs=[pl.BlockSpec((ng, L, LANES), lambda b, hp: (hp, b, 0)),
                  pl.BlockSpec((L, LANES), lambda b, hp: (b, hp)),
                  pl.BlockSpec((L, LANES), lambda b, hp: (b, hp))],
        out_specs=[pl.BlockSpec((L, LANES), lambda b, hp: (b, hp)), fmaj, fmaj],
        out_shape=[jax.ShapeDtypeStruct((B * L, 2 * LANES), jnp.bfloat16),
                   jax.ShapeDtypeStruct((B, 2 * LANES, L), jnp.float32),
                   jax.ShapeDtypeStruct((B, 2 * LANES, L), jnp.float32)],
        scratch_shapes=[pltpu.VMEM((ng, L, LANES), jnp.float32), pltpu.VMEM((ng, L, LANES), jnp.float32)],
        compiler_params=_cparams(("parallel", "parallel")), name="attn_dilated")(qb, kb, vb)


N_NEW = 4
SROWS = 8


def _softmax2_pv(s_c, s_n, vt_c, vt_n, sink=None):
    m = jnp.maximum(jnp.max(s_c, axis=-1, keepdims=True), jnp.max(s_n, axis=-1, keepdims=True))
    if sink is not None:
        m = jnp.maximum(m, sink)
    e_c = jnp.exp(s_c - m)
    e_n = jnp.exp(s_n - m)
    den = jnp.sum(e_c, axis=-1, keepdims=True) + jnp.sum(e_n, axis=-1, keepdims=True)
    if sink is not None:
        den = den + jnp.exp(sink - m)
    r = (lax.dot_general(e_c.astype(jnp.bfloat16), vt_c, _NT, preferred_element_type=jnp.float32)
         + lax.dot_general(e_n.astype(jnp.bfloat16), vt_n, _NT, preferred_element_type=jnp.float32))
    return r / den, m, den


def _advance(old_t, new_t):
    n = old_t.shape[1]
    lane = lax.broadcasted_iota(jnp.int32, (1, LANES), 1)
    shifted = pltpu.roll(old_t, n - N_NEW, axis=1)
    last = jnp.where(lane < LANES - N_NEW, shifted[:, n - LANES:], new_t)
    if n == LANES:
        return last
    return jnp.concatenate([shifted[:, :n - LANES], last], axis=1)


def _attn_sample_body(bt, sink_ref, qa_ref, qb_ref, qc_ref, nka_ref, nva_ref, nkb_ref, nvb_ref,
                      cak_ref, cav_ref, cbk_ref, cbv_ref, cmk_ref, cmv_ref,
                      oac_ref, ob_ref, oak_ref, oav_ref, obk_ref, obv_ref):
    f32, bf16 = jnp.float32, jnp.bfloat16
    lo, hi = _half_masks()
    halves = (lo, hi)
    S = SROWS
    la = cak_ref.shape[2]
    lb = cbk_ref.shape[2]
    wa = SWA_Q_HEADS * HEAD_DIM
    new0 = LANES - N_NEW

    na = SWA_Q_HEADS * S
    ia = lax.broadcasted_iota(jnp.int32, (na, la), 0) & (S - 1)
    valid_ac = lax.broadcasted_iota(jnp.int32, (na, la), 1) >= ia + 1
    ja = lax.broadcasted_iota(jnp.int32, (na, LANES), 1) - new0
    valid_an = (ja >= 0) & (ja <= (lax.broadcasted_iota(jnp.int32, (na, LANES), 0) & (S - 1)))
    rcol = lax.broadcasted_iota(jnp.int32, (na, 1), 0)
    sink_col = jnp.zeros((na, 1), f32)
    for h in range(SWA_Q_HEADS):
        sink_col = jnp.where((rcol >> 3) == h, sink_ref[h], sink_col)

    nb_rows = len(DIL_PAIRS) * 2 * S
    rb = lax.broadcasted_iota(jnp.int32, (nb_rows, lb), 0)
    t_c = lb + (rb & (S - 1)) - lax.broadcasted_iota(jnp.int32, (nb_rows, lb), 1)
    rn = lax.broadcasted_iota(jnp.int32, (nb_rows, LANES), 0)
    jn = lax.broadcasted_iota(jnp.int32, (nb_rows, LANES), 1) - new0
    t_n = (rn & (S - 1)) - jn
    valid_bc = jnp.zeros((nb_rows, lb), jnp.bool_)
    valid_bn = jnp.zeros((nb_rows, LANES), jnp.bool_)
    for g, (window, dil) in enumerate(DIL_PAIRS):
        valid_bc = valid_bc | (((rb >> 4) == g) & (t_c <= window) & ((t_c & (dil - 1)) == 0))
        valid_bn = valid_bn | (((rn >> 4) == g) & (jn >= 0) & (t_n >= 0) & ((t_n & (dil - 1)) == 0))

    def new_t(x):
        padded = jnp.concatenate([x, jnp.zeros((LANES - S, x.shape[1]), f32)], axis=0)
        return pltpu.roll(padded.T, new0, axis=1)

    for b in range(bt):
        rows = slice(b * S, (b + 1) * S)
        nka, nva = new_t(nka_ref[rows, :]), new_t(nva_ref[rows, :])
        nkb, nvb = new_t(nkb_ref[rows, :]), new_t(nvb_ref[rows, :])
        oak_ref[b] = _advance(cak_ref[b], nka)
        oav_ref[b] = _advance(cav_ref[b], nva)
        obk_ref[b] = _advance(cbk_ref[b], nkb)
        obv_ref[b] = _advance(cbv_ref[b], nvb)

        pieces = []
        for h in range(SWA_Q_HEADS):
            q = jnp.where(halves[h % 2], qa_ref[rows, (h // 2) * LANES:(h // 2 + 1) * LANES].astype(f32), 0.0)
            if h % 2 != h // 4:
                q = pltpu.roll(q, HEAD_DIM, axis=1)
            pieces.append(q)
        qm = jnp.concatenate(pieces, axis=0).astype(bf16)
        s_c = jnp.dot(qm, cak_ref[b].astype(bf16), preferred_element_type=f32)
        s_n = jnp.dot(qm, nka.astype(bf16), preferred_element_type=f32)
        out, _, _ = _softmax2_pv(jnp.where(valid_ac, s_c, NEG), jnp.where(valid_an, s_n, NEG),
                                 cav_ref[b].astype(bf16), nva.astype(bf16), sink_col)
        for j in range(SWA_Q_HEADS // 2):
            parts = []
            for p in range(2):
                h = 2 * j + p
                o = out[h * S:(h + 1) * S]
                if h % 2 != h // 4:
                    o = pltpu.roll(o, HEAD_DIM, axis=1)
                parts.append(o)
            oac_ref[rows, j * LANES:(j + 1) * LANES] = jnp.where(lo, parts[0], parts[1])

        for j in range(MEM_HEADS // 2):
            cols = slice(j * LANES, (j + 1) * LANES)
            q = qc_ref[rows, cols]
            qm = jnp.concatenate([jnp.where(halves[p], q, jnp.zeros_like(q)) for p in range(2)], axis=0)
            s = jnp.dot(qm, cmk_ref[b, cols, :].astype(bf16), preferred_element_type=f32)
            m = jnp.max(s, axis=-1, keepdims=True)
            e = jnp.exp(s - m)
            r = lax.dot_general(e.astype(bf16), cmv_ref[b, cols, :].astype(bf16), _NT, preferred_element_type=f32)
            out = r / jnp.sum(e, axis=-1, keepdims=True)
            oac_ref[rows, wa + j * LANES:wa + (j + 1) * LANES] = jnp.where(lo, out[:S], out[S:])

        for hp in range(DIL_KV_HEADS // 2):
            cols = slice(hp * LANES, (hp + 1) * LANES)
            pieces = [jnp.where(halves[p], qb_ref[hp * len(DIL_PAIRS) + g, rows, :], 0.0)
                      for g in range(len(DIL_PAIRS)) for p in range(2)]
            qm = jnp.concatenate(pieces, axis=0).astype(bf16)
            s_c = jnp.dot(qm, cbk_ref[b, cols, :].astype(bf16), preferred_element_type=f32)
            s_n = jnp.dot(qm, nkb[cols, :].astype(bf16), preferred_element_type=f32)
            out, m, den = _softmax2_pv(jnp.where(valid_bc, s_c, NEG), jnp.where(valid_bn, s_n, NEG),
                                       cbv_ref[b, cols, :].astype(bf16), nvb[cols, :].astype(bf16))
            lse = m + jnp.log(den)
            res = []
            for p in range(2):
                r = [slice((g * 2 + p) * S, (g * 2 + p + 1) * S) for g in range(len(DIL_PAIRS))]
                mx = jnp.maximum(jnp.maximum(lse[r[0]], lse[r[1]]), lse[r[2]])
                w = [jnp.exp(lse[x] - mx) for x in r]
                tot = w[0] + w[1] + w[2]
                res.append(sum((w[g] / tot) * out[r[g]] for g in range(len(DIL_PAIRS))))
            ob_ref[rows, cols] = jnp.where(lo, res[0], res[1])


def attn_sample(sinks, qa, qb, qc, nka, nva, nkb, nvb, cak, cav, cbk, cbv, cmk, cmv, bt):
    NB, wka, la = cak.shape
    wkb, lb = cbk.shape[1:]
    wm, M = cmk.shape[1:]
    wa, wc = SWA_Q_HEADS * HEAD_DIM, MEM_HEADS * HEAD_DIM
    ng = len(DIL_PAIRS)
    tok = lambda w: pl.BlockSpec((bt * SROWS, w), lambda i: (i, 0))
    buf = lambda f, n: pl.BlockSpec((bt, f, n), lambda i: (i, 0, 0))
    f32 = jnp.float32
    return pl.pallas_call(
        functools.partial(_attn_sample_body, bt),
        grid=(NB // bt,),
        in_specs=[pl.BlockSpec(memory_space=pltpu.SMEM), tok(wa),
                  pl.BlockSpec((2 * ng, bt * SROWS, LANES), lambda i: (0, i, 0)), tok(wc),
                  tok(wka), tok(wka), tok(wkb), tok(wkb),
                  buf(wka, la), buf(wka, la), buf(wkb, lb), buf(wkb, lb), buf(wm, M), buf(wm, M)],
        out_specs=[tok(wa + wc), tok(wkb), buf(wka, la), buf(wka, la), buf(wkb, lb), buf(wkb, lb)],
        out_shape=[jax.ShapeDtypeStruct((NB * SROWS, wa + wc), f32), jax.ShapeDtypeStruct((NB * SROWS, wkb), f32),
                   jax.ShapeDtypeStruct(cak.shape, f32), jax.ShapeDtypeStruct(cak.shape, f32),
                   jax.ShapeDtypeStruct(cbk.shape, f32), jax.ShapeDtypeStruct(cbk.shape, f32)],
        compiler_params=_cparams(("parallel",)), name="attn_sample")(
            sinks, qa, qb, qc, nka, nva, nkb, nvb, cak, cav, cbk, cbv, cmk, cmv)


MOE_BLOCK = 512
ROUTE_SUB = 256
SUBLANES = 8


def _store_row_tiles(ref, row0, y):
    n = y.shape[0]
    for c in range(SUBLANES):
        ref[pl.ds(row0 * SUBLANES + c, n, stride=SUBLANES), :] = y[:, c * LANES:(c + 1) * LANES]


def _load_row_tiles(ref, row0, n):
    return jnp.concatenate([ref[pl.ds(row0 * SUBLANES + c, n, stride=SUBLANES), :] for c in range(SUBLANES)],
                           axis=1)


def _row_tile(ref, r):
    return ref.at[pl.ds(pl.multiple_of(r * SUBLANES, SUBLANES), SUBLANES)]


def _merge_route_body(oac_ref, ob_ref, gate_ref, x_ref, wa_ref, wb_ref, wc_ref, wo_ref, gffn_ref, wr_ref, br_ref,
                      cnt0_ref, x1_ref, h2_ref, mi_ref, mf_ref, cnt_ref, base_ref):
    f32, bf16 = jnp.float32, jnp.bfloat16
    D = x_ref.shape[1]
    wa = SWA_Q_HEADS * HEAD_DIM
    ts = ROUTE_SUB

    @pl.when(pl.program_id(0) == 0)
    def _():
        base_ref[...] = cnt0_ref[...]

    erow = lax.broadcasted_iota(jnp.int32, (N_EXPERTS, ts), 0)
    r8 = lax.broadcasted_iota(jnp.int32, (SUBLANES, ts), 0)
    ti = lax.broadcasted_iota(jnp.int32, (ts, ts), 0)
    tj = lax.broadcasted_iota(jnp.int32, (ts, ts), 1)
    later = (ti < tj).astype(bf16)
    base = base_ref[:, 0:1]
    for sub in range(x_ref.shape[0] // ts):
        rows = slice(sub * ts, (sub + 1) * ts)
        ma = jnp.dot(oac_ref[rows, :wa].astype(bf16), wa_ref[...], preferred_element_type=f32)
        mb = jnp.dot(ob_ref[rows, :].astype(bf16), wb_ref[...], preferred_element_type=f32)
        mc = jnp.dot(oac_ref[rows, wa:].astype(bf16), wc_ref[...], preferred_element_type=f32)
        merged = (gate_ref[rows, :D].astype(f32) * ma + gate_ref[rows, D:2 * D].astype(f32) * mb
                  + gate_ref[rows, 2 * D:].astype(f32) * mc)
        x1 = x_ref[rows, :] + jnp.dot(merged.astype(bf16), wo_ref[...], preferred_element_type=f32)
        x1_ref[rows, :] = x1
        h2 = _rms(x1, gffn_ref[...])
        _store_row_tiles(h2_ref, sub * ts, h2)

        work = lax.dot_general(wr_ref[...], h2.astype(bf16), _NT, preferred_element_type=f32) + br_ref[:, 0:1]
        vals, idxs = [], []
        for _ in range(TOP_K):
            m = jnp.max(work, axis=0, keepdims=True)
            idx = jnp.min(jnp.where(work == m, erow, N_EXPERTS), axis=0, keepdims=True)
            vals.append(m)
            idxs.append(idx)
            work = jnp.where(erow == idx, -jnp.inf, work)
        es = [jnp.exp(v - vals[0]) for v in vals]
        tot = es[0] + es[1] + es[2] + es[3]

        onehot = [(erow == idx).astype(f32) for idx in idxs]
        assign = onehot[0] + onehot[1] + onehot[2] + onehot[3]
        before = jnp.dot(assign.astype(bf16), later, preferred_element_type=f32) + base
        base = base + jnp.sum(assign, axis=1, keepdims=True)

        mi = jnp.zeros((SUBLANES, ts), jnp.int32)
        gates = jnp.zeros((SUBLANES, ts), f32)
        for k in range(TOP_K):
            rank = jnp.sum(onehot[k] * before, axis=0, keepdims=True).astype(jnp.int32)
            mi = jnp.where(r8 == k, idxs[k], mi)
            mi = jnp.where(r8 == TOP_K + k, rank, mi)
            gates = jnp.where(r8 == k, es[k] / tot, gates)
        mi_ref[:, rows] = mi
        mf_ref[rows, :] = jnp.concatenate([gates, jnp.zeros((LANES - SUBLANES, ts), f32)], axis=0).T
    base_ref[...] = jnp.broadcast_to(base, base_ref.shape)
    cnt_ref[...] = jnp.broadcast_to(base, cnt_ref.shape)


def merge_route(o_ac, o_b, gates, x, wa, wb, wc, wo, g_ffn, wr, br, cnt0, tm):
    R, D = x.shape
    row = lambda w: pl.BlockSpec((tm, w), lambda i: (i, 0))
    full = lambda a: pl.BlockSpec(a.shape, lambda i: (0, 0))
    return pl.pallas_call(
        _merge_route_body,
        grid=(R // tm,),
        in_specs=[row(o_ac.shape[1]), row(o_b.shape[1]), row(gates.shape[1]), row(D),
                  full(wa), full(wb), full(wc), full(wo), full(g_ffn), full(wr), full(br), full(cnt0)],
        out_specs=[row(D), pl.BlockSpec((tm * SUBLANES, LANES), lambda i: (i, 0)),
                   pl.BlockSpec((SUBLANES, tm), lambda i: (0, i)), row(LANES),
                   pl.BlockSpec((N_EXPERTS, LANES), lambda i: (0, 0))],
        out_shape=[jax.ShapeDtypeStruct((R, D), jnp.float32), jax.ShapeDtypeStruct((R * SUBLANES, LANES), jnp.float32),
                   jax.ShapeDtypeStruct((SUBLANES, R), jnp.int32), jax.ShapeDtypeStruct((R, LANES), jnp.float32),
                   jax.ShapeDtypeStruct((N_EXPERTS, LANES), jnp.float32)],
        scratch_shapes=[pltpu.VMEM((N_EXPERTS, LANES), jnp.float32)],
        compiler_params=_cparams(("arbitrary",)), name="merge_route")(
            o_ac, o_b, gates, x, wa, wb, wc, wo, g_ffn, wr, br, cnt0)


def _route_tables_body(cnt_ref, mi_ref, dest_ref, blk_ref, pad_ref):
    tm = mi_ref.shape[1]
    nbl = blk_ref.shape[1]
    erow1 = lax.broadcasted_iota(jnp.int32, (N_EXPERTS, LANES), 0)
    shift = MOE_BLOCK.bit_length() - 1
    cnt = cnt_ref[...].astype(jnp.int32)
    padded = ((cnt + (MOE_BLOCK - 1)) >> shift) << shift
    pend = padded
    s = 1
    while s < N_EXPERTS:
        pend = pend + jnp.where(erow1 >= s, pltpu.roll(pend, s, axis=0), 0)
        s *= 2
    pstart = pend - padded
    mi = mi_ref[...]
    erow = lax.broadcasted_iota(jnp.int32, (N_EXPERTS, tm), 0)
    r8 = lax.broadcasted_iota(jnp.int32, (SUBLANES, tm), 0)
    dest = jnp.zeros((SUBLANES, tm), jnp.int32)
    for k in range(TOP_K):
        start = jnp.sum(jnp.where(erow == mi[k:k + 1, :], pstart[:, 0:1], 0), axis=0, keepdims=True)
        dest = jnp.where(r8 == k, start + mi[TOP_K + k:TOP_K + k + 1, :], dest)
    dest_ref[...] = dest

    @pl.when(pl.program_id(0) == 0)
    def _():
        row0 = lax.broadcasted_iota(jnp.int32, (N_EXPERTS, nbl), 1) * MOE_BLOCK
        ended = jnp.sum(jnp.where(pend[:, 0:1] <= row0, 1, 0), axis=0, keepdims=True)
        blk_ref[...] = jnp.broadcast_to(jnp.minimum(ended, N_EXPERTS - 1), blk_ref.shape)
        pad_ref[0] = pstart + cnt
        pad_ref[1] = pend


def route_tables(cnt, mi, nbl, tm):
    R = mi.shape[1]
    return pl.pallas_call(
        _route_tables_body,
        grid=(R // tm,),
        in_specs=[pl.BlockSpec((N_EXPERTS, LANES), lambda i: (0, 0)), pl.BlockSpec((SUBLANES, tm), lambda i: (0, i))],
        out_specs=[pl.BlockSpec((SUBLANES, tm), lambda i: (0, i)), pl.BlockSpec((SUBLANES, nbl), lambda i: (0, 0)),
                   pl.BlockSpec((2, N_EXPERTS, LANES), lambda i: (0, 0, 0))],
        out_shape=[jax.ShapeDtypeStruct((SUBLANES, R), jnp.int32), jax.ShapeDtypeStruct((SUBLANES, nbl), jnp.int32),
                   jax.ShapeDtypeStruct((2, N_EXPERTS, LANES), jnp.int32)],
        compiler_params=_cparams(("arbitrary",)), name="route_tables")(cnt, mi)


def _dispatch_body(n_first, n_steps, pad0_ref, pad1_ref, dest_ref, ha_ref, hb_ref, xs_ref, zero_ref, sem):
    i = pl.program_id(0)
    tm = ha_ref.shape[0] // SUBLANES
    rs = xs_ref.shape[0] // SUBLANES

    def scatter_tile(h_ref):
        def row_copy(r, k):
            return pltpu.make_async_copy(_row_tile(h_ref, r), _row_tile(xs_ref, dest_ref[0, 0, k * tm + r]), sem)

        def start(r, c):
            for k in range(TOP_K):
                row_copy(r, k).start(priority=k % 2)
            return c

        def wait(r, c):
            for k in range(TOP_K):
                row_copy(r, k).wait()
            return c

        lax.fori_loop(0, tm, start, 0, unroll=8)
        lax.fori_loop(0, tm, wait, 0, unroll=8)

    @pl.when(i < n_first)
    def _():
        scatter_tile(ha_ref)

    @pl.when(i >= n_first)
    def _():
        scatter_tile(hb_ref)

    @pl.when(i == n_steps - 1)
    def _():
        zero_ref[...] = jnp.zeros_like(zero_ref)
        zrows = zero_ref.shape[0] // SUBLANES

        def zero_range(lo, hi, go):
            n_big = (hi - lo) // zrows

            def big(j, c):
                start = pl.multiple_of((lo + j * zrows) * SUBLANES, SUBLANES)
                cp = pltpu.make_async_copy(zero_ref, xs_ref.at[pl.ds(start, zrows * SUBLANES)], sem)
                cp.start() if go else cp.wait()
                return c

            def one(r, c):
                cp = pltpu.make_async_copy(zero_ref.at[pl.ds(0, SUBLANES)], _row_tile(xs_ref, r), sem)
                cp.start() if go else cp.wait()
                return c

            lax.fori_loop(0, n_big, big, 0)
            lax.fori_loop(lo + n_big * zrows, hi, one, 0)

        for go in (True, False):
            for e in range(N_EXPERTS):
                zero_range(pad0_ref[e], pad1_ref[e], go)
            zero_range(pad1_ref[N_EXPERTS - 1], rs, go)


def dispatch(pad0, pad1, dest3, h_a, h_b, rs, tm):
    S = SUBLANES
    n_a, n_b = h_a.shape[0] // (tm * S), h_b.shape[0] // (tm * S)
    grid_spec = pltpu.PrefetchScalarGridSpec(
        num_scalar_prefetch=2,
        grid=(n_a + n_b,),
        in_specs=[pl.BlockSpec((1, 1, TOP_K * tm), lambda i, p0, p1: (i, 0, 0), memory_space=pltpu.SMEM),
                  pl.BlockSpec((tm * S, LANES), lambda i, p0, p1: (jnp.minimum(i, n_a - 1), 0)),
                  pl.BlockSpec((tm * S, LANES), lambda i, p0, p1: (jnp.maximum(i - n_a, 0), 0))],
        out_specs=pl.BlockSpec(memory_space=pl.ANY),
        scratch_shapes=[pltpu.VMEM((16 * S, LANES), h_a.dtype), pltpu.SemaphoreType.DMA])
    return pl.pallas_call(
        functools.partial(_dispatch_body, n_a, n_a + n_b), grid_spec=grid_spec,
        out_shape=jax.ShapeDtypeStruct((rs * S, LANES), h_a.dtype),
        compiler_params=_cparams(("arbitrary",)), name="dispatch")(pad0, pad1, dest3, h_a, h_b)


def _moe_body(nblk, be_ref, nact_ref, x_ref, wgu_hbm, bgu_ref, wd_hbm, bd_ref, y_ref, wgu_f, wd_f, wgu_s, wd_s,
              sems):
    f32, bf16 = jnp.float32, jnp.bfloat16
    b = pl.program_id(0)
    e = be_ref[b]

    def fetch(ex, go):
        for src, dst, s in ((wgu_hbm, wgu_f, 0), (wd_hbm, wd_f, 1)):
            cp = pltpu.make_async_copy(src.at[ex], dst, sems.at[s])
            cp.start() if go else cp.wait()

    @pl.when(b == 0)
    def _():
        fetch(e, True)

    @pl.when((b == 0) | (e != be_ref[jnp.maximum(b - 1, 0)]))
    def _():
        fetch(e, False)
        wgu_s[...] = wgu_f[...].astype(bf16)
        wd_s[...] = wd_f[...].astype(bf16)
        nxt = lax.while_loop(lambda j: (j < nblk) & (be_ref[jnp.minimum(j, nblk - 1)] == e), lambda j: j + 1, b + 1)

        @pl.when(nxt < nblk)
        def _():
            fetch(be_ref[jnp.minimum(nxt, nblk - 1)], True)

    @pl.when(b < nact_ref[0])
    def _():
        x = _load_row_tiles(x_ref, 0, MOE_BLOCK).astype(bf16)
        gu = jnp.dot(x, wgu_s[...], preferred_element_type=f32) + bgu_ref[0]
        gt = jnp.minimum(gu[:, :D_FF], SWIGLU_LIMIT)
        up = jnp.clip(gu[:, D_FF:], -SWIGLU_LIMIT, SWIGLU_LIMIT)
        act = (up + 1.0) * (gt * jax.nn.sigmoid(gt * SWIGLU_ALPHA))
        y = jnp.dot(act.astype(bf16), wd_s[...], preferred_element_type=f32) + bd_ref[0]
        _store_row_tiles(y_ref, 0, y)

    @pl.when(b >= nact_ref[0])
    def _():
        y_ref[...] = jnp.zeros_like(y_ref)


def moe_ffn(blk_e, n_active, xs, w_gate_up, b_gate_up, w_down, b_down):
    RS = xs.shape[0] // SUBLANES
    E, D, F2 = w_gate_up.shape
    blk_rows = MOE_BLOCK * SUBLANES
    grid_spec = pltpu.PrefetchScalarGridSpec(
        num_scalar_prefetch=2,
        grid=(RS // MOE_BLOCK,),
        in_specs=[pl.BlockSpec((blk_rows, LANES), lambda b, be, na: (b, 0)),
                  pl.BlockSpec(memory_space=pl.ANY),
                  pl.BlockSpec((1, 1, F2), lambda b, be, na: (be[b], 0, 0)),
                  pl.BlockSpec(memory_space=pl.ANY),
                  pl.BlockSpec((1, 1, D), lambda b, be, na: (be[b], 0, 0))],
        out_specs=pl.BlockSpec((blk_rows, LANES), lambda b, be, na: (b, 0)),
        scratch_shapes=[pltpu.VMEM((D, F2), w_gate_up.dtype), pltpu.VMEM((F2 // 2, D), w_down.dtype),
                        pltpu.VMEM((D, F2), jnp.bfloat16), pltpu.VMEM((F2 // 2, D), jnp.bfloat16),
                        pltpu.SemaphoreType.DMA((2,))])
    return pl.pallas_call(
        functools.partial(_moe_body, RS // MOE_BLOCK), grid_spec=grid_spec,
        out_shape=jax.ShapeDtypeStruct(xs.shape, jnp.float32),
        compiler_params=_cparams(("arbitrary",)), name="moe_ffn")(
            blk_e, n_active, xs, w_gate_up, b_gate_up.reshape(E, 1, F2), w_down, b_down.reshape(E, 1, D))


def _combine_body(n_tiles, dest_ref, dest_next_ref, x1_ref, mf_ref, g_ref, ys_ref, o_ref, buf, sems):
    tm = x1_ref.shape[0]
    i = pl.program_id(0)
    slot = i % 2
    per_slot = TOP_K * tm

    def gather(d_ref, s, go):
        def row_copy(r, k):
            return pltpu.make_async_copy(_row_tile(ys_ref, d_ref[0, 0, k * tm + r]),
                                         _row_tile(buf, s * per_slot + k * tm + r), sems.at[s])

        def body(r, c):
            for k in range(TOP_K):
                if go:
                    row_copy(r, k).start(priority=k % 2)
                else:
                    row_copy(r, k).wait()
            return c

        lax.fori_loop(0, tm, body, 0, unroll=8)

    @pl.when(i == 0)
    def _():
        gather(dest_ref, slot, True)

    @pl.when(i + 1 < n_tiles)
    def _():
        gather(dest_next_ref, 1 - slot, True)

    gather(dest_ref, slot, False)
    y = x1_ref[...]
    for k in range(TOP_K):
        y = y + mf_ref[:, k:k + 1] * _load_row_tiles(buf, slot * per_slot + k * tm, tm)
    o_ref[...] = _rms(y, g_ref[...])


def combine(dest3, x1, mf, g_final, ys, tm):
    R, D = x1.shape
    n = R // tm
    dest_spec = lambda f: pl.BlockSpec((1, 1, TOP_K * tm), f, memory_space=pltpu.SMEM)
    return pl.pallas_call(
        functools.partial(_combine_body, n),
        grid=(n,),
        in_specs=[dest_spec(lambda i: (i, 0, 0)), dest_spec(lambda i: (jnp.minimum(i + 1, n - 1), 0, 0)),
                  pl.BlockSpec((tm, D), lambda i: (i, 0)),
                  pl.BlockSpec((tm, LANES), lambda i: (i, 0)),
                  pl.BlockSpec((1, D), lambda i: (0, 0)),
                  pl.BlockSpec(memory_space=pl.ANY)],
        out_specs=pl.BlockSpec((tm, D), lambda i: (i, 0)),
        out_shape=jax.ShapeDtypeStruct((R, D), jnp.float32),
        scratch_shapes=[pltpu.VMEM((2 * TOP_K * tm * SUBLANES, LANES), ys.dtype), pltpu.SemaphoreType.DMA((2,))],
        compiler_params=_cparams(("arbitrary",)), name="combine")(dest3, dest3, x1, mf, g_final, ys)


def moe_layer(group_a, group_b, cnt, g_final, w_gate_up, b_gate_up, w_down, b_down, tm):
    n_assign = (group_a[0].shape[0] + group_b[0].shape[0]) * TOP_K
    nb = (n_assign + N_EXPERTS * (MOE_BLOCK - 1)) // MOE_BLOCK + 1
    nbl = -(-nb // LANES) * LANES
    dests = []
    for x1, h2, mi, mf in (group_a, group_b):
        dest, blk, pad = route_tables(cnt, mi, nbl, min(x1.shape[0], 4 * tm))
        n_tiles = x1.shape[0] // tm
        dests.append(jnp.transpose(dest[:TOP_K].reshape(TOP_K, n_tiles, tm), (1, 0, 2)).reshape(n_tiles, 1, TOP_K * tm))
    xs = dispatch(pad[0, :, 0], pad[1, :, 0], jnp.concatenate(dests), group_a[1], group_b[1], nb * MOE_BLOCK, tm)
    n_active = (pad[1, N_EXPERTS - 1, 0] // MOE_BLOCK).reshape(1)
    ys = moe_ffn(blk[0, :nb], n_active, xs, w_gate_up, b_gate_up, w_down, b_down)
    return [combine(dest3, x1, mf, g_final, ys, tm) for dest3, (x1, h2, mi, mf) in zip(dests, (group_a, group_b))]


def kernel(x_prompt, x_sample, cache_swa_k, cache_swa_v, cache_dil_k, cache_dil_v, cache_mem_k, cache_mem_v, mem_prompt, norm_attn, norm_mem, w_in, w_mem_kv, sinks, w_br_a, w_br_b, w_br_c, w_out, norm_ffn, w_router, b_router, w_gate_up, b_gate_up, w_down, b_down, norm_final):
    f32, bf16 = jnp.float32, jnp.bfloat16
    TM = 256
    B, L, D = x_prompt.shape
    NB, n_new, _ = x_sample.shape
    M = mem_prompt.shape[1]
    la, lb = cache_swa_k.shape[2], cache_dil_k.shape[2]
    wka, wkb, wc = SWA_KV_HEADS * HEAD_DIM, DIL_KV_HEADS * HEAD_DIM, MEM_HEADS * HEAD_DIM
    assert n_new == N_NEW and cache_swa_k.shape[0] == 1

    w_in_b = w_in[0].astype(bf16)
    g_attn = norm_attn[0].reshape(1, D)
    secs = in_sections()

    tabs_p = rope_tables(jnp.arange(L, dtype=jnp.int32))
    qa, ka, va, qb, kb, vb, qc, gates = norm_proj(x_prompt.reshape(B * L, D), g_attn, w_in_b, tabs_p, secs, TM)
    mk, mv = norm_proj(mem_prompt.reshape(B * M, D), norm_mem[0].reshape(1, D), w_mem_kv[0].astype(bf16), None,
                       [(_chunks(0, wc), "plain", False, f32), (_chunks(wc, wc), "plain", False, f32)], TM)
    o_ac, ka_t, va_t, mk_t, mv_t = attn_swa_mem(sinks[0], qa, ka, va, qc, mk, mv, B, L)
    o_b, kb_t, vb_t = attn_dilated(qb, kb, vb, B, L)

    xs_pad = jnp.pad(x_sample, ((0, 0), (0, SROWS - N_NEW), (0, 0))).reshape(NB * SROWS, D)
    tabs_s = rope_tables(PAST_LEN + (jnp.arange(TM, dtype=jnp.int32) % SROWS))
    qa_s, ka_s, va_s, qb_s, kb_s, vb_s, qc_s, gates_s = norm_proj(xs_pad, g_attn, w_in_b, tabs_s, secs, TM)
    real = lambda t: t.reshape(NB, SROWS, -1)[:, :N_NEW].reshape(NB * N_NEW, -1)
    fmaj = lambda c: jnp.transpose(c[0], (0, 2, 3, 1)).reshape(NB, -1, c.shape[2])
    o_ac_s, o_b_s, swa_k_s, swa_v_s, dil_k_s, dil_v_s = attn_sample(
        sinks[0], qa_s, qb_s, qc_s, ka_s, va_s, kb_s, vb_s,
        fmaj(cache_swa_k), fmaj(cache_swa_v), fmaj(cache_dil_k), fmaj(cache_dil_v),
        fmaj(cache_mem_k), fmaj(cache_mem_v), 2)

    wr = w_router[0].T.astype(bf16)
    br = jnp.broadcast_to(b_router[0].astype(f32)[:, None], (N_EXPERTS, LANES))
    wts = (w_br_a[0].astype(bf16), w_br_b[0].astype(bf16), w_br_c[0].astype(bf16), w_out[0].astype(bf16),
           norm_ffn[0].reshape(1, D), wr, br)
    x1_p, h2_p, mi_p, mf_p, cnt_p = merge_route(o_ac, o_b, gates, x_prompt.reshape(B * L, D), *wts,
                                                jnp.zeros((N_EXPERTS, LANES), f32), 2 * ROUTE_SUB)
    x1_s, h2_s, mi_s, mf_s, cnt = merge_route(real(o_ac_s), real(o_b_s), real(gates_s),
                                              x_sample.reshape(NB * N_NEW, D), *wts, cnt_p, 2 * ROUTE_SUB)
    y_p, y_s = moe_layer((x1_p, h2_p, mi_p, mf_p), (x1_s, h2_s, mi_s, mf_s), cnt, norm_final.reshape(1, D),
                         w_gate_up[0], b_gate_up[0], w_down[0], b_down[0], TM)

    tmaj = lambda t, h: jnp.transpose(t.reshape(t.shape[0], h, HEAD_DIM, t.shape[2]), (0, 3, 1, 2))[None]
    return (y_p.reshape(B, L, D), y_s.reshape(NB, N_NEW, D),
            tmaj(ka_t, SWA_KV_HEADS), tmaj(va_t, SWA_KV_HEADS),
            tmaj(kb_t, DIL_KV_HEADS), tmaj(vb_t, DIL_KV_HEADS),
            tmaj(mk_t, MEM_HEADS), tmaj(mv_t, MEM_HEADS),
            tmaj(swa_k_s, SWA_KV_HEADS), tmaj(swa_v_s, SWA_KV_HEADS),
            tmaj(dil_k_s, DIL_KV_HEADS), tmaj(dil_v_s, DIL_KV_HEADS))
```

```python
import functools

import jax
import jax.numpy as jnp
import numpy as np
from jax import lax
from jax.experimental import pallas as pl
from jax.experimental.pallas import tpu as pltpu

D_MODEL = 1024
HEAD_DIM = 64
ROPE_DIM = 16
ROPE_HALF = 8
ROPE_THETA = 500000.0
PAST_LEN = 16384
SWA_Q_HEADS = 8
SWA_KV_HEADS = 2
SWA_WINDOW = 128
DIL_PAIRS = ((128, 1), (512, 4), (2048, 16))
DIL_KV_HEADS = 4
MEM_HEADS = 4
N_EXPERTS = 32
TOP_K = 4
D_FF = 1024
SWIGLU_LIMIT = 7.0
SWIGLU_ALPHA = 1.702
RMS_EPS = 1e-5
ATT_BLOCK = 128
SCALE = HEAD_DIM ** -0.5

LANES = 128
NEG = -1e30
VMEM_LIMIT = 56 * 1024 * 1024


def _cparams(sem):
    return pltpu.CompilerParams(dimension_semantics=sem, vmem_limit_bytes=VMEM_LIMIT)


def _rms(x, g):
    return x * lax.rsqrt(jnp.mean(x * x, axis=-1, keepdims=True) + RMS_EPS) * g


def _norm_proj_body(sections, x_ref, g_ref, w_ref, cs_ref, *out_refs):
    h = _rms(x_ref[...], g_ref[...]).astype(jnp.bfloat16)
    if cs_ref is not None:
        cos = cs_ref[0]
        sin_lo = cs_ref[1]
        sin_hi = cs_ref[2]
    dest = {lo: (o_ref, c, kind, slabs)
            for (cols, kind, slabs), o_ref in zip(sections, out_refs) for c, lo in enumerate(cols)}
    todo = sorted(dest)
    while todo:
        lo = todo.pop(0)
        n = 2 if todo and todo[0] == lo + LANES else 1
        if n == 2:
            todo.pop(0)
        yy = jnp.dot(h, w_ref[:, lo:lo + n * LANES], preferred_element_type=jnp.float32)
        for part in range(n):
            o_ref, c, kind, slabs = dest[lo + part * LANES]
            y = yy[:, part * LANES:(part + 1) * LANES]
            if kind in ("rope", "rope_q"):
                y = (y * cos + pltpu.roll(y, LANES - ROPE_HALF, axis=1) * sin_lo
                     + pltpu.roll(y, ROPE_HALF, axis=1) * sin_hi)
            if kind in ("rope_q", "q"):
                y = y * SCALE
            if kind == "sigmoid":
                y = jax.nn.sigmoid(y)
            if slabs:
                o_ref[c] = y.astype(o_ref.dtype)
            else:
                o_ref[:, c * LANES:(c + 1) * LANES] = y.astype(o_ref.dtype)


def rope_tables(pos):
    inv_freq = ROPE_THETA ** (-jnp.arange(ROPE_HALF, dtype=jnp.float32) / ROPE_HALF)
    ang = pos.astype(jnp.float32)[:, None] * inv_freq[None, :]
    cos, sin = jnp.cos(ang), jnp.sin(ang)
    n = pos.shape[0]
    one = jnp.ones((n, HEAD_DIM - ROPE_DIM), jnp.float32)
    zero = jnp.zeros((n, HEAD_DIM - ROPE_HALF), jnp.float32)
    c = jnp.concatenate([cos, cos, one], axis=1)
    s_lo = jnp.concatenate([-sin, zero], axis=1)
    s_hi = jnp.concatenate([jnp.zeros((n, ROPE_HALF), jnp.float32), sin,
                            jnp.zeros((n, HEAD_DIM - ROPE_DIM), jnp.float32)], axis=1)
    tab = jnp.stack([c, s_lo, s_hi])
    return jnp.concatenate([tab, tab], axis=2)


def norm_proj(x, g, w, tables, sections, tm):
    R, D = x.shape
    in_specs = [pl.BlockSpec((tm, D), lambda i: (i, 0)),
                pl.BlockSpec((1, D), lambda i: (0, 0)),
                pl.BlockSpec(w.shape, lambda i: (0, 0))]
    args = [x, g, w]
    if tables is not None:
        nt = tables.shape[1] // tm
        in_specs.append(pl.BlockSpec((3, tm, LANES), lambda i: (0, i % nt, 0)))
        args.append(tables)
    out_shape, out_specs, secs = [], [], []
    for (cols, kind, slabs, dtype) in sections:
        secs.append((cols, kind, slabs))
        width = LANES * len(cols)
        if slabs:
            out_shape.append(jax.ShapeDtypeStruct((width // LANES, R, LANES), dtype))
            out_specs.append(pl.BlockSpec((width // LANES, tm, LANES), lambda i: (0, i, 0)))
        else:
            out_shape.append(jax.ShapeDtypeStruct((R, width), dtype))
            out_specs.append(pl.BlockSpec((tm, width), lambda i: (i, 0)))
    if tables is None:
        body = lambda x_ref, g_ref, w_ref, *o: _norm_proj_body(secs, x_ref, g_ref, w_ref, None, *o)
    else:
        body = functools.partial(_norm_proj_body, secs)
    return pl.pallas_call(
        body, grid=(R // tm,), in_specs=in_specs, out_specs=out_specs, out_shape=out_shape,
        compiler_params=_cparams(("parallel",)), name="norm_proj")(*args)


def _chunks(start, width):
    return tuple(range(start, start + width, LANES))


def in_sections():
    f32, bf16 = jnp.float32, jnp.bfloat16
    qb0 = 1024 - 256
    qb_cols = tuple(qb0 + HEAD_DIM * (4 * g + 2 * hp) for hp in range(2) for g in range(3))
    return [
        (_chunks(0, 512), "rope_q", False, bf16),
        (_chunks(512, 128), "rope", False, f32),
        (_chunks(640, 128), "plain", False, f32),
        (qb_cols, "rope_q", True, f32),
        (_chunks(1536, 256), "rope", False, f32),
        (_chunks(1792, 256), "plain", False, f32),
        (_chunks(2048, 256), "q", False, bf16),
        (_chunks(2304, 3072), "sigmoid", False, bf16),
    ]


_NT = (((1,), (1,)), ((), ()))


def _half_masks():
    lane = lax.broadcasted_iota(jnp.int32, (1, LANES), 1)
    return lane < HEAD_DIM, lane >= HEAD_DIM


def _softmax_pv(s, v_half, sink=None):
    m = jnp.max(s, axis=-1, keepdims=True)
    if sink is not None:
        m = jnp.maximum(m, sink)
    e = jnp.exp(s - m)
    den = jnp.sum(e, axis=-1, keepdims=True)
    if sink is not None:
        den = den + jnp.exp(sink - m)
    r = jnp.dot(e.astype(jnp.bfloat16), v_half, preferred_element_type=jnp.float32)
    return r / den, m, den


def _attn_swa_mem_body(L, sink_ref, qa_ref, ka_ref, va_ref, qc_ref, mk_ref, mv_ref, o_ref,
                       kat_ref, vat_ref, mkt_ref, mvt_ref, band_ref):
    bf16 = jnp.bfloat16
    lo, hi = _half_masks()
    halves = (lo, hi)
    T = ATT_BLOCK
    kat_ref[...] = ka_ref[L - SWA_WINDOW:, :].T
    vat_ref[...] = va_ref[L - SWA_WINDOW:, :].T
    mkt_ref[...] = mk_ref[...].T
    mvt_ref[...] = mv_ref[...].T
    G = SWA_Q_HEADS // SWA_KV_HEADS
    mem_k = [mk_ref[:, j * LANES:(j + 1) * LANES].astype(bf16) for j in range(2)]
    mem_v = [mv_ref[:, j * LANES:(j + 1) * LANES].astype(bf16) for j in range(2)]
    qi = lax.broadcasted_iota(jnp.int32, (2 * T, 2 * T), 0) & (T - 1)
    kj = lax.broadcasted_iota(jnp.int32, (2 * T, 2 * T), 1)
    hrow = lax.broadcasted_iota(jnp.int32, (2 * T, 1), 0) >> (T.bit_length() - 1)
    for case, off in enumerate((0, T)):
        dist = qi - kj + off
        band_ref[case] = jnp.where((dist >= 0) & (dist <= SWA_WINDOW - 1), 1.0, 0.0)

    def heads_of(pair):
        return jnp.concatenate([jnp.where(halves[p], pair, jnp.zeros_like(pair)) for p in range(2)], axis=0)

    def block(blk, carry):
        r0 = pl.multiple_of(blk * T, T)
        ws = pl.multiple_of(jnp.maximum(r0 - T, 0), T)
        valid = band_ref[jnp.minimum(blk, 1)] > 0.5
        k = ka_ref[pl.ds(ws, 2 * T), :]
        v = va_ref[pl.ds(ws, 2 * T), :]
        for kv in range(SWA_KV_HEADS):
            k1 = jnp.where(halves[kv], k, 0.0)
            v1 = jnp.where(halves[kv], v, 0.0)
            k_dup = (k1 + pltpu.roll(k1, HEAD_DIM, axis=1)).astype(bf16)
            v_dup = (v1 + pltpu.roll(v1, HEAD_DIM, axis=1)).astype(bf16)
            for j in range(kv * G // 2, (kv + 1) * G // 2):
                qm = heads_of(qa_ref[pl.ds(r0, T), j * LANES:(j + 1) * LANES])
                sink = jnp.where(hrow == 0, sink_ref[2 * j], sink_ref[2 * j + 1])
                s = lax.dot_general(qm, k_dup, _NT, preferred_element_type=jnp.float32)
                out, _, _ = _softmax_pv(jnp.where(valid, s, NEG), v_dup, sink)
                o_ref[pl.ds(r0, T), j * LANES:(j + 1) * LANES] = jnp.where(lo, out[:T], out[T:]).astype(o_ref.dtype)
        for j in range(MEM_HEADS // 2):
            qm = heads_of(qc_ref[pl.ds(r0, T), j * LANES:(j + 1) * LANES])
            s = lax.dot_general(qm, mem_k[j], _NT, preferred_element_type=jnp.float32)
            out, _, _ = _softmax_pv(s, mem_v[j])
            c0 = SWA_Q_HEADS * HEAD_DIM + j * LANES
            o_ref[pl.ds(r0, T), c0:c0 + LANES] = jnp.where(lo, out[:T], out[T:]).astype(o_ref.dtype)
        return carry

    lax.fori_loop(0, L // T, block, 0)


def attn_swa_mem(sinks, qa, ka, va, qc, mk, mv, B, L):
    M = mk.shape[0] // B
    wa, wc = SWA_Q_HEADS * HEAD_DIM, MEM_HEADS * HEAD_DIM
    wka = ka.shape[1]
    n_win = min(SWA_WINDOW, L)
    row = lambda w: pl.BlockSpec((L, w), lambda b: (b, 0))
    fmaj = lambda f, n: pl.BlockSpec((None, f, n), lambda b: (b, 0, 0))
    f32 = jnp.float32
    return pl.pallas_call(
        functools.partial(_attn_swa_mem_body, L),
        grid=(B,),
        in_specs=[pl.BlockSpec(memory_space=pltpu.SMEM), row(wa), row(LANES), row(LANES), row(wc),
                  pl.BlockSpec((M, wc), lambda b: (b, 0)), pl.BlockSpec((M, wc), lambda b: (b, 0))],
        out_specs=[row(wa + wc), fmaj(wka, n_win), fmaj(wka, n_win), fmaj(wc, M), fmaj(wc, M)],
        out_shape=[jax.ShapeDtypeStruct((B * L, wa + wc), jnp.bfloat16),
                   jax.ShapeDtypeStruct((B, wka, n_win), f32), jax.ShapeDtypeStruct((B, wka, n_win), f32),
                   jax.ShapeDtypeStruct((B, wc, M), f32), jax.ShapeDtypeStruct((B, wc, M), f32)],
        scratch_shapes=[pltpu.VMEM((2, 2 * ATT_BLOCK, 2 * ATT_BLOCK), f32)],
        compiler_params=_cparams(("parallel",)), name="attn_swa_mem")(sinks, qa, ka, va, qc, mk, mv)


def _attn_dil_body(L, qb_ref, kb_ref, vb_ref, o_ref, kt_ref, vt_ref, og_ref, lse_ref, band_ref):
    bf16 = jnp.bfloat16
    lo, hi = _half_masks()
    halves = (lo, hi)
    T = ATT_BLOCK
    kt_ref[...] = kb_ref[...].T
    vt_ref[...] = vb_ref[...].T
    for g, (window, dil) in enumerate(DIL_PAIRS):
        lc = L // dil
        nbc = lc // T
        W = min(2 * T, lc)
        max_dist = window // dil
        qi = lax.broadcasted_iota(jnp.int32, (2 * T, W), 0) & (T - 1)
        kj = lax.broadcasted_iota(jnp.int32, (2 * T, W), 1)
        for case, off in enumerate((0, T)):
            dist = qi - kj + off
            band_ref[g, case, :, 0:W] = jnp.where((dist >= 0) & (dist <= max_dist), 1.0, 0.0)

        def unit(u, carry, g=g, dil=dil, nbc=nbc, W=W):
            c = u >> (nbc.bit_length() - 1)
            n = u & (nbc - 1)
            wsc = jnp.maximum(n * T - T, 0) if W == 2 * T else 0
            q0 = c + dil * T * n
            k0 = c + dil * wsc
            valid = band_ref[g, jnp.minimum(n, 1), :, 0:W] > 0.5
            q = qb_ref[g, pl.ds(q0, T, stride=dil), :]
            k = kb_ref[pl.ds(k0, W, stride=dil), :]
            v = vb_ref[pl.ds(k0, W, stride=dil), :]
            qm = jnp.concatenate([jnp.where(halves[p], q, 0.0) for p in range(2)], axis=0).astype(bf16)
            s = lax.dot_general(qm, k.astype(bf16), _NT, preferred_element_type=jnp.float32)
            out, m, den = _softmax_pv(jnp.where(valid, s, NEG), v.astype(bf16))
            lse = m + jnp.log(den)
            og_ref[g, pl.ds(q0, T, stride=dil), :] = jnp.where(lo, out[:T], out[T:])
            lse_ref[g, pl.ds(q0, T, stride=dil), :] = jnp.where(lo, lse[:T], lse[T:])
            return carry

        lax.fori_loop(0, dil * nbc, unit, 0, unroll=8)

    def merge(i, carry):
        r0 = pl.multiple_of(i * T, T)
        ls = [lse_ref[g, pl.ds(r0, T), :] for g in range(len(DIL_PAIRS))]
        m = jnp.maximum(jnp.maximum(ls[0], ls[1]), ls[2])
        ws = [jnp.exp(l - m) for l in ls]
        tot = ws[0] + ws[1] + ws[2]
        out = sum((w / tot) * og_ref[g, pl.ds(r0, T), :] for g, w in enumerate(ws))
        o_ref[pl.ds(r0, T), :] = out.astype(o_ref.dtype)
        return carry

    lax.fori_loop(0, L // T, merge, 0)


def attn_dilated(qb, kb, vb, B, L):
    ng = len(DIL_PAIRS)
    fmaj = pl.BlockSpec((None, LANES, L), lambda b, hp: (b, hp, 0))
    return pl.pallas_call(
        functools.partial(_attn_dil_body, L),
        grid=(B, 2),
        in_specs=[pl.BlockSpec((ng, L, LANES), lambda b, hp: (hp, b, 0)),
                  pl.BlockSpec((L, LANES), lambda b, hp: (b, hp)),
                  pl.BlockSpec((L, LANES), lambda b, hp: (b, hp))],
        out_specs=[pl.BlockSpec((L, LANES), lambda b, hp: (b, hp)), fmaj, fmaj],
        out_shape=[jax.ShapeDtypeStruct((B * L, 2 * LANES), jnp.bfloat16),
                   jax.ShapeDtypeStruct((B, 2 * LANES, L), jnp.float32),
                   jax.ShapeDtypeStruct((B, 2 * LANES, L), jnp.float32)],
        scratch_shapes=[pltpu.VMEM((ng, L, LANES), jnp.float32), pltpu.VMEM((ng, L, LANES), jnp.float32),
                        pltpu.VMEM((ng, 2, 2 * ATT_BLOCK, 2 * ATT_BLOCK), jnp.float32)],
        compiler_params=_cparams(("parallel", "parallel")), name="attn_dilated")(qb, kb, vb)


N_NEW = 4
SROWS = 8


def _softmax2_pv(s_c, s_n, vt_c, vt_n, sink=None):
    m = jnp.maximum(jnp.max(s_c, axis=-1, keepdims=True), jnp.max(s_n, axis=-1, keepdims=True))
    if sink is not None:
        m = jnp.maximum(m, sink)
    e_c = jnp.exp(s_c - m)
    e_n = jnp.exp(s_n - m)
    den = jnp.sum(e_c, axis=-1, keepdims=True) + jnp.sum(e_n, axis=-1, keepdims=True)
    if sink is not None:
        den = den + jnp.exp(sink - m)
    r = (lax.dot_general(e_c.astype(jnp.bfloat16), vt_c, _NT, preferred_element_type=jnp.float32)
         + lax.dot_general(e_n.astype(jnp.bfloat16), vt_n, _NT, preferred_element_type=jnp.float32))
    return r / den, m, den


def _advance(old_t, new_t):
    n = old_t.shape[1]
    lane = lax.broadcasted_iota(jnp.int32, (1, LANES), 1)
    shifted = pltpu.roll(old_t, n - N_NEW, axis=1)
    last = jnp.where(lane < LANES - N_NEW, shifted[:, n - LANES:], new_t)
    if n == LANES:
        return last
    return jnp.concatenate([shifted[:, :n - LANES], last], axis=1)


def _attn_sample_body(bt, sink_ref, qa_ref, qb_ref, qc_ref, nka_ref, nva_ref, nkb_ref, nvb_ref,
                      cak_ref, cav_ref, cbk_ref, cbv_ref, cmk_ref, cmv_ref,
                      oac_ref, ob_ref, oak_ref, oav_ref, obk_ref, obv_ref):
    f32, bf16 = jnp.float32, jnp.bfloat16
    lo, hi = _half_masks()
    halves = (lo, hi)
    S = SROWS
    la = cak_ref.shape[2]
    lb = cbk_ref.shape[2]
    wa = SWA_Q_HEADS * HEAD_DIM
    new0 = LANES - N_NEW

    na = SWA_Q_HEADS * S
    ia = lax.broadcasted_iota(jnp.int32, (na, la), 0) & (S - 1)
    valid_ac = lax.broadcasted_iota(jnp.int32, (na, la), 1) >= ia + 1
    ja = lax.broadcasted_iota(jnp.int32, (na, LANES), 1) - new0
    valid_an = (ja >= 0) & (ja <= (lax.broadcasted_iota(jnp.int32, (na, LANES), 0) & (S - 1)))
    rcol = lax.broadcasted_iota(jnp.int32, (na, 1), 0)
    sink_col = jnp.zeros((na, 1), f32)
    for h in range(SWA_Q_HEADS):
        sink_col = jnp.where((rcol >> 3) == h, sink_ref[h], sink_col)

    nb_rows = len(DIL_PAIRS) * 2 * S
    rb = lax.broadcasted_iota(jnp.int32, (nb_rows, lb), 0)
    t_c = lb + (rb & (S - 1)) - lax.broadcasted_iota(jnp.int32, (nb_rows, lb), 1)
    rn = lax.broadcasted_iota(jnp.int32, (nb_rows, LANES), 0)
    jn = lax.broadcasted_iota(jnp.int32, (nb_rows, LANES), 1) - new0
    t_n = (rn & (S - 1)) - jn
    valid_bc = jnp.zeros((nb_rows, lb), jnp.bool_)
    valid_bn = jnp.zeros((nb_rows, LANES), jnp.bool_)
    for g, (window, dil) in enumerate(DIL_PAIRS):
        valid_bc = valid_bc | (((rb >> 4) == g) & (t_c <= window) & ((t_c & (dil - 1)) == 0))
        valid_bn = valid_bn | (((rn >> 4) == g) & (jn >= 0) & (t_n >= 0) & ((t_n & (dil - 1)) == 0))

    def new_t(x):
        padded = jnp.concatenate([x, jnp.zeros((LANES - S, x.shape[1]), f32)], axis=0)
        return pltpu.roll(padded.T, new0, axis=1)

    for b in range(bt):
        rows = slice(b * S, (b + 1) * S)
        nka, nva = new_t(nka_ref[rows, :]), new_t(nva_ref[rows, :])
        nkb, nvb = new_t(nkb_ref[rows, :]), new_t(nvb_ref[rows, :])
        oak_ref[b] = _advance(cak_ref[b], nka)
        oav_ref[b] = _advance(cav_ref[b], nva)
        obk_ref[b] = _advance(cbk_ref[b], nkb)
        obv_ref[b] = _advance(cbv_ref[b], nvb)

        pieces = []
        for h in range(SWA_Q_HEADS):
            q = jnp.where(halves[h % 2], qa_ref[rows, (h // 2) * LANES:(h // 2 + 1) * LANES].astype(f32), 0.0)
            if h % 2 != h // 4:
                q = pltpu.roll(q, HEAD_DIM, axis=1)
            pieces.append(q)
        qm = jnp.concatenate(pieces, axis=0).astype(bf16)
        s_c = jnp.dot(qm, cak_ref[b].astype(bf16), preferred_element_type=f32)
        s_n = jnp.dot(qm, nka.astype(bf16), preferred_element_type=f32)
        out, _, _ = _softmax2_pv(jnp.where(valid_ac, s_c, NEG), jnp.where(valid_an, s_n, NEG),
                                 cav_ref[b].astype(bf16), nva.astype(bf16), sink_col)
        for j in range(SWA_Q_HEADS // 2):
            parts = []
            for p in range(2):
                h = 2 * j + p
                o = out[h * S:(h + 1) * S]
                if h % 2 != h // 4:
                    o = pltpu.roll(o, HEAD_DIM, axis=1)
                parts.append(o)
            oac_ref[rows, j * LANES:(j + 1) * LANES] = jnp.where(lo, parts[0], parts[1])

        for j in range(MEM_HEADS // 2):
            cols = slice(j * LANES, (j + 1) * LANES)
            q = qc_ref[rows, cols]
            qm = jnp.concatenate([jnp.where(halves[p], q, jnp.zeros_like(q)) for p in range(2)], axis=0)
            s = jnp.dot(qm, cmk_ref[b, cols, :].astype(bf16), preferred_element_type=f32)
            m = jnp.max(s, axis=-1, keepdims=True)
            e = jnp.exp(s - m)
            r = lax.dot_general(e.astype(bf16), cmv_ref[b, cols, :].astype(bf16), _NT, preferred_element_type=f32)
            out = r / jnp.sum(e, axis=-1, keepdims=True)
            oac_ref[rows, wa + j * LANES:wa + (j + 1) * LANES] = jnp.where(lo, out[:S], out[S:])

        for hp in range(DIL_KV_HEADS // 2):
            cols = slice(hp * LANES, (hp + 1) * LANES)
            pieces = [jnp.where(halves[p], qb_ref[hp * len(DIL_PAIRS) + g, rows, :], 0.0)
                      for g in range(len(DIL_PAIRS)) for p in range(2)]
            qm = jnp.concatenate(pieces, axis=0).astype(bf16)
            s_c = jnp.dot(qm, cbk_ref[b, cols, :].astype(bf16), preferred_element_type=f32)
            s_n = jnp.dot(qm, nkb[cols, :].astype(bf16), preferred_element_type=f32)
            out, m, den = _softmax2_pv(jnp.where(valid_bc, s_c, NEG), jnp.where(valid_bn, s_n, NEG),
                                       cbv_ref[b, cols, :].astype(bf16), nvb[cols, :].astype(bf16))
            lse = m + jnp.log(den)
            res = []
            for p in range(2):
                r = [slice((g * 2 + p) * S, (g * 2 + p + 1) * S) for g in range(len(DIL_PAIRS))]
                mx = jnp.maximum(jnp.maximum(lse[r[0]], lse[r[1]]), lse[r[2]])
                w = [jnp.exp(lse[x] - mx) for x in r]
                tot = w[0] + w[1] + w[2]
                res.append(sum((w[g] / tot) * out[r[g]] for g in range(len(DIL_PAIRS))))
            ob_ref[rows, cols] = jnp.where(lo, res[0], res[1])


def attn_sample(sinks, qa, qb, qc, nka, nva, nkb, nvb, cak, cav, cbk, cbv, cmk, cmv, bt):
    NB, wka, la = cak.shape
    wkb, lb = cbk.shape[1:]
    wm, M = cmk.shape[1:]
    wa, wc = SWA_Q_HEADS * HEAD_DIM, MEM_HEADS * HEAD_DIM
    ng = len(DIL_PAIRS)
    tok = lambda w: pl.BlockSpec((bt * SROWS, w), lambda i: (i, 0))
    buf = lambda f, n: pl.BlockSpec((bt, f, n), lambda i: (i, 0, 0))
    f32 = jnp.float32
    return pl.pallas_call(
        functools.partial(_attn_sample_body, bt),
        grid=(NB // bt,),
        in_specs=[pl.BlockSpec(memory_space=pltpu.SMEM), tok(wa),
                  pl.BlockSpec((2 * ng, bt * SROWS, LANES), lambda i: (0, i, 0)), tok(wc),
                  tok(wka), tok(wka), tok(wkb), tok(wkb),
                  buf(wka, la), buf(wka, la), buf(wkb, lb), buf(wkb, lb), buf(wm, M), buf(wm, M)],
        out_specs=[tok(wa + wc), tok(wkb), buf(wka, la), buf(wka, la), buf(wkb, lb), buf(wkb, lb)],
        out_shape=[jax.ShapeDtypeStruct((NB * SROWS, wa + wc), f32), jax.ShapeDtypeStruct((NB * SROWS, wkb), f32),
                   jax.ShapeDtypeStruct(cak.shape, f32), jax.ShapeDtypeStruct(cak.shape, f32),
                   jax.ShapeDtypeStruct(cbk.shape, f32), jax.ShapeDtypeStruct(cbk.shape, f32)],
        compiler_params=_cparams(("parallel",)), name="attn_sample")(
            sinks, qa, qb, qc, nka, nva, nkb, nvb, cak, cav, cbk, cbv, cmk, cmv)


MOE_BLOCK = 512
ROUTE_SUB = 256
SUBLANES = 8


def _store_row_tiles(ref, row0, y):
    n = y.shape[0]
    for c in range(SUBLANES):
        ref[pl.ds(row0 * SUBLANES + c, n, stride=SUBLANES), :] = y[:, c * LANES:(c + 1) * LANES]


def _load_row_tiles(ref, row0, n):
    return jnp.concatenate([ref[pl.ds(row0 * SUBLANES + c, n, stride=SUBLANES), :] for c in range(SUBLANES)],
                           axis=1)


def _row_tile(ref, r):
    return ref.at[pl.ds(pl.multiple_of(r * SUBLANES, SUBLANES), SUBLANES)]


def _merge_route_body(oac_ref, ob_ref, gate_ref, x_ref, wa_ref, wb_ref, wc_ref, wo_ref, gffn_ref, wr_ref, br_ref,
                      cnt0_ref, x1_ref, h2_ref, mi_ref, mf_ref, cnt_ref, base_ref):
    f32, bf16 = jnp.float32, jnp.bfloat16
    D = x_ref.shape[1]
    wa = SWA_Q_HEADS * HEAD_DIM
    ts = ROUTE_SUB

    @pl.when(pl.program_id(0) == 0)
    def _():
        base_ref[...] = cnt0_ref[...]

    erow = lax.broadcasted_iota(jnp.int32, (N_EXPERTS, ts), 0)
    r8 = lax.broadcasted_iota(jnp.int32, (SUBLANES, ts), 0)
    ti = lax.broadcasted_iota(jnp.int32, (ts, ts), 0)
    tj = lax.broadcasted_iota(jnp.int32, (ts, ts), 1)
    later = (ti < tj).astype(bf16)
    base = base_ref[:, 0:1]
    for sub in range(x_ref.shape[0] // ts):
        rows = slice(sub * ts, (sub + 1) * ts)
        ma = jnp.dot(oac_ref[rows, :wa].astype(bf16), wa_ref[...], preferred_element_type=f32)
        mb = jnp.dot(ob_ref[rows, :].astype(bf16), wb_ref[...], preferred_element_type=f32)
        mc = jnp.dot(oac_ref[rows, wa:].astype(bf16), wc_ref[...], preferred_element_type=f32)
        merged = (gate_ref[rows, :D].astype(f32) * ma + gate_ref[rows, D:2 * D].astype(f32) * mb
                  + gate_ref[rows, 2 * D:].astype(f32) * mc)
        x1 = x_ref[rows, :] + jnp.dot(merged.astype(bf16), wo_ref[...], preferred_element_type=f32)
        x1_ref[rows, :] = x1
        h2 = _rms(x1, gffn_ref[...])
        _store_row_tiles(h2_ref, sub * ts, h2)

        work = lax.dot_general(wr_ref[...], h2.astype(bf16), _NT, preferred_element_type=f32) + br_ref[:, 0:1]
        vals, idxs = [], []
        for _ in range(TOP_K):
            m = jnp.max(work, axis=0, keepdims=True)
            idx = jnp.min(jnp.where(work == m, erow, N_EXPERTS), axis=0, keepdims=True)
            vals.append(m)
            idxs.append(idx)
            work = jnp.where(erow == idx, -jnp.inf, work)
        es = [jnp.exp(v - vals[0]) for v in vals]
        tot = es[0] + es[1] + es[2] + es[3]

        onehot = [(erow == idx).astype(f32) for idx in idxs]
        assign = onehot[0] + onehot[1] + onehot[2] + onehot[3]
        before = jnp.dot(assign.astype(bf16), later, preferred_element_type=f32) + base
        base = base + jnp.sum(assign, axis=1, keepdims=True)

        mi = jnp.zeros((SUBLANES, ts), jnp.int32)
        gates = jnp.zeros((SUBLANES, ts), f32)
        for k in range(TOP_K):
            rank = jnp.sum(onehot[k] * before, axis=0, keepdims=True).astype(jnp.int32)
            mi = jnp.where(r8 == k, idxs[k], mi)
            mi = jnp.where(r8 == TOP_K + k, rank, mi)
            gates = jnp.where(r8 == k, es[k] / tot, gates)
        mi_ref[:, rows] = mi
        mf_ref[rows, :] = jnp.concatenate([gates, jnp.zeros((LANES - SUBLANES, ts), f32)], axis=0).T
    base_ref[...] = jnp.broadcast_to(base, base_ref.shape)
    cnt_ref[...] = jnp.broadcast_to(base, cnt_ref.shape)


def merge_route(o_ac, o_b, gates, x, wa, wb, wc, wo, g_ffn, wr, br, cnt0, tm):
    R, D = x.shape
    row = lambda w: pl.BlockSpec((tm, w), lambda i: (i, 0))
    full = lambda a: pl.BlockSpec(a.shape, lambda i: (0, 0))
    return pl.pallas_call(
        _merge_route_body,
        grid=(R // tm,),
        in_specs=[row(o_ac.shape[1]), row(o_b.shape[1]), row(gates.shape[1]), row(D),
                  full(wa), full(wb), full(wc), full(wo), full(g_ffn), full(wr), full(br), full(cnt0)],
        out_specs=[row(D), pl.BlockSpec((tm * SUBLANES, LANES), lambda i: (i, 0)),
                   pl.BlockSpec((SUBLANES, tm), lambda i: (0, i)), row(LANES),
                   pl.BlockSpec((N_EXPERTS, LANES), lambda i: (0, 0))],
        out_shape=[jax.ShapeDtypeStruct((R, D), jnp.float32), jax.ShapeDtypeStruct((R * SUBLANES, LANES), jnp.float32),
                   jax.ShapeDtypeStruct((SUBLANES, R), jnp.int32), jax.ShapeDtypeStruct((R, LANES), jnp.float32),
                   jax.ShapeDtypeStruct((N_EXPERTS, LANES), jnp.float32)],
        scratch_shapes=[pltpu.VMEM((N_EXPERTS, LANES), jnp.float32)],
        compiler_params=_cparams(("arbitrary",)), name="merge_route")(
            o_ac, o_b, gates, x, wa, wb, wc, wo, g_ffn, wr, br, cnt0)


def _route_tables_body(cnt_ref, mi_ref, dest_ref, blk_ref, pad_ref):
    tm = mi_ref.shape[1]
    nbl = blk_ref.shape[1]
    erow1 = lax.broadcasted_iota(jnp.int32, (N_EXPERTS, LANES), 0)
    shift = MOE_BLOCK.bit_length() - 1
    cnt = cnt_ref[...].astype(jnp.int32)
    padded = ((cnt + (MOE_BLOCK - 1)) >> shift) << shift
    pend = padded
    s = 1
    while s < N_EXPERTS:
        pend = pend + jnp.where(erow1 >= s, pltpu.roll(pend, s, axis=0), 0)
        s *= 2
    pstart = pend - padded
    mi = mi_ref[...]
    erow = lax.broadcasted_iota(jnp.int32, (N_EXPERTS, tm), 0)
    r8 = lax.broadcasted_iota(jnp.int32, (SUBLANES, tm), 0)
    dest = jnp.zeros((SUBLANES, tm), jnp.int32)
    for k in range(TOP_K):
        start = jnp.sum(jnp.where(erow == mi[k:k + 1, :], pstart[:, 0:1], 0), axis=0, keepdims=True)
        dest = jnp.where(r8 == k, start + mi[TOP_K + k:TOP_K + k + 1, :], dest)
    dest_ref[...] = dest

    @pl.when(pl.program_id(0) == 0)
    def _():
        row0 = lax.broadcasted_iota(jnp.int32, (N_EXPERTS, nbl), 1) * MOE_BLOCK
        ended = jnp.sum(jnp.where(pend[:, 0:1] <= row0, 1, 0), axis=0, keepdims=True)
        blk_ref[...] = jnp.broadcast_to(jnp.minimum(ended, N_EXPERTS - 1), blk_ref.shape)
        pad_ref[0] = pstart + cnt
        pad_ref[1] = pend


def route_tables(cnt, mi, nbl, tm):
    R = mi.shape[1]
    return pl.pallas_call(
        _route_tables_body,
        grid=(R // tm,),
        in_specs=[pl.BlockSpec((N_EXPERTS, LANES), lambda i: (0, 0)), pl.BlockSpec((SUBLANES, tm), lambda i: (0, i))],
        out_specs=[pl.BlockSpec((SUBLANES, tm), lambda i: (0, i)), pl.BlockSpec((SUBLANES, nbl), lambda i: (0, 0)),
                   pl.BlockSpec((2, N_EXPERTS, LANES), lambda i: (0, 0, 0))],
        out_shape=[jax.ShapeDtypeStruct((SUBLANES, R), jnp.int32), jax.ShapeDtypeStruct((SUBLANES, nbl), jnp.int32),
                   jax.ShapeDtypeStruct((2, N_EXPERTS, LANES), jnp.int32)],
        compiler_params=_cparams(("arbitrary",)), name="route_tables")(cnt, mi)


def _dispatch_body(n_first, n_steps, pad0_ref, pad1_ref, dest_ref, ha_ref, hb_ref, xs_ref, zero_ref, sem):
    i = pl.program_id(0)
    tm = ha_ref.shape[0] // SUBLANES
    rs = xs_ref.shape[0] // SUBLANES

    def scatter_tile(h_ref):
        def row_copy(r, k):
            return pltpu.make_async_copy(_row_tile(h_ref, r), _row_tile(xs_ref, dest_ref[0, 0, k * tm + r]), sem)

        def start(r, c):
            for k in range(TOP_K):
                row_copy(r, k).start(priority=k % 2)
            return c

        def wait(r, c):
            for k in range(TOP_K):
                row_copy(r, k).wait()
            return c

        lax.fori_loop(0, tm, start, 0, unroll=8)
        lax.fori_loop(0, tm, wait, 0, unroll=8)

    @pl.when(i < n_first)
    def _():
        scatter_tile(ha_ref)

    @pl.when(i >= n_first)
    def _():
        scatter_tile(hb_ref)

    @pl.when(i == n_steps - 1)
    def _():
        zero_ref[...] = jnp.zeros_like(zero_ref)
        zrows = zero_ref.shape[0] // SUBLANES

        def zero_range(lo, hi, go):
            n_big = (hi - lo) // zrows

            def big(j, c):
                start = pl.multiple_of((lo + j * zrows) * SUBLANES, SUBLANES)
                cp = pltpu.make_async_copy(zero_ref, xs_ref.at[pl.ds(start, zrows * SUBLANES)], sem)
                cp.start() if go else cp.wait()
                return c

            def one(r, c):
                cp = pltpu.make_async_copy(zero_ref.at[pl.ds(0, SUBLANES)], _row_tile(xs_ref, r), sem)
                cp.start() if go else cp.wait()
                return c

            lax.fori_loop(0, n_big, big, 0)
            lax.fori_loop(lo + n_big * zrows, hi, one, 0)

        for go in (True, False):
            for e in range(N_EXPERTS):
                zero_range(pad0_ref[e], pad1_ref[e], go)
            zero_range(pad1_ref[N_EXPERTS - 1], rs, go)


def dispatch(pad0, pad1, dest3, h_a, h_b, rs, tm):
    S = SUBLANES
    n_a, n_b = h_a.shape[0] // (tm * S), h_b.shape[0] // (tm * S)
    grid_spec = pltpu.PrefetchScalarGridSpec(
        num_scalar_prefetch=2,
        grid=(n_a + n_b,),
        in_specs=[pl.BlockSpec((1, 1, TOP_K * tm), lambda i, p0, p1: (i, 0, 0), memory_space=pltpu.SMEM),
                  pl.BlockSpec((tm * S, LANES), lambda i, p0, p1: (jnp.minimum(i, n_a - 1), 0)),
                  pl.BlockSpec((tm * S, LANES), lambda i, p0, p1: (jnp.maximum(i - n_a, 0), 0))],
        out_specs=pl.BlockSpec(memory_space=pl.ANY),
        scratch_shapes=[pltpu.VMEM((16 * S, LANES), h_a.dtype), pltpu.SemaphoreType.DMA])
    return pl.pallas_call(
        functools.partial(_dispatch_body, n_a, n_a + n_b), grid_spec=grid_spec,
        out_shape=jax.ShapeDtypeStruct((rs * S, LANES), h_a.dtype),
        compiler_params=_cparams(("arbitrary",)), name="dispatch")(pad0, pad1, dest3, h_a, h_b)


def _moe_body(nblk, be_ref, nact_ref, x_ref, wgu_hbm, bgu_ref, wd_hbm, bd_ref, y_ref, wgu_f, wd_f, wgu_s, wd_s,
              sems):
    f32, bf16 = jnp.float32, jnp.bfloat16
    b = pl.program_id(0)
    e = be_ref[b]

    def fetch(ex, go):
        for src, dst, s in ((wgu_hbm, wgu_f, 0), (wd_hbm, wd_f, 1)):
            cp = pltpu.make_async_copy(src.at[ex], dst, sems.at[s])
            cp.start() if go else cp.wait()

    @pl.when(b == 0)
    def _():
        fetch(e, True)

    @pl.when((b == 0) | (e != be_ref[jnp.maximum(b - 1, 0)]))
    def _():
        fetch(e, False)
        wgu_s[...] = wgu_f[...].astype(bf16)
        wd_s[...] = wd_f[...].astype(bf16)
        nxt = lax.while_loop(lambda j: (j < nblk) & (be_ref[jnp.minimum(j, nblk - 1)] == e), lambda j: j + 1, b + 1)

        @pl.when(nxt < nblk)
        def _():
            fetch(be_ref[jnp.minimum(nxt, nblk - 1)], True)

    @pl.when(b < nact_ref[0])
    def _():
        x = _load_row_tiles(x_ref, 0, MOE_BLOCK).astype(bf16)
        gu = jnp.dot(x, wgu_s[...], preferred_element_type=f32) + bgu_ref[0]
        gt = jnp.minimum(gu[:, :D_FF], SWIGLU_LIMIT)
        up = jnp.clip(gu[:, D_FF:], -SWIGLU_LIMIT, SWIGLU_LIMIT)
        act = (up + 1.0) * (gt * jax.nn.sigmoid(gt * SWIGLU_ALPHA))
        y = jnp.dot(act.astype(bf16), wd_s[...], preferred_element_type=f32) + bd_ref[0]
        _store_row_tiles(y_ref, 0, y)

    @pl.when(b >= nact_ref[0])
    def _():
        y_ref[...] = jnp.zeros_like(y_ref)


def moe_ffn(blk_e, n_active, xs, w_gate_up, b_gate_up, w_down, b_down):
    RS = xs.shape[0] // SUBLANES
    E, D, F2 = w_gate_up.shape
    blk_rows = MOE_BLOCK * SUBLANES
    grid_spec = pltpu.PrefetchScalarGridSpec(
        num_scalar_prefetch=2,
        grid=(RS // MOE_BLOCK,),
        in_specs=[pl.BlockSpec((blk_rows, LANES), lambda b, be, na: (b, 0)),
                  pl.BlockSpec(memory_space=pl.ANY),
                  pl.BlockSpec((1, 1, F2), lambda b, be, na: (be[b], 0, 0)),
                  pl.BlockSpec(memory_space=pl.ANY),
                  pl.BlockSpec((1, 1, D), lambda b, be, na: (be[b], 0, 0))],
        out_specs=pl.BlockSpec((blk_rows, LANES), lambda b, be, na: (b, 0)),
        scratch_shapes=[pltpu.VMEM((D, F2), w_gate_up.dtype), pltpu.VMEM((F2 // 2, D), w_down.dtype),
                        pltpu.VMEM((D, F2), jnp.bfloat16), pltpu.VMEM((F2 // 2, D), jnp.bfloat16),
                        pltpu.SemaphoreType.DMA((2,))])
    return pl.pallas_call(
        functools.partial(_moe_body, RS // MOE_BLOCK), grid_spec=grid_spec,
        out_shape=jax.ShapeDtypeStruct(xs.shape, jnp.float32),
        compiler_params=_cparams(("arbitrary",)), name="moe_ffn")(
            blk_e, n_active, xs, w_gate_up, b_gate_up.reshape(E, 1, F2), w_down, b_down.reshape(E, 1, D))


def _combine_body(n_tiles, dest_ref, dest_next_ref, x1_ref, mf_ref, g_ref, ys_ref, o_ref, buf, sems):
    tm = x1_ref.shape[0]
    i = pl.program_id(0)
    slot = i % 2
    per_slot = TOP_K * tm

    def gather(d_ref, s, go):
        def row_copy(r, k):
            return pltpu.make_async_copy(_row_tile(ys_ref, d_ref[0, 0, k * tm + r]),
                                         _row_tile(buf, s * per_slot + k * tm + r), sems.at[s])

        def body(r, c):
            for k in range(TOP_K):
                if go:
                    row_copy(r, k).start(priority=k % 2)
                else:
                    row_copy(r, k).wait()
            return c

        lax.fori_loop(0, tm, body, 0, unroll=8)

    @pl.when(i == 0)
    def _():
        gather(dest_ref, slot, True)

    @pl.when(i + 1 < n_tiles)
    def _():
        gather(dest_next_ref, 1 - slot, True)

    gather(dest_ref, slot, False)
    y = x1_ref[...]
    for k in range(TOP_K):
        y = y + mf_ref[:, k:k + 1] * _load_row_tiles(buf, slot * per_slot + k * tm, tm)
    o_ref[...] = _rms(y, g_ref[...])


def combine(dest3, x1, mf, g_final, ys, tm):
    R, D = x1.shape
    n = R // tm
    dest_spec = lambda f: pl.BlockSpec((1, 1, TOP_K * tm), f, memory_space=pltpu.SMEM)
    return pl.pallas_call(
        functools.partial(_combine_body, n),
        grid=(n,),
        in_specs=[dest_spec(lambda i: (i, 0, 0)), dest_spec(lambda i: (jnp.minimum(i + 1, n - 1), 0, 0)),
                  pl.BlockSpec((tm, D), lambda i: (i, 0)),
                  pl.BlockSpec((tm, LANES), lambda i: (i, 0)),
                  pl.BlockSpec((1, D), lambda i: (0, 0)),
                  pl.BlockSpec(memory_space=pl.ANY)],
        out_specs=pl.BlockSpec((tm, D), lambda i: (i, 0)),
        out_shape=jax.ShapeDtypeStruct((R, D), jnp.float32),
        scratch_shapes=[pltpu.VMEM((2 * TOP_K * tm * SUBLANES, LANES), ys.dtype), pltpu.SemaphoreType.DMA((2,))],
        compiler_params=_cparams(("arbitrary",)), name="combine")(dest3, dest3, x1, mf, g_final, ys)


def moe_layer(group_a, group_b, cnt, g_final, w_gate_up, b_gate_up, w_down, b_down, tm):
    n_assign = (group_a[0].shape[0] + group_b[0].shape[0]) * TOP_K
    nb = (n_assign + N_EXPERTS * (MOE_BLOCK - 1)) // MOE_BLOCK + 1
    nbl = -(-nb // LANES) * LANES
    dests = []
    for x1, h2, mi, mf in (group_a, group_b):
        dest, blk, pad = route_tables(cnt, mi, nbl, min(x1.shape[0], 4 * tm))
        n_tiles = x1.shape[0] // tm
        dests.append(jnp.transpose(dest[:TOP_K].reshape(TOP_K, n_tiles, tm), (1, 0, 2)).reshape(n_tiles, 1, TOP_K * tm))
    xs = dispatch(pad[0, :, 0], pad[1, :, 0], jnp.concatenate(dests), group_a[1], group_b[1], nb * MOE_BLOCK, tm)
    n_active = (pad[1, N_EXPERTS - 1, 0] // MOE_BLOCK).reshape(1)
    ys = moe_ffn(blk[0, :nb], n_active, xs, w_gate_up, b_gate_up, w_down, b_down)
    return [combine(dest3, x1, mf, g_final, ys, tm) for dest3, (x1, h2, mi, mf) in zip(dests, (group_a, group_b))]


def kernel(x_prompt, x_sample, cache_swa_k, cache_swa_v, cache_dil_k, cache_dil_v, cache_mem_k, cache_mem_v, mem_prompt, norm_attn, norm_mem, w_in, w_mem_kv, sinks, w_br_a, w_br_b, w_br_c, w_out, norm_ffn, w_router, b_router, w_gate_up, b_gate_up, w_down, b_down, norm_final):
    f32, bf16 = jnp.float32, jnp.bfloat16
    TM = 256
    B, L, D = x_prompt.shape
    NB, n_new, _ = x_sample.shape
    M = mem_prompt.shape[1]
    la, lb = cache_swa_k.shape[2], cache_dil_k.shape[2]
    wka, wkb, wc = SWA_KV_HEADS * HEAD_DIM, DIL_KV_HEADS * HEAD_DIM, MEM_HEADS * HEAD_DIM
    assert n_new == N_NEW and cache_swa_k.shape[0] == 1

    w_in_b = w_in[0].astype(bf16)
    g_attn = norm_attn[0].reshape(1, D)
    secs = in_sections()

    tabs_p = rope_tables(jnp.arange(L, dtype=jnp.int32))
    qa, ka, va, qb, kb, vb, qc, gates = norm_proj(x_prompt.reshape(B * L, D), g_attn, w_in_b, tabs_p, secs, TM)
    mk, mv = norm_proj(mem_prompt.reshape(B * M, D), norm_mem[0].reshape(1, D), w_mem_kv[0].astype(bf16), None,
                       [(_chunks(0, wc), "plain", False, f32), (_chunks(wc, wc), "plain", False, f32)], TM)
    o_ac, ka_t, va_t, mk_t, mv_t = attn_swa_mem(sinks[0], qa, ka, va, qc, mk, mv, B, L)
    o_b, kb_t, vb_t = attn_dilated(qb, kb, vb, B, L)

    xs_pad = jnp.pad(x_sample, ((0, 0), (0, SROWS - N_NEW), (0, 0))).reshape(NB * SROWS, D)
    tabs_s = rope_tables(PAST_LEN + (jnp.arange(TM, dtype=jnp.int32) % SROWS))
    qa_s, ka_s, va_s, qb_s, kb_s, vb_s, qc_s, gates_s = norm_proj(xs_pad, g_attn, w_in_b, tabs_s, secs, TM)
    real = lambda t: t.reshape(NB, SROWS, -1)[:, :N_NEW].reshape(NB * N_NEW, -1)
    fmaj = lambda c: jnp.transpose(c[0], (0, 2, 3, 1)).reshape(NB, -1, c.shape[2])
    o_ac_s, o_b_s, swa_k_s, swa_v_s, dil_k_s, dil_v_s = attn_sample(
        sinks[0], qa_s, qb_s, qc_s, ka_s, va_s, kb_s, vb_s,
        fmaj(cache_swa_k), fmaj(cache_swa_v), fmaj(cache_dil_k), fmaj(cache_dil_v),
        fmaj(cache_mem_k), fmaj(cache_mem_v), 2)

    wr = w_router[0].T.astype(bf16)
    br = jnp.broadcast_to(b_router[0].astype(f32)[:, None], (N_EXPERTS, LANES))
    wts = (w_br_a[0].astype(bf16), w_br_b[0].astype(bf16), w_br_c[0].astype(bf16), w_out[0].astype(bf16),
           norm_ffn[0].reshape(1, D), wr, br)
    x1_p, h2_p, mi_p, mf_p, cnt_p = merge_route(o_ac, o_b, gates, x_prompt.reshape(B * L, D), *wts,
                                                jnp.zeros((N_EXPERTS, LANES), f32), 2 * ROUTE_SUB)
    x1_s, h2_s, mi_s, mf_s, cnt = merge_route(real(o_ac_s), real(o_b_s), real(gates_s),
                                              x_sample.reshape(NB * N_NEW, D), *wts, cnt_p, 2 * ROUTE_SUB)
    y_p, y_s = moe_layer((x1_p, h2_p, mi_p, mf_p), (x1_s, h2_s, mi_s, mf_s), cnt, norm_final.reshape(1, D),
                         w_gate_up[0], b_gate_up[0], w_down[0], b_down[0], TM)

    tmaj = lambda t, h: jnp.transpose(t.reshape(t.shape[0], h, HEAD_DIM, t.shape[2]), (0, 3, 1, 2))[None]
    return (y_p.reshape(B, L, D), y_s.reshape(NB, N_NEW, D),
            tmaj(ka_t, SWA_KV_HEADS), tmaj(va_t, SWA_KV_HEADS),
            tmaj(kb_t, DIL_KV_HEADS), tmaj(vb_t, DIL_KV_HEADS),
            tmaj(mk_t, MEM_HEADS), tmaj(mv_t, MEM_HEADS),
            tmaj(swa_k_s, SWA_KV_HEADS), tmaj(swa_v_s, SWA_KV_HEADS),
            tmaj(dil_k_s, DIL_KV_HEADS), tmaj(dil_v_s, DIL_KV_HEADS))
```

```python
import functools

import jax
import jax.numpy as jnp
from jax import lax
from jax.experimental import pallas as pl
from jax.experimental.pallas import tpu as pltpu

D_MODEL = 1024
HEAD_DIM = 64
ROPE_DIM = 16
ROPE_HALF = 8
ROPE_THETA = 500000.0
PAST_LEN = 16384
SWA_Q_HEADS = 8
SWA_KV_HEADS = 2
SWA_WINDOW = 128
DIL_PAIRS = ((128, 1), (512, 4), (2048, 16))
DIL_KV_HEADS = 4
MEM_HEADS = 4
N_EXPERTS = 32
TOP_K = 4
D_FF = 1024
SWIGLU_LIMIT = 7.0
SWIGLU_ALPHA = 1.702
RMS_EPS = 1e-5
ATT_BLOCK = 128
SCALE = HEAD_DIM ** -0.5

LANES = 128
NEG = -1e30
VMEM_LIMIT = 56 * 1024 * 1024


def _cparams(sem):
    return pltpu.CompilerParams(dimension_semantics=sem, vmem_limit_bytes=VMEM_LIMIT)


def _rms(x, g):
    return x * lax.rsqrt(jnp.mean(x * x, axis=-1, keepdims=True) + RMS_EPS) * g


def _norm_proj_body(sections, x_ref, g_ref, w_ref, cs_ref, *out_refs):
    h = _rms(x_ref[...], g_ref[...]).astype(jnp.bfloat16)
    if cs_ref is not None:
        cos = cs_ref[0]
        sin_lo = cs_ref[1]
        sin_hi = cs_ref[2]
    dest = {lo: (o_ref, c, kind, slabs)
            for (cols, kind, slabs), o_ref in zip(sections, out_refs) for c, lo in enumerate(cols)}
    todo = sorted(dest)
    while todo:
        lo = todo.pop(0)
        n = 2 if todo and todo[0] == lo + LANES else 1
        if n == 2:
            todo.pop(0)
        yy = jnp.dot(h, w_ref[:, lo:lo + n * LANES], preferred_element_type=jnp.float32)
        for part in range(n):
            o_ref, c, kind, slabs = dest[lo + part * LANES]
            y = yy[:, part * LANES:(part + 1) * LANES]
            if kind in ("rope", "rope_q"):
                y = (y * cos + pltpu.roll(y, LANES - ROPE_HALF, axis=1) * sin_lo
                     + pltpu.roll(y, ROPE_HALF, axis=1) * sin_hi)
            if kind in ("rope_q", "q"):
                y = y * SCALE
            if kind == "sigmoid":
                y = jax.nn.sigmoid(y)
            if slabs:
                o_ref[c] = y.astype(o_ref.dtype)
            else:
                o_ref[:, c * LANES:(c + 1) * LANES] = y.astype(o_ref.dtype)


def rope_tables(pos):
    inv_freq = ROPE_THETA ** (-jnp.arange(ROPE_HALF, dtype=jnp.float32) / ROPE_HALF)
    ang = pos.astype(jnp.float32)[:, None] * inv_freq[None, :]
    cos, sin = jnp.cos(ang), jnp.sin(ang)
    n = pos.shape[0]
    one = jnp.ones((n, HEAD_DIM - ROPE_DIM), jnp.float32)
    zero = jnp.zeros((n, HEAD_DIM - ROPE_HALF), jnp.float32)
    c = jnp.concatenate([cos, cos, one], axis=1)
    s_lo = jnp.concatenate([-sin, zero], axis=1)
    s_hi = jnp.concatenate([jnp.zeros((n, ROPE_HALF), jnp.float32), sin,
                            jnp.zeros((n, HEAD_DIM - ROPE_DIM), jnp.float32)], axis=1)
    tab = jnp.stack([c, s_lo, s_hi])
    return jnp.concatenate([tab, tab], axis=2)


def norm_proj(x, g, w, tables, sections, tm):
    R, D = x.shape
    in_specs = [pl.BlockSpec((tm, D), lambda i: (i, 0)),
                pl.BlockSpec((1, D), lambda i: (0, 0)),
                pl.BlockSpec(w.shape, lambda i: (0, 0))]
    args = [x, g, w]
    if tables is not None:
        nt = tables.shape[1] // tm
        in_specs.append(pl.BlockSpec((3, tm, LANES), lambda i: (0, i % nt, 0)))
        args.append(tables)
    out_shape, out_specs, secs = [], [], []
    for (cols, kind, slabs, dtype) in sections:
        secs.append((cols, kind, slabs))
        width = LANES * len(cols)
        if slabs:
            out_shape.append(jax.ShapeDtypeStruct((width // LANES, R, LANES), dtype))
            out_specs.append(pl.BlockSpec((width // LANES, tm, LANES), lambda i: (0, i, 0)))
        else:
            out_shape.append(jax.ShapeDtypeStruct((R, width), dtype))
            out_specs.append(pl.BlockSpec((tm, width), lambda i: (i, 0)))
    if tables is None:
        body = lambda x_ref, g_ref, w_ref, *o: _norm_proj_body(secs, x_ref, g_ref, w_ref, None, *o)
    else:
        body = functools.partial(_norm_proj_body, secs)
    return pl.pallas_call(
        body, grid=(R // tm,), in_specs=in_specs, out_specs=out_specs, out_shape=out_shape,
        compiler_params=_cparams(("parallel",)), name="norm_proj")(*args)


def _chunks(start, width):
    return tuple(range(start, start + width, LANES))


def in_sections():
    f32, bf16 = jnp.float32, jnp.bfloat16
    qb0 = 1024 - 256
    qb_cols = tuple(qb0 + HEAD_DIM * (4 * g + 2 * hp) for hp in range(2) for g in range(3))
    return [
        (_chunks(0, 512), "rope_q", False, bf16),
        (_chunks(512, 128), "rope", False, f32),
        (_chunks(640, 128), "plain", False, f32),
        (qb_cols, "rope_q", True, f32),
        (_chunks(1536, 256), "rope", False, f32),
        (_chunks(1792, 256), "plain", False, f32),
        (_chunks(2048, 256), "q", False, bf16),
        (_chunks(2304, 3072), "sigmoid", False, bf16),
    ]


_NT = (((1,), (1,)), ((), ()))


def _half_masks():
    lane = lax.broadcasted_iota(jnp.int32, (1, LANES), 1)
    return lane < HEAD_DIM, lane >= HEAD_DIM


def _softmax_pv(s, v_half, sink=None):
    m = jnp.max(s, axis=-1, keepdims=True)
    if sink is not None:
        m = jnp.maximum(m, sink)
    e = jnp.exp(s - m)
    den = jnp.sum(e, axis=-1, keepdims=True)
    if sink is not None:
        den = den + jnp.exp(sink - m)
    r = jnp.dot(e.astype(jnp.bfloat16), v_half, preferred_element_type=jnp.float32)
    return r / den, m, den


def _attn_swa_mem_body(L, sink_ref, qa_ref, ka_ref, va_ref, qc_ref, mk_ref, mv_ref, o_ref,
                       kat_ref, vat_ref, mkt_ref, mvt_ref, band_ref):
    bf16 = jnp.bfloat16
    lo, hi = _half_masks()
    halves = (lo, hi)
    T = ATT_BLOCK
    kat_ref[...] = ka_ref[L - SWA_WINDOW:, :].T
    vat_ref[...] = va_ref[L - SWA_WINDOW:, :].T
    mkt_ref[...] = mk_ref[...].T
    mvt_ref[...] = mv_ref[...].T
    G = SWA_Q_HEADS // SWA_KV_HEADS
    mem_k = [mk_ref[:, j * LANES:(j + 1) * LANES].astype(bf16) for j in range(2)]
    mem_v = [mv_ref[:, j * LANES:(j + 1) * LANES].astype(bf16) for j in range(2)]
    qi = lax.broadcasted_iota(jnp.int32, (2 * T, 2 * T), 0) & (T - 1)
    kj = lax.broadcasted_iota(jnp.int32, (2 * T, 2 * T), 1)
    hrow = lax.broadcasted_iota(jnp.int32, (2 * T, 1), 0) >> (T.bit_length() - 1)
    for case, off in enumerate((0, T)):
        dist = qi - kj + off
        band_ref[case] = jnp.where((dist >= 0) & (dist <= SWA_WINDOW - 1), 1.0, 0.0)

    def heads_of(pair):
        return jnp.concatenate([jnp.where(halves[p], pair, jnp.zeros_like(pair)) for p in range(2)], axis=0)

    def block(blk, carry):
        r0 = pl.multiple_of(blk * T, T)
        ws = pl.multiple_of(jnp.maximum(r0 - T, 0), T)
        valid = band_ref[jnp.minimum(blk, 1)] > 0.5
        k = ka_ref[pl.ds(ws, 2 * T), :]
        v = va_ref[pl.ds(ws, 2 * T), :]
        for kv in range(SWA_KV_HEADS):
            k1 = jnp.where(halves[kv], k, 0.0)
            v1 = jnp.where(halves[kv], v, 0.0)
            k_dup = (k1 + pltpu.roll(k1, HEAD_DIM, axis=1)).astype(bf16)
            v_dup = (v1 + pltpu.roll(v1, HEAD_DIM, axis=1)).astype(bf16)
            for j in range(kv * G // 2, (kv + 1) * G // 2):
                qm = heads_of(qa_ref[pl.ds(r0, T), j * LANES:(j + 1) * LANES])
                sink = jnp.where(hrow == 0, sink_ref[2 * j], sink_ref[2 * j + 1])
                s = lax.dot_general(qm, k_dup, _NT, preferred_element_type=jnp.float32)
                out, _, _ = _softmax_pv(jnp.where(valid, s, NEG), v_dup, sink)
                o_ref[pl.ds(r0, T), j * LANES:(j + 1) * LANES] = jnp.where(lo, out[:T], out[T:]).astype(o_ref.dtype)
        for j in range(MEM_HEADS // 2):
            qm = heads_of(qc_ref[pl.ds(r0, T), j * LANES:(j + 1) * LANES])
            s = lax.dot_general(qm, mem_k[j], _NT, preferred_element_type=jnp.float32)
            out, _, _ = _softmax_pv(s, mem_v[j])
            c0 = SWA_Q_HEADS * HEAD_DIM + j * LANES
            o_ref[pl.ds(r0, T), c0:c0 + LANES] = jnp.where(lo, out[:T], out[T:]).astype(o_ref.dtype)
        return carry

    lax.fori_loop(0, L // T, block, 0)


def attn_swa_mem(sinks, qa, ka, va, qc, mk, mv, B, L):
    M = mk.shape[0] // B
    wa, wc = SWA_Q_HEADS * HEAD_DIM, MEM_HEADS * HEAD_DIM
    wka = ka.shape[1]
    n_win = min(SWA_WINDOW, L)
    row = lambda w: pl.BlockSpec((L, w), lambda b: (b, 0))
    fmaj = lambda f, n: pl.BlockSpec((None, f, n), lambda b: (b, 0, 0))
    f32 = jnp.float32
    return pl.pallas_call(
        functools.partial(_attn_swa_mem_body, L),
        grid=(B,),
        in_specs=[pl.BlockSpec(memory_space=pltpu.SMEM), row(wa), row(LANES), row(LANES), row(wc),
                  pl.BlockSpec((M, wc), lambda b: (b, 0)), pl.BlockSpec((M, wc), lambda b: (b, 0))],
        out_specs=[row(wa + wc), fmaj(wka, n_win), fmaj(wka, n_win), fmaj(wc, M), fmaj(wc, M)],
        out_shape=[jax.ShapeDtypeStruct((B * L, wa + wc), jnp.bfloat16),
                   jax.ShapeDtypeStruct((B, wka, n_win), f32), jax.ShapeDtypeStruct((B, wka, n_win), f32),
                   jax.ShapeDtypeStruct((B, wc, M), f32), jax.ShapeDtypeStruct((B, wc, M), f32)],
        scratch_shapes=[pltpu.VMEM((2, 2 * ATT_BLOCK, 2 * ATT_BLOCK), f32)],
        compiler_params=_cparams(("parallel",)), name="attn_swa_mem")(sinks, qa, ka, va, qc, mk, mv)


def _attn_dil_body(L, qb_ref, kb_ref, vb_ref, o_ref, kt_ref, vt_ref, og_ref, lse_ref, band_ref):
    bf16 = jnp.bfloat16
    lo, hi = _half_masks()
    halves = (lo, hi)
    T = ATT_BLOCK
    kt_ref[...] = kb_ref[...].T
    vt_ref[...] = vb_ref[...].T
    for g, (window, dil) in enumerate(DIL_PAIRS):
        lc = L // dil
        nbc = lc // T
        W = min(2 * T, lc)
        max_dist = window // dil
        qi = lax.broadcasted_iota(jnp.int32, (2 * T, W), 0) & (T - 1)
        kj = lax.broadcasted_iota(jnp.int32, (2 * T, W), 1)
        for case, off in enumerate((0, T)):
            dist = qi - kj + off
            band_ref[g, case, :, 0:W] = jnp.where((dist >= 0) & (dist <= max_dist), 1.0, 0.0)

        def unit(u, carry, g=g, dil=dil, nbc=nbc, W=W):
            c = u >> (nbc.bit_length() - 1)
            n = u & (nbc - 1)
            wsc = jnp.maximum(n * T - T, 0) if W == 2 * T else 0
            q0 = c + dil * T * n
            k0 = c + dil * wsc
            valid = band_ref[g, jnp.minimum(n, 1), :, 0:W] > 0.5
            q = qb_ref[g, pl.ds(q0, T, stride=dil), :]
            k = kb_ref[pl.ds(k0, W, stride=dil), :]
            v = vb_ref[pl.ds(k0, W, stride=dil), :]
            qm = jnp.concatenate([jnp.where(halves[p], q, 0.0) for p in range(2)], axis=0).astype(bf16)
            s = lax.dot_general(qm, k.astype(bf16), _NT, preferred_element_type=jnp.float32)
            out, m, den = _softmax_pv(jnp.where(valid, s, NEG), v.astype(bf16))
            lse = m + jnp.log(den)
            og_ref[g, pl.ds(q0, T, stride=dil), :] = jnp.where(lo, out[:T], out[T:])
            lse_ref[g, pl.ds(q0, T, stride=dil), :] = jnp.where(lo, lse[:T], lse[T:])
            return carry

        lax.fori_loop(0, dil * nbc, unit, 0, unroll=16)

    def merge(i, carry):
        r0 = pl.multiple_of(i * T, T)
        ls = [lse_ref[g, pl.ds(r0, T), :] for g in range(len(DIL_PAIRS))]
        m = jnp.maximum(jnp.maximum(ls[0], ls[1]), ls[2])
        ws = [jnp.exp(l - m) for l in ls]
        tot = ws[0] + ws[1] + ws[2]
        out = sum((w / tot) * og_ref[g, pl.ds(r0, T), :] for g, w in enumerate(ws))
        o_ref[pl.ds(r0, T), :] = out.astype(o_ref.dtype)
        return carry

    lax.fori_loop(0, L // T, merge, 0)


def attn_dilated(qb, kb, vb, B, L):
    ng = len(DIL_PAIRS)
    fmaj = pl.BlockSpec((None, LANES, L), lambda b, hp: (b, hp, 0))
    return pl.pallas_call(
        functools.partial(_attn_dil_body, L),
        grid=(B, 2),
        in_specs=[pl.BlockSpec((ng, L, LANES), lambda b, hp: (hp, b, 0)),
                  pl.BlockSpec((L, LANES), lambda b, hp: (b, hp)),
                  pl.BlockSpec((L, LANES), lambda b, hp: (b, hp))],
        out_specs=[pl.BlockSpec((L, LANES), lambda b, hp: (b, hp)), fmaj, fmaj],
        out_shape=[jax.ShapeDtypeStruct((B * L, 2 * LANES), jnp.bfloat16),
                   jax.ShapeDtypeStruct((B, 2 * LANES, L), jnp.float32),
                   jax.ShapeDtypeStruct((B, 2 * LANES, L), jnp.float32)],
        scratch_shapes=[pltpu.VMEM((ng, L, LANES), jnp.float32), pltpu.VMEM((ng, L, LANES), jnp.float32),
                        pltpu.VMEM((ng, 2, 2 * ATT_BLOCK, 2 * ATT_BLOCK), jnp.float32)],
        compiler_params=_cparams(("parallel", "parallel")), name="attn_dilated")(qb, kb, vb)


N_NEW = 4
SROWS = 8


def _softmax2_pv(s_c, s_n, vt_c, vt_n, sink=None):
    m = jnp.maximum(jnp.max(s_c, axis=-1, keepdims=True), jnp.max(s_n, axis=-1, keepdims=True))
    if sink is not None:
        m = jnp.maximum(m, sink)
    e_c = jnp.exp(s_c - m)
    e_n = jnp.exp(s_n - m)
    den = jnp.sum(e_c, axis=-1, keepdims=True) + jnp.sum(e_n, axis=-1, keepdims=True)
    if sink is not None:
        den = den + jnp.exp(sink - m)
    r = (lax.dot_general(e_c.astype(jnp.bfloat16), vt_c, _NT, preferred_element_type=jnp.float32)
         + lax.dot_general(e_n.astype(jnp.bfloat16), vt_n, _NT, preferred_element_type=jnp.float32))
    return r / den, m, den


def _advance(old_t, new_t):
    n = old_t.shape[1]
    lane = lax.broadcasted_iota(jnp.int32, (1, LANES), 1)
    shifted = pltpu.roll(old_t, n - N_NEW, axis=1)
    last = jnp.where(lane < LANES - N_NEW, shifted[:, n - LANES:], new_t)
    if n == LANES:
        return last
    return jnp.concatenate([shifted[:, :n - LANES], last], axis=1)


def _attn_sample_body(bt, sink_ref, qa_ref, qb_ref, qc_ref, nka_ref, nva_ref, nkb_ref, nvb_ref,
                      cak_ref, cav_ref, cbk_ref, cbv_ref, cmk_ref, cmv_ref,
                      oac_ref, ob_ref, oak_ref, oav_ref, obk_ref, obv_ref):
    f32, bf16 = jnp.float32, jnp.bfloat16
    lo, hi = _half_masks()
    halves = (lo, hi)
    S = SROWS
    la = cak_ref.shape[2]
    lb = cbk_ref.shape[2]
    wa = SWA_Q_HEADS * HEAD_DIM
    new0 = LANES - N_NEW

    na = SWA_Q_HEADS * S
    ia = lax.broadcasted_iota(jnp.int32, (na, la), 0) & (S - 1)
    valid_ac = lax.broadcasted_iota(jnp.int32, (na, la), 1) >= ia + 1
    ja = lax.broadcasted_iota(jnp.int32, (na, LANES), 1) - new0
    valid_an = (ja >= 0) & (ja <= (lax.broadcasted_iota(jnp.int32, (na, LANES), 0) & (S - 1)))
    rcol = lax.broadcasted_iota(jnp.int32, (na, 1), 0)
    sink_col = jnp.zeros((na, 1), f32)
    for h in range(SWA_Q_HEADS):
        sink_col = jnp.where((rcol >> 3) == h, sink_ref[h], sink_col)

    nb_rows = len(DIL_PAIRS) * 2 * S
    rb = lax.broadcasted_iota(jnp.int32, (nb_rows, lb), 0)
    t_c = lb + (rb & (S - 1)) - lax.broadcasted_iota(jnp.int32, (nb_rows, lb), 1)
    rn = lax.broadcasted_iota(jnp.int32, (nb_rows, LANES), 0)
    jn = lax.broadcasted_iota(jnp.int32, (nb_rows, LANES), 1) - new0
    t_n = (rn & (S - 1)) - jn
    valid_bc = jnp.zeros((nb_rows, lb), jnp.bool_)
    valid_bn = jnp.zeros((nb_rows, LANES), jnp.bool_)
    for g, (window, dil) in enumerate(DIL_PAIRS):
        valid_bc = valid_bc | (((rb >> 4) == g) & (t_c <= window) & ((t_c & (dil - 1)) == 0))
        valid_bn = valid_bn | (((rn >> 4) == g) & (jn >= 0) & (t_n >= 0) & ((t_n & (dil - 1)) == 0))

    def new_t(x):
        padded = jnp.concatenate([x, jnp.zeros((LANES - S, x.shape[1]), f32)], axis=0)
        return pltpu.roll(padded.T, new0, axis=1)

    for b in range(bt):
        rows = slice(b * S, (b + 1) * S)
        nka, nva = new_t(nka_ref[rows, :]), new_t(nva_ref[rows, :])
        nkb, nvb = new_t(nkb_ref[rows, :]), new_t(nvb_ref[rows, :])
        oak_ref[b] = _advance(cak_ref[b], nka)
        oav_ref[b] = _advance(cav_ref[b], nva)
        obk_ref[b] = _advance(cbk_ref[b], nkb)
        obv_ref[b] = _advance(cbv_ref[b], nvb)

        pieces = []
        for h in range(SWA_Q_HEADS):
            q = jnp.where(halves[h % 2], qa_ref[rows, (h // 2) * LANES:(h // 2 + 1) * LANES].astype(f32), 0.0)
            if h % 2 != h // 4:
                q = pltpu.roll(q, HEAD_DIM, axis=1)
            pieces.append(q)
        qm = jnp.concatenate(pieces, axis=0).astype(bf16)
        s_c = jnp.dot(qm, cak_ref[b].astype(bf16), preferred_element_type=f32)
        s_n = jnp.dot(qm, nka.astype(bf16), preferred_element_type=f32)
        out, _, _ = _softmax2_pv(jnp.where(valid_ac, s_c, NEG), jnp.where(valid_an, s_n, NEG),
                                 cav_ref[b].astype(bf16), nva.astype(bf16), sink_col)
        for j in range(SWA_Q_HEADS // 2):
            parts = []
            for p in range(2):
                h = 2 * j + p
                o = out[h * S:(h + 1) * S]
                if h % 2 != h // 4:
                    o = pltpu.roll(o, HEAD_DIM, axis=1)
                parts.append(o)
            oac_ref[rows, j * LANES:(j + 1) * LANES] = jnp.where(lo, parts[0], parts[1])

        for j in range(MEM_HEADS // 2):
            cols = slice(j * LANES, (j + 1) * LANES)
            q = qc_ref[rows, cols]
            qm = jnp.concatenate([jnp.where(halves[p], q, jnp.zeros_like(q)) for p in range(2)], axis=0)
            s = jnp.dot(qm, cmk_ref[b, cols, :].astype(bf16), preferred_element_type=f32)
            m = jnp.max(s, axis=-1, keepdims=True)
            e = jnp.exp(s - m)
            r = lax.dot_general(e.astype(bf16), cmv_ref[b, cols, :].astype(bf16), _NT, preferred_element_type=f32)
            out = r / jnp.sum(e, axis=-1, keepdims=True)
            oac_ref[rows, wa + j * LANES:wa + (j + 1) * LANES] = jnp.where(lo, out[:S], out[S:])

        for hp in range(DIL_KV_HEADS // 2):
            cols = slice(hp * LANES, (hp + 1) * LANES)
            pieces = [jnp.where(halves[p], qb_ref[hp * len(DIL_PAIRS) + g, rows, :], 0.0)
                      for g in range(len(DIL_PAIRS)) for p in range(2)]
            qm = jnp.concatenate(pieces, axis=0).astype(bf16)
            s_c = jnp.dot(qm, cbk_ref[b, cols, :].astype(bf16), preferred_element_type=f32)
            s_n = jnp.dot(qm, nkb[cols, :].astype(bf16), preferred_element_type=f32)
            out, m, den = _softmax2_pv(jnp.where(valid_bc, s_c, NEG), jnp.where(valid_bn, s_n, NEG),
                                       cbv_ref[b, cols, :].astype(bf16), nvb[cols, :].astype(bf16))
            lse = m + jnp.log(den)
            res = []
            for p in range(2):
                r = [slice((g * 2 + p) * S, (g * 2 + p + 1) * S) for g in range(len(DIL_PAIRS))]
                mx = jnp.maximum(jnp.maximum(lse[r[0]], lse[r[1]]), lse[r[2]])
                w = [jnp.exp(lse[x] - mx) for x in r]
                tot = w[0] + w[1] + w[2]
                res.append(sum((w[g] / tot) * out[r[g]] for g in range(len(DIL_PAIRS))))
            ob_ref[rows, cols] = jnp.where(lo, res[0], res[1])


def attn_sample(sinks, qa, qb, qc, nka, nva, nkb, nvb, cak, cav, cbk, cbv, cmk, cmv, bt):
    NB, wka, la = cak.shape
    wkb, lb = cbk.shape[1:]
    wm, M = cmk.shape[1:]
    wa, wc = SWA_Q_HEADS * HEAD_DIM, MEM_HEADS * HEAD_DIM
    ng = len(DIL_PAIRS)
    tok = lambda w: pl.BlockSpec((bt * SROWS, w), lambda i: (i, 0))
    buf = lambda f, n: pl.BlockSpec((bt, f, n), lambda i: (i, 0, 0))
    f32 = jnp.float32
    return pl.pallas_call(
        functools.partial(_attn_sample_body, bt),
        grid=(NB // bt,),
        in_specs=[pl.BlockSpec(memory_space=pltpu.SMEM), tok(wa),
                  pl.BlockSpec((2 * ng, bt * SROWS, LANES), lambda i: (0, i, 0)), tok(wc),
                  tok(wka), tok(wka), tok(wkb), tok(wkb),
                  buf(wka, la), buf(wka, la), buf(wkb, lb), buf(wkb, lb), buf(wm, M), buf(wm, M)],
        out_specs=[tok(wa + wc), tok(wkb), buf(wka, la), buf(wka, la), buf(wkb, lb), buf(wkb, lb)],
        out_shape=[jax.ShapeDtypeStruct((NB * SROWS, wa + wc), f32), jax.ShapeDtypeStruct((NB * SROWS, wkb), f32),
                   jax.ShapeDtypeStruct(cak.shape, f32), jax.ShapeDtypeStruct(cak.shape, f32),
                   jax.ShapeDtypeStruct(cbk.shape, f32), jax.ShapeDtypeStruct(cbk.shape, f32)],
        compiler_params=_cparams(("parallel",)), name="attn_sample")(
            sinks, qa, qb, qc, nka, nva, nkb, nvb, cak, cav, cbk, cbv, cmk, cmv)


MOE_BLOCK = 512
ROUTE_SUB = 256
SUBLANES = 8


def _store_row_tiles(ref, row0, y):
    n = y.shape[0]
    for c in range(SUBLANES):
        ref[pl.ds(row0 * SUBLANES + c, n, stride=SUBLANES), :] = y[:, c * LANES:(c + 1) * LANES]


def _load_row_tiles(ref, row0, n):
    return jnp.concatenate([ref[pl.ds(row0 * SUBLANES + c, n, stride=SUBLANES), :] for c in range(SUBLANES)],
                           axis=1)


def _row_tile(ref, r):
    return ref.at[pl.ds(pl.multiple_of(r * SUBLANES, SUBLANES), SUBLANES)]


def _merge_route_body(oac_ref, ob_ref, gate_ref, x_ref, wa_ref, wb_ref, wc_ref, wo_ref, gffn_ref, wr_ref, br_ref,
                      cnt0_ref, x1_ref, h2_ref, mi_ref, mf_ref, cnt_ref, base_ref):
    f32, bf16 = jnp.float32, jnp.bfloat16
    D = x_ref.shape[1]
    wa = SWA_Q_HEADS * HEAD_DIM
    ts = ROUTE_SUB

    @pl.when(pl.program_id(0) == 0)
    def _():
        base_ref[...] = cnt0_ref[...]

    erow = lax.broadcasted_iota(jnp.int32, (N_EXPERTS, ts), 0)
    r8 = lax.broadcasted_iota(jnp.int32, (SUBLANES, ts), 0)
    ti = lax.broadcasted_iota(jnp.int32, (ts, ts), 0)
    tj = lax.broadcasted_iota(jnp.int32, (ts, ts), 1)
    later = (ti < tj).astype(bf16)
    base = base_ref[:, 0:1]
    for sub in range(x_ref.shape[0] // ts):
        rows = slice(sub * ts, (sub + 1) * ts)
        ma = jnp.dot(oac_ref[rows, :wa].astype(bf16), wa_ref[...], preferred_element_type=f32)
        mb = jnp.dot(ob_ref[rows, :].astype(bf16), wb_ref[...], preferred_element_type=f32)
        mc = jnp.dot(oac_ref[rows, wa:].astype(bf16), wc_ref[...], preferred_element_type=f32)
        merged = (gate_ref[rows, :D].astype(f32) * ma + gate_ref[rows, D:2 * D].astype(f32) * mb
                  + gate_ref[rows, 2 * D:].astype(f32) * mc)
        x1 = x_ref[rows, :] + jnp.dot(merged.astype(bf16), wo_ref[...], preferred_element_type=f32)
        x1_ref[rows, :] = x1
        h2 = _rms(x1, gffn_ref[...])
        _store_row_tiles(h2_ref, sub * ts, h2)

        work = lax.dot_general(wr_ref[...], h2.astype(bf16), _NT, preferred_element_type=f32) + br_ref[:, 0:1]
        vals, idxs = [], []
        for _ in range(TOP_K):
            m = jnp.max(work, axis=0, keepdims=True)
            idx = jnp.min(jnp.where(work == m, erow, N_EXPERTS), axis=0, keepdims=True)
            vals.append(m)
            idxs.append(idx)
            work = jnp.where(erow == idx, -jnp.inf, work)
        es = [jnp.exp(v - vals[0]) for v in vals]
        tot = es[0] + es[1] + es[2] + es[3]

        onehot = [(erow == idx).astype(f32) for idx in idxs]
        assign = onehot[0] + onehot[1] + onehot[2] + onehot[3]
        before = jnp.dot(assign.astype(bf16), later, preferred_element_type=f32) + base
        base = base + jnp.sum(assign, axis=1, keepdims=True)

        mi = jnp.zeros((SUBLANES, ts), jnp.int32)
        gates = jnp.zeros((SUBLANES, ts), f32)
        for k in range(TOP_K):
            rank = jnp.sum(onehot[k] * before, axis=0, keepdims=True).astype(jnp.int32)
            mi = jnp.where(r8 == k, idxs[k], mi)
            mi = jnp.where(r8 == TOP_K + k, rank, mi)
            gates = jnp.where(r8 == k, es[k] / tot, gates)
        mi_ref[:, rows] = mi
        mf_ref[rows, :] = jnp.concatenate([gates, jnp.zeros((LANES - SUBLANES, ts), f32)], axis=0).T
    base_ref[...] = jnp.broadcast_to(base, base_ref.shape)
    cnt_ref[...] = jnp.broadcast_to(base, cnt_ref.shape)


def merge_route(o_ac, o_b, gates, x, wa, wb, wc, wo, g_ffn, wr, br, cnt0, tm):
    R, D = x.shape
    row = lambda w: pl.BlockSpec((tm, w), lambda i: (i, 0))
    full = lambda a: pl.BlockSpec(a.shape, lambda i: (0, 0))
    return pl.pallas_call(
        _merge_route_body,
        grid=(R // tm,),
        in_specs=[row(o_ac.shape[1]), row(o_b.shape[1]), row(gates.shape[1]), row(D),
                  full(wa), full(wb), full(wc), full(wo), full(g_ffn), full(wr), full(br), full(cnt0)],
        out_specs=[row(D), pl.BlockSpec((tm * SUBLANES, LANES), lambda i: (i, 0)),
                   pl.BlockSpec((SUBLANES, tm), lambda i: (0, i)), row(LANES),
                   pl.BlockSpec((N_EXPERTS, LANES), lambda i: (0, 0))],
        out_shape=[jax.ShapeDtypeStruct((R, D), jnp.float32), jax.ShapeDtypeStruct((R * SUBLANES, LANES), jnp.float32),
                   jax.ShapeDtypeStruct((SUBLANES, R), jnp.int32), jax.ShapeDtypeStruct((R, LANES), jnp.float32),
                   jax.ShapeDtypeStruct((N_EXPERTS, LANES), jnp.float32)],
        scratch_shapes=[pltpu.VMEM((N_EXPERTS, LANES), jnp.float32)],
        compiler_params=_cparams(("arbitrary",)), name="merge_route")(
            o_ac, o_b, gates, x, wa, wb, wc, wo, g_ffn, wr, br, cnt0)


def _route_tables_body(cnt_ref, mi_ref, dest_ref, blk_ref, pad_ref):
    tm = mi_ref.shape[1]
    nbl = blk_ref.shape[1]
    erow1 = lax.broadcasted_iota(jnp.int32, (N_EXPERTS, LANES), 0)
    shift = MOE_BLOCK.bit_length() - 1
    cnt = cnt_ref[...].astype(jnp.int32)
    padded = ((cnt + (MOE_BLOCK - 1)) >> shift) << shift
    pend = padded
    s = 1
    while s < N_EXPERTS:
        pend = pend + jnp.where(erow1 >= s, pltpu.roll(pend, s, axis=0), 0)
        s *= 2
    pstart = pend - padded
    mi = mi_ref[...]
    erow = lax.broadcasted_iota(jnp.int32, (N_EXPERTS, tm), 0)
    r8 = lax.broadcasted_iota(jnp.int32, (SUBLANES, tm), 0)
    dest = jnp.zeros((SUBLANES, tm), jnp.int32)
    for k in range(TOP_K):
        start = jnp.sum(jnp.where(erow == mi[k:k + 1, :], pstart[:, 0:1], 0), axis=0, keepdims=True)
        dest = jnp.where(r8 == k, start + mi[TOP_K + k:TOP_K + k + 1, :], dest)
    dest_ref[...] = dest

    @pl.when(pl.program_id(0) == 0)
    def _():
        row0 = lax.broadcasted_iota(jnp.int32, (N_EXPERTS, nbl), 1) * MOE_BLOCK
        ended = jnp.sum(jnp.where(pend[:, 0:1] <= row0, 1, 0), axis=0, keepdims=True)
        blk_ref[...] = jnp.broadcast_to(jnp.minimum(ended, N_EXPERTS - 1), blk_ref.shape)
        pad_ref[0] = pstart + cnt
        pad_ref[1] = pend


def route_tables(cnt, mi, nbl, tm):
    R = mi.shape[1]
    return pl.pallas_call(
        _route_tables_body,
        grid=(R // tm,),
        in_specs=[pl.BlockSpec((N_EXPERTS, LANES), lambda i: (0, 0)), pl.BlockSpec((SUBLANES, tm), lambda i: (0, i))],
        out_specs=[pl.BlockSpec((SUBLANES, tm), lambda i: (0, i)), pl.BlockSpec((SUBLANES, nbl), lambda i: (0, 0)),
                   pl.BlockSpec((2, N_EXPERTS, LANES), lambda i: (0, 0, 0))],
        out_shape=[jax.ShapeDtypeStruct((SUBLANES, R), jnp.int32), jax.ShapeDtypeStruct((SUBLANES, nbl), jnp.int32),
                   jax.ShapeDtypeStruct((2, N_EXPERTS, LANES), jnp.int32)],
        compiler_params=_cparams(("arbitrary",)), name="route_tables")(cnt, mi)


def _dispatch_body(n_first, n_steps, pad0_ref, pad1_ref, dest_ref, ha_ref, hb_ref, xs_ref, zero_ref, sem):
    i = pl.program_id(0)
    tm = ha_ref.shape[0] // SUBLANES
    rs = xs_ref.shape[0] // SUBLANES

    def scatter_tile(h_ref):
        def row_copy(r, k):
            return pltpu.make_async_copy(_row_tile(h_ref, r), _row_tile(xs_ref, dest_ref[0, 0, k * tm + r]), sem)

        def start(r, c):
            for k in range(TOP_K):
                row_copy(r, k).start(priority=k % 2)
            return c

        def wait(r, c):
            for k in range(TOP_K):
                row_copy(r, k).wait()
            return c

        lax.fori_loop(0, tm, start, 0, unroll=8)
        lax.fori_loop(0, tm, wait, 0, unroll=8)

    @pl.when(i < n_first)
    def _():
        scatter_tile(ha_ref)

    @pl.when(i >= n_first)
    def _():
        scatter_tile(hb_ref)

    @pl.when(i == n_steps - 1)
    def _():
        zero_ref[...] = jnp.zeros_like(zero_ref)
        zrows = zero_ref.shape[0] // SUBLANES

        def zero_range(lo, hi, go):
            n_big = (hi - lo) // zrows

            def big(j, c):
                start = pl.multiple_of((lo + j * zrows) * SUBLANES, SUBLANES)
                cp = pltpu.make_async_copy(zero_ref, xs_ref.at[pl.ds(start, zrows * SUBLANES)], sem)
                cp.start() if go else cp.wait()
                return c

            def one(r, c):
                cp = pltpu.make_async_copy(zero_ref.at[pl.ds(0, SUBLANES)], _row_tile(xs_ref, r), sem)
                cp.start() if go else cp.wait()
                return c

            lax.fori_loop(0, n_big, big, 0)
            lax.fori_loop(lo + n_big * zrows, hi, one, 0)

        for go in (True, False):
            for e in range(N_EXPERTS):
                zero_range(pad0_ref[e], pad1_ref[e], go)
            zero_range(pad1_ref[N_EXPERTS - 1], rs, go)


def dispatch(pad0, pad1, dest3, h_a, h_b, rs, tm):
    S = SUBLANES
    n_a, n_b = h_a.shape[0] // (tm * S), h_b.shape[0] // (tm * S)
    grid_spec = pltpu.PrefetchScalarGridSpec(
        num_scalar_prefetch=2,
        grid=(n_a + n_b,),
        in_specs=[pl.BlockSpec((1, 1, TOP_K * tm), lambda i, p0, p1: (i, 0, 0), memory_space=pltpu.SMEM),
                  pl.BlockSpec((tm * S, LANES), lambda i, p0, p1: (jnp.minimum(i, n_a - 1), 0)),
                  pl.BlockSpec((tm * S, LANES), lambda i, p0, p1: (jnp.maximum(i - n_a, 0), 0))],
        out_specs=pl.BlockSpec(memory_space=pl.ANY),
        scratch_shapes=[pltpu.VMEM((16 * S, LANES), h_a.dtype), pltpu.SemaphoreType.DMA])
    return pl.pallas_call(
        functools.partial(_dispatch_body, n_a, n_a + n_b), grid_spec=grid_spec,
        out_shape=jax.ShapeDtypeStruct((rs * S, LANES), h_a.dtype),
        compiler_params=_cparams(("arbitrary",)), name="dispatch")(pad0, pad1, dest3, h_a, h_b)


def _moe_body(nblk, be_ref, nact_ref, x_ref, wgu_hbm, bgu_ref, wd_hbm, bd_ref, y_ref, wgu_f, wd_f, wgu_s, wd_s,
              sems):
    f32, bf16 = jnp.float32, jnp.bfloat16
    b = pl.program_id(0)
    e = be_ref[b]

    def fetch(ex, go):
        for src, dst, s in ((wgu_hbm, wgu_f, 0), (wd_hbm, wd_f, 1)):
            cp = pltpu.make_async_copy(src.at[ex], dst, sems.at[s])
            cp.start() if go else cp.wait()

    @pl.when(b == 0)
    def _():
        fetch(e, True)

    @pl.when((b == 0) | (e != be_ref[jnp.maximum(b - 1, 0)]))
    def _():
        fetch(e, False)
        wgu_s[...] = wgu_f[...].astype(bf16)
        wd_s[...] = wd_f[...].astype(bf16)
        nxt = lax.while_loop(lambda j: (j < nblk) & (be_ref[jnp.minimum(j, nblk - 1)] == e), lambda j: j + 1, b + 1)

        @pl.when(nxt < nblk)
        def _():
            fetch(be_ref[jnp.minimum(nxt, nblk - 1)], True)

    @pl.when(b < nact_ref[0])
    def _():
        x = _load_row_tiles(x_ref, 0, MOE_BLOCK).astype(bf16)
        gu = jnp.dot(x, wgu_s[...], preferred_element_type=f32) + bgu_ref[0]
        gt = jnp.minimum(gu[:, :D_FF], SWIGLU_LIMIT)
        up = jnp.clip(gu[:, D_FF:], -SWIGLU_LIMIT, SWIGLU_LIMIT)
        act = (up + 1.0) * (gt * jax.nn.sigmoid(gt * SWIGLU_ALPHA))
        y = jnp.dot(act.astype(bf16), wd_s[...], preferred_element_type=f32) + bd_ref[0]
        _store_row_tiles(y_ref, 0, y)

    @pl.when(b >= nact_ref[0])
    def _():
        y_ref[...] = jnp.zeros_like(y_ref)


def moe_ffn(blk_e, n_active, xs, w_gate_up, b_gate_up, w_down, b_down):
    RS = xs.shape[0] // SUBLANES
    E, D, F2 = w_gate_up.shape
    blk_rows = MOE_BLOCK * SUBLANES
    grid_spec = pltpu.PrefetchScalarGridSpec(
        num_scalar_prefetch=2,
        grid=(RS // MOE_BLOCK,),
        in_specs=[pl.BlockSpec((blk_rows, LANES), lambda b, be, na: (b, 0)),
                  pl.BlockSpec(memory_space=pl.ANY),
                  pl.BlockSpec((1, 1, F2), lambda b, be, na: (be[b], 0, 0)),
                  pl.BlockSpec(memory_space=pl.ANY),
                  pl.BlockSpec((1, 1, D), lambda b, be, na: (be[b], 0, 0))],
        out_specs=pl.BlockSpec((blk_rows, LANES), lambda b, be, na: (b, 0)),
        scratch_shapes=[pltpu.VMEM((D, F2), w_gate_up.dtype), pltpu.VMEM((F2 // 2, D), w_down.dtype),
                        pltpu.VMEM((D, F2), jnp.bfloat16), pltpu.VMEM((F2 // 2, D), jnp.bfloat16),
                        pltpu.SemaphoreType.DMA((2,))])
    return pl.pallas_call(
        functools.partial(_moe_body, RS // MOE_BLOCK), grid_spec=grid_spec,
        out_shape=jax.ShapeDtypeStruct(xs.shape, jnp.float32),
        compiler_params=_cparams(("arbitrary",)), name="moe_ffn")(
            blk_e, n_active, xs, w_gate_up, b_gate_up.reshape(E, 1, F2), w_down, b_down.reshape(E, 1, D))


def _combine_body(n_tiles, dest_ref, dest_next_ref, x1_ref, mf_ref, g_ref, ys_ref, o_ref, buf, sems):
    tm = x1_ref.shape[0]
    i = pl.program_id(0)
    slot = i % 2
    per_slot = TOP_K * tm

    def gather(d_ref, s, go):
        def row_copy(r, k):
            return pltpu.make_async_copy(_row_tile(ys_ref, d_ref[0, 0, k * tm + r]),
                                         _row_tile(buf, s * per_slot + k * tm + r), sems.at[s])

        def body(r, c):
            for k in range(TOP_K):
                if go:
                    row_copy(r, k).start(priority=k % 2)
                else:
                    row_copy(r, k).wait()
            return c

        lax.fori_loop(0, tm, body, 0, unroll=8)

    @pl.when(i == 0)
    def _():
        gather(dest_ref, slot, True)

    @pl.when(i + 1 < n_tiles)
    def _():
        gather(dest_next_ref, 1 - slot, True)

    gather(dest_ref, slot, False)
    y = x1_ref[...]
    for k in range(TOP_K):
        y = y + mf_ref[:, k:k + 1] * _load_row_tiles(buf, slot * per_slot + k * tm, tm)
    o_ref[...] = _rms(y, g_ref[...])


def combine(dest3, x1, mf, g_final, ys, tm):
    R, D = x1.shape
    n = R // tm
    dest_spec = lambda f: pl.BlockSpec((1, 1, TOP_K * tm), f, memory_space=pltpu.SMEM)
    return pl.pallas_call(
        functools.partial(_combine_body, n),
        grid=(n,),
        in_specs=[dest_spec(lambda i: (i, 0, 0)), dest_spec(lambda i: (jnp.minimum(i + 1, n - 1), 0, 0)),
                  pl.BlockSpec((tm, D), lambda i: (i, 0)),
                  pl.BlockSpec((tm, LANES), lambda i: (i, 0)),
                  pl.BlockSpec((1, D), lambda i: (0, 0)),
                  pl.BlockSpec(memory_space=pl.ANY)],
        out_specs=pl.BlockSpec((tm, D), lambda i: (i, 0)),
        out_shape=jax.ShapeDtypeStruct((R, D), jnp.float32),
        scratch_shapes=[pltpu.VMEM((2 * TOP_K * tm * SUBLANES, LANES), ys.dtype), pltpu.SemaphoreType.DMA((2,))],
        compiler_params=_cparams(("arbitrary",)), name="combine")(dest3, dest3, x1, mf, g_final, ys)


def moe_layer(group_a, group_b, cnt, g_final, w_gate_up, b_gate_up, w_down, b_down, tm):
    n_assign = (group_a[0].shape[0] + group_b[0].shape[0]) * TOP_K
    nb = (n_assign + N_EXPERTS * (MOE_BLOCK - 1)) // MOE_BLOCK + 1
    nbl = -(-nb // LANES) * LANES
    dests = []
    for x1, h2, mi, mf in (group_a, group_b):
        dest, blk, pad = route_tables(cnt, mi, nbl, min(x1.shape[0], 4 * tm))
        n_tiles = x1.shape[0] // tm
        dests.append(jnp.transpose(dest[:TOP_K].reshape(TOP_K, n_tiles, tm), (1, 0, 2)).reshape(n_tiles, 1, TOP_K * tm))
    xs = dispatch(pad[0, :, 0], pad[1, :, 0], jnp.concatenate(dests), group_a[1], group_b[1], nb * MOE_BLOCK, tm)
    n_active = (pad[1, N_EXPERTS - 1, 0] // MOE_BLOCK).reshape(1)
    ys = moe_ffn(blk[0, :nb], n_active, xs, w_gate_up, b_gate_up, w_down, b_down)
    return [combine(dest3, x1, mf, g_final, ys, tm) for dest3, (x1, h2, mi, mf) in zip(dests, (group_a, group_b))]


def kernel(x_prompt, x_sample, cache_swa_k, cache_swa_v, cache_dil_k, cache_dil_v, cache_mem_k, cache_mem_v, mem_prompt, norm_attn, norm_mem, w_in, w_mem_kv, sinks, w_br_a, w_br_b, w_br_c, w_out, norm_ffn, w_router, b_router, w_gate_up, b_gate_up, w_down, b_down, norm_final):
    f32, bf16 = jnp.float32, jnp.bfloat16
    TM = 256
    B, L, D = x_prompt.shape
    NB, n_new, _ = x_sample.shape
    M = mem_prompt.shape[1]
    wc = MEM_HEADS * HEAD_DIM
    assert n_new == N_NEW and cache_swa_k.shape[0] == 1

    w_in_b = w_in[0].astype(bf16)
    g_attn = norm_attn[0].reshape(1, D)
    secs = in_sections()

    tabs_p = rope_tables(jnp.arange(L, dtype=jnp.int32))
    qa, ka, va, qb, kb, vb, qc, gates = norm_proj(x_prompt.reshape(B * L, D), g_attn, w_in_b, tabs_p, secs, TM)
    mk, mv = norm_proj(mem_prompt.reshape(B * M, D), norm_mem[0].reshape(1, D), w_mem_kv[0].astype(bf16), None,
                       [(_chunks(0, wc), "plain", False, f32), (_chunks(wc, wc), "plain", False, f32)], TM)
    o_ac, ka_t, va_t, mk_t, mv_t = attn_swa_mem(sinks[0], qa, ka, va, qc, mk, mv, B, L)
    o_b, kb_t, vb_t = attn_dilated(qb, kb, vb, B, L)

    xs_pad = jnp.pad(x_sample, ((0, 0), (0, SROWS - N_NEW), (0, 0))).reshape(NB * SROWS, D)
    tabs_s = rope_tables(PAST_LEN + (jnp.arange(TM, dtype=jnp.int32) % SROWS))
    qa_s, ka_s, va_s, qb_s, kb_s, vb_s, qc_s, gates_s = norm_proj(xs_pad, g_attn, w_in_b, tabs_s, secs, TM)
    real = lambda t: t.reshape(NB, SROWS, -1)[:, :N_NEW].reshape(NB * N_NEW, -1)
    fmaj = lambda c: jnp.transpose(c[0], (0, 2, 3, 1)).reshape(NB, -1, c.shape[2])
    o_ac_s, o_b_s, swa_k_s, swa_v_s, dil_k_s, dil_v_s = attn_sample(
        sinks[0], qa_s, qb_s, qc_s, ka_s, va_s, kb_s, vb_s,
        fmaj(cache_swa_k), fmaj(cache_swa_v), fmaj(cache_dil_k), fmaj(cache_dil_v),
        fmaj(cache_mem_k), fmaj(cache_mem_v), 2)

    wr = w_router[0].T.astype(bf16)
    br = jnp.broadcast_to(b_router[0].astype(f32)[:, None], (N_EXPERTS, LANES))
    wts = (w_br_a[0].astype(bf16), w_br_b[0].astype(bf16), w_br_c[0].astype(bf16), w_out[0].astype(bf16),
           norm_ffn[0].reshape(1, D), wr, br)
    x1_p, h2_p, mi_p, mf_p, cnt_p = merge_route(o_ac, o_b, gates, x_prompt.reshape(B * L, D), *wts,
                                                jnp.zeros((N_EXPERTS, LANES), f32), 2 * ROUTE_SUB)
    x1_s, h2_s, mi_s, mf_s, cnt = merge_route(real(o_ac_s), real(o_b_s), real(gates_s),
                                              x_sample.reshape(NB * N_NEW, D), *wts, cnt_p, 2 * ROUTE_SUB)
    y_p, y_s = moe_layer((x1_p, h2_p, mi_p, mf_p), (x1_s, h2_s, mi_s, mf_s), cnt, norm_final.reshape(1, D),
                         w_gate_up[0], b_gate_up[0], w_down[0], b_down[0], TM)

    tmaj = lambda t, h: jnp.transpose(t.reshape(t.shape[0], h, HEAD_DIM, t.shape[2]), (0, 3, 1, 2))[None]
    return (y_p.reshape(B, L, D), y_s.reshape(NB, N_NEW, D),
            tmaj(ka_t, SWA_KV_HEADS), tmaj(va_t, SWA_KV_HEADS),
            tmaj(kb_t, DIL_KV_HEADS), tmaj(vb_t, DIL_KV_HEADS),
            tmaj(mk_t, MEM_HEADS), tmaj(mv_t, MEM_HEADS),
            tmaj(swa_k_s, SWA_KV_HEADS), tmaj(swa_v_s, SWA_KV_HEADS),
            tmaj(dil_k_s, DIL_KV_HEADS), tmaj(dil_v_s, DIL_KV_HEADS))
```

```python
import functools

import jax
import jax.numpy as jnp
from jax import lax
from jax.experimental import pallas as pl
from jax.experimental.pallas import tpu as pltpu

D_MODEL = 1024
HEAD_DIM = 64
ROPE_DIM = 16
ROPE_HALF = 8
ROPE_THETA = 500000.0
PAST_LEN = 16384
SWA_Q_HEADS = 8
SWA_KV_HEADS = 2
SWA_WINDOW = 128
DIL_PAIRS = ((128, 1), (512, 4), (2048, 16))
DIL_KV_HEADS = 4
MEM_HEADS = 4
N_EXPERTS = 32
TOP_K = 4
D_FF = 1024
SWIGLU_LIMIT = 7.0
SWIGLU_ALPHA = 1.702
RMS_EPS = 1e-5
ATT_BLOCK = 128
DIL_CHUNK = 4
SCALE = HEAD_DIM ** -0.5

LANES = 128
NEG = -1e30
VMEM_LIMIT = 56 * 1024 * 1024


def _cparams(sem):
    return pltpu.CompilerParams(dimension_semantics=sem, vmem_limit_bytes=VMEM_LIMIT)


def _rms(x, g):
    return x * lax.rsqrt(jnp.mean(x * x, axis=-1, keepdims=True) + RMS_EPS) * g


def _norm_proj_body(sections, x_ref, g_ref, w_ref, cs_ref, *out_refs):
    h = _rms(x_ref[...], g_ref[...]).astype(jnp.bfloat16)
    if cs_ref is not None:
        cos = cs_ref[0]
        sin_lo = cs_ref[1]
        sin_hi = cs_ref[2]
    dest = {lo: (o_ref, c, kind, slabs)
            for (cols, kind, slabs), o_ref in zip(sections, out_refs) for c, lo in enumerate(cols)}
    todo = sorted(dest)
    while todo:
        lo = todo.pop(0)
        n = 2 if todo and todo[0] == lo + LANES else 1
        if n == 2:
            todo.pop(0)
        yy = jnp.dot(h, w_ref[:, lo:lo + n * LANES], preferred_element_type=jnp.float32)
        for part in range(n):
            o_ref, c, kind, slabs = dest[lo + part * LANES]
            y = yy[:, part * LANES:(part + 1) * LANES]
            if kind in ("rope", "rope_q"):
                y = (y * cos + pltpu.roll(y, LANES - ROPE_HALF, axis=1) * sin_lo
                     + pltpu.roll(y, ROPE_HALF, axis=1) * sin_hi)
            if kind in ("rope_q", "q"):
                y = y * SCALE
            if kind == "sigmoid":
                y = jax.nn.sigmoid(y)
            if slabs:
                o_ref[c] = y.astype(o_ref.dtype)
            else:
                o_ref[:, c * LANES:(c + 1) * LANES] = y.astype(o_ref.dtype)


def rope_tables(pos):
    inv_freq = ROPE_THETA ** (-jnp.arange(ROPE_HALF, dtype=jnp.float32) / ROPE_HALF)
    ang = pos.astype(jnp.float32)[:, None] * inv_freq[None, :]
    cos, sin = jnp.cos(ang), jnp.sin(ang)
    n = pos.shape[0]
    one = jnp.ones((n, HEAD_DIM - ROPE_DIM), jnp.float32)
    zero = jnp.zeros((n, HEAD_DIM - ROPE_HALF), jnp.float32)
    c = jnp.concatenate([cos, cos, one], axis=1)
    s_lo = jnp.concatenate([-sin, zero], axis=1)
    s_hi = jnp.concatenate([jnp.zeros((n, ROPE_HALF), jnp.float32), sin,
                            jnp.zeros((n, HEAD_DIM - ROPE_DIM), jnp.float32)], axis=1)
    tab = jnp.stack([c, s_lo, s_hi])
    return jnp.concatenate([tab, tab], axis=2)


def norm_proj(x, g, w, tables, sections, tm):
    R, D = x.shape
    in_specs = [pl.BlockSpec((tm, D), lambda i: (i, 0)),
                pl.BlockSpec((1, D), lambda i: (0, 0)),
                pl.BlockSpec(w.shape, lambda i: (0, 0))]
    args = [x, g, w]
    if tables is not None:
        nt = tables.shape[1] // tm
        in_specs.append(pl.BlockSpec((3, tm, LANES), lambda i: (0, i % nt, 0)))
        args.append(tables)
    out_shape, out_specs, secs = [], [], []
    for (cols, kind, slabs, dtype) in sections:
        secs.append((cols, kind, slabs))
        width = LANES * len(cols)
        if slabs:
            out_shape.append(jax.ShapeDtypeStruct((width // LANES, R, LANES), dtype))
            out_specs.append(pl.BlockSpec((width // LANES, tm, LANES), lambda i: (0, i, 0)))
        else:
            out_shape.append(jax.ShapeDtypeStruct((R, width), dtype))
            out_specs.append(pl.BlockSpec((tm, width), lambda i: (i, 0)))
    if tables is None:
        body = lambda x_ref, g_ref, w_ref, *o: _norm_proj_body(secs, x_ref, g_ref, w_ref, None, *o)
    else:
        body = functools.partial(_norm_proj_body, secs)
    return pl.pallas_call(
        body, grid=(R // tm,), in_specs=in_specs, out_specs=out_specs, out_shape=out_shape,
        compiler_params=_cparams(("parallel",)), name="norm_proj")(*args)


def _chunks(start, width):
    return tuple(range(start, start + width, LANES))


def in_sections():
    f32, bf16 = jnp.float32, jnp.bfloat16
    qb0 = 1024 - 256
    qb_cols = tuple(qb0 + HEAD_DIM * (4 * g + 2 * hp) for hp in range(2) for g in range(3))
    return [
        (_chunks(0, 512), "rope_q", False, bf16),
        (_chunks(512, 128), "rope", False, f32),
        (_chunks(640, 128), "plain", False, f32),
        (qb_cols, "rope_q", True, f32),
        (_chunks(1536, 256), "rope", False, f32),
        (_chunks(1792, 256), "plain", False, f32),
        (_chunks(2048, 256), "q", False, bf16),
        (_chunks(2304, 3072), "sigmoid", False, bf16),
    ]


_NT = (((1,), (1,)), ((), ()))


def _half_masks():
    lane = lax.broadcasted_iota(jnp.int32, (1, LANES), 1)
    return lane < HEAD_DIM, lane >= HEAD_DIM


def _softmax_pv(s, v_half, sink=None):
    m = jnp.max(s, axis=-1, keepdims=True)
    if sink is not None:
        m = jnp.maximum(m, sink)
    e = jnp.exp(s - m)
    den = jnp.sum(e, axis=-1, keepdims=True)
    if sink is not None:
        den = den + jnp.exp(sink - m)
    r = jnp.dot(e.astype(jnp.bfloat16), v_half, preferred_element_type=jnp.float32)
    return r / den, m, den


def _attn_swa_mem_body(L, sink_ref, qa_ref, ka_ref, va_ref, qc_ref, mk_ref, mv_ref, o_ref,
                       kat_ref, vat_ref, mkt_ref, mvt_ref, band_ref):
    bf16 = jnp.bfloat16
    lo, hi = _half_masks()
    halves = (lo, hi)
    T = ATT_BLOCK
    kat_ref[...] = ka_ref[L - SWA_WINDOW:, :].T
    vat_ref[...] = va_ref[L - SWA_WINDOW:, :].T
    mkt_ref[...] = mk_ref[...].T
    mvt_ref[...] = mv_ref[...].T
    G = SWA_Q_HEADS // SWA_KV_HEADS
    mem_k = [mk_ref[:, j * LANES:(j + 1) * LANES].astype(bf16) for j in range(2)]
    mem_v = [mv_ref[:, j * LANES:(j + 1) * LANES].astype(bf16) for j in range(2)]
    qi = lax.broadcasted_iota(jnp.int32, (2 * T, 2 * T), 0) & (T - 1)
    kj = lax.broadcasted_iota(jnp.int32, (2 * T, 2 * T), 1)
    hrow = lax.broadcasted_iota(jnp.int32, (2 * T, 1), 0) >> (T.bit_length() - 1)
    for case, off in enumerate((0, T)):
        dist = qi - kj + off
        band_ref[case] = jnp.where((dist >= 0) & (dist <= SWA_WINDOW - 1), 1.0, 0.0)

    def heads_of(pair):
        return jnp.concatenate([jnp.where(halves[p], pair, jnp.zeros_like(pair)) for p in range(2)], axis=0)

    def block(blk, carry):
        r0 = pl.multiple_of(blk * T, T)
        ws = pl.multiple_of(jnp.maximum(r0 - T, 0), T)
        valid = band_ref[jnp.minimum(blk, 1)] > 0.5
        k = ka_ref[pl.ds(ws, 2 * T), :]
        v = va_ref[pl.ds(ws, 2 * T), :]
        units = []
        for kv in range(SWA_KV_HEADS):
            k1 = jnp.where(halves[kv], k, 0.0)
            v1 = jnp.where(halves[kv], v, 0.0)
            k_dup = (k1 + pltpu.roll(k1, HEAD_DIM, axis=1)).astype(bf16)
            v_dup = (v1 + pltpu.roll(v1, HEAD_DIM, axis=1)).astype(bf16)
            for j in range(kv * G // 2, (kv + 1) * G // 2):
                qm = heads_of(qa_ref[pl.ds(r0, T), j * LANES:(j + 1) * LANES])
                sink = jnp.where(hrow == 0, sink_ref[2 * j], sink_ref[2 * j + 1])
                s = lax.dot_general(qm, k_dup, _NT, preferred_element_type=jnp.float32)
                units.append((jnp.where(valid, s, NEG), v_dup, sink, j * LANES))
        for j in range(MEM_HEADS // 2):
            qm = heads_of(qc_ref[pl.ds(r0, T), j * LANES:(j + 1) * LANES])
            s = lax.dot_general(qm, mem_k[j], _NT, preferred_element_type=jnp.float32)
            units.append((s, mem_v[j], None, SWA_Q_HEADS * HEAD_DIM + j * LANES))
        for s, vals, sink, c0 in units:
            out, _, _ = _softmax_pv(s, vals, sink)
            o_ref[pl.ds(r0, T), c0:c0 + LANES] = jnp.where(lo, out[:T], out[T:]).astype(o_ref.dtype)
        return carry

    lax.fori_loop(0, L // T, block, 0)


def attn_swa_mem(sinks, qa, ka, va, qc, mk, mv, B, L):
    M = mk.shape[0] // B
    wa, wc = SWA_Q_HEADS * HEAD_DIM, MEM_HEADS * HEAD_DIM
    wka = ka.shape[1]
    n_win = min(SWA_WINDOW, L)
    row = lambda w: pl.BlockSpec((L, w), lambda b: (b, 0))
    fmaj = lambda f, n: pl.BlockSpec((None, f, n), lambda b: (b, 0, 0))
    f32 = jnp.float32
    return pl.pallas_call(
        functools.partial(_attn_swa_mem_body, L),
        grid=(B,),
        in_specs=[pl.BlockSpec(memory_space=pltpu.SMEM), row(wa), row(LANES), row(LANES), row(wc),
                  pl.BlockSpec((M, wc), lambda b: (b, 0)), pl.BlockSpec((M, wc), lambda b: (b, 0))],
        out_specs=[row(wa + wc), fmaj(wka, n_win), fmaj(wka, n_win), fmaj(wc, M), fmaj(wc, M)],
        out_shape=[jax.ShapeDtypeStruct((B * L, wa + wc), jnp.bfloat16),
                   jax.ShapeDtypeStruct((B, wka, n_win), f32), jax.ShapeDtypeStruct((B, wka, n_win), f32),
                   jax.ShapeDtypeStruct((B, wc, M), f32), jax.ShapeDtypeStruct((B, wc, M), f32)],
        scratch_shapes=[pltpu.VMEM((2, 2 * ATT_BLOCK, 2 * ATT_BLOCK), f32)],
        compiler_params=_cparams(("parallel",)), name="attn_swa_mem")(sinks, qa, ka, va, qc, mk, mv)


def _attn_dil_body(L, qb_ref, kb_ref, vb_ref, o_ref, kt_ref, vt_ref, og_ref, lse_ref, band_ref):
    bf16 = jnp.bfloat16
    lo, hi = _half_masks()
    halves = (lo, hi)
    T = ATT_BLOCK
    kt_ref[...] = kb_ref[...].T
    vt_ref[...] = vb_ref[...].T
    for g, (window, dil) in enumerate(DIL_PAIRS):
        lc = L // dil
        nbc = lc // T
        W = min(2 * T, lc)
        max_dist = window // dil
        qi = lax.broadcasted_iota(jnp.int32, (2 * T, W), 0) & (T - 1)
        kj = lax.broadcasted_iota(jnp.int32, (2 * T, W), 1)
        for case, off in enumerate((0, T)):
            dist = qi - kj + off
            band_ref[g, case, :, 0:W] = jnp.where((dist >= 0) & (dist <= max_dist), 1.0, 0.0)

        def scores(u, g=g, dil=dil, nbc=nbc, W=W):
            c = u >> (nbc.bit_length() - 1)
            n = u & (nbc - 1)
            wsc = jnp.maximum(n * T - T, 0) if W == 2 * T else 0
            q0 = c + dil * T * n
            k0 = c + dil * wsc
            valid = band_ref[g, jnp.minimum(n, 1), :, 0:W] > 0.5
            q = qb_ref[g, pl.ds(q0, T, stride=dil), :]
            k = kb_ref[pl.ds(k0, W, stride=dil), :]
            v = vb_ref[pl.ds(k0, W, stride=dil), :]
            qm = jnp.concatenate([jnp.where(halves[p], q, 0.0) for p in range(2)], axis=0).astype(bf16)
            s = lax.dot_general(qm, k.astype(bf16), _NT, preferred_element_type=jnp.float32)
            return jnp.where(valid, s, NEG), v.astype(bf16), q0

        def units(i, carry, g=g, dil=dil):
            staged = [scores(i * DIL_CHUNK + t) for t in range(DIL_CHUNK)]
            for s, v, q0 in staged:
                out, m, den = _softmax_pv(s, v)
                lse = m + jnp.log(den)
                og_ref[g, pl.ds(q0, T, stride=dil), :] = jnp.where(lo, out[:T], out[T:])
                lse_ref[g, pl.ds(q0, T, stride=dil), :] = jnp.where(lo, lse[:T], lse[T:])
            return carry

        lax.fori_loop(0, dil * nbc // DIL_CHUNK, units, 0, unroll=True)

    def merge(i, carry):
        r0 = pl.multiple_of(i * T, T)
        ls = [lse_ref[g, pl.ds(r0, T), :] for g in range(len(DIL_PAIRS))]
        m = jnp.maximum(jnp.maximum(ls[0], ls[1]), ls[2])
        ws = [jnp.exp(l - m) for l in ls]
        tot = ws[0] + ws[1] + ws[2]
        out = sum((w / tot) * og_ref[g, pl.ds(r0, T), :] for g, w in enumerate(ws))
        o_ref[pl.ds(r0, T), :] = out.astype(o_ref.dtype)
        return carry

    lax.fori_loop(0, L // T, merge, 0)


def attn_dilated(qb, kb, vb, B, L):
    ng = len(DIL_PAIRS)
    fmaj = pl.BlockSpec((None, LANES, L), lambda b, hp: (b, hp, 0))
    return pl.pallas_call(
        functools.partial(_attn_dil_body, L),
        grid=(B, 2),
        in_specs=[pl.BlockSpec((ng, L, LANES), lambda b, hp: (hp, b, 0)),
                  pl.BlockSpec((L, LANES), lambda b, hp: (b, hp)),
                  pl.BlockSpec((L, LANES), lambda b, hp: (b, hp))],
        out_specs=[pl.BlockSpec((L, LANES), lambda b, hp: (b, hp)), fmaj, fmaj],
        out_shape=[jax.ShapeDtypeStruct((B * L, 2 * LANES), jnp.bfloat16),
                   jax.ShapeDtypeStruct((B, 2 * LANES, L), jnp.float32),
                   jax.ShapeDtypeStruct((B, 2 * LANES, L), jnp.float32)],
        scratch_shapes=[pltpu.VMEM((ng, L, LANES), jnp.float32), pltpu.VMEM((ng, L, LANES), jnp.float32),
                        pltpu.VMEM((ng, 2, 2 * ATT_BLOCK, 2 * ATT_BLOCK), jnp.float32)],
        compiler_params=_cparams(("parallel", "parallel")), name="attn_dilated")(qb, kb, vb)


N_NEW = 4
SROWS = 8


def _softmax2_pv(s_c, s_n, vt_c, vt_n, sink=None):
    m = jnp.maximum(jnp.max(s_c, axis=-1, keepdims=True), jnp.max(s_n, axis=-1, keepdims=True))
    if sink is not None:
        m = jnp.maximum(m, sink)
    e_c = jnp.exp(s_c - m)
    e_n = jnp.exp(s_n - m)
    den = jnp.sum(e_c, axis=-1, keepdims=True) + jnp.sum(e_n, axis=-1, keepdims=True)
    if sink is not None:
        den = den + jnp.exp(sink - m)
    r = (lax.dot_general(e_c.astype(jnp.bfloat16), vt_c, _NT, preferred_element_type=jnp.float32)
         + lax.dot_general(e_n.astype(jnp.bfloat16), vt_n, _NT, preferred_element_type=jnp.float32))
    return r / den, m, den


def _advance(old_t, new_t):
    n = old_t.shape[1]
    lane = lax.broadcasted_iota(jnp.int32, (1, LANES), 1)
    shifted = pltpu.roll(old_t, n - N_NEW, axis=1)
    last = jnp.where(lane < LANES - N_NEW, shifted[:, n - LANES:], new_t)
    if n == LANES:
        return last
    return jnp.concatenate([shifted[:, :n - LANES], last], axis=1)


def _attn_sample_body(bt, sink_ref, qa_ref, qb_ref, qc_ref, nka_ref, nva_ref, nkb_ref, nvb_ref,
                      cak_ref, cav_ref, cbk_ref, cbv_ref, cmk_ref, cmv_ref,
                      oac_ref, ob_ref, oak_ref, oav_ref, obk_ref, obv_ref):
    f32, bf16 = jnp.float32, jnp.bfloat16
    lo, hi = _half_masks()
    halves = (lo, hi)
    S = SROWS
    la = cak_ref.shape[2]
    lb = cbk_ref.shape[2]
    wa = SWA_Q_HEADS * HEAD_DIM
    new0 = LANES - N_NEW

    na = SWA_Q_HEADS * S
    ia = lax.broadcasted_iota(jnp.int32, (na, la), 0) & (S - 1)
    valid_ac = lax.broadcasted_iota(jnp.int32, (na, la), 1) >= ia + 1
    ja = lax.broadcasted_iota(jnp.int32, (na, LANES), 1) - new0
    valid_an = (ja >= 0) & (ja <= (lax.broadcasted_iota(jnp.int32, (na, LANES), 0) & (S - 1)))
    rcol = lax.broadcasted_iota(jnp.int32, (na, 1), 0)
    sink_col = jnp.zeros((na, 1), f32)
    for h in range(SWA_Q_HEADS):
        sink_col = jnp.where((rcol >> 3) == h, sink_ref[h], sink_col)

    nb_rows = len(DIL_PAIRS) * 2 * S
    rb = lax.broadcasted_iota(jnp.int32, (nb_rows, lb), 0)
    t_c = lb + (rb & (S - 1)) - lax.broadcasted_iota(jnp.int32, (nb_rows, lb), 1)
    rn = lax.broadcasted_iota(jnp.int32, (nb_rows, LANES), 0)
    jn = lax.broadcasted_iota(jnp.int32, (nb_rows, LANES), 1) - new0
    t_n = (rn & (S - 1)) - jn
    valid_bc = jnp.zeros((nb_rows, lb), jnp.bool_)
    valid_bn = jnp.zeros((nb_rows, LANES), jnp.bool_)
    for g, (window, dil) in enumerate(DIL_PAIRS):
        valid_bc = valid_bc | (((rb >> 4) == g) & (t_c <= window) & ((t_c & (dil - 1)) == 0))
        valid_bn = valid_bn | (((rn >> 4) == g) & (jn >= 0) & (t_n >= 0) & ((t_n & (dil - 1)) == 0))

    def new_t(x):
        padded = jnp.concatenate([x, jnp.zeros((LANES - S, x.shape[1]), f32)], axis=0)
        return pltpu.roll(padded.T, new0, axis=1)

    for b in range(bt):
        rows = slice(b * S, (b + 1) * S)
        nka, nva = new_t(nka_ref[rows, :]), new_t(nva_ref[rows, :])
        nkb, nvb = new_t(nkb_ref[rows, :]), new_t(nvb_ref[rows, :])
        oak_ref[b] = _advance(cak_ref[b], nka)
        oav_ref[b] = _advance(cav_ref[b], nva)
        obk_ref[b] = _advance(cbk_ref[b], nkb)
        obv_ref[b] = _advance(cbv_ref[b], nvb)

        pieces = []
        for h in range(SWA_Q_HEADS):
            q = jnp.where(halves[h % 2], qa_ref[rows, (h // 2) * LANES:(h // 2 + 1) * LANES].astype(f32), 0.0)
            if h % 2 != h // 4:
                q = pltpu.roll(q, HEAD_DIM, axis=1)
            pieces.append(q)
        qm = jnp.concatenate(pieces, axis=0).astype(bf16)
        s_c = jnp.dot(qm, cak_ref[b].astype(bf16), preferred_element_type=f32)
        s_n = jnp.dot(qm, nka.astype(bf16), preferred_element_type=f32)
        out, _, _ = _softmax2_pv(jnp.where(valid_ac, s_c, NEG), jnp.where(valid_an, s_n, NEG),
                                 cav_ref[b].astype(bf16), nva.astype(bf16), sink_col)
        for j in range(SWA_Q_HEADS // 2):
            parts = []
            for p in range(2):
                h = 2 * j + p
                o = out[h * S:(h + 1) * S]
                if h % 2 != h // 4:
                    o = pltpu.roll(o, HEAD_DIM, axis=1)
                parts.append(o)
            oac_ref[rows, j * LANES:(j + 1) * LANES] = jnp.where(lo, parts[0], parts[1])

        for j in range(MEM_HEADS // 2):
            cols = slice(j * LANES, (j + 1) * LANES)
            q = qc_ref[rows, cols]
            qm = jnp.concatenate([jnp.where(halves[p], q, jnp.zeros_like(q)) for p in range(2)], axis=0)
            s = jnp.dot(qm, cmk_ref[b, cols, :].astype(bf16), preferred_element_type=f32)
            m = jnp.max(s, axis=-1, keepdims=True)
            e = jnp.exp(s - m)
            r = lax.dot_general(e.astype(bf16), cmv_ref[b, cols, :].astype(bf16), _NT, preferred_element_type=f32)
            out = r / jnp.sum(e, axis=-1, keepdims=True)
            oac_ref[rows, wa + j * LANES:wa + (j + 1) * LANES] = jnp.where(lo, out[:S], out[S:])

        for hp in range(DIL_KV_HEADS // 2):
            cols = slice(hp * LANES, (hp + 1) * LANES)
            pieces = [jnp.where(halves[p], qb_ref[hp * len(DIL_PAIRS) + g, rows, :], 0.0)
                      for g in range(len(DIL_PAIRS)) for p in range(2)]
            qm = jnp.concatenate(pieces, axis=0).astype(bf16)
            s_c = jnp.dot(qm, cbk_ref[b, cols, :].astype(bf16), preferred_element_type=f32)
            s_n = jnp.dot(qm, nkb[cols, :].astype(bf16), preferred_element_type=f32)
            out, m, den = _softmax2_pv(jnp.where(valid_bc, s_c, NEG), jnp.where(valid_bn, s_n, NEG),
                                       cbv_ref[b, cols, :].astype(bf16), nvb[cols, :].astype(bf16))
            lse = m + jnp.log(den)
            res = []
            for p in range(2):
                r = [slice((g * 2 + p) * S, (g * 2 + p + 1) * S) for g in range(len(DIL_PAIRS))]
                mx = jnp.maximum(jnp.maximum(lse[r[0]], lse[r[1]]), lse[r[2]])
                w = [jnp.exp(lse[x] - mx) for x in r]
                tot = w[0] + w[1] + w[2]
                res.append(sum((w[g] / tot) * out[r[g]] for g in range(len(DIL_PAIRS))))
            ob_ref[rows, cols] = jnp.where(lo, res[0], res[1])


def attn_sample(sinks, qa, qb, qc, nka, nva, nkb, nvb, cak, cav, cbk, cbv, cmk, cmv, bt):
    NB, wka, la = cak.shape
    wkb, lb = cbk.shape[1:]
    wm, M = cmk.shape[1:]
    wa, wc = SWA_Q_HEADS * HEAD_DIM, MEM_HEADS * HEAD_DIM
    ng = len(DIL_PAIRS)
    tok = lambda w: pl.BlockSpec((bt * SROWS, w), lambda i: (i, 0))
    buf = lambda f, n: pl.BlockSpec((bt, f, n), lambda i: (i, 0, 0))
    f32 = jnp.float32
    return pl.pallas_call(
        functools.partial(_attn_sample_body, bt),
        grid=(NB // bt,),
        in_specs=[pl.BlockSpec(memory_space=pltpu.SMEM), tok(wa),
                  pl.BlockSpec((2 * ng, bt * SROWS, LANES), lambda i: (0, i, 0)), tok(wc),
                  tok(wka), tok(wka), tok(wkb), tok(wkb),
                  buf(wka, la), buf(wka, la), buf(wkb, lb), buf(wkb, lb), buf(wm, M), buf(wm, M)],
        out_specs=[tok(wa + wc), tok(wkb), buf(wka, la), buf(wka, la), buf(wkb, lb), buf(wkb, lb)],
        out_shape=[jax.ShapeDtypeStruct((NB * SROWS, wa + wc), f32), jax.ShapeDtypeStruct((NB * SROWS, wkb), f32),
                   jax.ShapeDtypeStruct(cak.shape, f32), jax.ShapeDtypeStruct(cak.shape, f32),
                   jax.ShapeDtypeStruct(cbk.shape, f32), jax.ShapeDtypeStruct(cbk.shape, f32)],
        compiler_params=_cparams(("parallel",)), name="attn_sample")(
            sinks, qa, qb, qc, nka, nva, nkb, nvb, cak, cav, cbk, cbv, cmk, cmv)


MOE_BLOCK = 512
ROUTE_SUB = 256
SUBLANES = 8


def _store_row_tiles(ref, row0, y):
    n = y.shape[0]
    for c in range(SUBLANES):
        ref[pl.ds(row0 * SUBLANES + c, n, stride=SUBLANES), :] = y[:, c * LANES:(c + 1) * LANES]


def _load_row_tiles(ref, row0, n):
    return jnp.concatenate([ref[pl.ds(row0 * SUBLANES + c, n, stride=SUBLANES), :] for c in range(SUBLANES)],
                           axis=1)


def _row_tile(ref, r):
    return ref.at[pl.ds(pl.multiple_of(r * SUBLANES, SUBLANES), SUBLANES)]


def _merge_route_body(oac_ref, ob_ref, gate_ref, x_ref, wa_ref, wb_ref, wc_ref, wo_ref, gffn_ref, wr_ref, br_ref,
                      cnt0_ref, x1_ref, h2_ref, mi_ref, mf_ref, cnt_ref, base_ref):
    f32, bf16 = jnp.float32, jnp.bfloat16
    D = x_ref.shape[1]
    wa = SWA_Q_HEADS * HEAD_DIM
    ts = ROUTE_SUB

    @pl.when(pl.program_id(0) == 0)
    def _():
        base_ref[...] = cnt0_ref[...]

    erow = lax.broadcasted_iota(jnp.int32, (N_EXPERTS, ts), 0)
    r8 = lax.broadcasted_iota(jnp.int32, (SUBLANES, ts), 0)
    ti = lax.broadcasted_iota(jnp.int32, (ts, ts), 0)
    tj = lax.broadcasted_iota(jnp.int32, (ts, ts), 1)
    later = (ti < tj).astype(bf16)
    base = base_ref[:, 0:1]
    n_sub = x_ref.shape[0] // ts
    mixed, normed = [], []
    for sub in range(n_sub):
        rows = slice(sub * ts, (sub + 1) * ts)
        ma = jnp.dot(oac_ref[rows, :wa].astype(bf16), wa_ref[...], preferred_element_type=f32)
        mb = jnp.dot(ob_ref[rows, :].astype(bf16), wb_ref[...], preferred_element_type=f32)
        mc = jnp.dot(oac_ref[rows, wa:].astype(bf16), wc_ref[...], preferred_element_type=f32)
        merged = (gate_ref[rows, :D].astype(f32) * ma + gate_ref[rows, D:2 * D].astype(f32) * mb
                  + gate_ref[rows, 2 * D:].astype(f32) * mc)
        mixed.append(merged.astype(bf16))

    for sub in range(n_sub):
        rows = slice(sub * ts, (sub + 1) * ts)
        x1 = x_ref[rows, :] + jnp.dot(mixed[sub], wo_ref[...], preferred_element_type=f32)
        x1_ref[rows, :] = x1
        h2 = _rms(x1, gffn_ref[...])
        _store_row_tiles(h2_ref, sub * ts, h2)
        normed.append(h2.astype(bf16))

    for sub in range(n_sub):
        rows = slice(sub * ts, (sub + 1) * ts)
        work = lax.dot_general(wr_ref[...], normed[sub], _NT, preferred_element_type=f32) + br_ref[:, 0:1]
        vals, idxs = [], []
        for _ in range(TOP_K):
            m = jnp.max(work, axis=0, keepdims=True)
            idx = jnp.min(jnp.where(work == m, erow, N_EXPERTS), axis=0, keepdims=True)
            vals.append(m)
            idxs.append(idx)
            work = jnp.where(erow == idx, -jnp.inf, work)
        es = [jnp.exp(v - vals[0]) for v in vals]
        tot = es[0] + es[1] + es[2] + es[3]

        onehot = [(erow == idx).astype(f32) for idx in idxs]
        assign = onehot[0] + onehot[1] + onehot[2] + onehot[3]
        before = jnp.dot(assign.astype(bf16), later, preferred_element_type=f32) + base
        base = base + jnp.sum(assign, axis=1, keepdims=True)

        mi = jnp.zeros((SUBLANES, ts), jnp.int32)
        gates = jnp.zeros((SUBLANES, ts), f32)
        for k in range(TOP_K):
            rank = jnp.sum(onehot[k] * before, axis=0, keepdims=True).astype(jnp.int32)
            mi = jnp.where(r8 == k, idxs[k], mi)
            mi = jnp.where(r8 == TOP_K + k, rank, mi)
            gates = jnp.where(r8 == k, es[k] / tot, gates)
        mi_ref[:, rows] = mi
        mf_ref[rows, :] = jnp.concatenate([gates, jnp.zeros((LANES - SUBLANES, ts), f32)], axis=0).T
    base_ref[...] = jnp.broadcast_to(base, base_ref.shape)
    cnt_ref[...] = jnp.broadcast_to(base, cnt_ref.shape)


def merge_route(o_ac, o_b, gates, x, wa, wb, wc, wo, g_ffn, wr, br, cnt0, tm):
    R, D = x.shape
    row = lambda w: pl.BlockSpec((tm, w), lambda i: (i, 0))
    full = lambda a: pl.BlockSpec(a.shape, lambda i: (0, 0))
    return pl.pallas_call(
        _merge_route_body,
        grid=(R // tm,),
        in_specs=[row(o_ac.shape[1]), row(o_b.shape[1]), row(gates.shape[1]), row(D),
                  full(wa), full(wb), full(wc), full(wo), full(g_ffn), full(wr), full(br), full(cnt0)],
        out_specs=[row(D), pl.BlockSpec((tm * SUBLANES, LANES), lambda i: (i, 0)),
                   pl.BlockSpec((SUBLANES, tm), lambda i: (0, i)), row(LANES),
                   pl.BlockSpec((N_EXPERTS, LANES), lambda i: (0, 0))],
        out_shape=[jax.ShapeDtypeStruct((R, D), jnp.float32), jax.ShapeDtypeStruct((R * SUBLANES, LANES), jnp.float32),
                   jax.ShapeDtypeStruct((SUBLANES, R), jnp.int32), jax.ShapeDtypeStruct((R, LANES), jnp.float32),
                   jax.ShapeDtypeStruct((N_EXPERTS, LANES), jnp.float32)],
        scratch_shapes=[pltpu.VMEM((N_EXPERTS, LANES), jnp.float32)],
        compiler_params=_cparams(("arbitrary",)), name="merge_route")(
            o_ac, o_b, gates, x, wa, wb, wc, wo, g_ffn, wr, br, cnt0)


def _route_tables_body(cnt_ref, mi_ref, dest_ref, blk_ref, pad_ref):
    tm = mi_ref.shape[1]
    nbl = blk_ref.shape[1]
    erow1 = lax.broadcasted_iota(jnp.int32, (N_EXPERTS, LANES), 0)
    shift = MOE_BLOCK.bit_length() - 1
    cnt = cnt_ref[...].astype(jnp.int32)
    padded = ((cnt + (MOE_BLOCK - 1)) >> shift) << shift
    pend = padded
    s = 1
    while s < N_EXPERTS:
        pend = pend + jnp.where(erow1 >= s, pltpu.roll(pend, s, axis=0), 0)
        s *= 2
    pstart = pend - padded
    mi = mi_ref[...]
    erow = lax.broadcasted_iota(jnp.int32, (N_EXPERTS, tm), 0)
    r8 = lax.broadcasted_iota(jnp.int32, (SUBLANES, tm), 0)
    dest = jnp.zeros((SUBLANES, tm), jnp.int32)
    for k in range(TOP_K):
        start = jnp.sum(jnp.where(erow == mi[k:k + 1, :], pstart[:, 0:1], 0), axis=0, keepdims=True)
        dest = jnp.where(r8 == k, start + mi[TOP_K + k:TOP_K + k + 1, :], dest)
    dest_ref[...] = dest

    @pl.when(pl.program_id(0) == 0)
    def _():
        row0 = lax.broadcasted_iota(jnp.int32, (N_EXPERTS, nbl), 1) * MOE_BLOCK
        ended = jnp.sum(jnp.where(pend[:, 0:1] <= row0, 1, 0), axis=0, keepdims=True)
        blk_ref[...] = jnp.broadcast_to(jnp.minimum(ended, N_EXPERTS - 1), blk_ref.shape)
        pad_ref[0] = pstart + cnt
        pad_ref[1] = pend


def route_tables(cnt, mi, nbl, tm):
    R = mi.shape[1]
    return pl.pallas_call(
        _route_tables_body,
        grid=(R // tm,),
        in_specs=[pl.BlockSpec((N_EXPERTS, LANES), lambda i: (0, 0)), pl.BlockSpec((SUBLANES, tm), lambda i: (0, i))],
        out_specs=[pl.BlockSpec((SUBLANES, tm), lambda i: (0, i)), pl.BlockSpec((SUBLANES, nbl), lambda i: (0, 0)),
                   pl.BlockSpec((2, N_EXPERTS, LANES), lambda i: (0, 0, 0))],
        out_shape=[jax.ShapeDtypeStruct((SUBLANES, R), jnp.int32), jax.ShapeDtypeStruct((SUBLANES, nbl), jnp.int32),
                   jax.ShapeDtypeStruct((2, N_EXPERTS, LANES), jnp.int32)],
        compiler_params=_cparams(("arbitrary",)), name="route_tables")(cnt, mi)


def _dispatch_body(n_first, n_steps, pad0_ref, pad1_ref, dest_ref, ha_ref, hb_ref, xs_ref, zero_ref, sem):
    i = pl.program_id(0)
    tm = ha_ref.shape[0] // SUBLANES
    rs = xs_ref.shape[0] // SUBLANES

    def scatter_tile(h_ref):
        def row_copy(r, k):
            return pltpu.make_async_copy(_row_tile(h_ref, r), _row_tile(xs_ref, dest_ref[0, 0, k * tm + r]), sem)

        def start(r, c):
            for k in range(TOP_K):
                row_copy(r, k).start(priority=k % 2)
            return c

        def wait(r, c):
            for k in range(TOP_K):
                row_copy(r, k).wait()
            return c

        lax.fori_loop(0, tm, start, 0, unroll=8)
        lax.fori_loop(0, tm, wait, 0, unroll=8)

    @pl.when(i < n_first)
    def _():
        scatter_tile(ha_ref)

    @pl.when(i >= n_first)
    def _():
        scatter_tile(hb_ref)

    @pl.when(i == n_steps - 1)
    def _():
        zero_ref[...] = jnp.zeros_like(zero_ref)
        zrows = zero_ref.shape[0] // SUBLANES

        def zero_range(lo, hi, go):
            n_big = (hi - lo) // zrows

            def big(j, c):
                start = pl.multiple_of((lo + j * zrows) * SUBLANES, SUBLANES)
                cp = pltpu.make_async_copy(zero_ref, xs_ref.at[pl.ds(start, zrows * SUBLANES)], sem)
                cp.start() if go else cp.wait()
                return c

            def one(r, c):
                cp = pltpu.make_async_copy(zero_ref.at[pl.ds(0, SUBLANES)], _row_tile(xs_ref, r), sem)
                cp.start() if go else cp.wait()
                return c

            lax.fori_loop(0, n_big, big, 0)
            lax.fori_loop(lo + n_big * zrows, hi, one, 0)

        for go in (True, False):
            for e in range(N_EXPERTS):
                zero_range(pad0_ref[e], pad1_ref[e], go)
            zero_range(pad1_ref[N_EXPERTS - 1], rs, go)


def dispatch(pad0, pad1, dest3, h_a, h_b, rs, tm):
    S = SUBLANES
    n_a, n_b = h_a.shape[0] // (tm * S), h_b.shape[0] // (tm * S)
    grid_spec = pltpu.PrefetchScalarGridSpec(
        num_scalar_prefetch=2,
        grid=(n_a + n_b,),
        in_specs=[pl.BlockSpec((1, 1, TOP_K * tm), lambda i, p0, p1: (i, 0, 0), memory_space=pltpu.SMEM),
                  pl.BlockSpec((tm * S, LANES), lambda i, p0, p1: (jnp.minimum(i, n_a - 1), 0)),
                  pl.BlockSpec((tm * S, LANES), lambda i, p0, p1: (jnp.maximum(i - n_a, 0), 0))],
        out_specs=pl.BlockSpec(memory_space=pl.ANY),
        scratch_shapes=[pltpu.VMEM((16 * S, LANES), h_a.dtype), pltpu.SemaphoreType.DMA])
    return pl.pallas_call(
        functools.partial(_dispatch_body, n_a, n_a + n_b), grid_spec=grid_spec,
        out_shape=jax.ShapeDtypeStruct((rs * S, LANES), h_a.dtype),
        compiler_params=_cparams(("arbitrary",)), name="dispatch")(pad0, pad1, dest3, h_a, h_b)


def _moe_body(nblk, be_ref, nact_ref, x_ref, wgu_hbm, bgu_ref, wd_hbm, bd_ref, y_ref, wgu_f, wd_f, wgu_s, wd_s,
              sems):
    f32, bf16 = jnp.float32, jnp.bfloat16
    b = pl.program_id(0)
    e = be_ref[b]

    def fetch(ex, go):
        for src, dst, s in ((wgu_hbm, wgu_f, 0), (wd_hbm, wd_f, 1)):
            cp = pltpu.make_async_copy(src.at[ex], dst, sems.at[s])
            cp.start() if go else cp.wait()

    @pl.when(b == 0)
    def _():
        fetch(e, True)

    @pl.when((b == 0) | (e != be_ref[jnp.maximum(b - 1, 0)]))
    def _():
        fetch(e, False)
        wgu_s[...] = wgu_f[...].astype(bf16)
        wd_s[...] = wd_f[...].astype(bf16)
        nxt = lax.while_loop(lambda j: (j < nblk) & (be_ref[jnp.minimum(j, nblk - 1)] == e), lambda j: j + 1, b + 1)

        @pl.when(nxt < nblk)
        def _():
            fetch(be_ref[jnp.minimum(nxt, nblk - 1)], True)

    @pl.when(b < nact_ref[0])
    def _():
        x = _load_row_tiles(x_ref, 0, MOE_BLOCK).astype(bf16)
        gu = jnp.dot(x, wgu_s[...], preferred_element_type=f32) + bgu_ref[0]
        gt = jnp.minimum(gu[:, :D_FF], SWIGLU_LIMIT)
        up = jnp.clip(gu[:, D_FF:], -SWIGLU_LIMIT, SWIGLU_LIMIT)
        act = (up + 1.0) * (gt * jax.nn.sigmoid(gt * SWIGLU_ALPHA))
        y = jnp.dot(act.astype(bf16), wd_s[...], preferred_element_type=f32) + bd_ref[0]
        _store_row_tiles(y_ref, 0, y)

    @pl.when(b >= nact_ref[0])
    def _():
        y_ref[...] = jnp.zeros_like(y_ref)


def moe_ffn(blk_e, n_active, xs, w_gate_up, b_gate_up, w_down, b_down):
    RS = xs.shape[0] // SUBLANES
    E, D, F2 = w_gate_up.shape
    blk_rows = MOE_BLOCK * SUBLANES
    grid_spec = pltpu.PrefetchScalarGridSpec(
        num_scalar_prefetch=2,
        grid=(RS // MOE_BLOCK,),
        in_specs=[pl.BlockSpec((blk_rows, LANES), lambda b, be, na: (b, 0)),
                  pl.BlockSpec(memory_space=pl.ANY),
                  pl.BlockSpec((1, 1, F2), lambda b, be, na: (be[b], 0, 0)),
                  pl.BlockSpec(memory_space=pl.ANY),
                  pl.BlockSpec((1, 1, D), lambda b, be, na: (be[b], 0, 0))],
        out_specs=pl.BlockSpec((blk_rows, LANES), lambda b, be, na: (b, 0)),
        scratch_shapes=[pltpu.VMEM((D, F2), w_gate_up.dtype), pltpu.VMEM((F2 // 2, D), w_down.dtype),
                        pltpu.VMEM((D, F2), jnp.bfloat16), pltpu.VMEM((F2 // 2, D), jnp.bfloat16),
                        pltpu.SemaphoreType.DMA((2,))])
    return pl.pallas_call(
        functools.partial(_moe_body, RS // MOE_BLOCK), grid_spec=grid_spec,
        out_shape=jax.ShapeDtypeStruct(xs.shape, jnp.float32),
        compiler_params=_cparams(("arbitrary",)), name="moe_ffn")(
            blk_e, n_active, xs, w_gate_up, b_gate_up.reshape(E, 1, F2), w_down, b_down.reshape(E, 1, D))


def _combine_body(n_tiles, dest_ref, dest_next_ref, x1_ref, mf_ref, g_ref, ys_ref, o_ref, buf, sems):
    tm = x1_ref.shape[0]
    i = pl.program_id(0)
    slot = i % 2
    per_slot = TOP_K * tm

    def gather(d_ref, s, go):
        def row_copy(r, k):
            return pltpu.make_async_copy(_row_tile(ys_ref, d_ref[0, 0, k * tm + r]),
                                         _row_tile(buf, s * per_slot + k * tm + r), sems.at[s])

        def body(r, c):
            for k in range(TOP_K):
                if go:
                    row_copy(r, k).start(priority=k % 2)
                else:
                    row_copy(r, k).wait()
            return c

        lax.fori_loop(0, tm, body, 0, unroll=8)

    @pl.when(i == 0)
    def _():
        gather(dest_ref, slot, True)

    @pl.when(i + 1 < n_tiles)
    def _():
        gather(dest_next_ref, 1 - slot, True)

    gather(dest_ref, slot, False)
    y = x1_ref[...]
    for k in range(TOP_K):
        y = y + mf_ref[:, k:k + 1] * _load_row_tiles(buf, slot * per_slot + k * tm, tm)
    o_ref[...] = _rms(y, g_ref[...])


def combine(dest3, x1, mf, g_final, ys, tm):
    R, D = x1.shape
    n = R // tm
    dest_spec = lambda f: pl.BlockSpec((1, 1, TOP_K * tm), f, memory_space=pltpu.SMEM)
    return pl.pallas_call(
        functools.partial(_combine_body, n),
        grid=(n,),
        in_specs=[dest_spec(lambda i: (i, 0, 0)), dest_spec(lambda i: (jnp.minimum(i + 1, n - 1), 0, 0)),
                  pl.BlockSpec((tm, D), lambda i: (i, 0)),
                  pl.BlockSpec((tm, LANES), lambda i: (i, 0)),
                  pl.BlockSpec((1, D), lambda i: (0, 0)),
                  pl.BlockSpec(memory_space=pl.ANY)],
        out_specs=pl.BlockSpec((tm, D), lambda i: (i, 0)),
        out_shape=jax.ShapeDtypeStruct((R, D), jnp.float32),
        scratch_shapes=[pltpu.VMEM((2 * TOP_K * tm * SUBLANES, LANES), ys.dtype), pltpu.SemaphoreType.DMA((2,))],
        compiler_params=_cparams(("arbitrary",)), name="combine")(dest3, dest3, x1, mf, g_final, ys)


def moe_layer(group_a, group_b, cnt, g_final, w_gate_up, b_gate_up, w_down, b_down, tm):
    n_assign = (group_a[0].shape[0] + group_b[0].shape[0]) * TOP_K
    nb = (n_assign + N_EXPERTS * (MOE_BLOCK - 1)) // MOE_BLOCK + 1
    nbl = -(-nb // LANES) * LANES
    dests = []
    for x1, h2, mi, mf in (group_a, group_b):
        dest, blk, pad = route_tables(cnt, mi, nbl, min(x1.shape[0], 4 * tm))
        n_tiles = x1.shape[0] // tm
        dests.append(jnp.transpose(dest[:TOP_K].reshape(TOP_K, n_tiles, tm), (1, 0, 2)).reshape(n_tiles, 1, TOP_K * tm))
    xs = dispatch(pad[0, :, 0], pad[1, :, 0], jnp.concatenate(dests), group_a[1], group_b[1], nb * MOE_BLOCK, tm)
    n_active = (pad[1, N_EXPERTS - 1, 0] // MOE_BLOCK).reshape(1)
    ys = moe_ffn(blk[0, :nb], n_active, xs, w_gate_up, b_gate_up, w_down, b_down)
    return [combine(dest3, x1, mf, g_final, ys, tm) for dest3, (x1, h2, mi, mf) in zip(dests, (group_a, group_b))]


def kernel(x_prompt, x_sample, cache_swa_k, cache_swa_v, cache_dil_k, cache_dil_v, cache_mem_k, cache_mem_v, mem_prompt, norm_attn, norm_mem, w_in, w_mem_kv, sinks, w_br_a, w_br_b, w_br_c, w_out, norm_ffn, w_router, b_router, w_gate_up, b_gate_up, w_down, b_down, norm_final):
    f32, bf16 = jnp.float32, jnp.bfloat16
    TM = 256
    B, L, D = x_prompt.shape
    NB, n_new, _ = x_sample.shape
    M = mem_prompt.shape[1]
    wc = MEM_HEADS * HEAD_DIM
    assert n_new == N_NEW and cache_swa_k.shape[0] == 1

    w_in_b = w_in[0].astype(bf16)
    g_attn = norm_attn[0].reshape(1, D)
    secs = in_sections()

    tabs_p = rope_tables(jnp.arange(L, dtype=jnp.int32))
    qa, ka, va, qb, kb, vb, qc, gates = norm_proj(x_prompt.reshape(B * L, D), g_attn, w_in_b, tabs_p, secs, TM)
    mk, mv = norm_proj(mem_prompt.reshape(B * M, D), norm_mem[0].reshape(1, D), w_mem_kv[0].astype(bf16), None,
                       [(_chunks(0, wc), "plain", False, f32), (_chunks(wc, wc), "plain", False, f32)], TM)
    o_ac, ka_t, va_t, mk_t, mv_t = attn_swa_mem(sinks[0], qa, ka, va, qc, mk, mv, B, L)
    o_b, kb_t, vb_t = attn_dilated(qb, kb, vb, B, L)

    xs_pad = jnp.pad(x_sample, ((0, 0), (0, SROWS - N_NEW), (0, 0))).reshape(NB * SROWS, D)
    tabs_s = rope_tables(PAST_LEN + (jnp.arange(TM, dtype=jnp.int32) % SROWS))
    qa_s, ka_s, va_s, qb_s, kb_s, vb_s, qc_s, gates_s = norm_proj(xs_pad, g_attn, w_in_b, tabs_s, secs, TM)
    real = lambda t: t.reshape(NB, SROWS, -1)[:, :N_NEW].reshape(NB * N_NEW, -1)
    fmaj = lambda c: jnp.transpose(c[0], (0, 2, 3, 1)).reshape(NB, -1, c.shape[2])
    o_ac_s, o_b_s, swa_k_s, swa_v_s, dil_k_s, dil_v_s = attn_sample(
        sinks[0], qa_s, qb_s, qc_s, ka_s, va_s, kb_s, vb_s,
        fmaj(cache_swa_k), fmaj(cache_swa_v), fmaj(cache_dil_k), fmaj(cache_dil_v),
        fmaj(cache_mem_k), fmaj(cache_mem_v), 2)

    wr = w_router[0].T.astype(bf16)
    br = jnp.broadcast_to(b_router[0].astype(f32)[:, None], (N_EXPERTS, LANES))
    wts = (w_br_a[0].astype(bf16), w_br_b[0].astype(bf16), w_br_c[0].astype(bf16), w_out[0].astype(bf16),
           norm_ffn[0].reshape(1, D), wr, br)
    x1_p, h2_p, mi_p, mf_p, cnt_p = merge_route(o_ac, o_b, gates, x_prompt.reshape(B * L, D), *wts,
                                                jnp.zeros((N_EXPERTS, LANES), f32), 2 * ROUTE_SUB)
    x1_s, h2_s, mi_s, mf_s, cnt = merge_route(real(o_ac_s), real(o_b_s), real(gates_s),
                                              x_sample.reshape(NB * N_NEW, D), *wts, cnt_p, 2 * ROUTE_SUB)
    y_p, y_s = moe_layer((x1_p, h2_p, mi_p, mf_p), (x1_s, h2_s, mi_s, mf_s), cnt, norm_final.reshape(1, D),
                         w_gate_up[0], b_gate_up[0], w_down[0], b_down[0], TM)

    tmaj = lambda t, h: jnp.transpose(t.reshape(t.shape[0], h, HEAD_DIM, t.shape[2]), (0, 3, 1, 2))[None]
    return (y_p.reshape(B, L, D), y_s.reshape(NB, N_NEW, D),
            tmaj(ka_t, SWA_KV_HEADS), tmaj(va_t, SWA_KV_HEADS),
            tmaj(kb_t, DIL_KV_HEADS), tmaj(vb_t, DIL_KV_HEADS),
            tmaj(mk_t, MEM_HEADS), tmaj(mv_t, MEM_HEADS),
            tmaj(swa_k_s, SWA_KV_HEADS), tmaj(swa_v_s, SWA_KV_HEADS),
            tmaj(dil_k_s, DIL_KV_HEADS), tmaj(dil_v_s, DIL_KV_HEADS))
```

```python
import functools

import jax
import jax.numpy as jnp
from jax import lax
from jax.experimental import pallas as pl
from jax.experimental.pallas import tpu as pltpu

D_MODEL = 1024
HEAD_DIM = 64
ROPE_DIM = 16
ROPE_HALF = 8
ROPE_THETA = 500000.0
PAST_LEN = 16384
SWA_Q_HEADS = 8
SWA_KV_HEADS = 2
SWA_WINDOW = 128
DIL_PAIRS = ((128, 1), (512, 4), (2048, 16))
DIL_KV_HEADS = 4
MEM_HEADS = 4
N_EXPERTS = 32
TOP_K = 4
D_FF = 1024
SWIGLU_LIMIT = 7.0
SWIGLU_ALPHA = 1.702
RMS_EPS = 1e-5
ATT_BLOCK = 128
DIL_CHUNK = 4
SCALE = HEAD_DIM ** -0.5

LANES = 128
NEG = -1e30
VMEM_LIMIT = 56 * 1024 * 1024


def _cparams(sem):
    return pltpu.CompilerParams(dimension_semantics=sem, vmem_limit_bytes=VMEM_LIMIT)


def _rms(x, g):
    return x * lax.rsqrt(jnp.mean(x * x, axis=-1, keepdims=True) + RMS_EPS) * g


PROJ_SUB = 256


def _norm_proj_body(sections, x_ref, g_ref, w_ref, cs_ref, *out_refs):
    ts = min(x_ref.shape[0], PROJ_SUB)
    subs = [slice(r, r + ts) for r in range(0, x_ref.shape[0], ts)]
    hs = [_rms(x_ref[rows, :], g_ref[...]).astype(jnp.bfloat16) for rows in subs]
    dest = {lo: (o_ref, c, kind, slabs)
            for (cols, kind, slabs), o_ref in zip(sections, out_refs) for c, lo in enumerate(cols)}
    todo = sorted(dest)
    while todo:
        lo = todo.pop(0)
        n = 2 if todo and todo[0] == lo + LANES else 1
        if n == 2:
            todo.pop(0)
        for rows, h in zip(subs, hs):
            yy = jnp.dot(h, w_ref[:, lo:lo + n * LANES], preferred_element_type=jnp.float32)
            for part in range(n):
                o_ref, c, kind, slabs = dest[lo + part * LANES]
                y = yy[:, part * LANES:(part + 1) * LANES]
                if kind in ("rope", "rope_q"):
                    cos, sin_lo, sin_hi = cs_ref[0, rows, :], cs_ref[1, rows, :], cs_ref[2, rows, :]
                    y = (y * cos + pltpu.roll(y, LANES - ROPE_HALF, axis=1) * sin_lo
                         + pltpu.roll(y, ROPE_HALF, axis=1) * sin_hi)
                if kind in ("rope_q", "q"):
                    y = y * SCALE
                if kind == "sigmoid":
                    y = jax.nn.sigmoid(y)
                if slabs:
                    o_ref[c, rows, :] = y.astype(o_ref.dtype)
                else:
                    o_ref[rows, c * LANES:(c + 1) * LANES] = y.astype(o_ref.dtype)


def rope_tables(pos):
    inv_freq = ROPE_THETA ** (-jnp.arange(ROPE_HALF, dtype=jnp.float32) / ROPE_HALF)
    ang = pos.astype(jnp.float32)[:, None] * inv_freq[None, :]
    cos, sin = jnp.cos(ang), jnp.sin(ang)
    n = pos.shape[0]
    one = jnp.ones((n, HEAD_DIM - ROPE_DIM), jnp.float32)
    zero = jnp.zeros((n, HEAD_DIM - ROPE_HALF), jnp.float32)
    c = jnp.concatenate([cos, cos, one], axis=1)
    s_lo = jnp.concatenate([-sin, zero], axis=1)
    s_hi = jnp.concatenate([jnp.zeros((n, ROPE_HALF), jnp.float32), sin,
                            jnp.zeros((n, HEAD_DIM - ROPE_DIM), jnp.float32)], axis=1)
    tab = jnp.stack([c, s_lo, s_hi])
    return jnp.concatenate([tab, tab], axis=2)


def norm_proj(x, g, w, tables, sections, tm):
    R, D = x.shape
    in_specs = [pl.BlockSpec((tm, D), lambda i: (i, 0)),
                pl.BlockSpec((1, D), lambda i: (0, 0)),
                pl.BlockSpec(w.shape, lambda i: (0, 0))]
    args = [x, g, w]
    if tables is not None:
        nt = tables.shape[1] // tm
        in_specs.append(pl.BlockSpec((3, tm, LANES), lambda i: (0, i % nt, 0)))
        args.append(tables)
    out_shape, out_specs, secs = [], [], []
    for (cols, kind, slabs, dtype) in sections:
        secs.append((cols, kind, slabs))
        width = LANES * len(cols)
        if slabs:
            out_shape.append(jax.ShapeDtypeStruct((width // LANES, R, LANES), dtype))
            out_specs.append(pl.BlockSpec((width // LANES, tm, LANES), lambda i: (0, i, 0)))
        else:
            out_shape.append(jax.ShapeDtypeStruct((R, width), dtype))
            out_specs.append(pl.BlockSpec((tm, width), lambda i: (i, 0)))
    if tables is None:
        body = lambda x_ref, g_ref, w_ref, *o: _norm_proj_body(secs, x_ref, g_ref, w_ref, None, *o)
    else:
        body = functools.partial(_norm_proj_body, secs)
    return pl.pallas_call(
        body, grid=(R // tm,), in_specs=in_specs, out_specs=out_specs, out_shape=out_shape,
        compiler_params=_cparams(("parallel",)), name="norm_proj")(*args)


def _chunks(start, width):
    return tuple(range(start, start + width, LANES))


def in_sections():
    f32, bf16 = jnp.float32, jnp.bfloat16
    qb0 = 1024 - 256
    qb_cols = tuple(qb0 + HEAD_DIM * (4 * g + 2 * hp) for hp in range(2) for g in range(3))
    return [
        (_chunks(0, 512), "rope_q", False, bf16),
        (_chunks(512, 128), "rope", False, f32),
        (_chunks(640, 128), "plain", False, f32),
        (qb_cols, "rope_q", True, f32),
        (_chunks(1536, 256), "rope", False, f32),
        (_chunks(1792, 256), "plain", False, f32),
        (_chunks(2048, 256), "q", False, bf16),
        (_chunks(2304, 3072), "sigmoid", False, bf16),
    ]


_NT = (((1,), (1,)), ((), ()))


def _half_masks():
    lane = lax.broadcasted_iota(jnp.int32, (1, LANES), 1)
    return lane < HEAD_DIM, lane >= HEAD_DIM


def _softmax_pv(s, v_half, sink=None):
    m = jnp.max(s, axis=-1, keepdims=True)
    if sink is not None:
        m = jnp.maximum(m, sink)
    e = jnp.exp(s - m)
    den = jnp.sum(e, axis=-1, keepdims=True)
    if sink is not None:
        den = den + jnp.exp(sink - m)
    r = jnp.dot(e.astype(jnp.bfloat16), v_half, preferred_element_type=jnp.float32)
    return r / den, m, den


def _attn_swa_mem_body(L, sink_ref, qa_ref, ka_ref, va_ref, qc_ref, mk_ref, mv_ref, o_ref,
                       kat_ref, vat_ref, mkt_ref, mvt_ref, band_ref):
    bf16 = jnp.bfloat16
    lo, hi = _half_masks()
    halves = (lo, hi)
    T = ATT_BLOCK
    kat_ref[...] = ka_ref[L - SWA_WINDOW:, :].T
    vat_ref[...] = va_ref[L - SWA_WINDOW:, :].T
    mkt_ref[...] = mk_ref[...].T
    mvt_ref[...] = mv_ref[...].T
    G = SWA_Q_HEADS // SWA_KV_HEADS
    mem_k = [mk_ref[:, j * LANES:(j + 1) * LANES].astype(bf16) for j in range(2)]
    mem_v = [mv_ref[:, j * LANES:(j + 1) * LANES].astype(bf16) for j in range(2)]
    qi = lax.broadcasted_iota(jnp.int32, (2 * T, 2 * T), 0) & (T - 1)
    kj = lax.broadcasted_iota(jnp.int32, (2 * T, 2 * T), 1)
    hrow = lax.broadcasted_iota(jnp.int32, (2 * T, 1), 0) >> (T.bit_length() - 1)
    for case, off in enumerate((0, T)):
        dist = qi - kj + off
        band_ref[case] = jnp.where((dist >= 0) & (dist <= SWA_WINDOW - 1), 1.0, 0.0)

    def heads_of(pair):
        return jnp.concatenate([jnp.where(halves[p], pair, jnp.zeros_like(pair)) for p in range(2)], axis=0)

    def block(blk, carry):
        r0 = pl.multiple_of(blk * T, T)
        ws = pl.multiple_of(jnp.maximum(r0 - T, 0), T)
        valid = band_ref[jnp.minimum(blk, 1)] > 0.5
        k = ka_ref[pl.ds(ws, 2 * T), :]
        v = va_ref[pl.ds(ws, 2 * T), :]
        units = []
        for kv in range(SWA_KV_HEADS):
            k1 = jnp.where(halves[kv], k, 0.0)
            v1 = jnp.where(halves[kv], v, 0.0)
            k_dup = (k1 + pltpu.roll(k1, HEAD_DIM, axis=1)).astype(bf16)
            v_dup = (v1 + pltpu.roll(v1, HEAD_DIM, axis=1)).astype(bf16)
            for j in range(kv * G // 2, (kv + 1) * G // 2):
                qm = heads_of(qa_ref[pl.ds(r0, T), j * LANES:(j + 1) * LANES])
                sink = jnp.where(hrow == 0, sink_ref[2 * j], sink_ref[2 * j + 1])
                s = lax.dot_general(qm, k_dup, _NT, preferred_element_type=jnp.float32)
                units.append((jnp.where(valid, s, NEG), v_dup, sink, j * LANES))
        for j in range(MEM_HEADS // 2):
            qm = heads_of(qc_ref[pl.ds(r0, T), j * LANES:(j + 1) * LANES])
            s = lax.dot_general(qm, mem_k[j], _NT, preferred_element_type=jnp.float32)
            units.append((s, mem_v[j], None, SWA_Q_HEADS * HEAD_DIM + j * LANES))
        for s, vals, sink, c0 in units:
            out, _, _ = _softmax_pv(s, vals, sink)
            o_ref[pl.ds(r0, T), c0:c0 + LANES] = jnp.where(lo, out[:T], out[T:]).astype(o_ref.dtype)
        return carry

    lax.fori_loop(0, L // T, block, 0)


def attn_swa_mem(sinks, qa, ka, va, qc, mk, mv, B, L):
    M = mk.shape[0] // B
    wa, wc = SWA_Q_HEADS * HEAD_DIM, MEM_HEADS * HEAD_DIM
    wka = ka.shape[1]
    n_win = min(SWA_WINDOW, L)
    row = lambda w: pl.BlockSpec((L, w), lambda b: (b, 0))
    fmaj = lambda f, n: pl.BlockSpec((None, f, n), lambda b: (b, 0, 0))
    f32 = jnp.float32
    return pl.pallas_call(
        functools.partial(_attn_swa_mem_body, L),
        grid=(B,),
        in_specs=[pl.BlockSpec(memory_space=pltpu.SMEM), row(wa), row(LANES), row(LANES), row(wc),
                  pl.BlockSpec((M, wc), lambda b: (b, 0)), pl.BlockSpec((M, wc), lambda b: (b, 0))],
        out_specs=[row(wa + wc), fmaj(wka, n_win), fmaj(wka, n_win), fmaj(wc, M), fmaj(wc, M)],
        out_shape=[jax.ShapeDtypeStruct((B * L, wa + wc), jnp.bfloat16),
                   jax.ShapeDtypeStruct((B, wka, n_win), f32), jax.ShapeDtypeStruct((B, wka, n_win), f32),
                   jax.ShapeDtypeStruct((B, wc, M), f32), jax.ShapeDtypeStruct((B, wc, M), f32)],
        scratch_shapes=[pltpu.VMEM((2, 2 * ATT_BLOCK, 2 * ATT_BLOCK), f32)],
        compiler_params=_cparams(("parallel",)), name="attn_swa_mem")(sinks, qa, ka, va, qc, mk, mv)


def _attn_dil_body(L, qb_ref, kb_ref, vb_ref, o_ref, kt_ref, vt_ref, og_ref, lse_ref, band_ref):
    bf16 = jnp.bfloat16
    lo, hi = _half_masks()
    halves = (lo, hi)
    T = ATT_BLOCK
    kt_ref[...] = kb_ref[...].T
    vt_ref[...] = vb_ref[...].T
    for g, (window, dil) in enumerate(DIL_PAIRS):
        lc = L // dil
        nbc = lc // T
        W = min(2 * T, lc)
        max_dist = window // dil
        qi = lax.broadcasted_iota(jnp.int32, (2 * T, W), 0) & (T - 1)
        kj = lax.broadcasted_iota(jnp.int32, (2 * T, W), 1)
        for case, off in enumerate((0, T)):
            dist = qi - kj + off
            band_ref[g, case, :, 0:W] = jnp.where((dist >= 0) & (dist <= max_dist), 1.0, 0.0)

        def scores(u, g=g, dil=dil, nbc=nbc, W=W):
            c = u >> (nbc.bit_length() - 1)
            n = u & (nbc - 1)
            wsc = jnp.maximum(n * T - T, 0) if W == 2 * T else 0
            q0 = c + dil * T * n
            k0 = c + dil * wsc
            valid = band_ref[g, jnp.minimum(n, 1), :, 0:W] > 0.5
            q = qb_ref[g, pl.ds(q0, T, stride=dil), :]
            k = kb_ref[pl.ds(k0, W, stride=dil), :]
            v = vb_ref[pl.ds(k0, W, stride=dil), :]
            qm = jnp.concatenate([jnp.where(halves[p], q, 0.0) for p in range(2)], axis=0).astype(bf16)
            s = lax.dot_general(qm, k.astype(bf16), _NT, preferred_element_type=jnp.float32)
            return jnp.where(valid, s, NEG), v.astype(bf16), q0

        def units(i, carry, g=g, dil=dil):
            staged = [scores(i * DIL_CHUNK + t) for t in range(DIL_CHUNK)]
            for s, v, q0 in staged:
                out, m, den = _softmax_pv(s, v)
                lse = m + jnp.log(den)
                og_ref[g, pl.ds(q0, T, stride=dil), :] = jnp.where(lo, out[:T], out[T:])
                lse_ref[g, pl.ds(q0, T, stride=dil), :] = jnp.where(lo, lse[:T], lse[T:])
            return carry

        lax.fori_loop(0, dil * nbc // DIL_CHUNK, units, 0, unroll=True)

    def merge(i, carry):
        r0 = pl.multiple_of(i * T, T)
        ls = [lse_ref[g, pl.ds(r0, T), :] for g in range(len(DIL_PAIRS))]
        m = jnp.maximum(jnp.maximum(ls[0], ls[1]), ls[2])
        ws = [jnp.exp(l - m) for l in ls]
        tot = ws[0] + ws[1] + ws[2]
        out = sum((w / tot) * og_ref[g, pl.ds(r0, T), :] for g, w in enumerate(ws))
        o_ref[pl.ds(r0, T), :] = out.astype(o_ref.dtype)
        return carry

    lax.fori_loop(0, L // T, merge, 0)


def attn_dilated(qb, kb, vb, B, L):
    ng = len(DIL_PAIRS)
    fmaj = pl.BlockSpec((None, LANES, L), lambda b, hp: (b, hp, 0))
    return pl.pallas_call(
        functools.partial(_attn_dil_body, L),
        grid=(B, 2),
        in_specs=[pl.BlockSpec((ng, L, LANES), lambda b, hp: (hp, b, 0)),
                  pl.BlockSpec((L, LANES), lambda b, hp: (b, hp)),
                  pl.BlockSpec((L, LANES), lambda b, hp: (b, hp))],
        out_specs=[pl.BlockSpec((L, LANES), lambda b, hp: (b, hp)), fmaj, fmaj],
        out_shape=[jax.ShapeDtypeStruct((B * L, 2 * LANES), jnp.bfloat16),
                   jax.ShapeDtypeStruct((B, 2 * LANES, L), jnp.float32),
                   jax.ShapeDtypeStruct((B, 2 * LANES, L), jnp.float32)],
        scratch_shapes=[pltpu.VMEM((ng, L, LANES), jnp.float32), pltpu.VMEM((ng, L, LANES), jnp.float32),
                        pltpu.VMEM((ng, 2, 2 * ATT_BLOCK, 2 * ATT_BLOCK), jnp.float32)],
        compiler_params=_cparams(("parallel", "parallel")), name="attn_dilated")(qb, kb, vb)


N_NEW = 4
SROWS = 8


def _softmax2_pv(s_c, s_n, vt_c, vt_n, sink=None):
    m = jnp.maximum(jnp.max(s_c, axis=-1, keepdims=True), jnp.max(s_n, axis=-1, keepdims=True))
    if sink is not None:
        m = jnp.maximum(m, sink)
    e_c = jnp.exp(s_c - m)
    e_n = jnp.exp(s_n - m)
    den = jnp.sum(e_c, axis=-1, keepdims=True) + jnp.sum(e_n, axis=-1, keepdims=True)
    if sink is not None:
        den = den + jnp.exp(sink - m)
    r = (lax.dot_general(e_c.astype(jnp.bfloat16), vt_c, _NT, preferred_element_type=jnp.float32)
         + lax.dot_general(e_n.astype(jnp.bfloat16), vt_n, _NT, preferred_element_type=jnp.float32))
    return r / den, m, den


def _advance(old_t, new_t):
    n = old_t.shape[1]
    lane = lax.broadcasted_iota(jnp.int32, (1, LANES), 1)
    shifted = pltpu.roll(old_t, n - N_NEW, axis=1)
    last = jnp.where(lane < LANES - N_NEW, shifted[:, n - LANES:], new_t)
    if n == LANES:
        return last
    return jnp.concatenate([shifted[:, :n - LANES], last], axis=1)


def _attn_sample_body(bt, sink_ref, qa_ref, qb_ref, qc_ref, nka_ref, nva_ref, nkb_ref, nvb_ref,
                      cak_ref, cav_ref, cbk_ref, cbv_ref, cmk_ref, cmv_ref,
                      oac_ref, ob_ref, oak_ref, oav_ref, obk_ref, obv_ref):
    f32, bf16 = jnp.float32, jnp.bfloat16
    lo, hi = _half_masks()
    halves = (lo, hi)
    S = SROWS
    la = cak_ref.shape[2]
    lb = cbk_ref.shape[2]
    wa = SWA_Q_HEADS * HEAD_DIM
    new0 = LANES - N_NEW

    na = SWA_Q_HEADS * S
    ia = lax.broadcasted_iota(jnp.int32, (na, la), 0) & (S - 1)
    valid_ac = lax.broadcasted_iota(jnp.int32, (na, la), 1) >= ia + 1
    ja = lax.broadcasted_iota(jnp.int32, (na, LANES), 1) - new0
    valid_an = (ja >= 0) & (ja <= (lax.broadcasted_iota(jnp.int32, (na, LANES), 0) & (S - 1)))
    rcol = lax.broadcasted_iota(jnp.int32, (na, 1), 0)
    sink_col = jnp.zeros((na, 1), f32)
    for h in range(SWA_Q_HEADS):
        sink_col = jnp.where((rcol >> 3) == h, sink_ref[h], sink_col)

    nb_rows = len(DIL_PAIRS) * 2 * S
    rb = lax.broadcasted_iota(jnp.int32, (nb_rows, lb), 0)
    t_c = lb + (rb & (S - 1)) - lax.broadcasted_iota(jnp.int32, (nb_rows, lb), 1)
    rn = lax.broadcasted_iota(jnp.int32, (nb_rows, LANES), 0)
    jn = lax.broadcasted_iota(jnp.int32, (nb_rows, LANES), 1) - new0
    t_n = (rn & (S - 1)) - jn
    valid_bc = jnp.zeros((nb_rows, lb), jnp.bool_)
    valid_bn = jnp.zeros((nb_rows, LANES), jnp.bool_)
    for g, (window, dil) in enumerate(DIL_PAIRS):
        valid_bc = valid_bc | (((rb >> 4) == g) & (t_c <= window) & ((t_c & (dil - 1)) == 0))
        valid_bn = valid_bn | (((rn >> 4) == g) & (jn >= 0) & (t_n >= 0) & ((t_n & (dil - 1)) == 0))

    def new_t(x):
        padded = jnp.concatenate([x, jnp.zeros((LANES - S, x.shape[1]), f32)], axis=0)
        return pltpu.roll(padded.T, new0, axis=1)

    for b in range(bt):
        rows = slice(b * S, (b + 1) * S)
        nka, nva = new_t(nka_ref[rows, :]), new_t(nva_ref[rows, :])
        nkb, nvb = new_t(nkb_ref[rows, :]), new_t(nvb_ref[rows, :])
        oak_ref[b] = _advance(cak_ref[b], nka)
        oav_ref[b] = _advance(cav_ref[b], nva)
        obk_ref[b] = _advance(cbk_ref[b], nkb)
        obv_ref[b] = _advance(cbv_ref[b], nvb)

        pieces = []
        for h in range(SWA_Q_HEADS):
            q = jnp.where(halves[h % 2], qa_ref[rows, (h // 2) * LANES:(h // 2 + 1) * LANES].astype(f32), 0.0)
            if h % 2 != h // 4:
                q = pltpu.roll(q, HEAD_DIM, axis=1)
            pieces.append(q)
        qm = jnp.concatenate(pieces, axis=0).astype(bf16)
        s_c = jnp.dot(qm, cak_ref[b].astype(bf16), preferred_element_type=f32)
        s_n = jnp.dot(qm, nka.astype(bf16), preferred_element_type=f32)
        out, _, _ = _softmax2_pv(jnp.where(valid_ac, s_c, NEG), jnp.where(valid_an, s_n, NEG),
                                 cav_ref[b].astype(bf16), nva.astype(bf16), sink_col)
        for j in range(SWA_Q_HEADS // 2):
            parts = []
            for p in range(2):
                h = 2 * j + p
                o = out[h * S:(h + 1) * S]
                if h % 2 != h // 4:
                    o = pltpu.roll(o, HEAD_DIM, axis=1)
                parts.append(o)
            oac_ref[rows, j * LANES:(j + 1) * LANES] = jnp.where(lo, parts[0], parts[1])

        for j in range(MEM_HEADS // 2):
            cols = slice(j * LANES, (j + 1) * LANES)
            q = qc_ref[rows, cols]
            qm = jnp.concatenate([jnp.where(halves[p], q, jnp.zeros_like(q)) for p in range(2)], axis=0)
            s = jnp.dot(qm, cmk_ref[b, cols, :].astype(bf16), preferred_element_type=f32)
            m = jnp.max(s, axis=-1, keepdims=True)
            e = jnp.exp(s - m)
            r = lax.dot_general(e.astype(bf16), cmv_ref[b, cols, :].astype(bf16), _NT, preferred_element_type=f32)
            out = r / jnp.sum(e, axis=-1, keepdims=True)
            oac_ref[rows, wa + j * LANES:wa + (j + 1) * LANES] = jnp.where(lo, out[:S], out[S:])

        for hp in range(DIL_KV_HEADS // 2):
            cols = slice(hp * LANES, (hp + 1) * LANES)
            pieces = [jnp.where(halves[p], qb_ref[hp * len(DIL_PAIRS) + g, rows, :], 0.0)
                      for g in range(len(DIL_PAIRS)) for p in range(2)]
            qm = jnp.concatenate(pieces, axis=0).astype(bf16)
            s_c = jnp.dot(qm, cbk_ref[b, cols, :].astype(bf16), preferred_element_type=f32)
            s_n = jnp.dot(qm, nkb[cols, :].astype(bf16), preferred_element_type=f32)
            out, m, den = _softmax2_pv(jnp.where(valid_bc, s_c, NEG), jnp.where(valid_bn, s_n, NEG),
                                       cbv_ref[b, cols, :].astype(bf16), nvb[cols, :].astype(bf16))
            lse = m + jnp.log(den)
            res = []
            for p in range(2):
                r = [slice((g * 2 + p) * S, (g * 2 + p + 1) * S) for g in range(len(DIL_PAIRS))]
                mx = jnp.maximum(jnp.maximum(lse[r[0]], lse[r[1]]), lse[r[2]])
                w = [jnp.exp(lse[x] - mx) for x in r]
                tot = w[0] + w[1] + w[2]
                res.append(sum((w[g] / tot) * out[r[g]] for g in range(len(DIL_PAIRS))))
            ob_ref[rows, cols] = jnp.where(lo, res[0], res[1])


def attn_sample(sinks, qa, qb, qc, nka, nva, nkb, nvb, cak, cav, cbk, cbv, cmk, cmv, bt):
    NB, wka, la = cak.shape
    wkb, lb = cbk.shape[1:]
    wm, M = cmk.shape[1:]
    wa, wc = SWA_Q_HEADS * HEAD_DIM, MEM_HEADS * HEAD_DIM
    ng = len(DIL_PAIRS)
    tok = lambda w: pl.BlockSpec((bt * SROWS, w), lambda i: (i, 0))
    buf = lambda f, n: pl.BlockSpec((bt, f, n), lambda i: (i, 0, 0))
    f32 = jnp.float32
    return pl.pallas_call(
        functools.partial(_attn_sample_body, bt),
        grid=(NB // bt,),
        in_specs=[pl.BlockSpec(memory_space=pltpu.SMEM), tok(wa),
                  pl.BlockSpec((2 * ng, bt * SROWS, LANES), lambda i: (0, i, 0)), tok(wc),
                  tok(wka), tok(wka), tok(wkb), tok(wkb),
                  buf(wka, la), buf(wka, la), buf(wkb, lb), buf(wkb, lb), buf(wm, M), buf(wm, M)],
        out_specs=[tok(wa + wc), tok(wkb), buf(wka, la), buf(wka, la), buf(wkb, lb), buf(wkb, lb)],
        out_shape=[jax.ShapeDtypeStruct((NB * SROWS, wa + wc), f32), jax.ShapeDtypeStruct((NB * SROWS, wkb), f32),
                   jax.ShapeDtypeStruct(cak.shape, f32), jax.ShapeDtypeStruct(cak.shape, f32),
                   jax.ShapeDtypeStruct(cbk.shape, f32), jax.ShapeDtypeStruct(cbk.shape, f32)],
        compiler_params=_cparams(("parallel",)), name="attn_sample")(
            sinks, qa, qb, qc, nka, nva, nkb, nvb, cak, cav, cbk, cbv, cmk, cmv)


MOE_BLOCK = 512
ROUTE_SUB = 256
SUBLANES = 8


def _store_row_tiles(ref, row0, y):
    n = y.shape[0]
    for c in range(SUBLANES):
        ref[pl.ds(row0 * SUBLANES + c, n, stride=SUBLANES), :] = y[:, c * LANES:(c + 1) * LANES]


def _load_row_tiles(ref, row0, n):
    return jnp.concatenate([ref[pl.ds(row0 * SUBLANES + c, n, stride=SUBLANES), :] for c in range(SUBLANES)],
                           axis=1)


def _row_tile(ref, r):
    return ref.at[pl.ds(pl.multiple_of(r * SUBLANES, SUBLANES), SUBLANES)]


def _merge_route_body(oac_ref, ob_ref, gate_ref, x_ref, wa_ref, wb_ref, wc_ref, wo_ref, gffn_ref, wr_ref, br_ref,
                      cnt0_ref, x1_ref, h2_ref, mi_ref, mf_ref, cnt_ref, base_ref):
    f32, bf16 = jnp.float32, jnp.bfloat16
    D = x_ref.shape[1]
    wa = SWA_Q_HEADS * HEAD_DIM
    ts = ROUTE_SUB

    @pl.when(pl.program_id(0) == 0)
    def _():
        base_ref[...] = cnt0_ref[...]

    erow = lax.broadcasted_iota(jnp.int32, (N_EXPERTS, ts), 0)
    r8 = lax.broadcasted_iota(jnp.int32, (SUBLANES, ts), 0)
    ti = lax.broadcasted_iota(jnp.int32, (ts, ts), 0)
    tj = lax.broadcasted_iota(jnp.int32, (ts, ts), 1)
    later = (ti < tj).astype(bf16)
    base = base_ref[:, 0:1]
    n_sub = x_ref.shape[0] // ts
    mixed, normed = [], []
    for sub in range(n_sub):
        rows = slice(sub * ts, (sub + 1) * ts)
        ma = jnp.dot(oac_ref[rows, :wa].astype(bf16), wa_ref[...], preferred_element_type=f32)
        mb = jnp.dot(ob_ref[rows, :].astype(bf16), wb_ref[...], preferred_element_type=f32)
        mc = jnp.dot(oac_ref[rows, wa:].astype(bf16), wc_ref[...], preferred_element_type=f32)
        merged = (gate_ref[rows, :D].astype(f32) * ma + gate_ref[rows, D:2 * D].astype(f32) * mb
                  + gate_ref[rows, 2 * D:].astype(f32) * mc)
        mixed.append(merged.astype(bf16))

    for sub in range(n_sub):
        rows = slice(sub * ts, (sub + 1) * ts)
        x1 = x_ref[rows, :] + jnp.dot(mixed[sub], wo_ref[...], preferred_element_type=f32)
        x1_ref[rows, :] = x1
        h2 = _rms(x1, gffn_ref[...])
        _store_row_tiles(h2_ref, sub * ts, h2)
        normed.append(h2.astype(bf16))

    for sub in range(n_sub):
        rows = slice(sub * ts, (sub + 1) * ts)
        work = lax.dot_general(wr_ref[...], normed[sub], _NT, preferred_element_type=f32) + br_ref[:, 0:1]
        vals, idxs = [], []
        for _ in range(TOP_K):
            m = jnp.max(work, axis=0, keepdims=True)
            idx = jnp.min(jnp.where(work == m, erow, N_EXPERTS), axis=0, keepdims=True)
            vals.append(m)
            idxs.append(idx)
            work = jnp.where(erow == idx, -jnp.inf, work)
        es = [jnp.exp(v - vals[0]) for v in vals]
        tot = es[0] + es[1] + es[2] + es[3]

        onehot = [(erow == idx).astype(f32) for idx in idxs]
        assign = onehot[0] + onehot[1] + onehot[2] + onehot[3]
        before = jnp.dot(assign.astype(bf16), later, preferred_element_type=f32) + base
        base = base + jnp.sum(assign, axis=1, keepdims=True)

        mi = jnp.zeros((SUBLANES, ts), jnp.int32)
        gates = jnp.zeros((SUBLANES, ts), f32)
        for k in range(TOP_K):
            rank = jnp.sum(onehot[k] * before, axis=0, keepdims=True).astype(jnp.int32)
            mi = jnp.where(r8 == k, idxs[k], mi)
            mi = jnp.where(r8 == TOP_K + k, rank, mi)
            gates = jnp.where(r8 == k, es[k] / tot, gates)
        mi_ref[:, rows] = mi
        mf_ref[rows, :] = jnp.concatenate([gates, jnp.zeros((LANES - SUBLANES, ts), f32)], axis=0).T
    base_ref[...] = jnp.broadcast_to(base, base_ref.shape)
    cnt_ref[...] = jnp.broadcast_to(base, cnt_ref.shape)


def merge_route(o_ac, o_b, gates, x, wa, wb, wc, wo, g_ffn, wr, br, cnt0, tm):
    R, D = x.shape
    row = lambda w: pl.BlockSpec((tm, w), lambda i: (i, 0))
    full = lambda a: pl.BlockSpec(a.shape, lambda i: (0, 0))
    return pl.pallas_call(
        _merge_route_body,
        grid=(R // tm,),
        in_specs=[row(o_ac.shape[1]), row(o_b.shape[1]), row(gates.shape[1]), row(D),
                  full(wa), full(wb), full(wc), full(wo), full(g_ffn), full(wr), full(br), full(cnt0)],
        out_specs=[row(D), pl.BlockSpec((tm * SUBLANES, LANES), lambda i: (i, 0)),
                   pl.BlockSpec((SUBLANES, tm), lambda i: (0, i)), row(LANES),
                   pl.BlockSpec((N_EXPERTS, LANES), lambda i: (0, 0))],
        out_shape=[jax.ShapeDtypeStruct((R, D), jnp.float32), jax.ShapeDtypeStruct((R * SUBLANES, LANES), jnp.float32),
                   jax.ShapeDtypeStruct((SUBLANES, R), jnp.int32), jax.ShapeDtypeStruct((R, LANES), jnp.float32),
                   jax.ShapeDtypeStruct((N_EXPERTS, LANES), jnp.float32)],
        scratch_shapes=[pltpu.VMEM((N_EXPERTS, LANES), jnp.float32)],
        compiler_params=_cparams(("arbitrary",)), name="merge_route")(
            o_ac, o_b, gates, x, wa, wb, wc, wo, g_ffn, wr, br, cnt0)


def _route_tables_body(cnt_ref, mi_ref, dest_ref, blk_ref, pad_ref):
    tm = mi_ref.shape[1]
    nbl = blk_ref.shape[1]
    erow1 = lax.broadcasted_iota(jnp.int32, (N_EXPERTS, LANES), 0)
    shift = MOE_BLOCK.bit_length() - 1
    cnt = cnt_ref[...].astype(jnp.int32)
    padded = ((cnt + (MOE_BLOCK - 1)) >> shift) << shift
    pend = padded
    s = 1
    while s < N_EXPERTS:
        pend = pend + jnp.where(erow1 >= s, pltpu.roll(pend, s, axis=0), 0)
        s *= 2
    pstart = pend - padded
    mi = mi_ref[...]
    erow = lax.broadcasted_iota(jnp.int32, (N_EXPERTS, tm), 0)
    r8 = lax.broadcasted_iota(jnp.int32, (SUBLANES, tm), 0)
    dest = jnp.zeros((SUBLANES, tm), jnp.int32)
    for k in range(TOP_K):
        start = jnp.sum(jnp.where(erow == mi[k:k + 1, :], pstart[:, 0:1], 0), axis=0, keepdims=True)
        dest = jnp.where(r8 == k, start + mi[TOP_K + k:TOP_K + k + 1, :], dest)
    dest_ref[...] = dest

    @pl.when(pl.program_id(0) == 0)
    def _():
        row0 = lax.broadcasted_iota(jnp.int32, (N_EXPERTS, nbl), 1) * MOE_BLOCK
        ended = jnp.sum(jnp.where(pend[:, 0:1] <= row0, 1, 0), axis=0, keepdims=True)
        blk_ref[...] = jnp.broadcast_to(jnp.minimum(ended, N_EXPERTS - 1), blk_ref.shape)
        pad_ref[0] = pstart + cnt
        pad_ref[1] = pend


def route_tables(cnt, mi, nbl, tm):
    R = mi.shape[1]
    return pl.pallas_call(
        _route_tables_body,
        grid=(R // tm,),
        in_specs=[pl.BlockSpec((N_EXPERTS, LANES), lambda i: (0, 0)), pl.BlockSpec((SUBLANES, tm), lambda i: (0, i))],
        out_specs=[pl.BlockSpec((SUBLANES, tm), lambda i: (0, i)), pl.BlockSpec((SUBLANES, nbl), lambda i: (0, 0)),
                   pl.BlockSpec((2, N_EXPERTS, LANES), lambda i: (0, 0, 0))],
        out_shape=[jax.ShapeDtypeStruct((SUBLANES, R), jnp.int32), jax.ShapeDtypeStruct((SUBLANES, nbl), jnp.int32),
                   jax.ShapeDtypeStruct((2, N_EXPERTS, LANES), jnp.int32)],
        compiler_params=_cparams(("arbitrary",)), name="route_tables")(cnt, mi)


def _dispatch_body(n_first, n_steps, pad0_ref, pad1_ref, dest_ref, ha_ref, hb_ref, xs_ref, zero_ref, sem):
    i = pl.program_id(0)
    tm = ha_ref.shape[0] // SUBLANES
    rs = xs_ref.shape[0] // SUBLANES

    def scatter_tile(h_ref):
        def row_copy(r, k):
            return pltpu.make_async_copy(_row_tile(h_ref, r), _row_tile(xs_ref, dest_ref[0, 0, k * tm + r]), sem)

        def start(r, c):
            for k in range(TOP_K):
                row_copy(r, k).start(priority=k % 2)
            return c

        def wait(r, c):
            for k in range(TOP_K):
                row_copy(r, k).wait()
            return c

        lax.fori_loop(0, tm, start, 0, unroll=8)
        lax.fori_loop(0, tm, wait, 0, unroll=8)

    @pl.when(i < n_first)
    def _():
        scatter_tile(ha_ref)

    @pl.when(i >= n_first)
    def _():
        scatter_tile(hb_ref)

    @pl.when(i == n_steps - 1)
    def _():
        zero_ref[...] = jnp.zeros_like(zero_ref)
        zrows = zero_ref.shape[0] // SUBLANES

        def zero_range(lo, hi, go):
            n_big = (hi - lo) // zrows

            def big(j, c):
                start = pl.multiple_of((lo + j * zrows) * SUBLANES, SUBLANES)
                cp = pltpu.make_async_copy(zero_ref, xs_ref.at[pl.ds(start, zrows * SUBLANES)], sem)
                cp.start() if go else cp.wait()
                return c

            def one(r, c):
                cp = pltpu.make_async_copy(zero_ref.at[pl.ds(0, SUBLANES)], _row_tile(xs_ref, r), sem)
                cp.start() if go else cp.wait()
                return c

            lax.fori_loop(0, n_big, big, 0)
            lax.fori_loop(lo + n_big * zrows, hi, one, 0)

        for go in (True, False):
            for e in range(N_EXPERTS):
                zero_range(pad0_ref[e], pad1_ref[e], go)
            zero_range(pad1_ref[N_EXPERTS - 1], rs, go)


def dispatch(pad0, pad1, dest3, h_a, h_b, rs, tm):
    S = SUBLANES
    n_a, n_b = h_a.shape[0] // (tm * S), h_b.shape[0] // (tm * S)
    grid_spec = pltpu.PrefetchScalarGridSpec(
        num_scalar_prefetch=2,
        grid=(n_a + n_b,),
        in_specs=[pl.BlockSpec((1, 1, TOP_K * tm), lambda i, p0, p1: (i, 0, 0), memory_space=pltpu.SMEM),
                  pl.BlockSpec((tm * S, LANES), lambda i, p0, p1: (jnp.minimum(i, n_a - 1), 0)),
                  pl.BlockSpec((tm * S, LANES), lambda i, p0, p1: (jnp.maximum(i - n_a, 0), 0))],
        out_specs=pl.BlockSpec(memory_space=pl.ANY),
        scratch_shapes=[pltpu.VMEM((16 * S, LANES), h_a.dtype), pltpu.SemaphoreType.DMA])
    return pl.pallas_call(
        functools.partial(_dispatch_body, n_a, n_a + n_b), grid_spec=grid_spec,
        out_shape=jax.ShapeDtypeStruct((rs * S, LANES), h_a.dtype),
        compiler_params=_cparams(("arbitrary",)), name="dispatch")(pad0, pad1, dest3, h_a, h_b)


def _moe_body(nblk, be_ref, nact_ref, x_ref, wgu_hbm, bgu_ref, wd_hbm, bd_ref, y_ref, wgu_f, wd_f, wgu_s, wd_s,
              sems):
    f32, bf16 = jnp.float32, jnp.bfloat16
    b = pl.program_id(0)
    e = be_ref[b]

    def fetch(ex, go):
        for src, dst, s in ((wgu_hbm, wgu_f, 0), (wd_hbm, wd_f, 1)):
            cp = pltpu.make_async_copy(src.at[ex], dst, sems.at[s])
            cp.start() if go else cp.wait()

    @pl.when(b == 0)
    def _():
        fetch(e, True)

    @pl.when((b == 0) | (e != be_ref[jnp.maximum(b - 1, 0)]))
    def _():
        fetch(e, False)
        wgu_s[...] = wgu_f[...].astype(bf16)
        wd_s[...] = wd_f[...].astype(bf16)
        nxt = lax.while_loop(lambda j: (j < nblk) & (be_ref[jnp.minimum(j, nblk - 1)] == e), lambda j: j + 1, b + 1)

        @pl.when(nxt < nblk)
        def _():
            fetch(be_ref[jnp.minimum(nxt, nblk - 1)], True)

    @pl.when(b < nact_ref[0])
    def _():
        x = _load_row_tiles(x_ref, 0, MOE_BLOCK).astype(bf16)
        gu = jnp.dot(x, wgu_s[...], preferred_element_type=f32) + bgu_ref[0]
        gt = jnp.minimum(gu[:, :D_FF], SWIGLU_LIMIT)
        up = jnp.clip(gu[:, D_FF:], -SWIGLU_LIMIT, SWIGLU_LIMIT)
        act = (up + 1.0) * (gt * jax.nn.sigmoid(gt * SWIGLU_ALPHA))
        y = jnp.dot(act.astype(bf16), wd_s[...], preferred_element_type=f32) + bd_ref[0]
        _store_row_tiles(y_ref, 0, y)

    @pl.when(b >= nact_ref[0])
    def _():
        y_ref[...] = jnp.zeros_like(y_ref)


def moe_ffn(blk_e, n_active, xs, w_gate_up, b_gate_up, w_down, b_down):
    RS = xs.shape[0] // SUBLANES
    E, D, F2 = w_gate_up.shape
    blk_rows = MOE_BLOCK * SUBLANES
    grid_spec = pltpu.PrefetchScalarGridSpec(
        num_scalar_prefetch=2,
        grid=(RS // MOE_BLOCK,),
        in_specs=[pl.BlockSpec((blk_rows, LANES), lambda b, be, na: (b, 0)),
                  pl.BlockSpec(memory_space=pl.ANY),
                  pl.BlockSpec((1, 1, F2), lambda b, be, na: (be[b], 0, 0)),
                  pl.BlockSpec(memory_space=pl.ANY),
                  pl.BlockSpec((1, 1, D), lambda b, be, na: (be[b], 0, 0))],
        out_specs=pl.BlockSpec((blk_rows, LANES), lambda b, be, na: (b, 0)),
        scratch_shapes=[pltpu.VMEM((D, F2), w_gate_up.dtype), pltpu.VMEM((F2 // 2, D), w_down.dtype),
                        pltpu.VMEM((D, F2), jnp.bfloat16), pltpu.VMEM((F2 // 2, D), jnp.bfloat16),
                        pltpu.SemaphoreType.DMA((2,))])
    return pl.pallas_call(
        functools.partial(_moe_body, RS // MOE_BLOCK), grid_spec=grid_spec,
        out_shape=jax.ShapeDtypeStruct(xs.shape, jnp.float32),
        compiler_params=_cparams(("arbitrary",)), name="moe_ffn")(
            blk_e, n_active, xs, w_gate_up, b_gate_up.reshape(E, 1, F2), w_down, b_down.reshape(E, 1, D))


def _combine_body(n_tiles, dest_ref, dest_next_ref, x1_ref, mf_ref, g_ref, ys_ref, o_ref, buf, sems):
    tm = x1_ref.shape[0]
    i = pl.program_id(0)
    slot = i % 2
    per_slot = TOP_K * tm

    def gather(d_ref, s, go):
        def row_copy(r, k):
            return pltpu.make_async_copy(_row_tile(ys_ref, d_ref[0, 0, k * tm + r]),
                                         _row_tile(buf, s * per_slot + k * tm + r), sems.at[s])

        def body(r, c):
            for k in range(TOP_K):
                if go:
                    row_copy(r, k).start(priority=k % 2)
                else:
                    row_copy(r, k).wait()
            return c

        lax.fori_loop(0, tm, body, 0, unroll=8)

    @pl.when(i == 0)
    def _():
        gather(dest_ref, slot, True)

    @pl.when(i + 1 < n_tiles)
    def _():
        gather(dest_next_ref, 1 - slot, True)

    gather(dest_ref, slot, False)
    y = x1_ref[...]
    for k in range(TOP_K):
        y = y + mf_ref[:, k:k + 1] * _load_row_tiles(buf, slot * per_slot + k * tm, tm)
    o_ref[...] = _rms(y, g_ref[...])


def combine(dest3, x1, mf, g_final, ys, tm):
    R, D = x1.shape
    n = R // tm
    dest_spec = lambda f: pl.BlockSpec((1, 1, TOP_K * tm), f, memory_space=pltpu.SMEM)
    return pl.pallas_call(
        functools.partial(_combine_body, n),
        grid=(n,),
        in_specs=[dest_spec(lambda i: (i, 0, 0)), dest_spec(lambda i: (jnp.minimum(i + 1, n - 1), 0, 0)),
                  pl.BlockSpec((tm, D), lambda i: (i, 0)),
                  pl.BlockSpec((tm, LANES), lambda i: (i, 0)),
                  pl.BlockSpec((1, D), lambda i: (0, 0)),
                  pl.BlockSpec(memory_space=pl.ANY)],
        out_specs=pl.BlockSpec((tm, D), lambda i: (i, 0)),
        out_shape=jax.ShapeDtypeStruct((R, D), jnp.float32),
        scratch_shapes=[pltpu.VMEM((2 * TOP_K * tm * SUBLANES, LANES), ys.dtype), pltpu.SemaphoreType.DMA((2,))],
        compiler_params=_cparams(("arbitrary",)), name="combine")(dest3, dest3, x1, mf, g_final, ys)


def moe_layer(group_a, group_b, cnt, g_final, w_gate_up, b_gate_up, w_down, b_down, tm):
    n_assign = (group_a[0].shape[0] + group_b[0].shape[0]) * TOP_K
    nb = (n_assign + N_EXPERTS * (MOE_BLOCK - 1)) // MOE_BLOCK + 1
    nbl = -(-nb // LANES) * LANES
    dests = []
    for x1, h2, mi, mf in (group_a, group_b):
        dest, blk, pad = route_tables(cnt, mi, nbl, min(x1.shape[0], 4 * tm))
        n_tiles = x1.shape[0] // tm
        dests.append(jnp.transpose(dest[:TOP_K].reshape(TOP_K, n_tiles, tm), (1, 0, 2)).reshape(n_tiles, 1, TOP_K * tm))
    xs = dispatch(pad[0, :, 0], pad[1, :, 0], jnp.concatenate(dests), group_a[1], group_b[1], nb * MOE_BLOCK, tm)
    n_active = (pad[1, N_EXPERTS - 1, 0] // MOE_BLOCK).reshape(1)
    ys = moe_ffn(blk[0, :nb], n_active, xs, w_gate_up, b_gate_up, w_down, b_down)
    return [combine(dest3, x1, mf, g_final, ys, tm) for dest3, (x1, h2, mi, mf) in zip(dests, (group_a, group_b))]


def kernel(x_prompt, x_sample, cache_swa_k, cache_swa_v, cache_dil_k, cache_dil_v, cache_mem_k, cache_mem_v, mem_prompt, norm_attn, norm_mem, w_in, w_mem_kv, sinks, w_br_a, w_br_b, w_br_c, w_out, norm_ffn, w_router, b_router, w_gate_up, b_gate_up, w_down, b_down, norm_final):
    f32, bf16 = jnp.float32, jnp.bfloat16
    TM = 256
    B, L, D = x_prompt.shape
    NB, n_new, _ = x_sample.shape
    M = mem_prompt.shape[1]
    wc = MEM_HEADS * HEAD_DIM
    assert n_new == N_NEW and cache_swa_k.shape[0] == 1

    w_in_b = w_in[0].astype(bf16)
    g_attn = norm_attn[0].reshape(1, D)
    secs = in_sections()

    tabs_p = rope_tables(jnp.arange(L, dtype=jnp.int32))
    qa, ka, va, qb, kb, vb, qc, gates = norm_proj(x_prompt.reshape(B * L, D), g_attn, w_in_b, tabs_p, secs,
                                                  2 * PROJ_SUB)
    mk, mv = norm_proj(mem_prompt.reshape(B * M, D), norm_mem[0].reshape(1, D), w_mem_kv[0].astype(bf16), None,
                       [(_chunks(0, wc), "plain", False, f32), (_chunks(wc, wc), "plain", False, f32)], TM)
    o_ac, ka_t, va_t, mk_t, mv_t = attn_swa_mem(sinks[0], qa, ka, va, qc, mk, mv, B, L)
    o_b, kb_t, vb_t = attn_dilated(qb, kb, vb, B, L)

    xs_pad = jnp.pad(x_sample, ((0, 0), (0, SROWS - N_NEW), (0, 0))).reshape(NB * SROWS, D)
    tabs_s = rope_tables(PAST_LEN + (jnp.arange(TM, dtype=jnp.int32) % SROWS))
    qa_s, ka_s, va_s, qb_s, kb_s, vb_s, qc_s, gates_s = norm_proj(xs_pad, g_attn, w_in_b, tabs_s, secs, TM)
    real = lambda t: t.reshape(NB, SROWS, -1)[:, :N_NEW].reshape(NB * N_NEW, -1)
    fmaj = lambda c: jnp.transpose(c[0], (0, 2, 3, 1)).reshape(NB, -1, c.shape[2])
    o_ac_s, o_b_s, swa_k_s, swa_v_s, dil_k_s, dil_v_s = attn_sample(
        sinks[0], qa_s, qb_s, qc_s, ka_s, va_s, kb_s, vb_s,
        fmaj(cache_swa_k), fmaj(cache_swa_v), fmaj(cache_dil_k), fmaj(cache_dil_v),
        fmaj(cache_mem_k), fmaj(cache_mem_v), 2)

    wr = w_router[0].T.astype(bf16)
    br = jnp.broadcast_to(b_router[0].astype(f32)[:, None], (N_EXPERTS, LANES))
    wts = (w_br_a[0].astype(bf16), w_br_b[0].astype(bf16), w_br_c[0].astype(bf16), w_out[0].astype(bf16),
           norm_ffn[0].reshape(1, D), wr, br)
    x1_p, h2_p, mi_p, mf_p, cnt_p = merge_route(o_ac, o_b, gates, x_prompt.reshape(B * L, D), *wts,
                                                jnp.zeros((N_EXPERTS, LANES), f32), 2 * ROUTE_SUB)
    x1_s, h2_s, mi_s, mf_s, cnt = merge_route(real(o_ac_s), real(o_b_s), real(gates_s),
                                              x_sample.reshape(NB * N_NEW, D), *wts, cnt_p, 2 * ROUTE_SUB)
    y_p, y_s = moe_layer((x1_p, h2_p, mi_p, mf_p), (x1_s, h2_s, mi_s, mf_s), cnt, norm_final.reshape(1, D),
                         w_gate_up[0], b_gate_up[0], w_down[0], b_down[0], TM)

    tmaj = lambda t, h: jnp.transpose(t.reshape(t.shape[0], h, HEAD_DIM, t.shape[2]), (0, 3, 1, 2))[None]
    return (y_p.reshape(B, L, D), y_s.reshape(NB, N_NEW, D),
            tmaj(ka_t, SWA_KV_HEADS), tmaj(va_t, SWA_KV_HEADS),
            tmaj(kb_t, DIL_KV_HEADS), tmaj(vb_t, DIL_KV_HEADS),
            tmaj(mk_t, MEM_HEADS), tmaj(mv_t, MEM_HEADS),
            tmaj(swa_k_s, SWA_KV_HEADS), tmaj(swa_v_s, SWA_KV_HEADS),
            tmaj(dil_k_s, DIL_KV_HEADS), tmaj(dil_v_s, DIL_KV_HEADS))
```

```python
import functools

import jax
import jax.numpy as jnp
from jax import lax
from jax.experimental import pallas as pl
from jax.experimental.pallas import tpu as pltpu

D_MODEL = 1024
HEAD_DIM = 64
ROPE_DIM = 16
ROPE_HALF = 8
ROPE_THETA = 500000.0
PAST_LEN = 16384
SWA_Q_HEADS = 8
SWA_KV_HEADS = 2
SWA_WINDOW = 128
DIL_PAIRS = ((128, 1), (512, 4), (2048, 16))
DIL_KV_HEADS = 4
MEM_HEADS = 4
N_EXPERTS = 32
TOP_K = 4
D_FF = 1024
SWIGLU_LIMIT = 7.0
SWIGLU_ALPHA = 1.702
RMS_EPS = 1e-5
ATT_BLOCK = 128
DIL_CHUNK = 4
SCALE = HEAD_DIM ** -0.5

LANES = 128
NEG = -1e30
VMEM_LIMIT = 56 * 1024 * 1024


def _cparams(sem):
    return pltpu.CompilerParams(dimension_semantics=sem, vmem_limit_bytes=VMEM_LIMIT)


def _rms(x, g):
    return x * lax.rsqrt(jnp.mean(x * x, axis=-1, keepdims=True) + RMS_EPS) * g


PROJ_SUB = 256


def _norm_proj_body(sections, x_ref, g_ref, w_ref, cs_ref, *out_refs):
    ts = min(x_ref.shape[0], PROJ_SUB)
    subs = [slice(r, r + ts) for r in range(0, x_ref.shape[0], ts)]
    hs = [_rms(x_ref[rows, :], g_ref[...]).astype(jnp.bfloat16) for rows in subs]
    dest = {lo: (o_ref, c, kind, slabs)
            for (cols, kind, slabs), o_ref in zip(sections, out_refs) for c, lo in enumerate(cols)}
    todo = sorted(dest)
    while todo:
        lo = todo.pop(0)
        n = 2 if todo and todo[0] == lo + LANES else 1
        if n == 2:
            todo.pop(0)
        for rows, h in zip(subs, hs):
            yy = jnp.dot(h, w_ref[:, lo:lo + n * LANES], preferred_element_type=jnp.float32)
            for part in range(n):
                o_ref, c, kind, slabs = dest[lo + part * LANES]
                y = yy[:, part * LANES:(part + 1) * LANES]
                if kind in ("rope", "rope_q"):
                    cos, sin_lo, sin_hi = cs_ref[0, rows, :], cs_ref[1, rows, :], cs_ref[2, rows, :]
                    y = (y * cos + pltpu.roll(y, LANES - ROPE_HALF, axis=1) * sin_lo
                         + pltpu.roll(y, ROPE_HALF, axis=1) * sin_hi)
                if kind in ("rope_q", "q"):
                    y = y * SCALE
                if kind == "sigmoid":
                    y = jax.nn.sigmoid(y)
                if slabs:
                    o_ref[c, rows, :] = y.astype(o_ref.dtype)
                else:
                    o_ref[rows, c * LANES:(c + 1) * LANES] = y.astype(o_ref.dtype)


def rope_tables(pos):
    inv_freq = ROPE_THETA ** (-jnp.arange(ROPE_HALF, dtype=jnp.float32) / ROPE_HALF)
    ang = pos.astype(jnp.float32)[:, None] * inv_freq[None, :]
    cos, sin = jnp.cos(ang), jnp.sin(ang)
    n = pos.shape[0]
    one = jnp.ones((n, HEAD_DIM - ROPE_DIM), jnp.float32)
    zero = jnp.zeros((n, HEAD_DIM - ROPE_HALF), jnp.float32)
    c = jnp.concatenate([cos, cos, one], axis=1)
    s_lo = jnp.concatenate([-sin, zero], axis=1)
    s_hi = jnp.concatenate([jnp.zeros((n, ROPE_HALF), jnp.float32), sin,
                            jnp.zeros((n, HEAD_DIM - ROPE_DIM), jnp.float32)], axis=1)
    tab = jnp.stack([c, s_lo, s_hi])
    return jnp.concatenate([tab, tab], axis=2)


def norm_proj(x, g, w, tables, sections, tm):
    R, D = x.shape
    in_specs = [pl.BlockSpec((tm, D), lambda i: (i, 0)),
                pl.BlockSpec((1, D), lambda i: (0, 0)),
                pl.BlockSpec(w.shape, lambda i: (0, 0))]
    args = [x, g, w]
    if tables is not None:
        nt = tables.shape[1] // tm
        in_specs.append(pl.BlockSpec((3, tm, LANES), lambda i: (0, i % nt, 0)))
        args.append(tables)
    out_shape, out_specs, secs = [], [], []
    for (cols, kind, slabs, dtype) in sections:
        secs.append((cols, kind, slabs))
        width = LANES * len(cols)
        if slabs:
            out_shape.append(jax.ShapeDtypeStruct((width // LANES, R, LANES), dtype))
            out_specs.append(pl.BlockSpec((width // LANES, tm, LANES), lambda i: (0, i, 0)))
        else:
            out_shape.append(jax.ShapeDtypeStruct((R, width), dtype))
            out_specs.append(pl.BlockSpec((tm, width), lambda i: (i, 0)))
    if tables is None:
        body = lambda x_ref, g_ref, w_ref, *o: _norm_proj_body(secs, x_ref, g_ref, w_ref, None, *o)
    else:
        body = functools.partial(_norm_proj_body, secs)
    return pl.pallas_call(
        body, grid=(R // tm,), in_specs=in_specs, out_specs=out_specs, out_shape=out_shape,
        compiler_params=_cparams(("parallel",)), name="norm_proj")(*args)


def _chunks(start, width):
    return tuple(range(start, start + width, LANES))


def in_sections():
    f32, bf16 = jnp.float32, jnp.bfloat16
    qb0 = 1024 - 256
    qb_cols = tuple(qb0 + HEAD_DIM * (4 * g + 2 * hp) for hp in range(2) for g in range(3))
    return [
        (_chunks(0, 512), "rope_q", False, bf16),
        (_chunks(512, 128), "rope", False, f32),
        (_chunks(640, 128), "plain", False, f32),
        (qb_cols, "rope_q", True, f32),
        (_chunks(1536, 256), "rope", False, f32),
        (_chunks(1792, 256), "plain", False, f32),
        (_chunks(2048, 256), "q", False, bf16),
        (_chunks(2304, 3072), "sigmoid", False, bf16),
    ]


_NT = (((1,), (1,)), ((), ()))


def _half_masks():
    lane = lax.broadcasted_iota(jnp.int32, (1, LANES), 1)
    return lane < HEAD_DIM, lane >= HEAD_DIM


def _softmax_pv(s, v_half, sink=None):
    m = jnp.max(s, axis=-1, keepdims=True)
    if sink is not None:
        m = jnp.maximum(m, sink)
    e = jnp.exp(s - m)
    den = jnp.sum(e, axis=-1, keepdims=True)
    if sink is not None:
        den = den + jnp.exp(sink - m)
    r = jnp.dot(e.astype(jnp.bfloat16), v_half, preferred_element_type=jnp.float32)
    return r / den, m, den


def _attn_swa_mem_body(L, sink_ref, qa_ref, ka_ref, va_ref, qc_ref, mk_ref, mv_ref, o_ref,
                       kat_ref, vat_ref, mkt_ref, mvt_ref, band_ref):
    bf16 = jnp.bfloat16
    lo, hi = _half_masks()
    halves = (lo, hi)
    T = ATT_BLOCK
    kat_ref[...] = ka_ref[L - SWA_WINDOW:, :].T
    vat_ref[...] = va_ref[L - SWA_WINDOW:, :].T
    mkt_ref[...] = mk_ref[...].T
    mvt_ref[...] = mv_ref[...].T
    G = SWA_Q_HEADS // SWA_KV_HEADS
    mem_k = [mk_ref[:, j * LANES:(j + 1) * LANES].astype(bf16) for j in range(2)]
    mem_v = [mv_ref[:, j * LANES:(j + 1) * LANES].astype(bf16) for j in range(2)]
    qi = lax.broadcasted_iota(jnp.int32, (2 * T, 2 * T), 0) & (T - 1)
    kj = lax.broadcasted_iota(jnp.int32, (2 * T, 2 * T), 1)
    hrow = lax.broadcasted_iota(jnp.int32, (2 * T, 1), 0) >> (T.bit_length() - 1)
    for case, off in enumerate((0, T)):
        dist = qi - kj + off
        band_ref[case] = jnp.where((dist >= 0) & (dist <= SWA_WINDOW - 1), 0.0, NEG)

    def heads_of(pair):
        return jnp.concatenate([jnp.where(halves[p], pair, jnp.zeros_like(pair)) for p in range(2)], axis=0)

    def block(blk, carry):
        r0 = pl.multiple_of(blk * T, T)
        ws = pl.multiple_of(jnp.maximum(r0 - T, 0), T)
        band = band_ref[jnp.minimum(blk, 1)]
        k = ka_ref[pl.ds(ws, 2 * T), :]
        v = va_ref[pl.ds(ws, 2 * T), :]
        units = []
        for kv in range(SWA_KV_HEADS):
            k1 = jnp.where(halves[kv], k, 0.0)
            v1 = jnp.where(halves[kv], v, 0.0)
            k_dup = (k1 + pltpu.roll(k1, HEAD_DIM, axis=1)).astype(bf16)
            v_dup = (v1 + pltpu.roll(v1, HEAD_DIM, axis=1)).astype(bf16)
            for j in range(kv * G // 2, (kv + 1) * G // 2):
                qm = heads_of(qa_ref[pl.ds(r0, T), j * LANES:(j + 1) * LANES])
                sink = jnp.where(hrow == 0, sink_ref[2 * j], sink_ref[2 * j + 1])
                s = lax.dot_general(qm, k_dup, _NT, preferred_element_type=jnp.float32)
                units.append((s + band, v_dup, sink, j * LANES))
        for j in range(MEM_HEADS // 2):
            qm = heads_of(qc_ref[pl.ds(r0, T), j * LANES:(j + 1) * LANES])
            s = lax.dot_general(qm, mem_k[j], _NT, preferred_element_type=jnp.float32)
            units.append((s, mem_v[j], None, SWA_Q_HEADS * HEAD_DIM + j * LANES))
        for s, vals, sink, c0 in units:
            out, _, _ = _softmax_pv(s, vals, sink)
            o_ref[pl.ds(r0, T), c0:c0 + LANES] = jnp.where(lo, out[:T], out[T:]).astype(o_ref.dtype)
        return carry

    lax.fori_loop(0, L // T, block, 0)


def attn_swa_mem(sinks, qa, ka, va, qc, mk, mv, B, L):
    M = mk.shape[0] // B
    wa, wc = SWA_Q_HEADS * HEAD_DIM, MEM_HEADS * HEAD_DIM
    wka = ka.shape[1]
    n_win = min(SWA_WINDOW, L)
    row = lambda w: pl.BlockSpec((L, w), lambda b: (b, 0))
    fmaj = lambda f, n: pl.BlockSpec((None, f, n), lambda b: (b, 0, 0))
    f32 = jnp.float32
    return pl.pallas_call(
        functools.partial(_attn_swa_mem_body, L),
        grid=(B,),
        in_specs=[pl.BlockSpec(memory_space=pltpu.SMEM), row(wa), row(LANES), row(LANES), row(wc),
                  pl.BlockSpec((M, wc), lambda b: (b, 0)), pl.BlockSpec((M, wc), lambda b: (b, 0))],
        out_specs=[row(wa + wc), fmaj(wka, n_win), fmaj(wka, n_win), fmaj(wc, M), fmaj(wc, M)],
        out_shape=[jax.ShapeDtypeStruct((B * L, wa + wc), jnp.bfloat16),
                   jax.ShapeDtypeStruct((B, wka, n_win), f32), jax.ShapeDtypeStruct((B, wka, n_win), f32),
                   jax.ShapeDtypeStruct((B, wc, M), f32), jax.ShapeDtypeStruct((B, wc, M), f32)],
        scratch_shapes=[pltpu.VMEM((2, 2 * ATT_BLOCK, 2 * ATT_BLOCK), f32)],
        compiler_params=_cparams(("parallel",)), name="attn_swa_mem")(sinks, qa, ka, va, qc, mk, mv)


def _attn_dil_body(L, qb_ref, kb_ref, vb_ref, o_ref, kt_ref, vt_ref, og_ref, lse_ref, band_ref):
    bf16 = jnp.bfloat16
    lo, hi = _half_masks()
    halves = (lo, hi)
    T = ATT_BLOCK
    kt_ref[...] = kb_ref[...].T
    vt_ref[...] = vb_ref[...].T
    for g, (window, dil) in enumerate(DIL_PAIRS):
        lc = L // dil
        nbc = lc // T
        W = min(2 * T, lc)
        max_dist = window // dil
        qi = lax.broadcasted_iota(jnp.int32, (2 * T, W), 0) & (T - 1)
        kj = lax.broadcasted_iota(jnp.int32, (2 * T, W), 1)
        for case, off in enumerate((0, T)):
            dist = qi - kj + off
            band_ref[g, case, :, 0:W] = jnp.where((dist >= 0) & (dist <= max_dist), 0.0, NEG)

        def scores(u, g=g, dil=dil, nbc=nbc, W=W):
            c = u >> (nbc.bit_length() - 1)
            n = u & (nbc - 1)
            wsc = jnp.maximum(n * T - T, 0) if W == 2 * T else 0
            q0 = c + dil * T * n
            k0 = c + dil * wsc
            q = qb_ref[g, pl.ds(q0, T, stride=dil), :]
            k = kb_ref[pl.ds(k0, W, stride=dil), :]
            v = vb_ref[pl.ds(k0, W, stride=dil), :]
            qm = jnp.concatenate([jnp.where(halves[p], q, 0.0) for p in range(2)], axis=0).astype(bf16)
            s = lax.dot_general(qm, k.astype(bf16), _NT, preferred_element_type=jnp.float32)
            return s + band_ref[g, jnp.minimum(n, 1), :, 0:W], v.astype(bf16), q0

        def units(i, carry, g=g, dil=dil):
            staged = [scores(i * DIL_CHUNK + t) for t in range(DIL_CHUNK)]
            for s, v, q0 in staged:
                out, m, den = _softmax_pv(s, v)
                lse = m + jnp.log(den)
                og_ref[g, pl.ds(q0, T, stride=dil), :] = jnp.where(lo, out[:T], out[T:])
                lse_ref[g, pl.ds(q0, T, stride=dil), :] = jnp.where(lo, lse[:T], lse[T:])
            return carry

        lax.fori_loop(0, dil * nbc // DIL_CHUNK, units, 0, unroll=True)

    def merge(i, carry):
        r0 = pl.multiple_of(i * T, T)
        ls = [lse_ref[g, pl.ds(r0, T), :] for g in range(len(DIL_PAIRS))]
        m = jnp.maximum(jnp.maximum(ls[0], ls[1]), ls[2])
        ws = [jnp.exp(l - m) for l in ls]
        tot = ws[0] + ws[1] + ws[2]
        out = sum((w / tot) * og_ref[g, pl.ds(r0, T), :] for g, w in enumerate(ws))
        o_ref[pl.ds(r0, T), :] = out.astype(o_ref.dtype)
        return carry

    lax.fori_loop(0, L // T, merge, 0)


def attn_dilated(qb, kb, vb, B, L):
    ng = len(DIL_PAIRS)
    fmaj = pl.BlockSpec((None, LANES, L), lambda b, hp: (b, hp, 0))
    return pl.pallas_call(
        functools.partial(_attn_dil_body, L),
        grid=(B, 2),
        in_specs=[pl.BlockSpec((ng, L, LANES), lambda b, hp: (hp, b, 0)),
                  pl.BlockSpec((L, LANES), lambda b, hp: (b, hp)),
                  pl.BlockSpec((L, LANES), lambda b, hp: (b, hp))],
        out_specs=[pl.BlockSpec((L, LANES), lambda b, hp: (b, hp)), fmaj, fmaj],
        out_shape=[jax.ShapeDtypeStruct((B * L, 2 * LANES), jnp.bfloat16),
                   jax.ShapeDtypeStruct((B, 2 * LANES, L), jnp.float32),
                   jax.ShapeDtypeStruct((B, 2 * LANES, L), jnp.float32)],
        scratch_shapes=[pltpu.VMEM((ng, L, LANES), jnp.float32), pltpu.VMEM((ng, L, LANES), jnp.float32),
                        pltpu.VMEM((ng, 2, 2 * ATT_BLOCK, 2 * ATT_BLOCK), jnp.float32)],
        compiler_params=_cparams(("parallel", "parallel")), name="attn_dilated")(qb, kb, vb)


N_NEW = 4
SROWS = 8


def _softmax2_pv(s_c, s_n, vt_c, vt_n, sink=None):
    m = jnp.maximum(jnp.max(s_c, axis=-1, keepdims=True), jnp.max(s_n, axis=-1, keepdims=True))
    if sink is not None:
        m = jnp.maximum(m, sink)
    e_c = jnp.exp(s_c - m)
    e_n = jnp.exp(s_n - m)
    den = jnp.sum(e_c, axis=-1, keepdims=True) + jnp.sum(e_n, axis=-1, keepdims=True)
    if sink is not None:
        den = den + jnp.exp(sink - m)
    r = (lax.dot_general(e_c.astype(jnp.bfloat16), vt_c, _NT, preferred_element_type=jnp.float32)
         + lax.dot_general(e_n.astype(jnp.bfloat16), vt_n, _NT, preferred_element_type=jnp.float32))
    return r / den, m, den


def _advance(old_t, new_t):
    n = old_t.shape[1]
    lane = lax.broadcasted_iota(jnp.int32, (1, LANES), 1)
    shifted = pltpu.roll(old_t, n - N_NEW, axis=1)
    last = jnp.where(lane < LANES - N_NEW, shifted[:, n - LANES:], new_t)
    if n == LANES:
        return last
    return jnp.concatenate([shifted[:, :n - LANES], last], axis=1)


def _attn_sample_body(bt, sink_ref, qa_ref, qb_ref, qc_ref, nka_ref, nva_ref, nkb_ref, nvb_ref,
                      cak_ref, cav_ref, cbk_ref, cbv_ref, cmk_ref, cmv_ref,
                      oac_ref, ob_ref, oak_ref, oav_ref, obk_ref, obv_ref):
    f32, bf16 = jnp.float32, jnp.bfloat16
    lo, hi = _half_masks()
    halves = (lo, hi)
    S = SROWS
    la = cak_ref.shape[2]
    lb = cbk_ref.shape[2]
    wa = SWA_Q_HEADS * HEAD_DIM
    new0 = LANES - N_NEW

    na = SWA_Q_HEADS * S
    ia = lax.broadcasted_iota(jnp.int32, (na, la), 0) & (S - 1)
    valid_ac = lax.broadcasted_iota(jnp.int32, (na, la), 1) >= ia + 1
    ja = lax.broadcasted_iota(jnp.int32, (na, LANES), 1) - new0
    valid_an = (ja >= 0) & (ja <= (lax.broadcasted_iota(jnp.int32, (na, LANES), 0) & (S - 1)))
    rcol = lax.broadcasted_iota(jnp.int32, (na, 1), 0)
    sink_col = jnp.zeros((na, 1), f32)
    for h in range(SWA_Q_HEADS):
        sink_col = jnp.where((rcol >> 3) == h, sink_ref[h], sink_col)

    nb_rows = len(DIL_PAIRS) * 2 * S
    rb = lax.broadcasted_iota(jnp.int32, (nb_rows, lb), 0)
    t_c = lb + (rb & (S - 1)) - lax.broadcasted_iota(jnp.int32, (nb_rows, lb), 1)
    rn = lax.broadcasted_iota(jnp.int32, (nb_rows, LANES), 0)
    jn = lax.broadcasted_iota(jnp.int32, (nb_rows, LANES), 1) - new0
    t_n = (rn & (S - 1)) - jn
    valid_bc = jnp.zeros((nb_rows, lb), jnp.bool_)
    valid_bn = jnp.zeros((nb_rows, LANES), jnp.bool_)
    for g, (window, dil) in enumerate(DIL_PAIRS):
        valid_bc = valid_bc | (((rb >> 4) == g) & (t_c <= window) & ((t_c & (dil - 1)) == 0))
        valid_bn = valid_bn | (((rn >> 4) == g) & (jn >= 0) & (t_n >= 0) & ((t_n & (dil - 1)) == 0))

    def new_t(x):
        padded = jnp.concatenate([x, jnp.zeros((LANES - S, x.shape[1]), f32)], axis=0)
        return pltpu.roll(padded.T, new0, axis=1)

    for b in range(bt):
        rows = slice(b * S, (b + 1) * S)
        nka, nva = new_t(nka_ref[rows, :]), new_t(nva_ref[rows, :])
        nkb, nvb = new_t(nkb_ref[rows, :]), new_t(nvb_ref[rows, :])
        oak_ref[b] = _advance(cak_ref[b], nka)
        oav_ref[b] = _advance(cav_ref[b], nva)
        obk_ref[b] = _advance(cbk_ref[b], nkb)
        obv_ref[b] = _advance(cbv_ref[b], nvb)

        pieces = []
        for h in range(SWA_Q_HEADS):
            q = jnp.where(halves[h % 2], qa_ref[rows, (h // 2) * LANES:(h // 2 + 1) * LANES].astype(f32), 0.0)
            if h % 2 != h // 4:
                q = pltpu.roll(q, HEAD_DIM, axis=1)
            pieces.append(q)
        qm = jnp.concatenate(pieces, axis=0).astype(bf16)
        s_c = jnp.dot(qm, cak_ref[b].astype(bf16), preferred_element_type=f32)
        s_n = jnp.dot(qm, nka.astype(bf16), preferred_element_type=f32)
        out, _, _ = _softmax2_pv(jnp.where(valid_ac, s_c, NEG), jnp.where(valid_an, s_n, NEG),
                                 cav_ref[b].astype(bf16), nva.astype(bf16), sink_col)
        for j in range(SWA_Q_HEADS // 2):
            parts = []
            for p in range(2):
                h = 2 * j + p
                o = out[h * S:(h + 1) * S]
                if h % 2 != h // 4:
                    o = pltpu.roll(o, HEAD_DIM, axis=1)
                parts.append(o)
            oac_ref[rows, j * LANES:(j + 1) * LANES] = jnp.where(lo, parts[0], parts[1])

        for j in range(MEM_HEADS // 2):
            cols = slice(j * LANES, (j + 1) * LANES)
            q = qc_ref[rows, cols]
            qm = jnp.concatenate([jnp.where(halves[p], q, jnp.zeros_like(q)) for p in range(2)], axis=0)
            s = jnp.dot(qm, cmk_ref[b, cols, :].astype(bf16), preferred_element_type=f32)
            m = jnp.max(s, axis=-1, keepdims=True)
            e = jnp.exp(s - m)
            r = lax.dot_general(e.astype(bf16), cmv_ref[b, cols, :].astype(bf16), _NT, preferred_element_type=f32)
            out = r / jnp.sum(e, axis=-1, keepdims=True)
            oac_ref[rows, wa + j * LANES:wa + (j + 1) * LANES] = jnp.where(lo, out[:S], out[S:])

        for hp in range(DIL_KV_HEADS // 2):
            cols = slice(hp * LANES, (hp + 1) * LANES)
            pieces = [jnp.where(halves[p], qb_ref[hp * len(DIL_PAIRS) + g, rows, :], 0.0)
                      for g in range(len(DIL_PAIRS)) for p in range(2)]
            qm = jnp.concatenate(pieces, axis=0).astype(bf16)
            s_c = jnp.dot(qm, cbk_ref[b, cols, :].astype(bf16), preferred_element_type=f32)
            s_n = jnp.dot(qm, nkb[cols, :].astype(bf16), preferred_element_type=f32)
            out, m, den = _softmax2_pv(jnp.where(valid_bc, s_c, NEG), jnp.where(valid_bn, s_n, NEG),
                                       cbv_ref[b, cols, :].astype(bf16), nvb[cols, :].astype(bf16))
            lse = m + jnp.log(den)
            res = []
            for p in range(2):
                r = [slice((g * 2 + p) * S, (g * 2 + p + 1) * S) for g in range(len(DIL_PAIRS))]
                mx = jnp.maximum(jnp.maximum(lse[r[0]], lse[r[1]]), lse[r[2]])
                w = [jnp.exp(lse[x] - mx) for x in r]
                tot = w[0] + w[1] + w[2]
                res.append(sum((w[g] / tot) * out[r[g]] for g in range(len(DIL_PAIRS))))
            ob_ref[rows, cols] = jnp.where(lo, res[0], res[1])


def attn_sample(sinks, qa, qb, qc, nka, nva, nkb, nvb, cak, cav, cbk, cbv, cmk, cmv, bt):
    NB, wka, la = cak.shape
    wkb, lb = cbk.shape[1:]
    wm, M = cmk.shape[1:]
    wa, wc = SWA_Q_HEADS * HEAD_DIM, MEM_HEADS * HEAD_DIM
    ng = len(DIL_PAIRS)
    tok = lambda w: pl.BlockSpec((bt * SROWS, w), lambda i: (i, 0))
    buf = lambda f, n: pl.BlockSpec((bt, f, n), lambda i: (i, 0, 0))
    f32 = jnp.float32
    return pl.pallas_call(
        functools.partial(_attn_sample_body, bt),
        grid=(NB // bt,),
        in_specs=[pl.BlockSpec(memory_space=pltpu.SMEM), tok(wa),
                  pl.BlockSpec((2 * ng, bt * SROWS, LANES), lambda i: (0, i, 0)), tok(wc),
                  tok(wka), tok(wka), tok(wkb), tok(wkb),
                  buf(wka, la), buf(wka, la), buf(wkb, lb), buf(wkb, lb), buf(wm, M), buf(wm, M)],
        out_specs=[tok(wa + wc), tok(wkb), buf(wka, la), buf(wka, la), buf(wkb, lb), buf(wkb, lb)],
        out_shape=[jax.ShapeDtypeStruct((NB * SROWS, wa + wc), f32), jax.ShapeDtypeStruct((NB * SROWS, wkb), f32),
                   jax.ShapeDtypeStruct(cak.shape, f32), jax.ShapeDtypeStruct(cak.shape, f32),
                   jax.ShapeDtypeStruct(cbk.shape, f32), jax.ShapeDtypeStruct(cbk.shape, f32)],
        compiler_params=_cparams(("parallel",)), name="attn_sample")(
            sinks, qa, qb, qc, nka, nva, nkb, nvb, cak, cav, cbk, cbv, cmk, cmv)


MOE_BLOCK = 512
ROUTE_SUB = 256
SUBLANES = 8


def _store_row_tiles(ref, row0, y):
    n = y.shape[0]
    for c in range(SUBLANES):
        ref[pl.ds(row0 * SUBLANES + c, n, stride=SUBLANES), :] = y[:, c * LANES:(c + 1) * LANES]


def _load_row_tiles(ref, row0, n):
    return jnp.concatenate([ref[pl.ds(row0 * SUBLANES + c, n, stride=SUBLANES), :] for c in range(SUBLANES)],
                           axis=1)


def _row_tile(ref, r):
    return ref.at[pl.ds(pl.multiple_of(r * SUBLANES, SUBLANES), SUBLANES)]


def _merge_route_body(oac_ref, ob_ref, gate_ref, x_ref, wa_ref, wb_ref, wc_ref, wo_ref, gffn_ref, wr_ref, br_ref,
                      cnt0_ref, x1_ref, h2_ref, mi_ref, mf_ref, cnt_ref, base_ref):
    f32, bf16 = jnp.float32, jnp.bfloat16
    D = x_ref.shape[1]
    wa = SWA_Q_HEADS * HEAD_DIM
    ts = ROUTE_SUB

    @pl.when(pl.program_id(0) == 0)
    def _():
        base_ref[...] = cnt0_ref[...]

    erow = lax.broadcasted_iota(jnp.int32, (N_EXPERTS, ts), 0)
    r8 = lax.broadcasted_iota(jnp.int32, (SUBLANES, ts), 0)
    ti = lax.broadcasted_iota(jnp.int32, (ts, ts), 0)
    tj = lax.broadcasted_iota(jnp.int32, (ts, ts), 1)
    later = (ti < tj).astype(bf16)
    base = base_ref[:, 0:1]
    n_sub = x_ref.shape[0] // ts
    mixed, normed = [], []
    for sub in range(n_sub):
        rows = slice(sub * ts, (sub + 1) * ts)
        ma = jnp.dot(oac_ref[rows, :wa].astype(bf16), wa_ref[...], preferred_element_type=f32)
        mb = jnp.dot(ob_ref[rows, :].astype(bf16), wb_ref[...], preferred_element_type=f32)
        mc = jnp.dot(oac_ref[rows, wa:].astype(bf16), wc_ref[...], preferred_element_type=f32)
        merged = (gate_ref[rows, :D].astype(f32) * ma + gate_ref[rows, D:2 * D].astype(f32) * mb
                  + gate_ref[rows, 2 * D:].astype(f32) * mc)
        mixed.append(merged.astype(bf16))

    for sub in range(n_sub):
        rows = slice(sub * ts, (sub + 1) * ts)
        x1 = x_ref[rows, :] + jnp.dot(mixed[sub], wo_ref[...], preferred_element_type=f32)
        x1_ref[rows, :] = x1
        h2 = _rms(x1, gffn_ref[...])
        _store_row_tiles(h2_ref, sub * ts, h2)
        normed.append(h2.astype(bf16))

    for sub in range(n_sub):
        rows = slice(sub * ts, (sub + 1) * ts)
        work = lax.dot_general(wr_ref[...], normed[sub], _NT, preferred_element_type=f32) + br_ref[:, 0:1]
        vals, idxs = [], []
        for _ in range(TOP_K):
            m = jnp.max(work, axis=0, keepdims=True)
            idx = jnp.min(jnp.where(work == m, erow, N_EXPERTS), axis=0, keepdims=True)
            vals.append(m)
            idxs.append(idx)
            work = jnp.where(erow == idx, -jnp.inf, work)
        es = [jnp.exp(v - vals[0]) for v in vals]
        tot = es[0] + es[1] + es[2] + es[3]

        onehot = [(erow == idx).astype(f32) for idx in idxs]
        assign = onehot[0] + onehot[1] + onehot[2] + onehot[3]
        before = jnp.dot(assign.astype(bf16), later, preferred_element_type=f32) + base
        base = base + jnp.sum(assign, axis=1, keepdims=True)

        mi = jnp.zeros((SUBLANES, ts), jnp.int32)
        gates = jnp.zeros((SUBLANES, ts), f32)
        for k in range(TOP_K):
            rank = jnp.sum(onehot[k] * before, axis=0, keepdims=True).astype(jnp.int32)
            mi = jnp.where(r8 == k, idxs[k], mi)
            mi = jnp.where(r8 == TOP_K + k, rank, mi)
            gates = jnp.where(r8 == k, es[k] / tot, gates)
        mi_ref[:, rows] = mi
        mf_ref[rows, :] = jnp.concatenate([gates, jnp.zeros((LANES - SUBLANES, ts), f32)], axis=0).T
    base_ref[...] = jnp.broadcast_to(base, base_ref.shape)
    cnt_ref[...] = jnp.broadcast_to(base, cnt_ref.shape)


def merge_route(o_ac, o_b, gates, x, wa, wb, wc, wo, g_ffn, wr, br, cnt0, tm):
    R, D = x.shape
    row = lambda w: pl.BlockSpec((tm, w), lambda i: (i, 0))
    full = lambda a: pl.BlockSpec(a.shape, lambda i: (0, 0))
    return pl.pallas_call(
        _merge_route_body,
        grid=(R // tm,),
        in_specs=[row(o_ac.shape[1]), row(o_b.shape[1]), row(gates.shape[1]), row(D),
                  full(wa), full(wb), full(wc), full(wo), full(g_ffn), full(wr), full(br), full(cnt0)],
        out_specs=[row(D), pl.BlockSpec((tm * SUBLANES, LANES), lambda i: (i, 0)),
                   pl.BlockSpec((SUBLANES, tm), lambda i: (0, i)), row(LANES),
                   pl.BlockSpec((N_EXPERTS, LANES), lambda i: (0, 0))],
        out_shape=[jax.ShapeDtypeStruct((R, D), jnp.float32), jax.ShapeDtypeStruct((R * SUBLANES, LANES), jnp.float32),
                   jax.ShapeDtypeStruct((SUBLANES, R), jnp.int32), jax.ShapeDtypeStruct((R, LANES), jnp.float32),
                   jax.ShapeDtypeStruct((N_EXPERTS, LANES), jnp.float32)],
        scratch_shapes=[pltpu.VMEM((N_EXPERTS, LANES), jnp.float32)],
        compiler_params=_cparams(("arbitrary",)), name="merge_route")(
            o_ac, o_b, gates, x, wa, wb, wc, wo, g_ffn, wr, br, cnt0)


def _route_tables_body(cnt_ref, mi_ref, dest_ref, blk_ref, pad_ref):
    tm = mi_ref.shape[1]
    nbl = blk_ref.shape[1]
    erow1 = lax.broadcasted_iota(jnp.int32, (N_EXPERTS, LANES), 0)
    shift = MOE_BLOCK.bit_length() - 1
    cnt = cnt_ref[...].astype(jnp.int32)
    padded = ((cnt + (MOE_BLOCK - 1)) >> shift) << shift
    pend = padded
    s = 1
    while s < N_EXPERTS:
        pend = pend + jnp.where(erow1 >= s, pltpu.roll(pend, s, axis=0), 0)
        s *= 2
    pstart = pend - padded
    mi = mi_ref[...]
    erow = lax.broadcasted_iota(jnp.int32, (N_EXPERTS, tm), 0)
    r8 = lax.broadcasted_iota(jnp.int32, (SUBLANES, tm), 0)
    dest = jnp.zeros((SUBLANES, tm), jnp.int32)
    for k in range(TOP_K):
        start = jnp.sum(jnp.where(erow == mi[k:k + 1, :], pstart[:, 0:1], 0), axis=0, keepdims=True)
        dest = jnp.where(r8 == k, start + mi[TOP_K + k:TOP_K + k + 1, :], dest)
    dest_ref[...] = dest

    @pl.when(pl.program_id(0) == 0)
    def _():
        row0 = lax.broadcasted_iota(jnp.int32, (N_EXPERTS, nbl), 1) * MOE_BLOCK
        ended = jnp.sum(jnp.where(pend[:, 0:1] <= row0, 1, 0), axis=0, keepdims=True)
        blk_ref[...] = jnp.broadcast_to(jnp.minimum(ended, N_EXPERTS - 1), blk_ref.shape)
        pad_ref[0] = pstart + cnt
        pad_ref[1] = pend


def route_tables(cnt, mi, nbl, tm):
    R = mi.shape[1]
    return pl.pallas_call(
        _route_tables_body,
        grid=(R // tm,),
        in_specs=[pl.BlockSpec((N_EXPERTS, LANES), lambda i: (0, 0)), pl.BlockSpec((SUBLANES, tm), lambda i: (0, i))],
        out_specs=[pl.BlockSpec((SUBLANES, tm), lambda i: (0, i)), pl.BlockSpec((SUBLANES, nbl), lambda i: (0, 0)),
                   pl.BlockSpec((2, N_EXPERTS, LANES), lambda i: (0, 0, 0))],
        out_shape=[jax.ShapeDtypeStruct((SUBLANES, R), jnp.int32), jax.ShapeDtypeStruct((SUBLANES, nbl), jnp.int32),
                   jax.ShapeDtypeStruct((2, N_EXPERTS, LANES), jnp.int32)],
        compiler_params=_cparams(("arbitrary",)), name="route_tables")(cnt, mi)


def _dispatch_body(n_first, n_steps, pad0_ref, pad1_ref, dest_ref, ha_ref, hb_ref, xs_ref, zero_ref, sem):
    i = pl.program_id(0)
    tm = ha_ref.shape[0] // SUBLANES
    rs = xs_ref.shape[0] // SUBLANES

    def scatter_tile(h_ref):
        def row_copy(r, k):
            return pltpu.make_async_copy(_row_tile(h_ref, r), _row_tile(xs_ref, dest_ref[0, 0, k * tm + r]), sem)

        def start(r, c):
            for k in range(TOP_K):
                row_copy(r, k).start(priority=k % 2)
            return c

        def wait(r, c):
            for k in range(TOP_K):
                row_copy(r, k).wait()
            return c

        lax.fori_loop(0, tm, start, 0, unroll=8)
        lax.fori_loop(0, tm, wait, 0, unroll=8)

    @pl.when(i < n_first)
    def _():
        scatter_tile(ha_ref)

    @pl.when(i >= n_first)
    def _():
        scatter_tile(hb_ref)

    @pl.when(i == n_steps - 1)
    def _():
        zero_ref[...] = jnp.zeros_like(zero_ref)
        zrows = zero_ref.shape[0] // SUBLANES

        def zero_range(lo, hi, go):
            n_big = (hi - lo) // zrows

            def big(j, c):
                start = pl.multiple_of((lo + j * zrows) * SUBLANES, SUBLANES)
                cp = pltpu.make_async_copy(zero_ref, xs_ref.at[pl.ds(start, zrows * SUBLANES)], sem)
                cp.start() if go else cp.wait()
                return c

            def one(r, c):
                cp = pltpu.make_async_copy(zero_ref.at[pl.ds(0, SUBLANES)], _row_tile(xs_ref, r), sem)
                cp.start() if go else cp.wait()
                return c

            lax.fori_loop(0, n_big, big, 0)
            lax.fori_loop(lo + n_big * zrows, hi, one, 0)

        for go in (True, False):
            for e in range(N_EXPERTS):
                zero_range(pad0_ref[e], pad1_ref[e], go)
            zero_range(pad1_ref[N_EXPERTS - 1], rs, go)


def dispatch(pad0, pad1, dest3, h_a, h_b, rs, tm):
    S = SUBLANES
    n_a, n_b = h_a.shape[0] // (tm * S), h_b.shape[0] // (tm * S)
    grid_spec = pltpu.PrefetchScalarGridSpec(
        num_scalar_prefetch=2,
        grid=(n_a + n_b,),
        in_specs=[pl.BlockSpec((1, 1, TOP_K * tm), lambda i, p0, p1: (i, 0, 0), memory_space=pltpu.SMEM),
                  pl.BlockSpec((tm * S, LANES), lambda i, p0, p1: (jnp.minimum(i, n_a - 1), 0)),
                  pl.BlockSpec((tm * S, LANES), lambda i, p0, p1: (jnp.maximum(i - n_a, 0), 0))],
        out_specs=pl.BlockSpec(memory_space=pl.ANY),
        scratch_shapes=[pltpu.VMEM((16 * S, LANES), h_a.dtype), pltpu.SemaphoreType.DMA])
    return pl.pallas_call(
        functools.partial(_dispatch_body, n_a, n_a + n_b), grid_spec=grid_spec,
        out_shape=jax.ShapeDtypeStruct((rs * S, LANES), h_a.dtype),
        compiler_params=_cparams(("arbitrary",)), name="dispatch")(pad0, pad1, dest3, h_a, h_b)


def _moe_body(nblk, be_ref, nact_ref, x_ref, wgu_hbm, bgu_ref, wd_hbm, bd_ref, y_ref, wgu_f, wd_f, wgu_s, wd_s,
              sems):
    f32, bf16 = jnp.float32, jnp.bfloat16
    b = pl.program_id(0)
    e = be_ref[b]

    def fetch(ex, go):
        for src, dst, s in ((wgu_hbm, wgu_f, 0), (wd_hbm, wd_f, 1)):
            cp = pltpu.make_async_copy(src.at[ex], dst, sems.at[s])
            cp.start() if go else cp.wait()

    @pl.when(b == 0)
    def _():
        fetch(e, True)

    @pl.when((b == 0) | (e != be_ref[jnp.maximum(b - 1, 0)]))
    def _():
        fetch(e, False)
        wgu_s[...] = wgu_f[...].astype(bf16)
        wd_s[...] = wd_f[...].astype(bf16)
        nxt = lax.while_loop(lambda j: (j < nblk) & (be_ref[jnp.minimum(j, nblk - 1)] == e), lambda j: j + 1, b + 1)

        @pl.when(nxt < nblk)
        def _():
            fetch(be_ref[jnp.minimum(nxt, nblk - 1)], True)

    @pl.when(b < nact_ref[0])
    def _():
        x = _load_row_tiles(x_ref, 0, MOE_BLOCK).astype(bf16)
        gu = jnp.dot(x, wgu_s[...], preferred_element_type=f32) + bgu_ref[0]
        gt = jnp.minimum(gu[:, :D_FF], SWIGLU_LIMIT)
        up = jnp.clip(gu[:, D_FF:], -SWIGLU_LIMIT, SWIGLU_LIMIT)
        act = (up + 1.0) * (gt * jax.nn.sigmoid(gt * SWIGLU_ALPHA))
        y = jnp.dot(act.astype(bf16), wd_s[...], preferred_element_type=f32) + bd_ref[0]
        _store_row_tiles(y_ref, 0, y)

    @pl.when(b >= nact_ref[0])
    def _():
        y_ref[...] = jnp.zeros_like(y_ref)


def moe_ffn(blk_e, n_active, xs, w_gate_up, b_gate_up, w_down, b_down):
    RS = xs.shape[0] // SUBLANES
    E, D, F2 = w_gate_up.shape
    blk_rows = MOE_BLOCK * SUBLANES
    grid_spec = pltpu.PrefetchScalarGridSpec(
        num_scalar_prefetch=2,
        grid=(RS // MOE_BLOCK,),
        in_specs=[pl.BlockSpec((blk_rows, LANES), lambda b, be, na: (b, 0)),
                  pl.BlockSpec(memory_space=pl.ANY),
                  pl.BlockSpec((1, 1, F2), lambda b, be, na: (be[b], 0, 0)),
                  pl.BlockSpec(memory_space=pl.ANY),
                  pl.BlockSpec((1, 1, D), lambda b, be, na: (be[b], 0, 0))],
        out_specs=pl.BlockSpec((blk_rows, LANES), lambda b, be, na: (b, 0)),
        scratch_shapes=[pltpu.VMEM((D, F2), w_gate_up.dtype), pltpu.VMEM((F2 // 2, D), w_down.dtype),
                        pltpu.VMEM((D, F2), jnp.bfloat16), pltpu.VMEM((F2 // 2, D), jnp.bfloat16),
                        pltpu.SemaphoreType.DMA((2,))])
    return pl.pallas_call(
        functools.partial(_moe_body, RS // MOE_BLOCK), grid_spec=grid_spec,
        out_shape=jax.ShapeDtypeStruct(xs.shape, jnp.float32),
        compiler_params=_cparams(("arbitrary",)), name="moe_ffn")(
            blk_e, n_active, xs, w_gate_up, b_gate_up.reshape(E, 1, F2), w_down, b_down.reshape(E, 1, D))


def _combine_body(n_tiles, dest_ref, dest_next_ref, x1_ref, mf_ref, g_ref, ys_ref, o_ref, buf, sems):
    tm = x1_ref.shape[0]
    i = pl.program_id(0)
    slot = i % 2
    per_slot = TOP_K * tm

    def gather(d_ref, s, go):
        def row_copy(r, k):
            return pltpu.make_async_copy(_row_tile(ys_ref, d_ref[0, 0, k * tm + r]),
                                         _row_tile(buf, s * per_slot + k * tm + r), sems.at[s])

        def body(r, c):
            for k in range(TOP_K):
                if go:
                    row_copy(r, k).start(priority=k % 2)
                else:
                    row_copy(r, k).wait()
            return c

        lax.fori_loop(0, tm, body, 0, unroll=8)

    @pl.when(i == 0)
    def _():
        gather(dest_ref, slot, True)

    @pl.when(i + 1 < n_tiles)
    def _():
        gather(dest_next_ref, 1 - slot, True)

    gather(dest_ref, slot, False)
    y = x1_ref[...]
    for k in range(TOP_K):
        y = y + mf_ref[:, k:k + 1] * _load_row_tiles(buf, slot * per_slot + k * tm, tm)
    o_ref[...] = _rms(y, g_ref[...])


def combine(dest3, x1, mf, g_final, ys, tm):
    R, D = x1.shape
    n = R // tm
    dest_spec = lambda f: pl.BlockSpec((1, 1, TOP_K * tm), f, memory_space=pltpu.SMEM)
    return pl.pallas_call(
        functools.partial(_combine_body, n),
        grid=(n,),
        in_specs=[dest_spec(lambda i: (i, 0, 0)), dest_spec(lambda i: (jnp.minimum(i + 1, n - 1), 0, 0)),
                  pl.BlockSpec((tm, D), lambda i: (i, 0)),
                  pl.BlockSpec((tm, LANES), lambda i: (i, 0)),
                  pl.BlockSpec((1, D), lambda i: (0, 0)),
                  pl.BlockSpec(memory_space=pl.ANY)],
        out_specs=pl.BlockSpec((tm, D), lambda i: (i, 0)),
        out_shape=jax.ShapeDtypeStruct((R, D), jnp.float32),
        scratch_shapes=[pltpu.VMEM((2 * TOP_K * tm * SUBLANES, LANES), ys.dtype), pltpu.SemaphoreType.DMA((2,))],
        compiler_params=_cparams(("arbitrary",)), name="combine")(dest3, dest3, x1, mf, g_final, ys)


def moe_layer(group_a, group_b, cnt, g_final, w_gate_up, b_gate_up, w_down, b_down, tm):
    n_assign = (group_a[0].shape[0] + group_b[0].shape[0]) * TOP_K
    nb = (n_assign + N_EXPERTS * (MOE_BLOCK - 1)) // MOE_BLOCK + 1
    nbl = -(-nb // LANES) * LANES
    dests = []
    for x1, h2, mi, mf in (group_a, group_b):
        dest, blk, pad = route_tables(cnt, mi, nbl, min(x1.shape[0], 4 * tm))
        n_tiles = x1.shape[0] // tm
        dests.append(jnp.transpose(dest[:TOP_K].reshape(TOP_K, n_tiles, tm), (1, 0, 2)).reshape(n_tiles, 1, TOP_K * tm))
    xs = dispatch(pad[0, :, 0], pad[1, :, 0], jnp.concatenate(dests), group_a[1], group_b[1], nb * MOE_BLOCK, tm)
    n_active = (pad[1, N_EXPERTS - 1, 0] // MOE_BLOCK).reshape(1)
    ys = moe_ffn(blk[0, :nb], n_active, xs, w_gate_up, b_gate_up, w_down, b_down)
    return [combine(dest3, x1, mf, g_final, ys, tm) for dest3, (x1, h2, mi, mf) in zip(dests, (group_a, group_b))]


def kernel(x_prompt, x_sample, cache_swa_k, cache_swa_v, cache_dil_k, cache_dil_v, cache_mem_k, cache_mem_v, mem_prompt, norm_attn, norm_mem, w_in, w_mem_kv, sinks, w_br_a, w_br_b, w_br_c, w_out, norm_ffn, w_router, b_router, w_gate_up, b_gate_up, w_down, b_down, norm_final):
    f32, bf16 = jnp.float32, jnp.bfloat16
    TM = 256
    B, L, D = x_prompt.shape
    NB, n_new, _ = x_sample.shape
    M = mem_prompt.shape[1]
    wc = MEM_HEADS * HEAD_DIM
    assert n_new == N_NEW and cache_swa_k.shape[0] == 1

    w_in_b = w_in[0].astype(bf16)
    g_attn = norm_attn[0].reshape(1, D)
    secs = in_sections()

    tabs_p = rope_tables(jnp.arange(L, dtype=jnp.int32))
    qa, ka, va, qb, kb, vb, qc, gates = norm_proj(x_prompt.reshape(B * L, D), g_attn, w_in_b, tabs_p, secs,
                                                  2 * PROJ_SUB)
    mk, mv = norm_proj(mem_prompt.reshape(B * M, D), norm_mem[0].reshape(1, D), w_mem_kv[0].astype(bf16), None,
                       [(_chunks(0, wc), "plain", False, f32), (_chunks(wc, wc), "plain", False, f32)], TM)
    o_ac, ka_t, va_t, mk_t, mv_t = attn_swa_mem(sinks[0], qa, ka, va, qc, mk, mv, B, L)
    o_b, kb_t, vb_t = attn_dilated(qb, kb, vb, B, L)

    xs_pad = jnp.pad(x_sample, ((0, 0), (0, SROWS - N_NEW), (0, 0))).reshape(NB * SROWS, D)
    tabs_s = rope_tables(PAST_LEN + (jnp.arange(TM, dtype=jnp.int32) % SROWS))
    qa_s, ka_s, va_s, qb_s, kb_s, vb_s, qc_s, gates_s = norm_proj(xs_pad, g_attn, w_in_b, tabs_s, secs, TM)
    real = lambda t: t.reshape(NB, SROWS, -1)[:, :N_NEW].reshape(NB * N_NEW, -1)
    fmaj = lambda c: jnp.transpose(c[0], (0, 2, 3, 1)).reshape(NB, -1, c.shape[2])
    o_ac_s, o_b_s, swa_k_s, swa_v_s, dil_k_s, dil_v_s = attn_sample(
        sinks[0], qa_s, qb_s, qc_s, ka_s, va_s, kb_s, vb_s,
        fmaj(cache_swa_k), fmaj(cache_swa_v), fmaj(cache_dil_k), fmaj(cache_dil_v),
        fmaj(cache_mem_k), fmaj(cache_mem_v), 2)

    wr = w_router[0].T.astype(bf16)
    br = jnp.broadcast_to(b_router[0].astype(f32)[:, None], (N_EXPERTS, LANES))
    wts = (w_br_a[0].astype(bf16), w_br_b[0].astype(bf16), w_br_c[0].astype(bf16), w_out[0].astype(bf16),
           norm_ffn[0].reshape(1, D), wr, br)
    x1_p, h2_p, mi_p, mf_p, cnt_p = merge_route(o_ac, o_b, gates, x_prompt.reshape(B * L, D), *wts,
                                                jnp.zeros((N_EXPERTS, LANES), f32), 4 * ROUTE_SUB)
    x1_s, h2_s, mi_s, mf_s, cnt = merge_route(real(o_ac_s), real(o_b_s), real(gates_s),
                                              x_sample.reshape(NB * N_NEW, D), *wts, cnt_p, 2 * ROUTE_SUB)
    y_p, y_s = moe_layer((x1_p, h2_p, mi_p, mf_p), (x1_s, h2_s, mi_s, mf_s), cnt, norm_final.reshape(1, D),
                         w_gate_up[0], b_gate_up[0], w_down[0], b_down[0], TM)

    tmaj = lambda t, h: jnp.transpose(t.reshape(t.shape[0], h, HEAD_DIM, t.shape[2]), (0, 3, 1, 2))[None]
    return (y_p.reshape(B, L, D), y_s.reshape(NB, N_NEW, D),
            tmaj(ka_t, SWA_KV_HEADS), tmaj(va_t, SWA_KV_HEADS),
            tmaj(kb_t, DIL_KV_HEADS), tmaj(vb_t, DIL_KV_HEADS),
            tmaj(mk_t, MEM_HEADS), tmaj(mv_t, MEM_HEADS),
            tmaj(swa_k_s, SWA_KV_HEADS), tmaj(swa_v_s, SWA_KV_HEADS),
            tmaj(dil_k_s, DIL_KV_HEADS), tmaj(dil_v_s, DIL_KV_HEADS))
```

```python
import functools

import jax
import jax.numpy as jnp
from jax import lax
from jax.experimental import pallas as pl
from jax.experimental.pallas import tpu as pltpu

D_MODEL = 1024
HEAD_DIM = 64
ROPE_DIM = 16
ROPE_HALF = 8
ROPE_THETA = 500000.0
PAST_LEN = 16384
SWA_Q_HEADS = 8
SWA_KV_HEADS = 2
SWA_WINDOW = 128
DIL_PAIRS = ((128, 1), (512, 4), (2048, 16))
DIL_KV_HEADS = 4
MEM_HEADS = 4
N_EXPERTS = 32
TOP_K = 4
D_FF = 1024
SWIGLU_LIMIT = 7.0
SWIGLU_ALPHA = 1.702
RMS_EPS = 1e-5
ATT_BLOCK = 128
DIL_CHUNK = 4
SCALE = HEAD_DIM ** -0.5

LANES = 128
NEG = -1e30
VMEM_LIMIT = 56 * 1024 * 1024


def _cparams(sem):
    return pltpu.CompilerParams(dimension_semantics=sem, vmem_limit_bytes=VMEM_LIMIT)


def _rms(x, g):
    return x * lax.rsqrt(jnp.mean(x * x, axis=-1, keepdims=True) + RMS_EPS) * g


PROJ_SUB = 256


def _norm_proj_body(sections, x_ref, g_ref, w_ref, cs_ref, *out_refs):
    ts = min(x_ref.shape[0], PROJ_SUB)
    subs = [slice(r, r + ts) for r in range(0, x_ref.shape[0], ts)]
    hs = [_rms(x_ref[rows, :], g_ref[...]).astype(jnp.bfloat16) for rows in subs]
    dest = {lo: (o_ref, c, kind, slabs)
            for (cols, kind, slabs), o_ref in zip(sections, out_refs) for c, lo in enumerate(cols)}
    todo = sorted(dest)
    while todo:
        lo = todo.pop(0)
        n = 2 if todo and todo[0] == lo + LANES else 1
        if n == 2:
            todo.pop(0)
        for rows, h in zip(subs, hs):
            yy = jnp.dot(h, w_ref[:, lo:lo + n * LANES], preferred_element_type=jnp.float32)
            for part in range(n):
                o_ref, c, kind, slabs = dest[lo + part * LANES]
                y = yy[:, part * LANES:(part + 1) * LANES]
                if kind in ("rope", "rope_q"):
                    cos, sin_lo, sin_hi = cs_ref[0, rows, :], cs_ref[1, rows, :], cs_ref[2, rows, :]
                    y = (y * cos + pltpu.roll(y, LANES - ROPE_HALF, axis=1) * sin_lo
                         + pltpu.roll(y, ROPE_HALF, axis=1) * sin_hi)
                if kind in ("rope_q", "q"):
                    y = y * SCALE
                if kind == "sigmoid":
                    y = jax.nn.sigmoid(y)
                if slabs:
                    o_ref[c, rows, :] = y.astype(o_ref.dtype)
                else:
                    o_ref[rows, c * LANES:(c + 1) * LANES] = y.astype(o_ref.dtype)


def rope_tables(pos):
    inv_freq = ROPE_THETA ** (-jnp.arange(ROPE_HALF, dtype=jnp.float32) / ROPE_HALF)
    ang = pos.astype(jnp.float32)[:, None] * inv_freq[None, :]
    cos, sin = jnp.cos(ang), jnp.sin(ang)
    n = pos.shape[0]
    one = jnp.ones((n, HEAD_DIM - ROPE_DIM), jnp.float32)
    zero = jnp.zeros((n, HEAD_DIM - ROPE_HALF), jnp.float32)
    c = jnp.concatenate([cos, cos, one], axis=1)
    s_lo = jnp.concatenate([-sin, zero], axis=1)
    s_hi = jnp.concatenate([jnp.zeros((n, ROPE_HALF), jnp.float32), sin,
                            jnp.zeros((n, HEAD_DIM - ROPE_DIM), jnp.float32)], axis=1)
    tab = jnp.stack([c, s_lo, s_hi])
    return jnp.concatenate([tab, tab], axis=2)


def norm_proj(x, g, w, tables, sections, tm):
    R, D = x.shape
    in_specs = [pl.BlockSpec((tm, D), lambda i: (i, 0)),
                pl.BlockSpec((1, D), lambda i: (0, 0)),
                pl.BlockSpec(w.shape, lambda i: (0, 0))]
    args = [x, g, w]
    if tables is not None:
        nt = tables.shape[1] // tm
        in_specs.append(pl.BlockSpec((3, tm, LANES), lambda i: (0, i % nt, 0)))
        args.append(tables)
    out_shape, out_specs, secs = [], [], []
    for (cols, kind, slabs, dtype) in sections:
        secs.append((cols, kind, slabs))
        width = LANES * len(cols)
        if slabs:
            out_shape.append(jax.ShapeDtypeStruct((width // LANES, R, LANES), dtype))
            out_specs.append(pl.BlockSpec((width // LANES, tm, LANES), lambda i: (0, i, 0)))
        else:
            out_shape.append(jax.ShapeDtypeStruct((R, width), dtype))
            out_specs.append(pl.BlockSpec((tm, width), lambda i: (i, 0)))
    if tables is None:
        body = lambda x_ref, g_ref, w_ref, *o: _norm_proj_body(secs, x_ref, g_ref, w_ref, None, *o)
    else:
        body = functools.partial(_norm_proj_body, secs)
    return pl.pallas_call(
        body, grid=(R // tm,), in_specs=in_specs, out_specs=out_specs, out_shape=out_shape,
        compiler_params=_cparams(("parallel",)), name="norm_proj")(*args)


def _chunks(start, width):
    return tuple(range(start, start + width, LANES))


def in_sections():
    f32, bf16 = jnp.float32, jnp.bfloat16
    qb0 = 1024 - 256
    qb_cols = tuple(qb0 + HEAD_DIM * (4 * g + 2 * hp) for hp in range(2) for g in range(3))
    return [
        (_chunks(0, 512), "rope_q", False, bf16),
        (_chunks(512, 128), "rope", False, f32),
        (_chunks(640, 128), "plain", False, f32),
        (qb_cols, "rope_q", True, f32),
        (_chunks(1536, 256), "rope", False, f32),
        (_chunks(1792, 256), "plain", False, f32),
        (_chunks(2048, 256), "q", False, bf16),
        (_chunks(2304, 3072), "sigmoid", False, bf16),
    ]


_NT = (((1,), (1,)), ((), ()))


def _half_masks():
    lane = lax.broadcasted_iota(jnp.int32, (1, LANES), 1)
    return lane < HEAD_DIM, lane >= HEAD_DIM


def _softmax_pv(s, v_half, sink=None):
    m = jnp.max(s, axis=-1, keepdims=True)
    if sink is not None:
        m = jnp.maximum(m, sink)
    e = jnp.exp(s - m)
    den = jnp.sum(e, axis=-1, keepdims=True)
    if sink is not None:
        den = den + jnp.exp(sink - m)
    r = jnp.dot(e.astype(jnp.bfloat16), v_half, preferred_element_type=jnp.float32)
    return r / den, m, den


def _attn_swa_mem_body(L, sink_ref, qa_ref, ka_ref, va_ref, qc_ref, mk_ref, mv_ref, o_ref,
                       kat_ref, vat_ref, mkt_ref, mvt_ref, band_ref):
    bf16 = jnp.bfloat16
    lo, hi = _half_masks()
    halves = (lo, hi)
    T = ATT_BLOCK
    kat_ref[...] = ka_ref[L - SWA_WINDOW:, :].T
    vat_ref[...] = va_ref[L - SWA_WINDOW:, :].T
    mkt_ref[...] = mk_ref[...].T
    mvt_ref[...] = mv_ref[...].T
    G = SWA_Q_HEADS // SWA_KV_HEADS
    mem_k = [mk_ref[:, j * LANES:(j + 1) * LANES].astype(bf16) for j in range(2)]
    mem_v = [mv_ref[:, j * LANES:(j + 1) * LANES].astype(bf16) for j in range(2)]
    qi = lax.broadcasted_iota(jnp.int32, (2 * T, 2 * T), 0) & (T - 1)
    kj = lax.broadcasted_iota(jnp.int32, (2 * T, 2 * T), 1)
    hrow = lax.broadcasted_iota(jnp.int32, (2 * T, 1), 0) >> (T.bit_length() - 1)
    for case, off in enumerate((0, T)):
        dist = qi - kj + off
        band_ref[case] = jnp.where((dist >= 0) & (dist <= SWA_WINDOW - 1), 0.0, NEG)

    def heads_of(pair):
        return jnp.concatenate([jnp.where(halves[p], pair, jnp.zeros_like(pair)) for p in range(2)], axis=0)

    def block(blk, carry):
        r0 = pl.multiple_of(blk * T, T)
        ws = pl.multiple_of(jnp.maximum(r0 - T, 0), T)
        band = band_ref[jnp.minimum(blk, 1)]
        k = ka_ref[pl.ds(ws, 2 * T), :]
        v = va_ref[pl.ds(ws, 2 * T), :]
        units = []
        for kv in range(SWA_KV_HEADS):
            k1 = jnp.where(halves[kv], k, 0.0)
            v1 = jnp.where(halves[kv], v, 0.0)
            k_dup = (k1 + pltpu.roll(k1, HEAD_DIM, axis=1)).astype(bf16)
            v_dup = (v1 + pltpu.roll(v1, HEAD_DIM, axis=1)).astype(bf16)
            for j in range(kv * G // 2, (kv + 1) * G // 2):
                qm = heads_of(qa_ref[pl.ds(r0, T), j * LANES:(j + 1) * LANES])
                sink = jnp.where(hrow == 0, sink_ref[2 * j], sink_ref[2 * j + 1])
                s = lax.dot_general(qm, k_dup, _NT, preferred_element_type=jnp.float32)
                units.append((s + band, v_dup, sink, j * LANES))
        for j in range(MEM_HEADS // 2):
            qm = heads_of(qc_ref[pl.ds(r0, T), j * LANES:(j + 1) * LANES])
            s = lax.dot_general(qm, mem_k[j], _NT, preferred_element_type=jnp.float32)
            units.append((s, mem_v[j], None, SWA_Q_HEADS * HEAD_DIM + j * LANES))
        for s, vals, sink, c0 in units:
            out, _, _ = _softmax_pv(s, vals, sink)
            o_ref[pl.ds(r0, T), c0:c0 + LANES] = jnp.where(lo, out[:T], out[T:]).astype(o_ref.dtype)
        return carry

    lax.fori_loop(0, L // T, block, 0)


def attn_swa_mem(sinks, qa, ka, va, qc, mk, mv, B, L):
    M = mk.shape[0] // B
    wa, wc = SWA_Q_HEADS * HEAD_DIM, MEM_HEADS * HEAD_DIM
    wka = ka.shape[1]
    n_win = min(SWA_WINDOW, L)
    row = lambda w: pl.BlockSpec((L, w), lambda b: (b, 0))
    fmaj = lambda f, n: pl.BlockSpec((None, f, n), lambda b: (b, 0, 0))
    f32 = jnp.float32
    return pl.pallas_call(
        functools.partial(_attn_swa_mem_body, L),
        grid=(B,),
        in_specs=[pl.BlockSpec(memory_space=pltpu.SMEM), row(wa), row(LANES), row(LANES), row(wc),
                  pl.BlockSpec((M, wc), lambda b: (b, 0)), pl.BlockSpec((M, wc), lambda b: (b, 0))],
        out_specs=[row(wa + wc), fmaj(wka, n_win), fmaj(wka, n_win), fmaj(wc, M), fmaj(wc, M)],
        out_shape=[jax.ShapeDtypeStruct((B * L, wa + wc), jnp.bfloat16),
                   jax.ShapeDtypeStruct((B, wka, n_win), f32), jax.ShapeDtypeStruct((B, wka, n_win), f32),
                   jax.ShapeDtypeStruct((B, wc, M), f32), jax.ShapeDtypeStruct((B, wc, M), f32)],
        scratch_shapes=[pltpu.VMEM((2, 2 * ATT_BLOCK, 2 * ATT_BLOCK), f32)],
        compiler_params=_cparams(("parallel",)), name="attn_swa_mem")(sinks, qa, ka, va, qc, mk, mv)


def _attn_dil_body(L, qb_ref, kb_ref, vb_ref, o_ref, kt_ref, vt_ref, og_ref, lse_ref, band_ref):
    bf16 = jnp.bfloat16
    lo, hi = _half_masks()
    halves = (lo, hi)
    T = ATT_BLOCK
    kt_ref[...] = kb_ref[...].T
    vt_ref[...] = vb_ref[...].T
    for g, (window, dil) in enumerate(DIL_PAIRS):
        lc = L // dil
        nbc = lc // T
        W = min(2 * T, lc)
        max_dist = window // dil
        qi = lax.broadcasted_iota(jnp.int32, (2 * T, W), 0) & (T - 1)
        kj = lax.broadcasted_iota(jnp.int32, (2 * T, W), 1)
        for case, off in enumerate((0, T)):
            dist = qi - kj + off
            band_ref[g, case, :, 0:W] = jnp.where((dist >= 0) & (dist <= max_dist), 0.0, NEG)

        def scores(u, g=g, dil=dil, nbc=nbc, W=W):
            c = u >> (nbc.bit_length() - 1)
            n = u & (nbc - 1)
            wsc = jnp.maximum(n * T - T, 0) if W == 2 * T else 0
            q0 = c + dil * T * n
            k0 = c + dil * wsc
            q = qb_ref[g, pl.ds(q0, T, stride=dil), :]
            k = kb_ref[pl.ds(k0, W, stride=dil), :]
            v = vb_ref[pl.ds(k0, W, stride=dil), :]
            qm = jnp.concatenate([jnp.where(halves[p], q, 0.0) for p in range(2)], axis=0).astype(bf16)
            s = lax.dot_general(qm, k.astype(bf16), _NT, preferred_element_type=jnp.float32)
            return s + band_ref[g, jnp.minimum(n, 1), :, 0:W], v.astype(bf16), q0

        def units(i, carry, g=g, dil=dil):
            staged = [scores(i * DIL_CHUNK + t) for t in range(DIL_CHUNK)]
            for s, v, q0 in staged:
                out, m, den = _softmax_pv(s, v)
                lse = m + jnp.log(den)
                og_ref[g, pl.ds(q0, T, stride=dil), :] = jnp.where(lo, out[:T], out[T:])
                lse_ref[g, pl.ds(q0, T, stride=dil), :] = jnp.where(lo, lse[:T], lse[T:])
            return carry

        lax.fori_loop(0, dil * nbc // DIL_CHUNK, units, 0, unroll=True)

    def merge(i, carry):
        r0 = pl.multiple_of(i * T, T)
        ls = [lse_ref[g, pl.ds(r0, T), :] for g in range(len(DIL_PAIRS))]
        m = jnp.maximum(jnp.maximum(ls[0], ls[1]), ls[2])
        ws = [jnp.exp(l - m) for l in ls]
        tot = ws[0] + ws[1] + ws[2]
        out = sum((w / tot) * og_ref[g, pl.ds(r0, T), :] for g, w in enumerate(ws))
        o_ref[pl.ds(r0, T), :] = out.astype(o_ref.dtype)
        return carry

    lax.fori_loop(0, L // T, merge, 0)


def attn_dilated(qb, kb, vb, B, L):
    ng = len(DIL_PAIRS)
    fmaj = pl.BlockSpec((None, LANES, L), lambda b, hp: (b, hp, 0))
    return pl.pallas_call(
        functools.partial(_attn_dil_body, L),
        grid=(B, 2),
        in_specs=[pl.BlockSpec((ng, L, LANES), lambda b, hp: (hp, b, 0)),
                  pl.BlockSpec((L, LANES), lambda b, hp: (b, hp)),
                  pl.BlockSpec((L, LANES), lambda b, hp: (b, hp))],
        out_specs=[pl.BlockSpec((L, LANES), lambda b, hp: (b, hp)), fmaj, fmaj],
        out_shape=[jax.ShapeDtypeStruct((B * L, 2 * LANES), jnp.bfloat16),
                   jax.ShapeDtypeStruct((B, 2 * LANES, L), jnp.float32),
                   jax.ShapeDtypeStruct((B, 2 * LANES, L), jnp.float32)],
        scratch_shapes=[pltpu.VMEM((ng, L, LANES), jnp.float32), pltpu.VMEM((ng, L, LANES), jnp.float32),
                        pltpu.VMEM((ng, 2, 2 * ATT_BLOCK, 2 * ATT_BLOCK), jnp.float32)],
        compiler_params=_cparams(("parallel", "parallel")), name="attn_dilated")(qb, kb, vb)


N_NEW = 4
SROWS = 8


def _softmax2_pv(s_c, s_n, vt_c, vt_n, sink=None):
    m = jnp.maximum(jnp.max(s_c, axis=-1, keepdims=True), jnp.max(s_n, axis=-1, keepdims=True))
    if sink is not None:
        m = jnp.maximum(m, sink)
    e_c = jnp.exp(s_c - m)
    e_n = jnp.exp(s_n - m)
    den = jnp.sum(e_c, axis=-1, keepdims=True) + jnp.sum(e_n, axis=-1, keepdims=True)
    if sink is not None:
        den = den + jnp.exp(sink - m)
    r = (lax.dot_general(e_c.astype(jnp.bfloat16), vt_c, _NT, preferred_element_type=jnp.float32)
         + lax.dot_general(e_n.astype(jnp.bfloat16), vt_n, _NT, preferred_element_type=jnp.float32))
    return r / den, m, den


def _advance(old_t, new_t):
    n = old_t.shape[1]
    lane = lax.broadcasted_iota(jnp.int32, (1, LANES), 1)
    shifted = pltpu.roll(old_t, n - N_NEW, axis=1)
    last = jnp.where(lane < LANES - N_NEW, shifted[:, n - LANES:], new_t)
    if n == LANES:
        return last
    return jnp.concatenate([shifted[:, :n - LANES], last], axis=1)


def _attn_sample_body(bt, sink_ref, qa_ref, qb_ref, qc_ref, nka_ref, nva_ref, nkb_ref, nvb_ref,
                      cak_ref, cav_ref, cbk_ref, cbv_ref, cmk_ref, cmv_ref,
                      oac_ref, ob_ref, oak_ref, oav_ref, obk_ref, obv_ref):
    f32, bf16 = jnp.float32, jnp.bfloat16
    lo, hi = _half_masks()
    halves = (lo, hi)
    S = SROWS
    la = cak_ref.shape[2]
    lb = cbk_ref.shape[2]
    wa = SWA_Q_HEADS * HEAD_DIM
    new0 = LANES - N_NEW

    na = SWA_Q_HEADS * S
    ia = lax.broadcasted_iota(jnp.int32, (na, la), 0) & (S - 1)
    valid_ac = lax.broadcasted_iota(jnp.int32, (na, la), 1) >= ia + 1
    ja = lax.broadcasted_iota(jnp.int32, (na, LANES), 1) - new0
    valid_an = (ja >= 0) & (ja <= (lax.broadcasted_iota(jnp.int32, (na, LANES), 0) & (S - 1)))
    rcol = lax.broadcasted_iota(jnp.int32, (na, 1), 0)
    sink_col = jnp.zeros((na, 1), f32)
    for h in range(SWA_Q_HEADS):
        sink_col = jnp.where((rcol >> 3) == h, sink_ref[h], sink_col)

    nb_rows = len(DIL_PAIRS) * 2 * S
    rb = lax.broadcasted_iota(jnp.int32, (nb_rows, lb), 0)
    t_c = lb + (rb & (S - 1)) - lax.broadcasted_iota(jnp.int32, (nb_rows, lb), 1)
    rn = lax.broadcasted_iota(jnp.int32, (nb_rows, LANES), 0)
    jn = lax.broadcasted_iota(jnp.int32, (nb_rows, LANES), 1) - new0
    t_n = (rn & (S - 1)) - jn
    valid_bc = jnp.zeros((nb_rows, lb), jnp.bool_)
    valid_bn = jnp.zeros((nb_rows, LANES), jnp.bool_)
    for g, (window, dil) in enumerate(DIL_PAIRS):
        valid_bc = valid_bc | (((rb >> 4) == g) & (t_c <= window) & ((t_c & (dil - 1)) == 0))
        valid_bn = valid_bn | (((rn >> 4) == g) & (jn >= 0) & (t_n >= 0) & ((t_n & (dil - 1)) == 0))

    def new_t(x):
        padded = jnp.concatenate([x, jnp.zeros((LANES - S, x.shape[1]), f32)], axis=0)
        return pltpu.roll(padded.T, new0, axis=1)

    for b in range(bt):
        rows = slice(b * S, (b + 1) * S)
        nka, nva = new_t(nka_ref[rows, :]), new_t(nva_ref[rows, :])
        nkb, nvb = new_t(nkb_ref[rows, :]), new_t(nvb_ref[rows, :])
        oak_ref[b] = _advance(cak_ref[b], nka)
        oav_ref[b] = _advance(cav_ref[b], nva)
        obk_ref[b] = _advance(cbk_ref[b], nkb)
        obv_ref[b] = _advance(cbv_ref[b], nvb)

        pieces = []
        for h in range(SWA_Q_HEADS):
            q = jnp.where(halves[h % 2], qa_ref[rows, (h // 2) * LANES:(h // 2 + 1) * LANES].astype(f32), 0.0)
            if h % 2 != h // 4:
                q = pltpu.roll(q, HEAD_DIM, axis=1)
            pieces.append(q)
        qm = jnp.concatenate(pieces, axis=0).astype(bf16)
        s_c = jnp.dot(qm, cak_ref[b].astype(bf16), preferred_element_type=f32)
        s_n = jnp.dot(qm, nka.astype(bf16), preferred_element_type=f32)
        out, _, _ = _softmax2_pv(jnp.where(valid_ac, s_c, NEG), jnp.where(valid_an, s_n, NEG),
                                 cav_ref[b].astype(bf16), nva.astype(bf16), sink_col)
        for j in range(SWA_Q_HEADS // 2):
            parts = []
            for p in range(2):
                h = 2 * j + p
                o = out[h * S:(h + 1) * S]
                if h % 2 != h // 4:
                    o = pltpu.roll(o, HEAD_DIM, axis=1)
                parts.append(o)
            oac_ref[rows, j * LANES:(j + 1) * LANES] = jnp.where(lo, parts[0], parts[1])

        for j in range(MEM_HEADS // 2):
            cols = slice(j * LANES, (j + 1) * LANES)
            q = qc_ref[rows, cols]
            qm = jnp.concatenate([jnp.where(halves[p], q, jnp.zeros_like(q)) for p in range(2)], axis=0)
            s = jnp.dot(qm, cmk_ref[b, cols, :].astype(bf16), preferred_element_type=f32)
            m = jnp.max(s, axis=-1, keepdims=True)
            e = jnp.exp(s - m)
            r = lax.dot_general(e.astype(bf16), cmv_ref[b, cols, :].astype(bf16), _NT, preferred_element_type=f32)
            out = r / jnp.sum(e, axis=-1, keepdims=True)
            oac_ref[rows, wa + j * LANES:wa + (j + 1) * LANES] = jnp.where(lo, out[:S], out[S:])

        for hp in range(DIL_KV_HEADS // 2):
            cols = slice(hp * LANES, (hp + 1) * LANES)
            pieces = [jnp.where(halves[p], qb_ref[hp * len(DIL_PAIRS) + g, rows, :], 0.0)
                      for g in range(len(DIL_PAIRS)) for p in range(2)]
            qm = jnp.concatenate(pieces, axis=0).astype(bf16)
            s_c = jnp.dot(qm, cbk_ref[b, cols, :].astype(bf16), preferred_element_type=f32)
            s_n = jnp.dot(qm, nkb[cols, :].astype(bf16), preferred_element_type=f32)
            out, m, den = _softmax2_pv(jnp.where(valid_bc, s_c, NEG), jnp.where(valid_bn, s_n, NEG),
                                       cbv_ref[b, cols, :].astype(bf16), nvb[cols, :].astype(bf16))
            lse = m + jnp.log(den)
            res = []
            for p in range(2):
                r = [slice((g * 2 + p) * S, (g * 2 + p + 1) * S) for g in range(len(DIL_PAIRS))]
                mx = jnp.maximum(jnp.maximum(lse[r[0]], lse[r[1]]), lse[r[2]])
                w = [jnp.exp(lse[x] - mx) for x in r]
                tot = w[0] + w[1] + w[2]
                res.append(sum((w[g] / tot) * out[r[g]] for g in range(len(DIL_PAIRS))))
            ob_ref[rows, cols] = jnp.where(lo, res[0], res[1])


def attn_sample(sinks, qa, qb, qc, nka, nva, nkb, nvb, cak, cav, cbk, cbv, cmk, cmv, bt):
    NB, wka, la = cak.shape
    wkb, lb = cbk.shape[1:]
    wm, M = cmk.shape[1:]
    wa, wc = SWA_Q_HEADS * HEAD_DIM, MEM_HEADS * HEAD_DIM
    ng = len(DIL_PAIRS)
    tok = lambda w: pl.BlockSpec((bt * SROWS, w), lambda i: (i, 0))
    buf = lambda f, n: pl.BlockSpec((bt, f, n), lambda i: (i, 0, 0))
    f32 = jnp.float32
    return pl.pallas_call(
        functools.partial(_attn_sample_body, bt),
        grid=(NB // bt,),
        in_specs=[pl.BlockSpec(memory_space=pltpu.SMEM), tok(wa),
                  pl.BlockSpec((2 * ng, bt * SROWS, LANES), lambda i: (0, i, 0)), tok(wc),
                  tok(wka), tok(wka), tok(wkb), tok(wkb),
                  buf(wka, la), buf(wka, la), buf(wkb, lb), buf(wkb, lb), buf(wm, M), buf(wm, M)],
        out_specs=[tok(wa + wc), tok(wkb), buf(wka, la), buf(wka, la), buf(wkb, lb), buf(wkb, lb)],
        out_shape=[jax.ShapeDtypeStruct((NB * SROWS, wa + wc), f32), jax.ShapeDtypeStruct((NB * SROWS, wkb), f32),
                   jax.ShapeDtypeStruct(cak.shape, f32), jax.ShapeDtypeStruct(cak.shape, f32),
                   jax.ShapeDtypeStruct(cbk.shape, f32), jax.ShapeDtypeStruct(cbk.shape, f32)],
        compiler_params=_cparams(("parallel",)), name="attn_sample")(
            sinks, qa, qb, qc, nka, nva, nkb, nvb, cak, cav, cbk, cbv, cmk, cmv)


MOE_BLOCK = 512
ROUTE_SUB = 256
SUBLANES = 8


def _store_row_tiles(ref, row0, y):
    n = y.shape[0]
    for c in range(SUBLANES):
        ref[pl.ds(row0 * SUBLANES + c, n, stride=SUBLANES), :] = y[:, c * LANES:(c + 1) * LANES]


def _load_row_tiles(ref, row0, n):
    return jnp.concatenate([ref[pl.ds(row0 * SUBLANES + c, n, stride=SUBLANES), :] for c in range(SUBLANES)],
                           axis=1)


def _row_tile(ref, r):
    return ref.at[pl.ds(pl.multiple_of(r * SUBLANES, SUBLANES), SUBLANES)]


def _merge_route_body(oac_ref, ob_ref, gate_ref, x_ref, wa_ref, wb_ref, wc_ref, wo_ref, gffn_ref, wr_ref, br_ref,
                      cnt0_ref, x1_ref, h2_ref, mi_ref, mf_ref, cnt_ref, base_ref):
    f32, bf16 = jnp.float32, jnp.bfloat16
    D = x_ref.shape[1]
    wa = SWA_Q_HEADS * HEAD_DIM
    ts = ROUTE_SUB

    @pl.when(pl.program_id(0) == 0)
    def _():
        base_ref[...] = cnt0_ref[...]

    erow = lax.broadcasted_iota(jnp.int32, (N_EXPERTS, ts), 0)
    r8 = lax.broadcasted_iota(jnp.int32, (SUBLANES, ts), 0)
    ti = lax.broadcasted_iota(jnp.int32, (ts, ts), 0)
    tj = lax.broadcasted_iota(jnp.int32, (ts, ts), 1)
    later = (ti < tj).astype(bf16)
    base = base_ref[:, 0:1]
    n_sub = x_ref.shape[0] // ts
    mixed, normed = [], []
    for sub in range(n_sub):
        rows = slice(sub * ts, (sub + 1) * ts)
        ma = jnp.dot(oac_ref[rows, :wa].astype(bf16), wa_ref[...], preferred_element_type=f32)
        mb = jnp.dot(ob_ref[rows, :].astype(bf16), wb_ref[...], preferred_element_type=f32)
        mc = jnp.dot(oac_ref[rows, wa:].astype(bf16), wc_ref[...], preferred_element_type=f32)
        merged = (gate_ref[rows, :D].astype(f32) * ma + gate_ref[rows, D:2 * D].astype(f32) * mb
                  + gate_ref[rows, 2 * D:].astype(f32) * mc)
        mixed.append(merged.astype(bf16))

    for sub in range(n_sub):
        rows = slice(sub * ts, (sub + 1) * ts)
        x1 = x_ref[rows, :] + jnp.dot(mixed[sub], wo_ref[...], preferred_element_type=f32)
        x1_ref[rows, :] = x1
        h2 = _rms(x1, gffn_ref[...])
        _store_row_tiles(h2_ref, sub * ts, h2)
        normed.append(h2.astype(bf16))

    for sub in range(n_sub):
        rows = slice(sub * ts, (sub + 1) * ts)
        work = lax.dot_general(wr_ref[...], normed[sub], _NT, preferred_element_type=f32) + br_ref[:, 0:1]
        vals, idxs = [], []
        for _ in range(TOP_K):
            m = jnp.max(work, axis=0, keepdims=True)
            idx = jnp.min(jnp.where(work == m, erow, N_EXPERTS), axis=0, keepdims=True)
            vals.append(m)
            idxs.append(idx)
            work = jnp.where(erow == idx, -jnp.inf, work)
        es = [jnp.exp(v - vals[0]) for v in vals]
        tot = es[0] + es[1] + es[2] + es[3]

        onehot = [(erow == idx).astype(f32) for idx in idxs]
        assign = onehot[0] + onehot[1] + onehot[2] + onehot[3]
        before = jnp.dot(assign.astype(bf16), later, preferred_element_type=f32) + base
        base = base + jnp.sum(assign, axis=1, keepdims=True)

        mi = jnp.zeros((SUBLANES, ts), jnp.int32)
        gates = jnp.zeros((SUBLANES, ts), f32)
        for k in range(TOP_K):
            rank = jnp.sum(onehot[k] * before, axis=0, keepdims=True).astype(jnp.int32)
            mi = jnp.where(r8 == k, idxs[k], mi)
            mi = jnp.where(r8 == TOP_K + k, rank, mi)
            gates = jnp.where(r8 == k, es[k] / tot, gates)
        mi_ref[:, rows] = mi
        mf_ref[rows, :] = jnp.concatenate([gates, jnp.zeros((LANES - SUBLANES, ts), f32)], axis=0).T
    base_ref[...] = jnp.broadcast_to(base, base_ref.shape)
    cnt_ref[...] = jnp.broadcast_to(base, cnt_ref.shape)


def merge_route(o_ac, o_b, gates, x, wa, wb, wc, wo, g_ffn, wr, br, cnt0, tm):
    R, D = x.shape
    row = lambda w: pl.BlockSpec((tm, w), lambda i: (i, 0))
    full = lambda a: pl.BlockSpec(a.shape, lambda i: (0, 0))
    return pl.pallas_call(
        _merge_route_body,
        grid=(R // tm,),
        in_specs=[row(o_ac.shape[1]), row(o_b.shape[1]), row(gates.shape[1]), row(D),
                  full(wa), full(wb), full(wc), full(wo), full(g_ffn), full(wr), full(br), full(cnt0)],
        out_specs=[row(D), pl.BlockSpec((tm * SUBLANES, LANES), lambda i: (i, 0)),
                   pl.BlockSpec((SUBLANES, tm), lambda i: (0, i)), row(LANES),
                   pl.BlockSpec((N_EXPERTS, LANES), lambda i: (0, 0))],
        out_shape=[jax.ShapeDtypeStruct((R, D), jnp.float32), jax.ShapeDtypeStruct((R * SUBLANES, LANES), jnp.float32),
                   jax.ShapeDtypeStruct((SUBLANES, R), jnp.int32), jax.ShapeDtypeStruct((R, LANES), jnp.float32),
                   jax.ShapeDtypeStruct((N_EXPERTS, LANES), jnp.float32)],
        scratch_shapes=[pltpu.VMEM((N_EXPERTS, LANES), jnp.float32)],
        compiler_params=_cparams(("arbitrary",)), name="merge_route")(
            o_ac, o_b, gates, x, wa, wb, wc, wo, g_ffn, wr, br, cnt0)


def _route_tables_body(cnt_ref, mi_ref, dest_ref, blk_ref, pad_ref):
    tm = mi_ref.shape[1]
    nbl = blk_ref.shape[1]
    erow1 = lax.broadcasted_iota(jnp.int32, (N_EXPERTS, LANES), 0)
    shift = MOE_BLOCK.bit_length() - 1
    cnt = cnt_ref[...].astype(jnp.int32)
    padded = ((cnt + (MOE_BLOCK - 1)) >> shift) << shift
    pend = padded
    s = 1
    while s < N_EXPERTS:
        pend = pend + jnp.where(erow1 >= s, pltpu.roll(pend, s, axis=0), 0)
        s *= 2
    pstart = pend - padded
    mi = mi_ref[...]
    erow = lax.broadcasted_iota(jnp.int32, (N_EXPERTS, tm), 0)
    r8 = lax.broadcasted_iota(jnp.int32, (SUBLANES, tm), 0)
    dest = jnp.zeros((SUBLANES, tm), jnp.int32)
    for k in range(TOP_K):
        start = jnp.sum(jnp.where(erow == mi[k:k + 1, :], pstart[:, 0:1], 0), axis=0, keepdims=True)
        dest = jnp.where(r8 == k, start + mi[TOP_K + k:TOP_K + k + 1, :], dest)
    dest_ref[...] = dest

    @pl.when(pl.program_id(0) == 0)
    def _():
        row0 = lax.broadcasted_iota(jnp.int32, (N_EXPERTS, nbl), 1) * MOE_BLOCK
        ended = jnp.sum(jnp.where(pend[:, 0:1] <= row0, 1, 0), axis=0, keepdims=True)
        blk_ref[...] = jnp.broadcast_to(jnp.minimum(ended, N_EXPERTS - 1), blk_ref.shape)
        pad_ref[0] = pstart + cnt
        pad_ref[1] = pend


def route_tables(cnt, mi, nbl, tm):
    R = mi.shape[1]
    return pl.pallas_call(
        _route_tables_body,
        grid=(R // tm,),
        in_specs=[pl.BlockSpec((N_EXPERTS, LANES), lambda i: (0, 0)), pl.BlockSpec((SUBLANES, tm), lambda i: (0, i))],
        out_specs=[pl.BlockSpec((SUBLANES, tm), lambda i: (0, i)), pl.BlockSpec((SUBLANES, nbl), lambda i: (0, 0)),
                   pl.BlockSpec((2, N_EXPERTS, LANES), lambda i: (0, 0, 0))],
        out_shape=[jax.ShapeDtypeStruct((SUBLANES, R), jnp.int32), jax.ShapeDtypeStruct((SUBLANES, nbl), jnp.int32),
                   jax.ShapeDtypeStruct((2, N_EXPERTS, LANES), jnp.int32)],
        compiler_params=_cparams(("arbitrary",)), name="route_tables")(cnt, mi)


def _dispatch_body(n_first, n_steps, pad0_ref, pad1_ref, dest_ref, ha_ref, hb_ref, xs_ref, zero_ref, sem):
    i = pl.program_id(0)
    tm = ha_ref.shape[0] // SUBLANES
    rs = xs_ref.shape[0] // SUBLANES

    def scatter_tile(h_ref):
        def row_copy(r, k):
            return pltpu.make_async_copy(_row_tile(h_ref, r), _row_tile(xs_ref, dest_ref[0, 0, k * tm + r]), sem)

        def start(r, c):
            for k in range(TOP_K):
                row_copy(r, k).start(priority=k % 2)
            return c

        def wait(r, c):
            for k in range(TOP_K):
                row_copy(r, k).wait()
            return c

        lax.fori_loop(0, tm, start, 0, unroll=8)
        lax.fori_loop(0, tm, wait, 0, unroll=8)

    @pl.when(i < n_first)
    def _():
        scatter_tile(ha_ref)

    @pl.when(i >= n_first)
    def _():
        scatter_tile(hb_ref)

    @pl.when(i == n_steps - 1)
    def _():
        zero_ref[...] = jnp.zeros_like(zero_ref)
        zrows = zero_ref.shape[0] // SUBLANES

        def zero_range(lo, hi, go):
            n_big = (hi - lo) // zrows

            def big(j, c):
                start = pl.multiple_of((lo + j * zrows) * SUBLANES, SUBLANES)
                cp = pltpu.make_async_copy(zero_ref, xs_ref.at[pl.ds(start, zrows * SUBLANES)], sem)
                cp.start() if go else cp.wait()
                return c

            def one(r, c):
                cp = pltpu.make_async_copy(zero_ref.at[pl.ds(0, SUBLANES)], _row_tile(xs_ref, r), sem)
                cp.start() if go else cp.wait()
                return c

            lax.fori_loop(0, n_big, big, 0)
            lax.fori_loop(lo + n_big * zrows, hi, one, 0)

        for go in (True, False):
            for e in range(N_EXPERTS):
                zero_range(pad0_ref[e], pad1_ref[e], go)
            zero_range(pad1_ref[N_EXPERTS - 1], rs, go)


def dispatch(pad0, pad1, dest3, h_a, h_b, rs, tm):
    S = SUBLANES
    n_a, n_b = h_a.shape[0] // (tm * S), h_b.shape[0] // (tm * S)
    grid_spec = pltpu.PrefetchScalarGridSpec(
        num_scalar_prefetch=2,
        grid=(n_a + n_b,),
        in_specs=[pl.BlockSpec((1, 1, TOP_K * tm), lambda i, p0, p1: (i, 0, 0), memory_space=pltpu.SMEM),
                  pl.BlockSpec((tm * S, LANES), lambda i, p0, p1: (jnp.minimum(i, n_a - 1), 0)),
                  pl.BlockSpec((tm * S, LANES), lambda i, p0, p1: (jnp.maximum(i - n_a, 0), 0))],
        out_specs=pl.BlockSpec(memory_space=pl.ANY),
        scratch_shapes=[pltpu.VMEM((16 * S, LANES), h_a.dtype), pltpu.SemaphoreType.DMA])
    return pl.pallas_call(
        functools.partial(_dispatch_body, n_a, n_a + n_b), grid_spec=grid_spec,
        out_shape=jax.ShapeDtypeStruct((rs * S, LANES), h_a.dtype),
        compiler_params=_cparams(("arbitrary",)), name="dispatch")(pad0, pad1, dest3, h_a, h_b)


def _moe_body(nblk, be_ref, nact_ref, x_ref, wgu_hbm, bgu_ref, wd_hbm, bd_ref, y_ref, wgu_f, wd_f, wgu_s, wd_s,
              sems):
    f32, bf16 = jnp.float32, jnp.bfloat16
    b = pl.program_id(0)
    e = be_ref[b]

    def fetch(ex, go):
        for src, dst, s in ((wgu_hbm, wgu_f, 0), (wd_hbm, wd_f, 1)):
            cp = pltpu.make_async_copy(src.at[ex], dst, sems.at[s])
            cp.start() if go else cp.wait()

    @pl.when(b == 0)
    def _():
        fetch(e, True)

    @pl.when((b == 0) | (e != be_ref[jnp.maximum(b - 1, 0)]))
    def _():
        fetch(e, False)
        wgu_s[...] = wgu_f[...].astype(bf16)
        wd_s[...] = wd_f[...].astype(bf16)
        nxt = lax.while_loop(lambda j: (j < nblk) & (be_ref[jnp.minimum(j, nblk - 1)] == e), lambda j: j + 1, b + 1)

        @pl.when(nxt < nblk)
        def _():
            fetch(be_ref[jnp.minimum(nxt, nblk - 1)], True)

    @pl.when(b < nact_ref[0])
    def _():
        x = _load_row_tiles(x_ref, 0, MOE_BLOCK).astype(bf16)
        gu = jnp.dot(x, wgu_s[...], preferred_element_type=f32) + bgu_ref[0]
        gt = jnp.minimum(gu[:, :D_FF], SWIGLU_LIMIT)
        up = jnp.clip(gu[:, D_FF:], -SWIGLU_LIMIT, SWIGLU_LIMIT)
        act = (up + 1.0) * (gt * jax.nn.sigmoid(gt * SWIGLU_ALPHA))
        y = jnp.dot(act.astype(bf16), wd_s[...], preferred_element_type=f32) + bd_ref[0]
        _store_row_tiles(y_ref, 0, y)

    @pl.when(b >= nact_ref[0])
    def _():
        y_ref[...] = jnp.zeros_like(y_ref)


def moe_ffn(blk_e, n_active, xs, w_gate_up, b_gate_up, w_down, b_down):
    RS = xs.shape[0] // SUBLANES
    E, D, F2 = w_gate_up.shape
    blk_rows = MOE_BLOCK * SUBLANES
    grid_spec = pltpu.PrefetchScalarGridSpec(
        num_scalar_prefetch=2,
        grid=(RS // MOE_BLOCK,),
        in_specs=[pl.BlockSpec((blk_rows, LANES), lambda b, be, na: (jnp.minimum(b, jnp.maximum(na[0] - 1, 0)), 0)),
                  pl.BlockSpec(memory_space=pl.ANY),
                  pl.BlockSpec((1, 1, F2), lambda b, be, na: (be[b], 0, 0)),
                  pl.BlockSpec(memory_space=pl.ANY),
                  pl.BlockSpec((1, 1, D), lambda b, be, na: (be[b], 0, 0))],
        out_specs=pl.BlockSpec((blk_rows, LANES), lambda b, be, na: (b, 0)),
        scratch_shapes=[pltpu.VMEM((D, F2), w_gate_up.dtype), pltpu.VMEM((F2 // 2, D), w_down.dtype),
                        pltpu.VMEM((D, F2), jnp.bfloat16), pltpu.VMEM((F2 // 2, D), jnp.bfloat16),
                        pltpu.SemaphoreType.DMA((2,))])
    return pl.pallas_call(
        functools.partial(_moe_body, RS // MOE_BLOCK), grid_spec=grid_spec,
        out_shape=jax.ShapeDtypeStruct(xs.shape, jnp.float32),
        compiler_params=_cparams(("arbitrary",)), name="moe_ffn")(
            blk_e, n_active, xs, w_gate_up, b_gate_up.reshape(E, 1, F2), w_down, b_down.reshape(E, 1, D))


def _combine_body(n_tiles, dest_ref, dest_next_ref, x1_ref, mf_ref, g_ref, ys_ref, o_ref, buf, sems):
    tm = x1_ref.shape[0]
    i = pl.program_id(0)
    slot = i % 2
    per_slot = TOP_K * tm

    def gather(d_ref, s, go):
        def row_copy(r, k):
            return pltpu.make_async_copy(_row_tile(ys_ref, d_ref[0, 0, k * tm + r]),
                                         _row_tile(buf, s * per_slot + k * tm + r), sems.at[s])

        def body(r, c):
            for k in range(TOP_K):
                if go:
                    row_copy(r, k).start(priority=k % 2)
                else:
                    row_copy(r, k).wait()
            return c

        lax.fori_loop(0, tm, body, 0, unroll=8)

    @pl.when(i == 0)
    def _():
        gather(dest_ref, slot, True)

    @pl.when(i + 1 < n_tiles)
    def _():
        gather(dest_next_ref, 1 - slot, True)

    gather(dest_ref, slot, False)
    y = x1_ref[...]
    for k in range(TOP_K):
        y = y + mf_ref[:, k:k + 1] * _load_row_tiles(buf, slot * per_slot + k * tm, tm)
    o_ref[...] = _rms(y, g_ref[...])


def combine(dest3, x1, mf, g_final, ys, tm):
    R, D = x1.shape
    n = R // tm
    dest_spec = lambda f: pl.BlockSpec((1, 1, TOP_K * tm), f, memory_space=pltpu.SMEM)
    return pl.pallas_call(
        functools.partial(_combine_body, n),
        grid=(n,),
        in_specs=[dest_spec(lambda i: (i, 0, 0)), dest_spec(lambda i: (jnp.minimum(i + 1, n - 1), 0, 0)),
                  pl.BlockSpec((tm, D), lambda i: (i, 0)),
                  pl.BlockSpec((tm, LANES), lambda i: (i, 0)),
                  pl.BlockSpec((1, D), lambda i: (0, 0)),
                  pl.BlockSpec(memory_space=pl.ANY)],
        out_specs=pl.BlockSpec((tm, D), lambda i: (i, 0)),
        out_shape=jax.ShapeDtypeStruct((R, D), jnp.float32),
        scratch_shapes=[pltpu.VMEM((2 * TOP_K * tm * SUBLANES, LANES), ys.dtype), pltpu.SemaphoreType.DMA((2,))],
        compiler_params=_cparams(("arbitrary",)), name="combine")(dest3, dest3, x1, mf, g_final, ys)


def moe_layer(group_a, group_b, cnt, g_final, w_gate_up, b_gate_up, w_down, b_down, tm):
    n_assign = (group_a[0].shape[0] + group_b[0].shape[0]) * TOP_K
    nb = (n_assign + N_EXPERTS * (MOE_BLOCK - 1)) // MOE_BLOCK + 1
    nbl = -(-nb // LANES) * LANES
    dests = []
    for x1, h2, mi, mf in (group_a, group_b):
        dest, blk, pad = route_tables(cnt, mi, nbl, min(x1.shape[0], 4 * tm))
        n_tiles = x1.shape[0] // tm
        dests.append(jnp.transpose(dest[:TOP_K].reshape(TOP_K, n_tiles, tm), (1, 0, 2)).reshape(n_tiles, 1, TOP_K * tm))
    xs = dispatch(pad[0, :, 0], pad[1, :, 0], jnp.concatenate(dests), group_a[1], group_b[1], nb * MOE_BLOCK, tm)
    n_active = (pad[1, N_EXPERTS - 1, 0] // MOE_BLOCK).reshape(1)
    ys = moe_ffn(blk[0, :nb], n_active, xs, w_gate_up, b_gate_up, w_down, b_down)
    return [combine(dest3, x1, mf, g_final, ys, tm) for dest3, (x1, h2, mi, mf) in zip(dests, (group_a, group_b))]


def kernel(x_prompt, x_sample, cache_swa_k, cache_swa_v, cache_dil_k, cache_dil_v, cache_mem_k, cache_mem_v, mem_prompt, norm_attn, norm_mem, w_in, w_mem_kv, sinks, w_br_a, w_br_b, w_br_c, w_out, norm_ffn, w_router, b_router, w_gate_up, b_gate_up, w_down, b_down, norm_final):
    f32, bf16 = jnp.float32, jnp.bfloat16
    TM = 256
    B, L, D = x_prompt.shape
    NB, n_new, _ = x_sample.shape
    M = mem_prompt.shape[1]
    wc = MEM_HEADS * HEAD_DIM
    assert n_new == N_NEW and cache_swa_k.shape[0] == 1

    w_in_b = w_in[0].astype(bf16)
    g_attn = norm_attn[0].reshape(1, D)
    secs = in_sections()

    tabs_p = rope_tables(jnp.arange(L, dtype=jnp.int32))
    qa, ka, va, qb, kb, vb, qc, gates = norm_proj(x_prompt.reshape(B * L, D), g_attn, w_in_b, tabs_p, secs,
                                                  2 * PROJ_SUB)
    mk, mv = norm_proj(mem_prompt.reshape(B * M, D), norm_mem[0].reshape(1, D), w_mem_kv[0].astype(bf16), None,
                       [(_chunks(0, wc), "plain", False, f32), (_chunks(wc, wc), "plain", False, f32)], TM)
    o_ac, ka_t, va_t, mk_t, mv_t = attn_swa_mem(sinks[0], qa, ka, va, qc, mk, mv, B, L)
    o_b, kb_t, vb_t = attn_dilated(qb, kb, vb, B, L)

    xs_pad = jnp.pad(x_sample, ((0, 0), (0, SROWS - N_NEW), (0, 0))).reshape(NB * SROWS, D)
    tabs_s = rope_tables(PAST_LEN + (jnp.arange(TM, dtype=jnp.int32) % SROWS))
    qa_s, ka_s, va_s, qb_s, kb_s, vb_s, qc_s, gates_s = norm_proj(xs_pad, g_attn, w_in_b, tabs_s, secs, TM)
    real = lambda t: t.reshape(NB, SROWS, -1)[:, :N_NEW].reshape(NB * N_NEW, -1)
    fmaj = lambda c: jnp.transpose(c[0], (0, 2, 3, 1)).reshape(NB, -1, c.shape[2])
    o_ac_s, o_b_s, swa_k_s, swa_v_s, dil_k_s, dil_v_s = attn_sample(
        sinks[0], qa_s, qb_s, qc_s, ka_s, va_s, kb_s, vb_s,
        fmaj(cache_swa_k), fmaj(cache_swa_v), fmaj(cache_dil_k), fmaj(cache_dil_v),
        fmaj(cache_mem_k), fmaj(cache_mem_v), 2)

    wr = w_router[0].T.astype(bf16)
    br = jnp.broadcast_to(b_router[0].astype(f32)[:, None], (N_EXPERTS, LANES))
    wts = (w_br_a[0].astype(bf16), w_br_b[0].astype(bf16), w_br_c[0].astype(bf16), w_out[0].astype(bf16),
           norm_ffn[0].reshape(1, D), wr, br)
    x1_p, h2_p, mi_p, mf_p, cnt_p = merge_route(o_ac, o_b, gates, x_prompt.reshape(B * L, D), *wts,
                                                jnp.zeros((N_EXPERTS, LANES), f32), 4 * ROUTE_SUB)
    x1_s, h2_s, mi_s, mf_s, cnt = merge_route(real(o_ac_s), real(o_b_s), real(gates_s),
                                              x_sample.reshape(NB * N_NEW, D), *wts, cnt_p, 2 * ROUTE_SUB)
    y_p, y_s = moe_layer((x1_p, h2_p, mi_p, mf_p), (x1_s, h2_s, mi_s, mf_s), cnt, norm_final.reshape(1, D),
                         w_gate_up[0], b_gate_up[0], w_down[0], b_down[0], TM)

    tmaj = lambda t, h: jnp.transpose(t.reshape(t.shape[0], h, HEAD_DIM, t.shape[2]), (0, 3, 1, 2))[None]
    return (y_p.reshape(B, L, D), y_s.reshape(NB, N_NEW, D),
            tmaj(ka_t, SWA_KV_HEADS), tmaj(va_t, SWA_KV_HEADS),
            tmaj(kb_t, DIL_KV_HEADS), tmaj(vb_t, DIL_KV_HEADS),
            tmaj(mk_t, MEM_HEADS), tmaj(mv_t, MEM_HEADS),
            tmaj(swa_k_s, SWA_KV_HEADS), tmaj(swa_v_s, SWA_KV_HEADS),
            tmaj(dil_k_s, DIL_KV_HEADS), tmaj(dil_v_s, DIL_KV_HEADS))
```
